```python
import math, functools
import jax, jax.numpy as jnp
from jax import lax
import numpy as np

D_MODEL = 2048
BATCH = 8
SEQ = 4096
DEPTH = 1
DEC_BATCH = 16
DEC_SEQ = 16
PAST_LEN = 4096

CHUNK = 64
BAND_PAST_CHUNKS = 8
BAND_PAST = BAND_PAST_CHUNKS * CHUNK
BAND = BAND_PAST + CHUNK
ATT_HEADS = 16
ATT_HEAD_DIM = 64
ATT_WIDTH = ATT_HEADS * ATT_HEAD_DIM
REL_CLIP = 128
SSM_GROUP = 16
SSM_WIDTH = 1024
SSM_GROUPS = SSM_WIDTH // SSM_GROUP
SSM_STATE = 64
N_MEM = 256
MEM_HEADS = 4
MEM_HEAD_DIM = 256
MEM_WIDTH = MEM_HEADS * MEM_HEAD_DIM
N_BRANCH = 3
D_FF = 5632
EPS = 1e-6
DT_MIN = 1e-3
DT_MAX = 1e-1
IN_WIDTH = SSM_WIDTH + 3 * ATT_WIDTH + MEM_WIDTH + N_BRANCH * D_MODEL
SPLITS = [SSM_WIDTH, SSM_WIDTH + ATT_WIDTH, SSM_WIDTH + 2 * ATT_WIDTH, SSM_WIDTH + 3 * ATT_WIDTH,
          SSM_WIDTH + 3 * ATT_WIDTH + MEM_WIDTH]
NEG_INF = -1e30

kernel_name = "hybrid_streaming_encoder_step"


def rms_norm(x, g):
    xf = x.astype(jnp.float32)
    y = xf * lax.rsqrt(jnp.mean(xf * xf, axis=-1, keepdims=True) + EPS)
    return (y * g.astype(jnp.float32)).astype(x.dtype)


def swiglu(x, wg, wu, wd):
    return (jax.nn.silu(x @ wg) * (x @ wu)) @ wd


def macaron_ffn(h, g_pre, g_post, wg, wu, wd):
    return h + 0.5 * rms_norm(swiglu(rms_norm(h, g_pre), wg, wu, wd), g_post)


def s5_discretize(a_re, a_im, log_dt, b_re, b_im):
    a_re = a_re.astype(jnp.float32)
    a_im = a_im.astype(jnp.float32)
    dt = jnp.exp(log_dt.astype(jnp.float32))[:, None]
    mag = jnp.exp(a_re * dt)
    ab_re = mag * jnp.cos(a_im * dt)
    ab_im = mag * jnp.sin(a_im * dt)
    den = a_re * a_re + a_im * a_im
    n_re = ab_re - 1.0
    n_im = ab_im
    c_re = (n_re * a_re + n_im * a_im) / den
    c_im = (n_im * a_re - n_re * a_im) / den
    b_re = b_re.astype(jnp.float32)
    b_im = b_im.astype(jnp.float32)
    bb_re = c_re[..., None] * b_re - c_im[..., None] * b_im
    bb_im = c_re[..., None] * b_im + c_im[..., None] * b_re
    return ab_re, ab_im, bb_re, bb_im


def _complex_affine_combine(e1, e2):
    a1r, a1i, b1r, b1i = e1
    a2r, a2i, b2r, b2i = e2
    ar = a2r * a1r - a2i * a1i
    ai = a2r * a1i + a2i * a1r
    br = a2r * b1r - a2i * b1i + b2r
    bi = a2r * b1i + a2i * b1r + b2i
    return ar, ai, br, bi


def s5_branch(u, s0_re, s0_im, p):
    n_b, t, _ = u.shape
    uf = u.astype(jnp.float32).reshape(n_b, t, SSM_GROUPS, SSM_GROUP)
    ab_re, ab_im, bb_re, bb_im = s5_discretize(p["ssm_a_re"], p["ssm_a_im"], p["ssm_log_dt"],
                                               p["ssm_b_re"], p["ssm_b_im"])
    bu_re = jnp.einsum("gph,btgh->btgp", bb_re, uf)
    bu_im = jnp.einsum("gph,btgh->btgp", bb_im, uf)
    s0_re = s0_re.astype(jnp.float32)
    s0_im = s0_im.astype(jnp.float32)
    bu_re = bu_re.at[:, 0].add(ab_re * s0_re - ab_im * s0_im)
    bu_im = bu_im.at[:, 0].add(ab_re * s0_im + ab_im * s0_re)
    a_re_t = jnp.broadcast_to(ab_re, (1, t, SSM_GROUPS, SSM_STATE))
    a_im_t = jnp.broadcast_to(ab_im, (1, t, SSM_GROUPS, SSM_STATE))
    _, _, s_re, s_im = lax.associative_scan(_complex_affine_combine, (a_re_t, a_im_t, bu_re, bu_im), axis=1)
    c_re = p["ssm_c_re"].astype(jnp.float32)
    c_im = p["ssm_c_im"].astype(jnp.float32)
    y = (jnp.einsum("ghp,btgp->btgh", c_re, s_re) - jnp.einsum("ghp,btgp->btgh", c_im, s_im)
         + p["ssm_d"].astype(jnp.float32) * uf)
    y = jax.nn.gelu(y.reshape(n_b, t, SSM_WIDTH))
    out = y * jax.nn.sigmoid(y @ p["ssm_w_glu"].astype(jnp.float32) + p["ssm_b_glu"].astype(jnp.float32))
    return out.astype(u.dtype), s_re[:, -1], s_im[:, -1]


def band_attend(q, k, v, q_pos, k_pos, rel_bias):
    s = jnp.einsum("bqhd,bkhd->bhqk", q, k).astype(jnp.float32) * (ATT_HEAD_DIM ** -0.5)
    rel = jnp.clip(q_pos[:, None] - k_pos[None, :], -REL_CLIP, REL_CLIP) + REL_CLIP
    s = s + rel_bias.astype(jnp.float32)[:, rel]
    q_chunk = q_pos // CHUNK
    k_chunk = k_pos // CHUNK
    ok = ((k_pos[None, :] >= 0) & (k_chunk[None, :] <= q_chunk[:, None])
          & (k_chunk[None, :] >= q_chunk[:, None] - BAND_PAST_CHUNKS))
    s = jnp.where(ok, s, NEG_INF)
    prob = jax.nn.softmax(s, axis=-1)
    return jnp.einsum("bhqk,bkhd->bqhd", prob.astype(v.dtype), v)


def band_attention_prompt(q, k, v, rel_bias):
    n_b, t, _, _ = q.shape
    n_chunks = t // CHUNK
    kp = jnp.pad(k, ((0, 0), (BAND_PAST, 0), (0, 0), (0, 0)))
    vp = jnp.pad(v, ((0, 0), (BAND_PAST, 0), (0, 0), (0, 0)))

    def one_chunk(c):
        start = c * CHUNK
        q_c = lax.dynamic_slice_in_dim(q, start, CHUNK, axis=1)
        k_c = lax.dynamic_slice_in_dim(kp, start, BAND, axis=1)
        v_c = lax.dynamic_slice_in_dim(vp, start, BAND, axis=1)
        q_pos = start + jnp.arange(CHUNK)
        k_pos = start - BAND_PAST + jnp.arange(BAND)
        return band_attend(q_c, k_c, v_c, q_pos, k_pos, rel_bias)

    o = lax.map(one_chunk, jnp.arange(n_chunks))
    return jnp.moveaxis(o, 0, 1).reshape(n_b, t, ATT_WIDTH)


def band_attention_sample(q, k, v, rel_bias, k_cache, v_cache):
    n_b, t, _, _ = q.shape
    w = k_cache.shape[1]
    k_all = jnp.concatenate([k_cache.astype(k.dtype), k], axis=1)
    v_all = jnp.concatenate([v_cache.astype(v.dtype), v], axis=1)
    q_pos = PAST_LEN + jnp.arange(t)
    k_pos = PAST_LEN - w + jnp.arange(w + t)
    o = band_attend(q, k_all, v_all, q_pos, k_pos, rel_bias)
    return o.reshape(n_b, t, ATT_WIDTH)


def memory_attend(q, mem_k, mem_v):
    n_b, t, _, _ = q.shape
    s = jnp.einsum("bqhd,bmhd->bhqm", q, mem_k.astype(q.dtype)).astype(jnp.float32) * (MEM_HEAD_DIM ** -0.5)
    prob = jax.nn.softmax(s, axis=-1)
    o = jnp.einsum("bhqm,bmhd->bqhd", prob.astype(q.dtype), mem_v.astype(q.dtype))
    return o.reshape(n_b, t, MEM_WIDTH)


def memory_kv(mem, g, w_k, w_v):
    n_b = mem.shape[0]
    m = rms_norm(mem, g)
    mk = (m @ w_k).reshape(n_b, N_MEM, MEM_HEADS, MEM_HEAD_DIM)
    mv = (m @ w_v).reshape(n_b, N_MEM, MEM_HEADS, MEM_HEAD_DIM)
    return mk, mv


def layer_forward(h, p, attn_fn, s0_re, s0_im, mem_k, mem_v):
    h = macaron_ffn(h, p["ffn1_norm_pre"], p["ffn1_norm_post"], p["ffn1_w_gate"], p["ffn1_w_up"], p["ffn1_w_down"])
    n_b, t, _ = h.shape
    u = rms_norm(h, p["mix_norm_pre"])
    proj = u @ p["w_in"]
    u_s, q, k, v, q_mem, gate_logits = jnp.split(proj, SPLITS, axis=-1)
    q = q.reshape(n_b, t, ATT_HEADS, ATT_HEAD_DIM)
    k = k.reshape(n_b, t, ATT_HEADS, ATT_HEAD_DIM)
    v = v.reshape(n_b, t, ATT_HEADS, ATT_HEAD_DIM)
    q_mem = q_mem.reshape(n_b, t, MEM_HEADS, MEM_HEAD_DIM)
    o_s, s_re, s_im = s5_branch(u_s, s0_re, s0_im, p)
    o_a = attn_fn(q, k, v, p["att_rel_bias"])
    o_m = memory_attend(q_mem, mem_k, mem_v)
    gates = jax.nn.sigmoid(gate_logits.reshape(n_b, t, N_BRANCH, D_MODEL))
    merged = (gates[:, :, 0] * (o_s @ p["w_branch_ssm"])
              + gates[:, :, 1] * (o_a @ p["w_branch_att"])
              + gates[:, :, 2] * (o_m @ p["w_branch_mem"]))
    h = h + rms_norm(merged @ p["w_out"], p["mix_norm_post"])
    h = macaron_ffn(h, p["ffn2_norm_pre"], p["ffn2_norm_post"], p["ffn2_w_gate"], p["ffn2_w_up"], p["ffn2_w_down"])
    return h, k, v, s_re, s_im


def setup_inputs(seed: int = 0) -> dict:
    key = jax.random.key(seed)
    ks = iter(jax.random.split(key, 64))
    f32 = jnp.float32

    def nrm(shape, scale):
        return jax.random.normal(next(ks), shape, f32) * scale

    def gain(shape):
        return 1.0 + nrm(shape, 0.02)

    att_cache = min(BAND_PAST, PAST_LEN)
    inp = {}
    inp["x_prompt"] = nrm((BATCH, SEQ, D_MODEL), 1.0)
    inp["x_sample"] = nrm((DEC_BATCH, DEC_SEQ, D_MODEL), 1.0)
    inp["mem_prompt"] = nrm((BATCH, N_MEM, D_MODEL), 1.0)
    inp["cache_att_k"] = nrm((DEPTH, DEC_BATCH, att_cache, ATT_HEADS, ATT_HEAD_DIM), 1.0)
    inp["cache_att_v"] = nrm((DEPTH, DEC_BATCH, att_cache, ATT_HEADS, ATT_HEAD_DIM), 1.0)
    inp["cache_mem_k"] = nrm((DEPTH, DEC_BATCH, N_MEM, MEM_HEADS, MEM_HEAD_DIM), 1.0)
    inp["cache_mem_v"] = nrm((DEPTH, DEC_BATCH, N_MEM, MEM_HEADS, MEM_HEAD_DIM), 1.0)
    inp["state_ssm_re"] = nrm((DEPTH, DEC_BATCH, SSM_GROUPS, SSM_STATE), 0.1)
    inp["state_ssm_im"] = nrm((DEPTH, DEC_BATCH, SSM_GROUPS, SSM_STATE), 0.1)
    inp["ffn1_norm_pre"] = gain((DEPTH, D_MODEL))
    inp["ffn1_norm_post"] = gain((DEPTH, D_MODEL))
    inp["ffn1_w_gate"] = nrm((DEPTH, D_MODEL, D_FF), D_MODEL ** -0.5)
    inp["ffn1_w_up"] = nrm((DEPTH, D_MODEL, D_FF), D_MODEL ** -0.5)
    inp["ffn1_w_down"] = nrm((DEPTH, D_FF, D_MODEL), D_FF ** -0.5)
    inp["mix_norm_pre"] = gain((DEPTH, D_MODEL))
    inp["mix_norm_post"] = gain((DEPTH, D_MODEL))
    inp["w_in"] = nrm((DEPTH, D_MODEL, IN_WIDTH), D_MODEL ** -0.5)
    inp["ssm_a_re"] = -0.5 + nrm((DEPTH, SSM_GROUPS, SSM_STATE), 0.01)
    inp["ssm_a_im"] = (math.pi * jnp.arange(SSM_STATE, dtype=f32))[None, None, :] + nrm((DEPTH, SSM_GROUPS, SSM_STATE), 0.01)
    inp["ssm_log_dt"] = jax.random.uniform(next(ks), (DEPTH, SSM_GROUPS), f32, math.log(DT_MIN), math.log(DT_MAX))
    inp["ssm_b_re"] = nrm((DEPTH, SSM_GROUPS, SSM_STATE, SSM_GROUP), (2 * SSM_GROUP) ** -0.5)
    inp["ssm_b_im"] = nrm((DEPTH, SSM_GROUPS, SSM_STATE, SSM_GROUP), (2 * SSM_GROUP) ** -0.5)
    inp["ssm_c_re"] = nrm((DEPTH, SSM_GROUPS, SSM_GROUP, SSM_STATE), SSM_STATE ** -0.5)
    inp["ssm_c_im"] = nrm((DEPTH, SSM_GROUPS, SSM_GROUP, SSM_STATE), SSM_STATE ** -0.5)
    inp["ssm_d"] = nrm((DEPTH, SSM_GROUPS, SSM_GROUP), 1.0)
    inp["ssm_w_glu"] = nrm((DEPTH, SSM_WIDTH, SSM_WIDTH), SSM_WIDTH ** -0.5)
    inp["ssm_b_glu"] = nrm((DEPTH, SSM_WIDTH), 0.01)
    inp["att_rel_bias"] = nrm((DEPTH, ATT_HEADS, 2 * REL_CLIP + 1), 0.5)
    inp["mem_norm"] = gain((DEPTH, D_MODEL))
    inp["w_mem_k"] = nrm((DEPTH, D_MODEL, MEM_WIDTH), D_MODEL ** -0.5)
    inp["w_mem_v"] = nrm((DEPTH, D_MODEL, MEM_WIDTH), D_MODEL ** -0.5)
    inp["w_branch_ssm"] = nrm((DEPTH, SSM_WIDTH, D_MODEL), SSM_WIDTH ** -0.5)
    inp["w_branch_att"] = nrm((DEPTH, ATT_WIDTH, D_MODEL), ATT_WIDTH ** -0.5)
    inp["w_branch_mem"] = nrm((DEPTH, MEM_WIDTH, D_MODEL), MEM_WIDTH ** -0.5)
    inp["w_out"] = nrm((DEPTH, D_MODEL, D_MODEL), D_MODEL ** -0.5)
    inp["ffn2_norm_pre"] = gain((DEPTH, D_MODEL))
    inp["ffn2_norm_post"] = gain((DEPTH, D_MODEL))
    inp["ffn2_w_gate"] = nrm((DEPTH, D_MODEL, D_FF), D_MODEL ** -0.5)
    inp["ffn2_w_up"] = nrm((DEPTH, D_MODEL, D_FF), D_MODEL ** -0.5)
    inp["ffn2_w_down"] = nrm((DEPTH, D_FF, D_MODEL), D_FF ** -0.5)
    return inp


def reference(x_prompt, x_sample, mem_prompt, cache_att_k, cache_att_v, cache_mem_k, cache_mem_v,
              state_ssm_re, state_ssm_im,
              ffn1_norm_pre, ffn1_norm_post, ffn1_w_gate, ffn1_w_up, ffn1_w_down,
              mix_norm_pre, mix_norm_post, w_in,
              ssm_a_re, ssm_a_im, ssm_log_dt, ssm_b_re, ssm_b_im, ssm_c_re, ssm_c_im, ssm_d,
              ssm_w_glu, ssm_b_glu, att_rel_bias, mem_norm, w_mem_k, w_mem_v,
              w_branch_ssm, w_branch_att, w_branch_mem, w_out,
              ffn2_norm_pre, ffn2_norm_post, ffn2_w_gate, ffn2_w_up, ffn2_w_down):
    n_bp, t_p, _ = x_prompt.shape
    keep = min(BAND_PAST, t_p)
    h_p = x_prompt
    h_s = x_sample
    att_k_p, att_v_p, mem_k_p, mem_v_p, ssm_re_p, ssm_im_p = [], [], [], [], [], []
    att_k_s, att_v_s, ssm_re_s, ssm_im_s = [], [], [], []
    for l in range(DEPTH):
        p = {
            "ffn1_norm_pre": ffn1_norm_pre[l], "ffn1_norm_post": ffn1_norm_post[l],
            "ffn1_w_gate": ffn1_w_gate[l], "ffn1_w_up": ffn1_w_up[l], "ffn1_w_down": ffn1_w_down[l],
            "mix_norm_pre": mix_norm_pre[l], "mix_norm_post": mix_norm_post[l], "w_in": w_in[l],
            "ssm_a_re": ssm_a_re[l], "ssm_a_im": ssm_a_im[l], "ssm_log_dt": ssm_log_dt[l],
            "ssm_b_re": ssm_b_re[l], "ssm_b_im": ssm_b_im[l], "ssm_c_re": ssm_c_re[l], "ssm_c_im": ssm_c_im[l],
            "ssm_d": ssm_d[l], "ssm_w_glu": ssm_w_glu[l], "ssm_b_glu": ssm_b_glu[l],
            "att_rel_bias": att_rel_bias[l],
            "w_branch_ssm": w_branch_ssm[l], "w_branch_att": w_branch_att[l], "w_branch_mem": w_branch_mem[l],
            "w_out": w_out[l],
            "ffn2_norm_pre": ffn2_norm_pre[l], "ffn2_norm_post": ffn2_norm_post[l],
            "ffn2_w_gate": ffn2_w_gate[l], "ffn2_w_up": ffn2_w_up[l], "ffn2_w_down": ffn2_w_down[l],
        }
        mk_p, mv_p = memory_kv(mem_prompt, mem_norm[l], w_mem_k[l], w_mem_v[l])
        zero_state = jnp.zeros((n_bp, SSM_GROUPS, SSM_STATE), jnp.float32)
        h_p, k_p, v_p, sr_p, si_p = layer_forward(h_p, p, band_attention_prompt, zero_state, zero_state, mk_p, mv_p)
        att_k_p.append(k_p[:, t_p - keep:])
        att_v_p.append(v_p[:, t_p - keep:])
        mem_k_p.append(mk_p)
        mem_v_p.append(mv_p)
        ssm_re_p.append(sr_p)
        ssm_im_p.append(si_p)
        attn_s = functools.partial(band_attention_sample, k_cache=cache_att_k[l], v_cache=cache_att_v[l])
        h_s, k_s, v_s, sr_s, si_s = layer_forward(h_s, p, attn_s, state_ssm_re[l], state_ssm_im[l],
                                                  cache_mem_k[l], cache_mem_v[l])
        att_k_s.append(k_s)
        att_v_s.append(v_s)
        ssm_re_s.append(sr_s)
        ssm_im_s.append(si_s)
    return (h_p, h_s,
            jnp.stack(att_k_p), jnp.stack(att_v_p), jnp.stack(mem_k_p), jnp.stack(mem_v_p),
            jnp.stack(ssm_re_p), jnp.stack(ssm_im_p),
            jnp.stack(att_k_s), jnp.stack(att_v_s), jnp.stack(ssm_re_s), jnp.stack(ssm_im_s))
```

```python
import functools
import math

import numpy as np
import jax
import jax.numpy as jnp
from jax import lax
from jax.experimental import pallas as pl
from jax.experimental.pallas import tpu as pltpu

F32 = jnp.float32
BF16 = jnp.bfloat16

D_MODEL = 2048
CHUNK = 64
BAND_PAST_CHUNKS = 8
BAND_PAST = BAND_PAST_CHUNKS * CHUNK
ATT_HEADS = 16
ATT_HEAD_DIM = 64
ATT_WIDTH = ATT_HEADS * ATT_HEAD_DIM
REL_CLIP = 128
SSM_GROUP = 16
SSM_WIDTH = 1024
SSM_GROUPS = SSM_WIDTH // SSM_GROUP
SSM_STATE = 64
N_MEM = 256
MEM_HEADS = 4
MEM_HEAD_DIM = 256
MEM_WIDTH = MEM_HEADS * MEM_HEAD_DIM
N_BRANCH = 3
EPS = 1e-6
NEG_INF = -1e30
PAST_LEN = 4096

COL_SSM = 0
COL_Q = SSM_WIDTH
COL_K = COL_Q + ATT_WIDTH
COL_GATE = SSM_WIDTH + 3 * ATT_WIDTH + MEM_WIDTH

SSM_CHUNK = 16
OCTETS = 8
OCT_GROUPS = SSM_GROUPS // OCTETS
LANES = 128
OCT_COLS = SSM_CHUNK * LANES
OCT_STATE = 2 * OCT_GROUPS * SSM_STATE

Q_TILE = 512
Q_SUB = 128
K_WIN = Q_SUB + BAND_PAST

MIB = 1024 * 1024


def _params(sem, vmem_mib):
    return pltpu.CompilerParams(dimension_semantics=sem, vmem_limit_bytes=vmem_mib * MIB)


def _dot(a, b):
    return jnp.dot(a, b, preferred_element_type=F32)


def _dot_nt(a, b):
    return lax.dot_general(a, b, (((1,), (1,)), ((), ())), preferred_element_type=F32)


def _rms(xf, g):
    y = xf * lax.rsqrt(jnp.mean(xf * xf, axis=-1, keepdims=True) + EPS)
    return y * g


def _ffn_kernel(h_ref, gpre_ref, gpost_ref, wg_ref, wu_ref, wd_ref, *rest, nf, emit_next):
    if emit_next:
        gnext_ref, out_ref, nxt_ref, xn_ref = rest
    else:
        out_ref, xn_ref = rest
    f = pl.program_id(1)

    @pl.when(f == 0)
    def _():
        xn_ref[...] = _rms(h_ref[...], gpre_ref[...]).astype(BF16)
        out_ref[...] = jnp.zeros_like(out_ref)

    xn = xn_ref[...]
    g = _dot(xn, wg_ref[...])
    u = _dot(xn, wu_ref[...])
    a = (g * jax.nn.sigmoid(g)) * u
    out_ref[...] += _dot(a.astype(BF16), wd_ref[...])

    @pl.when(f == nf - 1)
    def _():
        hn = h_ref[...] + 0.5 * _rms(out_ref[...], gpost_ref[...])
        out_ref[...] = hn
        if emit_next:
            nxt_ref[...] = _rms(hn, gnext_ref[...]).astype(BF16)


def _ffn(h, g_pre, g_post, wg, wu, wd, g_next, tm, tf):
    emit_next = g_next is not None
    m, d = h.shape
    f_dim = wg.shape[1]
    nf = f_dim // tf
    assert m % tm == 0 and f_dim % tf == 0
    vec = pl.BlockSpec((1, d), lambda i, f: (0, 0))
    row = pl.BlockSpec((tm, d), lambda i, f: (i, 0))
    in_specs = [row, vec, vec,
                pl.BlockSpec((d, tf), lambda i, f: (0, f)),
                pl.BlockSpec((d, tf), lambda i, f: (0, f)),
                pl.BlockSpec((tf, d), lambda i, f: (f, 0))]
    args = [h, g_pre, g_post, wg, wu, wd]
    out_shape = [jax.ShapeDtypeStruct((m, d), F32)]
    out_specs = [row]
    if emit_next:
        in_specs.append(vec)
        args.append(g_next)
        out_shape.append(jax.ShapeDtypeStruct((m, d), BF16))
        out_specs.append(row)
    res = pl.pallas_call(
        functools.partial(_ffn_kernel, nf=nf, emit_next=emit_next),
        grid=(m // tm, nf),
        in_specs=in_specs,
        out_specs=out_specs,
        out_shape=out_shape,
        scratch_shapes=[pltpu.VMEM((tm, d), BF16)],
        compiler_params=_params(("parallel", "arbitrary"), 56),
        name="ffn",
    )(*args)
    return res if emit_next else res[0]


def _mm_kernel(x_ref, w_ref, o_ref):
    o_ref[...] = _dot(x_ref[...], w_ref[...]).astype(o_ref.dtype)


def _matmul(x, w, col0, n_cols, out_dtype, tm, tn, m_out=None, row_map=None, name="proj"):
    m, k = x.shape
    m_out = m if m_out is None else m_out
    row_map = (lambda i: i) if row_map is None else row_map
    assert m_out % tm == 0 and n_cols % tn == 0 and col0 % tn == 0
    cb0 = col0 // tn
    return pl.pallas_call(
        _mm_kernel,
        grid=(m_out // tm, n_cols // tn),
        in_specs=[pl.BlockSpec((tm, k), lambda i, j: (row_map(i), 0)),
                  pl.BlockSpec((k, tn), lambda i, j: (0, cb0 + j))],
        out_specs=pl.BlockSpec((tm, tn), lambda i, j: (i, j)),
        out_shape=jax.ShapeDtypeStruct((m_out, n_cols), out_dtype),
        compiler_params=_params(("parallel", "arbitrary"), 48),
        name=name,
    )(x, w)


def _norm_mm_kernel(x_ref, g_ref, w_ref, o_ref):
    xn = _rms(x_ref[...], g_ref[...]).astype(BF16)
    o_ref[...] = _dot(xn, w_ref[...]).astype(o_ref.dtype)


def _norm_matmul(x, g, w, tm, tn):
    m, k = x.shape
    n = w.shape[1]
    tm = min(tm, m)
    assert m % tm == 0 and n % tn == 0
    return pl.pallas_call(
        _norm_mm_kernel,
        grid=(m // tm, n // tn),
        in_specs=[pl.BlockSpec((tm, k), lambda i, j: (i, 0)),
                  pl.BlockSpec((1, k), lambda i, j: (0, 0)),
                  pl.BlockSpec((k, tn), lambda i, j: (0, j))],
        out_specs=pl.BlockSpec((tm, tn), lambda i, j: (i, j)),
        out_shape=jax.ShapeDtypeStruct((m, n), F32),
        compiler_params=_params(("parallel", "arbitrary"), 48),
        name="memory_kv",
    )(x, g, w)


def _ssm_inproj_kernel(x_ref, w_ref, o_ref):
    r = _dot(x_ref[...], w_ref[...])
    for m in range(OCTETS):
        o_ref[m] = r[:, m * LANES:(m + 1) * LANES]


def _ssm_inproj(u2, w_in, tc):
    nc = u2.shape[0]
    assert nc % tc == 0
    return pl.pallas_call(
        _ssm_inproj_kernel,
        grid=(nc // tc, SSM_CHUNK),
        in_specs=[pl.BlockSpec((tc, D_MODEL), lambda c, j: (c, j)),
                  pl.BlockSpec((D_MODEL, SSM_WIDTH), lambda c, j: (0, COL_SSM // SSM_WIDTH))],
        out_specs=pl.BlockSpec((OCTETS, tc, LANES), lambda c, j: (0, c, j)),
        out_shape=jax.ShapeDtypeStruct((OCTETS, nc, OCT_COLS), F32),
        compiler_params=_params(("parallel", "arbitrary"), 48),
        name="ssm_inproj",
    )(u2, w_in)


def _ssm_operators(a_re, a_im, log_dt, b_re, b_im, c_re, c_im):
    hi = lax.Precision.HIGHEST
    n_st, n_ch = SSM_STATE, SSM_GROUP
    dt = jnp.exp(log_dt)[:, None]
    mag = jnp.exp(a_re * dt)
    ab_re = mag * jnp.cos(a_im * dt)
    ab_im = mag * jnp.sin(a_im * dt)
    den = a_re * a_re + a_im * a_im
    n_re = ab_re - 1.0
    n_im = ab_im
    k_re = (n_re * a_re + n_im * a_im) / den
    k_im = (n_im * a_re - n_re * a_im) / den
    bb_re = k_re[..., None] * b_re - k_im[..., None] * b_im
    bb_im = k_re[..., None] * b_im + k_im[..., None] * b_re
    pr = [jnp.ones_like(ab_re)]
    pi = [jnp.zeros_like(ab_re)]
    for _ in range(SSM_CHUNK):
        pr.append(pr[-1] * ab_re - pi[-1] * ab_im)
        pi.append(pr[-2] * ab_im + pi[-1] * ab_re)
    pw_re = jnp.stack(pr)
    pw_im = jnp.stack(pi)
    lo_re = pw_re[:SSM_CHUNK, :, :, None]
    lo_im = pw_im[:SSM_CHUNK, :, :, None]
    e_re = lo_re * bb_re - lo_im * bb_im
    e_im = lo_re * bb_im + lo_im * bb_re
    kk = (jnp.einsum("ghp,kgpx->kghx", c_re, e_re, precision=hi)
          - jnp.einsum("ghp,kgpx->kghx", c_im, e_im, precision=hi))
    eye = jnp.eye(OCT_GROUPS, dtype=F32)
    kk5 = kk.reshape(SSM_CHUNK, OCTETS, OCT_GROUPS, n_ch, n_ch)
    kt = jnp.transpose(kk5, (1, 0, 4, 2, 3))
    kblk = eye[None, None, :, None, :, None] * kt[:, :, None, :, :, :]
    kblk = kblk.reshape(OCTETS, SSM_CHUNK, LANES, LANES).astype(BF16)
    e = jnp.stack([e_re, e_im], 0)[:, ::-1]
    e6 = e.reshape(2, SSM_CHUNK, OCTETS, OCT_GROUPS, n_st, n_ch)
    et = jnp.transpose(e6, (2, 1, 5, 0, 3, 4))
    wus = eye[None, None, :, None, None, :, None] * et[:, :, None]
    wus = wus.reshape(OCTETS, OCT_COLS, OCT_STATE).astype(BF16)
    hi_re = pw_re[1:, :, None, :]
    hi_im = pw_im[1:, :, None, :]
    g_re = c_re[None] * hi_re - c_im[None] * hi_im
    g_im = c_re[None] * hi_im + c_im[None] * hi_re
    gg = jnp.stack([g_re, -g_im], 0)
    g6 = gg.reshape(2, SSM_CHUNK, OCTETS, OCT_GROUPS, n_ch, n_st)
    gt = jnp.transpose(g6, (2, 0, 5, 1, 3, 4))
    wso = eye[None, None, :, None, None, :, None] * gt[:, :, None]
    wso = wso.reshape(OCTETS, OCT_STATE, OCT_COLS).astype(BF16)
    a16 = _state_to_octets(pw_re[SSM_CHUNK][None], pw_im[SSM_CHUNK][None])
    return kblk, wus, wso, a16


def _state_to_octets(s_re, s_im):
    b = s_re.shape[0]
    s = jnp.stack([s_re, s_im], 0).reshape(2, b, OCTETS, OCT_GROUPS, SSM_STATE)
    return jnp.transpose(s, (2, 1, 0, 3, 4)).reshape(OCTETS, b, OCT_STATE)


def _state_from_octets(s):
    b = s.shape[1]
    s = s.reshape(OCTETS, b, 2, OCT_GROUPS, SSM_STATE)
    s = jnp.transpose(s, (2, 1, 0, 3, 4)).reshape(2, b, SSM_GROUPS, SSM_STATE)
    return s[0], s[1]


def _ssm_core_kernel(us_ref, s0_ref, kblk_ref, wus_ref, wso_ref, a16_ref, y_ref, sfin_ref,
                     t8_ref, ds_ref, sp_ref, *, nb, n_chunks):
    half = OCT_STATE // 2

    @pl.when(pl.program_id(1) == 0)
    def _():
        t8_ref[...] = jnp.zeros_like(t8_ref)
        for j in range(SSM_CHUNK):
            for i in range(j, SSM_CHUNK):
                t8_ref[j * LANES:(j + 1) * LANES, i * LANES:(i + 1) * LANES] = kblk_ref[0, i - j]

    ub = us_ref[0].astype(BF16)
    ds_ref[...] = _dot(ub, wus_ref[0])
    a_re = a16_ref[0][:, :half]
    a_im = a16_ref[0][:, half:]

    def advance(s, d):
        s_re, s_im = s[:, :half], s[:, half:]
        n_re = a_re * s_re - a_im * s_im + d[:, :half]
        n_im = a_re * s_im + a_im * s_re + d[:, half:]
        return jnp.concatenate([n_re, n_im], axis=1)

    if n_chunks == 1:
        s0 = s0_ref[0, 0]
        sp_ref[...] = s0
        sfin_ref[0, 0] = advance(s0, ds_ref[...])
    else:
        def body(c, carry):
            new = []
            for bl in range(nb):
                row = bl * n_chunks + c
                sp_ref[pl.ds(row, 1), :] = carry[bl]
                new.append(advance(carry[bl], ds_ref[pl.ds(row, 1), :]))
            return tuple(new)

        init = tuple(s0_ref[0, 0, bl:bl + 1, :] for bl in range(nb))
        fin = lax.fori_loop(0, n_chunks, body, init)
        for bl in range(nb):
            sfin_ref[0, 0, bl:bl + 1, :] = fin[bl]

    y_ref[0] = _dot(ub, t8_ref[...]) + _dot(sp_ref[...].astype(BF16), wso_ref[0])


def _ssm_core(us_r, s0, kblk, wus, wso, a16, n_batch, n_chunks, nb):
    nc = us_r.shape[1]
    rows = nb * n_chunks
    assert n_batch % nb == 0 and nc == n_batch * n_chunks
    nr = n_batch // nb
    s0 = s0.reshape(OCTETS, nr, nb, OCT_STATE)
    y, sfin = pl.pallas_call(
        functools.partial(_ssm_core_kernel, nb=nb, n_chunks=n_chunks),
        grid=(OCTETS, nr),
        in_specs=[pl.BlockSpec((1, rows, OCT_COLS), lambda m, r: (m, r, 0)),
                  pl.BlockSpec((1, 1, nb, OCT_STATE), lambda m, r: (m, r, 0, 0)),
                  pl.BlockSpec((1, SSM_CHUNK, LANES, LANES), lambda m, r: (m, 0, 0, 0)),
                  pl.BlockSpec((1, OCT_COLS, OCT_STATE), lambda m, r: (m, 0, 0)),
                  pl.BlockSpec((1, OCT_STATE, OCT_COLS), lambda m, r: (m, 0, 0)),
                  pl.BlockSpec((1, 1, OCT_STATE), lambda m, r: (m, 0, 0))],
        out_specs=[pl.BlockSpec((1, rows, OCT_COLS), lambda m, r: (m, r, 0)),
                   pl.BlockSpec((1, 1, nb, OCT_STATE), lambda m, r: (m, r, 0, 0))],
        out_shape=[jax.ShapeDtypeStruct((OCTETS, nc, OCT_COLS), F32),
                   jax.ShapeDtypeStruct((OCTETS, nr, nb, OCT_STATE), F32)],
        scratch_shapes=[pltpu.VMEM((OCT_COLS, OCT_COLS), BF16),
                        pltpu.VMEM((rows, OCT_STATE), F32),
                        pltpu.VMEM((rows, OCT_STATE), F32)],
        compiler_params=_params(("arbitrary", "arbitrary"), 56),
        name="ssm_core",
    )(us_r, s0, kblk, wus, wso, a16)
    return y, sfin.reshape(OCTETS, n_batch, OCT_STATE)


def _gelu_tanh(x):
    c = math.sqrt(2.0 / math.pi)
    return x * (0.5 * (1.0 + jnp.tanh(c * (x + 0.044715 * (x * x * x)))))


def _ssm_glu_kernel(y_ref, us_ref, d_ref, w_ref, b_ref, o_ref):
    y = jnp.concatenate([y_ref[m] for m in range(OCTETS)], axis=1)
    u = jnp.concatenate([us_ref[m] for m in range(OCTETS)], axis=1)
    yg = _gelu_tanh(y + d_ref[...] * u)
    z = _dot(yg.astype(BF16), w_ref[...]) + b_ref[...]
    o_ref[...] = (yg * jax.nn.sigmoid(z)).astype(o_ref.dtype)


def _ssm_glu(y_r, us_r, d, w_glu, b_glu, tc):
    nc = y_r.shape[1]
    oct_spec = pl.BlockSpec((OCTETS, tc, LANES), lambda c, i: (0, c, i))
    vec = pl.BlockSpec((1, SSM_WIDTH), lambda c, i: (0, 0))
    return pl.pallas_call(
        _ssm_glu_kernel,
        grid=(nc // tc, SSM_CHUNK),
        in_specs=[oct_spec, oct_spec, vec,
                  pl.BlockSpec((SSM_WIDTH, SSM_WIDTH), lambda c, i: (0, 0)), vec],
        out_specs=pl.BlockSpec((tc, SSM_WIDTH), lambda c, i: (c, i)),
        out_shape=jax.ShapeDtypeStruct((nc, SSM_CHUNK * SSM_WIDTH), BF16),
        compiler_params=_params(("parallel", "arbitrary"), 48),
        name="ssm_glu",
    )(y_r, us_r, d, w_glu, b_glu)


def _head_masks():
    lane = lax.broadcasted_iota(jnp.int32, (1, LANES), 1)
    return (lane < ATT_HEAD_DIM, lane >= ATT_HEAD_DIM)


def _pair_attention(q2, kw, vw, biases, masks):
    acc = None
    for hh in range(2):
        qh = jnp.where(masks[hh], q2, jnp.zeros_like(q2))
        sc = _dot_nt(qh, kw) * (ATT_HEAD_DIM ** -0.5) + biases[hh]
        mx = jnp.max(sc, axis=1, keepdims=True)
        p = jnp.exp(sc - mx)
        l = jnp.sum(p, axis=1, keepdims=True)
        vh = jnp.where(masks[hh], vw, jnp.zeros_like(vw))
        o = _dot(p.astype(BF16), vh) * (1.0 / l)
        acc = o if acc is None else acc + o
    return acc


def _band_attn_kernel(q_ref, kp_ref, kc_ref, vp_ref, vc_ref, bias_ref, o_ref, kw_ref, vw_ref):
    kw_ref[0:Q_TILE] = kp_ref[...]
    kw_ref[Q_TILE:2 * Q_TILE] = kc_ref[...]
    vw_ref[0:Q_TILE] = vp_ref[...]
    vw_ref[Q_TILE:2 * Q_TILE] = vc_ref[...]
    masks = _head_masks()
    n_invalid = jnp.where(pl.program_id(1) == 0, Q_TILE, 0)
    col = lax.broadcasted_iota(jnp.int32, (1, K_WIN), 1)

    def sub(s, carry):
        r0 = pl.multiple_of(s * Q_SUB, Q_SUB)
        extra = jnp.where(col + r0 < n_invalid, NEG_INF, 0.0)
        for hp in range(ATT_HEADS // 2):
            cs = slice(hp * LANES, (hp + 1) * LANES)
            biases = [bias_ref[2 * hp + hh] + extra for hh in range(2)]
            o = _pair_attention(q_ref[pl.ds(r0, Q_SUB), cs], kw_ref[pl.ds(r0, K_WIN), cs],
                                vw_ref[pl.ds(r0, K_WIN), cs], biases, masks)
            o_ref[pl.ds(r0, Q_SUB), cs] = o.astype(o_ref.dtype)
        return carry

    lax.fori_loop(0, Q_TILE // Q_SUB, sub, 0)


def _band_bias_prompt(rel_bias):
    qi = np.arange(Q_SUB)[:, None]
    kj = np.arange(K_WIN)[None, :]
    rel = np.clip(qi - kj + BAND_PAST, -REL_CLIP, REL_CLIP) + REL_CLIP
    qc = qi // CHUNK
    kc = kj // CHUNK
    ok = (kc >= qc) & (kc <= qc + BAND_PAST_CHUNKS)
    return jnp.where(ok[None], rel_bias[:, rel], NEG_INF)


def _band_attention_prompt(qkvm, rel_bias, n_batch, seq):
    assert seq % Q_TILE == 0
    nt = seq // Q_TILE
    bias = _band_bias_prompt(rel_bias)
    blk = (Q_TILE, ATT_WIDTH)
    cur = lambda col: pl.BlockSpec(blk, lambda b, t: (b * nt + t, col))
    prev = lambda col: pl.BlockSpec(blk, lambda b, t: (b * nt + jnp.maximum(t - 1, 0), col))
    return pl.pallas_call(
        _band_attn_kernel,
        grid=(n_batch, nt),
        in_specs=[cur(0), prev(1), cur(1), prev(2), cur(2),
                  pl.BlockSpec((ATT_HEADS, Q_SUB, K_WIN), lambda b, t: (0, 0, 0))],
        out_specs=pl.BlockSpec(blk, lambda b, t: (b * nt + t, 0)),
        out_shape=jax.ShapeDtypeStruct((n_batch * seq, ATT_WIDTH), BF16),
        scratch_shapes=[pltpu.VMEM((2 * Q_TILE, ATT_WIDTH), BF16),
                        pltpu.VMEM((2 * Q_TILE, ATT_WIDTH), BF16)],
        compiler_params=_params(("parallel", "arbitrary"), 48),
        name="band_attn",
    )(qkvm, qkvm, qkvm, qkvm, qkvm, bias)


def _band_attn_sample_kernel(q_ref, kn_ref, vn_ref, ck_ref, cv_ref, bias_ref, o_ref, kw_ref, vw_ref,
                             *, n_cache, n_new):
    kw_ref[...] = jnp.zeros_like(kw_ref)
    vw_ref[...] = jnp.zeros_like(vw_ref)
    kw_ref[0:n_cache] = ck_ref[0].astype(BF16)
    vw_ref[0:n_cache] = cv_ref[0].astype(BF16)
    kw_ref[n_cache:n_cache + n_new] = kn_ref[...]
    vw_ref[n_cache:n_cache + n_new] = vn_ref[...]
    masks = _head_masks()
    for hp in range(ATT_HEADS // 2):
        cs = slice(hp * LANES, (hp + 1) * LANES)
        biases = [bias_ref[2 * hp + hh] for hh in range(2)]
        o = _pair_attention(q_ref[:, cs], kw_ref[:, cs], vw_ref[:, cs], biases, masks)
        o_ref[:, cs] = o.astype(o_ref.dtype)


def _band_attention_sample(qkvm, rel_bias, cache_k, cache_v, n_batch, n_new):
    n_cache = cache_k.shape[1]
    n_keys = -(-(n_cache + n_new) // LANES) * LANES
    t = np.arange(n_new)[:, None]
    j = np.arange(n_keys)[None, :]
    q_pos = PAST_LEN + t
    k_pos = PAST_LEN - n_cache + j
    rel = np.clip(q_pos - k_pos, -REL_CLIP, REL_CLIP) + REL_CLIP
    ok = ((j < n_cache + n_new) & (k_pos >= 0) & (k_pos // CHUNK <= q_pos // CHUNK)
          & (k_pos // CHUNK >= q_pos // CHUNK - BAND_PAST_CHUNKS))
    bias = jnp.where(ok[None], rel_bias[:, rel], NEG_INF)
    new = lambda col: pl.BlockSpec((n_new, ATT_WIDTH), lambda b: (b, col))
    cache = pl.BlockSpec((1, n_cache, ATT_WIDTH), lambda b: (b, 0, 0))
    return pl.pallas_call(
        functools.partial(_band_attn_sample_kernel, n_cache=n_cache, n_new=n_new),
        grid=(n_batch,),
        in_specs=[new(0), new(1), new(2), cache, cache,
                  pl.BlockSpec((ATT_HEADS, n_new, n_keys), lambda b: (0, 0, 0))],
        out_specs=pl.BlockSpec((n_new, ATT_WIDTH), lambda b: (b, 0)),
        out_shape=jax.ShapeDtypeStruct((n_batch * n_new, ATT_WIDTH), BF16),
        scratch_shapes=[pltpu.VMEM((n_keys, ATT_WIDTH), BF16),
                        pltpu.VMEM((n_keys, ATT_WIDTH), BF16)],
        compiler_params=_params(("parallel",), 48),
        name="band_attn_sample",
    )(qkvm, qkvm, qkvm, cache_k, cache_v, bias)


def _mem_attn_kernel(q_ref, k_ref, v_ref, o_ref):
    k = k_ref[0].astype(BF16)
    v = v_ref[0].astype(BF16)
    for h in range(MEM_HEADS):
        cs = slice(h * MEM_HEAD_DIM, (h + 1) * MEM_HEAD_DIM)
        sc = _dot_nt(q_ref[:, cs], k[:, cs]) * (MEM_HEAD_DIM ** -0.5)
        mx = jnp.max(sc, axis=1, keepdims=True)
        p = jnp.exp(sc - mx)
        l = jnp.sum(p, axis=1, keepdims=True)
        o = _dot(p.astype(BF16), v[:, cs]) * (1.0 / l)
        o_ref[:, cs] = o.astype(o_ref.dtype)


def _memory_attention(qkvm, mem_k, mem_v, n_batch, seq, tq):
    nt = seq // tq
    mem = pl.BlockSpec((1, N_MEM, MEM_WIDTH), lambda b, t: (b, 0, 0))
    return pl.pallas_call(
        _mem_attn_kernel,
        grid=(n_batch, nt),
        in_specs=[pl.BlockSpec((tq, MEM_WIDTH), lambda b, t: (b * nt + t, 3)), mem, mem],
        out_specs=pl.BlockSpec((tq, MEM_WIDTH), lambda b, t: (b * nt + t, 0)),
        out_shape=jax.ShapeDtypeStruct((n_batch * seq, MEM_WIDTH), BF16),
        compiler_params=_params(("parallel", "arbitrary"), 48),
        name="mem_attn",
    )(qkvm, mem_k, mem_v)


def _merge_kernel(u_ref, os_ref, oa_ref, om_ref, h_ref, wgs_ref, wga_ref, wgm_ref,
                  wbs_ref, wba_ref, wbm_ref, wout_ref, gpost_ref, out_ref, *, nn):
    n = pl.program_id(1)

    @pl.when(n == 0)
    def _():
        out_ref[...] = jnp.zeros_like(out_ref)

    u = u_ref[...]

    def branch(o_ref, wg_ref, wb_ref):
        return jax.nn.sigmoid(_dot(u, wg_ref[...])) * _dot(o_ref[...], wb_ref[...])

    merged = (branch(os_ref, wgs_ref, wbs_ref) + branch(oa_ref, wga_ref, wba_ref)
              + branch(om_ref, wgm_ref, wbm_ref))
    out_ref[...] += _dot(merged.astype(BF16), wout_ref[...])

    @pl.when(n == nn - 1)
    def _():
        out_ref[...] = h_ref[...] + _rms(out_ref[...], gpost_ref[...])


def _merge(u, o_s, o_a, o_m, h, w_in, wb_s, wb_a, wb_m, w_out, g_post, tm, tn):
    m, d = h.shape
    nn = d // tn
    gate = lambda b: pl.BlockSpec((d, tn), lambda i, n: (0, (COL_GATE + b * d) // tn + n))
    wb = pl.BlockSpec((SSM_WIDTH, tn), lambda i, n: (0, n))
    ob = pl.BlockSpec((tm, SSM_WIDTH), lambda i, n: (i, 0))
    row = pl.BlockSpec((tm, d), lambda i, n: (i, 0))
    return pl.pallas_call(
        functools.partial(_merge_kernel, nn=nn),
        grid=(m // tm, nn),
        in_specs=[row, ob, ob, ob, row, gate(0), gate(1), gate(2), wb, wb, wb,
                  pl.BlockSpec((tn, d), lambda i, n: (n, 0)),
                  pl.BlockSpec((1, d), lambda i, n: (0, 0))],
        out_specs=row,
        out_shape=jax.ShapeDtypeStruct((m, d), F32),
        compiler_params=_params(("parallel", "arbitrary"), 56),
        name="merge",
    )(u, o_s, o_a, o_m, h, w_in, w_in, w_in, wb_s, wb_a, wb_m, w_out, g_post)


def _layer(x, w, *, n_batch, seq, s0, mem_k, mem_v, cache_k, cache_v, tm, keep):
    n_tok = n_batch * seq
    n_chunks = seq // SSM_CHUNK
    h1, u = _ffn(x, w["ffn1_norm_pre"], w["ffn1_norm_post"], w["ffn1_w_gate"], w["ffn1_w_up"],
                 w["ffn1_w_down"], w["mix_norm_pre"], tm, 512)

    qkvm = _matmul(u, w["w_in"], COL_Q, 3 * ATT_WIDTH + MEM_WIDTH, BF16, tm, 1024, name="qkvm")
    if keep == seq:
        kv = _matmul(u, w["w_in"], COL_K, 2 * ATT_WIDTH, F32, tm, 1024, name="kv_tail")
    else:
        n_keep_blocks = seq // keep
        kv = _matmul(u, w["w_in"], COL_K, 2 * ATT_WIDTH, F32, keep, 1024, m_out=n_batch * keep,
                     row_map=lambda i: i * n_keep_blocks + (n_keep_blocks - 1), name="kv_tail")

    nc = n_batch * n_chunks
    tc = min(512, nc)
    us_r = _ssm_inproj(u.reshape(nc, SSM_CHUNK * D_MODEL), w["w_in"], tc)
    y_r, s_fin = _ssm_core(us_r, s0, w["ssm_kblk"], w["ssm_wus"], w["ssm_wso"], w["ssm_a16"],
                           n_batch, n_chunks, 1 if n_chunks > 1 else n_batch)
    o_s = _ssm_glu(y_r, us_r, w["ssm_d"], w["ssm_w_glu"], w["ssm_b_glu"], tc).reshape(n_tok, SSM_WIDTH)

    if cache_k is None:
        o_a = _band_attention_prompt(qkvm, w["att_rel_bias"], n_batch, seq)
    else:
        o_a = _band_attention_sample(qkvm, w["att_rel_bias"], cache_k, cache_v, n_batch, seq)
    o_m = _memory_attention(qkvm, mem_k, mem_v, n_batch, seq, min(512, seq))

    h2 = _merge(u, o_s, o_a, o_m, h1, w["w_in"], w["w_branch_ssm"], w["w_branch_att"],
                w["w_branch_mem"], w["w_out"], w["mix_norm_post"], tm, 256)
    y = _ffn(h2, w["ffn2_norm_pre"], w["ffn2_norm_post"], w["ffn2_w_gate"], w["ffn2_w_up"],
             w["ffn2_w_down"], None, tm, 512)
    return y, kv, s_fin


def kernel(x_prompt, x_sample, mem_prompt, cache_att_k, cache_att_v, cache_mem_k, cache_mem_v, state_ssm_re, state_ssm_im, ffn1_norm_pre, ffn1_norm_post, ffn1_w_gate, ffn1_w_up, ffn1_w_down, mix_norm_pre, mix_norm_post, w_in, ssm_a_re, ssm_a_im, ssm_log_dt, ssm_b_re, ssm_b_im, ssm_c_re, ssm_c_im, ssm_d, ssm_w_glu, ssm_b_glu, att_rel_bias, mem_norm, w_mem_k, w_mem_v, w_branch_ssm, w_branch_att, w_branch_mem, w_out, ffn2_norm_pre, ffn2_norm_post, ffn2_w_gate, ffn2_w_up, ffn2_w_down):
    n_bp, t_p, d = x_prompt.shape
    n_bs, t_s, _ = x_sample.shape
    depth = ffn1_norm_pre.shape[0]
    assert depth == 1 and d == D_MODEL
    keep = min(BAND_PAST, t_p)
    l = 0

    vec = lambda a: a[l].reshape(1, -1).astype(F32)
    mat = lambda a: a[l].astype(BF16)
    kblk, wus, wso, a16 = _ssm_operators(ssm_a_re[l], ssm_a_im[l], ssm_log_dt[l], ssm_b_re[l],
                                         ssm_b_im[l], ssm_c_re[l], ssm_c_im[l])
    w = {
        "ffn1_norm_pre": vec(ffn1_norm_pre), "ffn1_norm_post": vec(ffn1_norm_post),
        "ffn1_w_gate": mat(ffn1_w_gate), "ffn1_w_up": mat(ffn1_w_up), "ffn1_w_down": mat(ffn1_w_down),
        "mix_norm_pre": vec(mix_norm_pre), "mix_norm_post": vec(mix_norm_post), "w_in": mat(w_in),
        "ssm_kblk": kblk, "ssm_wus": wus, "ssm_wso": wso, "ssm_a16": a16,
        "ssm_d": vec(ssm_d), "ssm_w_glu": mat(ssm_w_glu), "ssm_b_glu": vec(ssm_b_glu),
        "att_rel_bias": att_rel_bias[l].astype(F32),
        "w_branch_ssm": mat(w_branch_ssm), "w_branch_att": mat(w_branch_att),
        "w_branch_mem": mat(w_branch_mem), "w_out": mat(w_out),
        "ffn2_norm_pre": vec(ffn2_norm_pre), "ffn2_norm_post": vec(ffn2_norm_post),
        "ffn2_w_gate": mat(ffn2_w_gate), "ffn2_w_up": mat(ffn2_w_up), "ffn2_w_down": mat(ffn2_w_down),
    }

    w_mem = jnp.concatenate([w_mem_k[l], w_mem_v[l]], axis=1).astype(BF16)
    mkv = _norm_matmul(mem_prompt.reshape(n_bp * N_MEM, d), vec(mem_norm), w_mem, 512, 1024)
    mk_p = mkv[:, :MEM_WIDTH].reshape(n_bp, N_MEM, MEM_WIDTH)
    mv_p = mkv[:, MEM_WIDTH:].reshape(n_bp, N_MEM, MEM_WIDTH)
    zero_state = jnp.zeros((OCTETS, n_bp, OCT_STATE), F32)
    y_p, kv_p, sfin_p = _layer(x_prompt.reshape(n_bp * t_p, d), w, n_batch=n_bp, seq=t_p, s0=zero_state,
                               mem_k=mk_p, mem_v=mv_p, cache_k=None, cache_v=None, tm=512, keep=keep)
    sre_p, sim_p = _state_from_octets(sfin_p)

    n_cache = cache_att_k.shape[2]
    s0_s = _state_to_octets(state_ssm_re[l].astype(F32), state_ssm_im[l].astype(F32))
    y_s, kv_s, sfin_s = _layer(x_sample.reshape(n_bs * t_s, d), w, n_batch=n_bs, seq=t_s, s0=s0_s,
                               mem_k=cache_mem_k[l].reshape(n_bs, N_MEM, MEM_WIDTH),
                               mem_v=cache_mem_v[l].reshape(n_bs, N_MEM, MEM_WIDTH),
                               cache_k=cache_att_k[l].reshape(n_bs, n_cache, ATT_WIDTH),
                               cache_v=cache_att_v[l].reshape(n_bs, n_cache, ATT_WIDTH),
                               tm=n_bs * t_s, keep=t_s)
    sre_s, sim_s = _state_from_octets(sfin_s)

    heads = lambda a, nb, t: a.reshape(1, nb, t, ATT_HEADS, ATT_HEAD_DIM)
    memh = lambda a: a.reshape(1, n_bp, N_MEM, MEM_HEADS, MEM_HEAD_DIM)
    return (y_p.reshape(n_bp, t_p, d), y_s.reshape(n_bs, t_s, d),
            heads(kv_p[:, :ATT_WIDTH], n_bp, keep), heads(kv_p[:, ATT_WIDTH:], n_bp, keep),
            memh(mk_p), memh(mv_p), sre_p[None], sim_p[None],
            heads(kv_s[:, :ATT_WIDTH], n_bs, t_s), heads(kv_s[:, ATT_WIDTH:], n_bs, t_s),
            sre_s[None], sim_s[None])
```

```python
import functools
import math

import numpy as np
import jax
import jax.numpy as jnp
from jax import lax
from jax.experimental import pallas as pl
from jax.experimental.pallas import tpu as pltpu

F32 = jnp.float32
BF16 = jnp.bfloat16

D_MODEL = 2048
CHUNK = 64
BAND_PAST_CHUNKS = 8
BAND_PAST = BAND_PAST_CHUNKS * CHUNK
ATT_HEADS = 16
ATT_HEAD_DIM = 64
ATT_WIDTH = ATT_HEADS * ATT_HEAD_DIM
REL_CLIP = 128
SSM_GROUP = 16
SSM_WIDTH = 1024
SSM_GROUPS = SSM_WIDTH // SSM_GROUP
SSM_STATE = 64
N_MEM = 256
MEM_HEADS = 4
MEM_HEAD_DIM = 256
MEM_WIDTH = MEM_HEADS * MEM_HEAD_DIM
N_BRANCH = 3
EPS = 1e-6
NEG_INF = -1e30
PAST_LEN = 4096
LOG2E = math.log2(math.e)

COL_Q = SSM_WIDTH
COL_GATE = SSM_WIDTH + 3 * ATT_WIDTH + MEM_WIDTH

SSM_CHUNK = 16
OCTETS = 8
OCT_GROUPS = SSM_GROUPS // OCTETS
LANES = 128
MXU_DIM = 256
OCT_COLS = SSM_CHUNK * LANES
OCT_HALF = OCT_GROUPS * SSM_STATE
OCT_STATE = 2 * OCT_HALF

Q_TILE = 512
Q_SUB = 128
K_WIN = Q_SUB + BAND_PAST

MIB = 1024 * 1024


def _params(sem, vmem_mib):
    return pltpu.CompilerParams(dimension_semantics=sem, vmem_limit_bytes=vmem_mib * MIB)


def _dot(a, b):
    return jnp.dot(a, b, preferred_element_type=F32)


def _dot_nt(a, b, precision=None):
    return lax.dot_general(a, b, (((1,), (1,)), ((), ())), precision=precision,
                           preferred_element_type=F32)


def _rms(xf, g):
    y = xf * lax.rsqrt(jnp.mean(xf * xf, axis=-1, keepdims=True) + EPS)
    return y * g


def _ffn_kernel(h_ref, gpre_ref, gpost_ref, wg_ref, wu_ref, wd_ref, *rest, nf, emit_next):
    if emit_next:
        gnext_ref, out_ref, nxt_ref, xn_ref = rest
    else:
        out_ref, xn_ref = rest
    f = pl.program_id(1)

    @pl.when(f == 0)
    def _():
        xn_ref[...] = _rms(h_ref[...], gpre_ref[...]).astype(BF16)
        out_ref[...] = jnp.zeros_like(out_ref)

    xn = xn_ref[...]
    g = _dot(xn, wg_ref[...])
    u = _dot(xn, wu_ref[...])
    a = (g * jax.nn.sigmoid(g)) * u
    out_ref[...] += _dot(a.astype(BF16), wd_ref[...])

    @pl.when(f == nf - 1)
    def _():
        hn = h_ref[...] + 0.5 * _rms(out_ref[...], gpost_ref[...])
        out_ref[...] = hn
        if emit_next:
            nxt_ref[...] = _rms(hn, gnext_ref[...]).astype(BF16)


def _ffn(h, g_pre, g_post, wg, wu, wd, g_next, tm, tf):
    emit_next = g_next is not None
    m, d = h.shape
    f_dim = wg.shape[1]
    nf = f_dim // tf
    assert m % tm == 0 and f_dim % tf == 0
    vec = pl.BlockSpec((1, d), lambda i, f: (0, 0))
    row = pl.BlockSpec((tm, d), lambda i, f: (i, 0))
    in_specs = [row, vec, vec,
                pl.BlockSpec((d, tf), lambda i, f: (0, f)),
                pl.BlockSpec((d, tf), lambda i, f: (0, f)),
                pl.BlockSpec((tf, d), lambda i, f: (f, 0))]
    args = [h, g_pre, g_post, wg, wu, wd]
    out_shape = [jax.ShapeDtypeStruct((m, d), F32)]
    out_specs = [row]
    if emit_next:
        in_specs.append(vec)
        args.append(g_next)
        out_shape.append(jax.ShapeDtypeStruct((m, d), BF16))
        out_specs.append(row)
    res = pl.pallas_call(
        functools.partial(_ffn_kernel, nf=nf, emit_next=emit_next),
        grid=(m // tm, nf),
        in_specs=in_specs,
        out_specs=out_specs,
        out_shape=out_shape,
        scratch_shapes=[pltpu.VMEM((tm, d), BF16)],
        compiler_params=_params(("parallel", "arbitrary"), 56),
        name="ffn",
    )(*args)
    return res if emit_next else res[0]


def _mm_kernel(x_ref, w_ref, o_ref, *, tn):
    x = x_ref[...]
    for j in range(o_ref.shape[1] // tn):
        cs = slice(j * tn, (j + 1) * tn)
        o_ref[:, cs] = _dot(x, w_ref[:, cs]).astype(o_ref.dtype)


def _matmul(x, w, out_dtype, tm, m_out=None, row_map=None, name="proj"):
    m, k = x.shape
    n = w.shape[1]
    m_out = m if m_out is None else m_out
    row_map = (lambda i: i) if row_map is None else row_map
    tn = min(n, 1024)
    assert m_out % tm == 0 and n % tn == 0
    return pl.pallas_call(
        functools.partial(_mm_kernel, tn=tn),
        grid=(m_out // tm,),
        in_specs=[pl.BlockSpec((tm, k), lambda i: (row_map(i), 0)),
                  pl.BlockSpec((k, n), lambda i: (0, 0))],
        out_specs=pl.BlockSpec((tm, n), lambda i: (i, 0)),
        out_shape=jax.ShapeDtypeStruct((m_out, n), out_dtype),
        compiler_params=_params(("parallel",), 56),
        name=name,
    )(x, w)


def _norm_mm_kernel(x_ref, g_ref, w_ref, o_ref):
    xn = _rms(x_ref[...], g_ref[...]).astype(BF16)
    o_ref[...] = _dot(xn, w_ref[...]).astype(o_ref.dtype)


def _norm_matmul(x, g, w, tm, tn):
    m, k = x.shape
    n = w.shape[1]
    tm = min(tm, m)
    assert m % tm == 0 and n % tn == 0
    return pl.pallas_call(
        _norm_mm_kernel,
        grid=(m // tm, n // tn),
        in_specs=[pl.BlockSpec((tm, k), lambda i, j: (i, 0)),
                  pl.BlockSpec((1, k), lambda i, j: (0, 0)),
                  pl.BlockSpec((k, tn), lambda i, j: (0, j))],
        out_specs=pl.BlockSpec((tm, tn), lambda i, j: (i, j)),
        out_shape=jax.ShapeDtypeStruct((m, n), F32),
        compiler_params=_params(("parallel", "arbitrary"), 48),
        name="memory_kv",
    )(x, g, w)


def _ssm_inproj_kernel(x_ref, w_ref, o_ref, nat_ref):
    r = _dot(x_ref[...], w_ref[...])
    tc = nat_ref.shape[1] // SSM_CHUNK
    for m in range(OCTETS):
        nat_ref[m] = r[:, m * LANES:(m + 1) * LANES]
    for j in range(SSM_CHUNK):
        for m in range(OCTETS):
            o_ref[m, :, j * LANES:(j + 1) * LANES] = nat_ref[m, pl.ds(j, tc, stride=SSM_CHUNK), :]


def _ssm_inproj(u, w_ssm, tm):
    n_tok = u.shape[0]
    assert n_tok % tm == 0 and tm % SSM_CHUNK == 0
    tc = tm // SSM_CHUNK
    return pl.pallas_call(
        _ssm_inproj_kernel,
        grid=(n_tok // tm,),
        in_specs=[pl.BlockSpec((tm, D_MODEL), lambda i: (i, 0)),
                  pl.BlockSpec((D_MODEL, SSM_WIDTH), lambda i: (0, 0))],
        out_specs=pl.BlockSpec((OCTETS, tc, OCT_COLS), lambda i: (0, i, 0)),
        out_shape=jax.ShapeDtypeStruct((OCTETS, n_tok // SSM_CHUNK, OCT_COLS), F32),
        scratch_shapes=[pltpu.VMEM((OCTETS, tm, LANES), F32)],
        compiler_params=_params(("parallel",), 48),
        name="ssm_inproj",
    )(u, w_ssm)


def _ssm_params(a_re, a_im, log_dt, b_re, b_im, c_re, c_im):
    dt = jnp.exp(log_dt)[:, None]
    mag = jnp.exp(a_re * dt)
    ab_re = mag * jnp.cos(a_im * dt)
    ab_im = mag * jnp.sin(a_im * dt)
    den = a_re * a_re + a_im * a_im
    n_re = ab_re - 1.0
    n_im = ab_im
    k_re = (n_re * a_re + n_im * a_im) / den
    k_im = (n_im * a_re - n_re * a_im) / den
    bb_re = k_re[..., None] * b_re - k_im[..., None] * b_im
    bb_im = k_re[..., None] * b_im + k_im[..., None] * b_re
    pr = [jnp.ones_like(ab_re)]
    pi = [jnp.zeros_like(ab_re)]
    for _ in range(SSM_CHUNK):
        pr.append(pr[-1] * ab_re - pi[-1] * ab_im)
        pi.append(pr[-2] * ab_im + pi[-1] * ab_re)
    n_pw = SSM_CHUNK + 1
    pw = jnp.concatenate([jnp.stack(pr).reshape(n_pw, OCTETS, OCT_HALF),
                          jnp.stack(pi).reshape(n_pw, OCTETS, OCT_HALF)], axis=2)
    pw = jnp.transpose(pw, (1, 0, 2))
    eye = jnp.eye(OCT_GROUPS, dtype=F32)

    def expand(x):
        x4 = x.reshape(OCTETS, OCT_GROUPS, x.shape[1], SSM_STATE)
        out = eye[None, :, None, :, None] * x4[:, :, :, None, :]
        return out.reshape(OCTETS, OCT_GROUPS * x.shape[1], OCT_HALF)

    bd = jnp.concatenate([expand(jnp.transpose(bb_re, (0, 2, 1))),
                          expand(jnp.transpose(bb_im, (0, 2, 1)))], axis=2)
    cd = jnp.concatenate([expand(c_re), expand(c_im)], axis=2)
    return bd, cd, pw


def _state_to_octets(s_re, s_im):
    b = s_re.shape[0]
    s = jnp.stack([s_re, s_im], 0).reshape(2, b, OCTETS, OCT_GROUPS, SSM_STATE)
    return jnp.transpose(s, (2, 1, 0, 3, 4)).reshape(OCTETS, b, OCT_STATE)


def _state_from_octets(s):
    b = s.shape[1]
    s = s.reshape(OCTETS, b, 2, OCT_GROUPS, SSM_STATE)
    s = jnp.transpose(s, (2, 1, 0, 3, 4)).reshape(2, b, SSM_GROUPS, SSM_STATE)
    return s[0], s[1]


def _ssm_build_operators(bd_ref, cd_ref, pw_ref, t8_ref, wus_ref, wso_ref):
    bd_re, bd_im = bd_ref[0, :, :OCT_HALF], bd_ref[0, :, OCT_HALF:]
    cd_re, cd_im = cd_ref[0, :, :OCT_HALF], cd_ref[0, :, OCT_HALF:]
    cdm = jnp.concatenate([cd_re, -cd_im], axis=1)
    blk = lambda i: slice(i * LANES, (i + 1) * LANES)
    for a in range(SSM_CHUNK // 2):
        t8_ref[blk(2 * a + 1), blk(2 * a)] = jnp.zeros((LANES, LANES), BF16)
    for k in range(SSM_CHUNK + 1):
        p_re = pw_ref[0, k:k + 1, :OCT_HALF]
        p_im = pw_ref[0, k:k + 1, OCT_HALF:]
        if k < SSM_CHUNK:
            e = jnp.concatenate([bd_re * p_re - bd_im * p_im, bd_re * p_im + bd_im * p_re], axis=1)
            wus_ref[blk(SSM_CHUNK - 1 - k), :] = e.astype(BF16)
            lag = _dot_nt(e, cdm, precision=lax.Precision.HIGHEST).astype(BF16)
            for j in range(SSM_CHUNK - k):
                t8_ref[blk(j), blk(j + k)] = lag
        if k >= 1:
            g = jnp.concatenate([cd_re * p_re - cd_im * p_im, -(cd_re * p_im + cd_im * p_re)], axis=1)
            wso_ref[blk(k - 1), :] = g.astype(BF16)


def _ssm_core_kernel(us_ref, s0_ref, bd_ref, cd_ref, pw_ref, y_ref, sfin_ref,
                     t8_ref, wus_ref, wso_ref, ds_ref, sp_ref, *, nb, n_chunks):
    @pl.when(pl.program_id(1) == 0)
    def _():
        _ssm_build_operators(bd_ref, cd_ref, pw_ref, t8_ref, wus_ref, wso_ref)

    ub = us_ref[0].astype(BF16)
    ds_ref[...] = _dot(ub, wus_ref[...])
    a_re = pw_ref[0, SSM_CHUNK:SSM_CHUNK + 1, :OCT_HALF]
    a_im = pw_ref[0, SSM_CHUNK:SSM_CHUNK + 1, OCT_HALF:]

    def advance(s, d):
        s_re, s_im = s[:, :OCT_HALF], s[:, OCT_HALF:]
        n_re = a_re * s_re - a_im * s_im + d[:, :OCT_HALF]
        n_im = a_re * s_im + a_im * s_re + d[:, OCT_HALF:]
        return jnp.concatenate([n_re, n_im], axis=1)

    if n_chunks == 1:
        s0 = s0_ref[0, 0]
        sp_ref[...] = s0
        sfin_ref[0, 0] = advance(s0, ds_ref[...])
    else:
        def body(c, carry):
            new = []
            for bl in range(nb):
                row = bl * n_chunks + c
                sp_ref[pl.ds(row, 1), :] = carry[bl]
                new.append(advance(carry[bl], ds_ref[pl.ds(row, 1), :]))
            return tuple(new)

        init = tuple(s0_ref[0, 0, bl:bl + 1, :] for bl in range(nb))
        fin = lax.fori_loop(0, n_chunks, body, init, unroll=2)
        for bl in range(nb):
            sfin_ref[0, 0, bl:bl + 1, :] = fin[bl]

    spb = sp_ref[...].astype(BF16)
    for nt in range(OCT_COLS // MXU_DIM):
        k_hi = (nt + 1) * MXU_DIM
        cs = slice(nt * MXU_DIM, k_hi)
        y_ref[0, :, cs] = _dot(ub[:, :k_hi], t8_ref[:k_hi, cs]) + _dot_nt(spb, wso_ref[cs, :])


def _ssm_core(us_r, s0, bd, cd, pw, n_batch, n_chunks, nb):
    nc = us_r.shape[1]
    rows = nb * n_chunks
    assert n_batch % nb == 0 and nc == n_batch * n_chunks
    nr = n_batch // nb
    s0 = s0.reshape(OCTETS, nr, nb, OCT_STATE)
    par = lambda rows_: pl.BlockSpec((1, rows_, OCT_STATE), lambda m, r: (m, 0, 0))
    y, sfin = pl.pallas_call(
        functools.partial(_ssm_core_kernel, nb=nb, n_chunks=n_chunks),
        grid=(OCTETS, nr),
        in_specs=[pl.BlockSpec((1, rows, OCT_COLS), lambda m, r: (m, r, 0)),
                  pl.BlockSpec((1, 1, nb, OCT_STATE), lambda m, r: (m, r, 0, 0)),
                  par(LANES), par(LANES), par(SSM_CHUNK + 1)],
        out_specs=[pl.BlockSpec((1, rows, OCT_COLS), lambda m, r: (m, r, 0)),
                   pl.BlockSpec((1, 1, nb, OCT_STATE), lambda m, r: (m, r, 0, 0))],
        out_shape=[jax.ShapeDtypeStruct((OCTETS, nc, OCT_COLS), F32),
                   jax.ShapeDtypeStruct((OCTETS, nr, nb, OCT_STATE), F32)],
        scratch_shapes=[pltpu.VMEM((OCT_COLS, OCT_COLS), BF16),
                        pltpu.VMEM((OCT_COLS, OCT_STATE), BF16),
                        pltpu.VMEM((OCT_COLS, OCT_STATE), BF16),
                        pltpu.VMEM((rows, OCT_STATE), F32),
                        pltpu.VMEM((rows, OCT_STATE), F32)],
        compiler_params=_params(("arbitrary", "arbitrary"), 56),
        name="ssm_core",
    )(us_r, s0, bd, cd, pw)
    return y, sfin.reshape(OCTETS, n_batch, OCT_STATE)


def _gelu_tanh(x):
    c = math.sqrt(2.0 / math.pi)
    return x * (0.5 * (1.0 + jnp.tanh(c * (x + 0.044715 * (x * x * x)))))


def _ssm_glu_kernel(y_ref, us_ref, d_ref, w_ref, b_ref, o_ref, nat_ref):
    tc = y_ref.shape[1]
    for i in range(SSM_CHUNK):
        cs = slice(i * LANES, (i + 1) * LANES)
        for m in range(OCTETS):
            d = d_ref[:, m * LANES:(m + 1) * LANES]
            nat_ref[m, pl.ds(i, tc, stride=SSM_CHUNK), :] = y_ref[m, :, cs] + d * us_ref[m, :, cs]
    yg = _gelu_tanh(jnp.concatenate([nat_ref[m] for m in range(OCTETS)], axis=1))
    z = _dot(yg.astype(BF16), w_ref[...]) + b_ref[...]
    o_ref[...] = (yg * jax.nn.sigmoid(z)).astype(o_ref.dtype)


def _ssm_glu(y_r, us_r, d, w_glu, b_glu, tc):
    nc = y_r.shape[1]
    assert nc % tc == 0
    oct_spec = pl.BlockSpec((OCTETS, tc, OCT_COLS), lambda c: (0, c, 0))
    vec = pl.BlockSpec((1, SSM_WIDTH), lambda c: (0, 0))
    return pl.pallas_call(
        _ssm_glu_kernel,
        grid=(nc // tc,),
        in_specs=[oct_spec, oct_spec, vec,
                  pl.BlockSpec((SSM_WIDTH, SSM_WIDTH), lambda c: (0, 0)), vec],
        out_specs=pl.BlockSpec((tc * SSM_CHUNK, SSM_WIDTH), lambda c: (c, 0)),
        out_shape=jax.ShapeDtypeStruct((nc * SSM_CHUNK, SSM_WIDTH), BF16),
        scratch_shapes=[pltpu.VMEM((OCTETS, tc * SSM_CHUNK, LANES), F32)],
        compiler_params=_params(("parallel",), 48),
        name="ssm_glu",
    )(y_r, us_r, d, w_glu, b_glu)


def _head_masks():
    lane = lax.broadcasted_iota(jnp.int32, (1, LANES), 1)
    return (lane < ATT_HEAD_DIM, lane >= ATT_HEAD_DIM)


def _pair_attention(q2, kw, vw, biases, masks):
    acc = None
    for hh in range(2):
        qh = jnp.where(masks[hh], q2, jnp.zeros_like(q2))
        sc = _dot_nt(qh, kw) * (ATT_HEAD_DIM ** -0.5 * LOG2E) + biases[hh]
        mx = jnp.max(sc, axis=1, keepdims=True)
        p = jnp.exp2(sc - mx)
        l = jnp.sum(p, axis=1, keepdims=True)
        vh = jnp.where(masks[hh], vw, jnp.zeros_like(vw))
        o = _dot(p.astype(BF16), vh) * (1.0 / l)
        acc = o if acc is None else acc + o
    return acc


def _band_attn_kernel(q_ref, kp_ref, kc_ref, vp_ref, vc_ref, bias_ref, o_ref, kw_ref, vw_ref):
    kw_ref[0:Q_TILE] = kp_ref[...]
    kw_ref[Q_TILE:2 * Q_TILE] = kc_ref[...]
    vw_ref[0:Q_TILE] = vp_ref[...]
    vw_ref[Q_TILE:2 * Q_TILE] = vc_ref[...]
    masks = _head_masks()
    n_invalid = jnp.where(pl.program_id(1) == 0, Q_TILE, 0)
    col = lax.broadcasted_iota(jnp.int32, (1, K_WIN), 1)

    def sub(s, carry):
        r0 = pl.multiple_of(s * Q_SUB, Q_SUB)
        extra = jnp.where(col + r0 < n_invalid, NEG_INF, 0.0)
        for hp in range(ATT_HEADS // 2):
            cs = slice(hp * LANES, (hp + 1) * LANES)
            biases = [bias_ref[2 * hp + hh] + extra for hh in range(2)]
            o = _pair_attention(q_ref[pl.ds(r0, Q_SUB), cs], kw_ref[pl.ds(r0, K_WIN), cs],
                                vw_ref[pl.ds(r0, K_WIN), cs], biases, masks)
            o_ref[pl.ds(r0, Q_SUB), cs] = o.astype(o_ref.dtype)
        return carry

    lax.fori_loop(0, Q_TILE // Q_SUB, sub, 0)


def _rel_bias_tile(rel_bias, n_q, n_k, offset, ok):
    r = np.arange(n_q - 1 + n_k)
    idx = np.clip(offset + n_q - 1 - r, -REL_CLIP, REL_CLIP) + REL_CLIP
    v = jnp.pad(rel_bias[:, idx] * LOG2E, ((0, 0), (0, 1)))
    w = n_q + n_k
    flat = jnp.tile(v, (1, n_q))[:, :n_q * (w - 1)]
    toep = flat.reshape(rel_bias.shape[0], n_q, w - 1)[:, :, n_q - 1:]
    return jnp.where(ok[None], toep, NEG_INF)


def _band_attention_prompt(qkvm, rel_bias, n_batch, seq):
    assert seq % Q_TILE == 0
    nt = seq // Q_TILE
    qc = np.arange(Q_SUB)[:, None] // CHUNK
    kc = np.arange(K_WIN)[None, :] // CHUNK
    bias = _rel_bias_tile(rel_bias, Q_SUB, K_WIN, BAND_PAST, (kc >= qc) & (kc <= qc + BAND_PAST_CHUNKS))
    blk = (Q_TILE, ATT_WIDTH)
    cur = lambda col: pl.BlockSpec(blk, lambda b, t: (b * nt + t, col))
    prev = lambda col: pl.BlockSpec(blk, lambda b, t: (b * nt + jnp.maximum(t - 1, 0), col))
    return pl.pallas_call(
        _band_attn_kernel,
        grid=(n_batch, nt),
        in_specs=[cur(0), prev(1), cur(1), prev(2), cur(2),
                  pl.BlockSpec((ATT_HEADS, Q_SUB, K_WIN), lambda b, t: (0, 0, 0))],
        out_specs=pl.BlockSpec(blk, lambda b, t: (b * nt + t, 0)),
        out_shape=jax.ShapeDtypeStruct((n_batch * seq, ATT_WIDTH), BF16),
        scratch_shapes=[pltpu.VMEM((2 * Q_TILE, ATT_WIDTH), BF16),
                        pltpu.VMEM((2 * Q_TILE, ATT_WIDTH), BF16)],
        compiler_params=_params(("parallel", "arbitrary"), 48),
        name="band_attn",
    )(qkvm, qkvm, qkvm, qkvm, qkvm, bias)


def _band_attn_sample_kernel(q_ref, kn_ref, vn_ref, ck_ref, cv_ref, bias_ref, o_ref, kw_ref, vw_ref,
                             *, n_cache, n_new):
    kw_ref[...] = jnp.zeros_like(kw_ref)
    vw_ref[...] = jnp.zeros_like(vw_ref)
    kw_ref[0:n_cache] = ck_ref[0].astype(BF16)
    vw_ref[0:n_cache] = cv_ref[0].astype(BF16)
    kw_ref[n_cache:n_cache + n_new] = kn_ref[...]
    vw_ref[n_cache:n_cache + n_new] = vn_ref[...]
    masks = _head_masks()
    for hp in range(ATT_HEADS // 2):
        cs = slice(hp * LANES, (hp + 1) * LANES)
        biases = [bias_ref[2 * hp + hh] for hh in range(2)]
        o = _pair_attention(q_ref[:, cs], kw_ref[:, cs], vw_ref[:, cs], biases, masks)
        o_ref[:, cs] = o.astype(o_ref.dtype)


def _band_attention_sample(qkvm, rel_bias, cache_k, cache_v, n_batch, n_new):
    n_cache = cache_k.shape[1]
    n_keys = -(-(n_cache + n_new) // LANES) * LANES
    q_pos = PAST_LEN + np.arange(n_new)[:, None]
    j = np.arange(n_keys)[None, :]
    k_pos = PAST_LEN - n_cache + j
    ok = ((j < n_cache + n_new) & (k_pos >= 0) & (k_pos // CHUNK <= q_pos // CHUNK)
          & (k_pos // CHUNK >= q_pos // CHUNK - BAND_PAST_CHUNKS))
    bias = _rel_bias_tile(rel_bias, n_new, n_keys, n_cache, ok)
    new = lambda col: pl.BlockSpec((n_new, ATT_WIDTH), lambda b: (b, col))
    cache = pl.BlockSpec((1, n_cache, ATT_WIDTH), lambda b: (b, 0, 0))
    return pl.pallas_call(
        functools.partial(_band_attn_sample_kernel, n_cache=n_cache, n_new=n_new),
        grid=(n_batch,),
        in_specs=[new(0), new(1), new(2), cache, cache,
                  pl.BlockSpec((ATT_HEADS, n_new, n_keys), lambda b: (0, 0, 0))],
        out_specs=pl.BlockSpec((n_new, ATT_WIDTH), lambda b: (b, 0)),
        out_shape=jax.ShapeDtypeStruct((n_batch * n_new, ATT_WIDTH), BF16),
        scratch_shapes=[pltpu.VMEM((n_keys, ATT_WIDTH), BF16),
                        pltpu.VMEM((n_keys, ATT_WIDTH), BF16)],
        compiler_params=_params(("parallel",), 48),
        name="band_attn_sample",
    )(qkvm, qkvm, qkvm, cache_k, cache_v, bias)


def _mem_attn_kernel(q_ref, k_ref, v_ref, o_ref):
    k = k_ref[0].astype(BF16)
    v = v_ref[0].astype(BF16)
    for h in range(MEM_HEADS):
        cs = slice(h * MEM_HEAD_DIM, (h + 1) * MEM_HEAD_DIM)
        sc = _dot_nt(q_ref[:, cs], k[:, cs]) * (MEM_HEAD_DIM ** -0.5 * LOG2E)
        mx = jnp.max(sc, axis=1, keepdims=True)
        p = jnp.exp2(sc - mx)
        l = jnp.sum(p, axis=1, keepdims=True)
        o = _dot(p.astype(BF16), v[:, cs]) * (1.0 / l)
        o_ref[:, cs] = o.astype(o_ref.dtype)


def _memory_attention(qkvm, mem_k, mem_v, n_batch, seq, tq):
    nt = seq // tq
    mem = pl.BlockSpec((1, N_MEM, MEM_WIDTH), lambda b, t: (b, 0, 0))
    return pl.pallas_call(
        _mem_attn_kernel,
        grid=(n_batch, nt),
        in_specs=[pl.BlockSpec((tq, MEM_WIDTH), lambda b, t: (b * nt + t, 3)), mem, mem],
        out_specs=pl.BlockSpec((tq, MEM_WIDTH), lambda b, t: (b * nt + t, 0)),
        out_shape=jax.ShapeDtypeStruct((n_batch * seq, MEM_WIDTH), BF16),
        compiler_params=_params(("parallel", "arbitrary"), 48),
        name="mem_attn",
    )(qkvm, mem_k, mem_v)


def _gate_merge_kernel(u_ref, os_ref, oa_ref, om_ref, wgs_ref, wga_ref, wgm_ref,
                       wbs_ref, wba_ref, wbm_ref, out_ref):
    u = u_ref[...]

    def branch(o_ref, wg_ref, wb_ref):
        return jax.nn.sigmoid(_dot(u, wg_ref[...])) * _dot(o_ref[...], wb_ref[...])

    merged = (branch(os_ref, wgs_ref, wbs_ref) + branch(oa_ref, wga_ref, wba_ref)
              + branch(om_ref, wgm_ref, wbm_ref))
    out_ref[...] = merged.astype(out_ref.dtype)


def _gate_merge(u, o_s, o_a, o_m, w_gate, wb_s, wb_a, wb_m, tm, tn):
    m, d = u.shape
    gate = lambda b: pl.BlockSpec((d, tn), lambda i, n: (0, b * (d // tn) + n))
    wb = pl.BlockSpec((SSM_WIDTH, tn), lambda i, n: (0, n))
    ob = pl.BlockSpec((tm, SSM_WIDTH), lambda i, n: (i, 0))
    return pl.pallas_call(
        _gate_merge_kernel,
        grid=(m // tm, d // tn),
        in_specs=[pl.BlockSpec((tm, d), lambda i, n: (i, 0)), ob, ob, ob,
                  gate(0), gate(1), gate(2), wb, wb, wb],
        out_specs=pl.BlockSpec((tm, tn), lambda i, n: (i, n)),
        out_shape=jax.ShapeDtypeStruct((m, d), BF16),
        compiler_params=_params(("parallel", "arbitrary"), 56),
        name="gate_merge",
    )(u, o_s, o_a, o_m, w_gate, w_gate, w_gate, wb_s, wb_a, wb_m)


def _out_proj_kernel(x_ref, w_ref, h_ref, g_ref, o_ref):
    o_ref[...] = h_ref[...] + _rms(_dot(x_ref[...], w_ref[...]), g_ref[...])


def _out_proj(x, w_out, h, g_post, tm):
    m, d = h.shape
    row = lambda: pl.BlockSpec((tm, d), lambda i: (i, 0))
    return pl.pallas_call(
        _out_proj_kernel,
        grid=(m // tm,),
        in_specs=[row(), pl.BlockSpec((d, d), lambda i: (0, 0)), row(),
                  pl.BlockSpec((1, d), lambda i: (0, 0))],
        out_specs=row(),
        out_shape=jax.ShapeDtypeStruct((m, d), F32),
        compiler_params=_params(("parallel",), 56),
        name="out_proj",
    )(x, w_out, h, g_post)


def _layer(x, w, *, n_batch, seq, s0, mem_k, mem_v, cache_k, cache_v, tm, keep):
    n_tok = n_batch * seq
    n_chunks = seq // SSM_CHUNK
    h1, u = _ffn(x, w["ffn1_norm_pre"], w["ffn1_norm_post"], w["ffn1_w_gate"], w["ffn1_w_up"],
                 w["ffn1_w_down"], w["mix_norm_pre"], tm, 512)

    qkvm = _matmul(u, w["w_qkvm"], BF16, tm, name="qkvm")
    if keep == seq:
        kv = _matmul(u, w["w_kv"], F32, tm, name="kv_tail")
    else:
        n_keep_blocks = seq // keep
        kv = _matmul(u, w["w_kv"], F32, keep, m_out=n_batch * keep,
                     row_map=lambda i: i * n_keep_blocks + (n_keep_blocks - 1), name="kv_tail")

    us_r = _ssm_inproj(u, w["w_ssm"], tm)
    nb = 2 if (n_chunks > 1 and n_batch % 2 == 0) else (1 if n_chunks > 1 else n_batch)
    y_r, s_fin = _ssm_core(us_r, s0, w["ssm_bd"], w["ssm_cd"], w["ssm_pw"], n_batch, n_chunks, nb)
    o_s = _ssm_glu(y_r, us_r, w["ssm_d"], w["ssm_w_glu"], w["ssm_b_glu"], min(32, n_tok // SSM_CHUNK))

    if cache_k is None:
        o_a = _band_attention_prompt(qkvm, w["att_rel_bias"], n_batch, seq)
    else:
        o_a = _band_attention_sample(qkvm, w["att_rel_bias"], cache_k, cache_v, n_batch, seq)
    o_m = _memory_attention(qkvm, mem_k, mem_v, n_batch, seq, min(512, seq))

    merged = _gate_merge(u, o_s, o_a, o_m, w["w_gate"], w["w_branch_ssm"], w["w_branch_att"],
                         w["w_branch_mem"], min(1024, n_tok), 512)
    h2 = _out_proj(merged, w["w_out"], h1, w["mix_norm_post"], tm)
    y = _ffn(h2, w["ffn2_norm_pre"], w["ffn2_norm_post"], w["ffn2_w_gate"], w["ffn2_w_up"],
             w["ffn2_w_down"], None, tm, 512)
    return y, kv, s_fin


def kernel(x_prompt, x_sample, mem_prompt, cache_att_k, cache_att_v, cache_mem_k, cache_mem_v, state_ssm_re, state_ssm_im, ffn1_norm_pre, ffn1_norm_post, ffn1_w_gate, ffn1_w_up, ffn1_w_down, mix_norm_pre, mix_norm_post, w_in, ssm_a_re, ssm_a_im, ssm_log_dt, ssm_b_re, ssm_b_im, ssm_c_re, ssm_c_im, ssm_d, ssm_w_glu, ssm_b_glu, att_rel_bias, mem_norm, w_mem_k, w_mem_v, w_branch_ssm, w_branch_att, w_branch_mem, w_out, ffn2_norm_pre, ffn2_norm_post, ffn2_w_gate, ffn2_w_up, ffn2_w_down):
    n_bp, t_p, d = x_prompt.shape
    n_bs, t_s, _ = x_sample.shape
    depth = ffn1_norm_pre.shape[0]
    assert depth == 1 and d == D_MODEL
    keep = min(BAND_PAST, t_p)
    l = 0

    vec = lambda a: a[l].reshape(1, -1).astype(F32)
    mat = lambda a: a[l].astype(BF16)
    bd, cd, pw = _ssm_params(ssm_a_re[l], ssm_a_im[l], ssm_log_dt[l], ssm_b_re[l],
                             ssm_b_im[l], ssm_c_re[l], ssm_c_im[l])
    w = {
        "ffn1_norm_pre": vec(ffn1_norm_pre), "ffn1_norm_post": vec(ffn1_norm_post),
        "ffn1_w_gate": mat(ffn1_w_gate), "ffn1_w_up": mat(ffn1_w_up), "ffn1_w_down": mat(ffn1_w_down),
        "mix_norm_pre": vec(mix_norm_pre), "mix_norm_post": vec(mix_norm_post),
        "w_ssm": w_in[l, :, :COL_Q].astype(BF16),
        "w_qkvm": w_in[l, :, COL_Q:COL_GATE].astype(BF16),
        "w_kv": w_in[l, :, COL_Q + ATT_WIDTH:COL_Q + 3 * ATT_WIDTH].astype(BF16),
        "w_gate": w_in[l, :, COL_GATE:].astype(BF16),
        "ssm_bd": bd, "ssm_cd": cd, "ssm_pw": pw,
        "ssm_d": vec(ssm_d), "ssm_w_glu": mat(ssm_w_glu), "ssm_b_glu": vec(ssm_b_glu),
        "att_rel_bias": att_rel_bias[l].astype(F32),
        "w_branch_ssm": mat(w_branch_ssm), "w_branch_att": mat(w_branch_att),
        "w_branch_mem": mat(w_branch_mem), "w_out": mat(w_out),
        "ffn2_norm_pre": vec(ffn2_norm_pre), "ffn2_norm_post": vec(ffn2_norm_post),
        "ffn2_w_gate": mat(ffn2_w_gate), "ffn2_w_up": mat(ffn2_w_up), "ffn2_w_down": mat(ffn2_w_down),
    }

    w_mem = jnp.concatenate([w_mem_k[l], w_mem_v[l]], axis=1).astype(BF16)
    mkv = _norm_matmul(mem_prompt.reshape(n_bp * N_MEM, d), vec(mem_norm), w_mem, 512, 1024)
    mk_p = mkv[:, :MEM_WIDTH].reshape(n_bp, N_MEM, MEM_WIDTH)
    mv_p = mkv[:, MEM_WIDTH:].reshape(n_bp, N_MEM, MEM_WIDTH)
    zero_state = jnp.zeros((OCTETS, n_bp, OCT_STATE), F32)
    y_p, kv_p, sfin_p = _layer(x_prompt.reshape(n_bp * t_p, d), w, n_batch=n_bp, seq=t_p, s0=zero_state,
                               mem_k=mk_p, mem_v=mv_p, cache_k=None, cache_v=None, tm=512, keep=keep)
    sre_p, sim_p = _state_from_octets(sfin_p)

    n_cache = cache_att_k.shape[2]
    s0_s = _state_to_octets(state_ssm_re[l].astype(F32), state_ssm_im[l].astype(F32))
    y_s, kv_s, sfin_s = _layer(x_sample.reshape(n_bs * t_s, d), w, n_batch=n_bs, seq=t_s, s0=s0_s,
                               mem_k=cache_mem_k[l].reshape(n_bs, N_MEM, MEM_WIDTH),
                               mem_v=cache_mem_v[l].reshape(n_bs, N_MEM, MEM_WIDTH),
                               cache_k=cache_att_k[l].reshape(n_bs, n_cache, ATT_WIDTH),
                               cache_v=cache_att_v[l].reshape(n_bs, n_cache, ATT_WIDTH),
                               tm=n_bs * t_s, keep=t_s)
    sre_s, sim_s = _state_from_octets(sfin_s)

    heads = lambda a, nb, t: a.reshape(1, nb, t, ATT_HEADS, ATT_HEAD_DIM)
    memh = lambda a: a.reshape(1, n_bp, N_MEM, MEM_HEADS, MEM_HEAD_DIM)
    return (y_p.reshape(n_bp, t_p, d), y_s.reshape(n_bs, t_s, d),
            heads(kv_p[:, :ATT_WIDTH], n_bp, keep), heads(kv_p[:, ATT_WIDTH:], n_bp, keep),
            memh(mk_p), memh(mv_p), sre_p[None], sim_p[None],
            heads(kv_s[:, :ATT_WIDTH], n_bs, t_s), heads(kv_s[:, ATT_WIDTH:], n_bs, t_s),
            sre_s[None], sim_s[None])
```

```python
import functools
import math

import numpy as np
import jax
import jax.numpy as jnp
from jax import lax
from jax.experimental import pallas as pl
from jax.experimental.pallas import tpu as pltpu

F32 = jnp.float32
BF16 = jnp.bfloat16

D_MODEL = 2048
CHUNK = 64
BAND_PAST_CHUNKS = 8
BAND_PAST = BAND_PAST_CHUNKS * CHUNK
ATT_HEADS = 16
ATT_HEAD_DIM = 64
ATT_WIDTH = ATT_HEADS * ATT_HEAD_DIM
REL_CLIP = 128
SSM_GROUP = 16
SSM_WIDTH = 1024
SSM_GROUPS = SSM_WIDTH // SSM_GROUP
SSM_STATE = 64
N_MEM = 256
MEM_HEADS = 4
MEM_HEAD_DIM = 256
MEM_WIDTH = MEM_HEADS * MEM_HEAD_DIM
N_BRANCH = 3
EPS = 1e-6
NEG_INF = -1e30
PAST_LEN = 4096
LOG2E = math.log2(math.e)

COL_Q = SSM_WIDTH
COL_GATE = SSM_WIDTH + 3 * ATT_WIDTH + MEM_WIDTH

SSM_CHUNK = 16
OCTETS = 8
OCT_GROUPS = SSM_GROUPS // OCTETS
LANES = 128
MXU_DIM = 256
OCT_COLS = SSM_CHUNK * LANES
OCT_HALF = OCT_GROUPS * SSM_STATE
OCT_STATE = 2 * OCT_HALF

Q_TILE = 512
Q_SUB = 128
K_WIN = Q_SUB + BAND_PAST

MIB = 1024 * 1024


def _params(sem, vmem_mib):
    return pltpu.CompilerParams(dimension_semantics=sem, vmem_limit_bytes=vmem_mib * MIB)


def _dot(a, b):
    return jnp.dot(a, b, preferred_element_type=F32)


def _dot_nt(a, b, precision=None):
    return lax.dot_general(a, b, (((1,), (1,)), ((), ())), precision=precision,
                           preferred_element_type=F32)


def _rms(xf, g):
    y = xf * lax.rsqrt(jnp.mean(xf * xf, axis=-1, keepdims=True) + EPS)
    return y * g


def _ffn_kernel(h_ref, gpre_ref, gpost_ref, wg_ref, wu_ref, wd_ref, *rest, nf, emit_next):
    if emit_next:
        gnext_ref, out_ref, nxt_ref, xn_ref = rest
    else:
        out_ref, xn_ref = rest
    f = pl.program_id(1)

    @pl.when(f == 0)
    def _():
        xn_ref[...] = _rms(h_ref[...], gpre_ref[...]).astype(BF16)
        out_ref[...] = jnp.zeros_like(out_ref)

    xn = xn_ref[...]
    g = _dot(xn, wg_ref[...])
    u = _dot(xn, wu_ref[...])
    a = (g * jax.nn.sigmoid(g)) * u
    out_ref[...] += _dot(a.astype(BF16), wd_ref[...])

    @pl.when(f == nf - 1)
    def _():
        hn = h_ref[...] + 0.5 * _rms(out_ref[...], gpost_ref[...])
        out_ref[...] = hn
        if emit_next:
            nxt_ref[...] = _rms(hn, gnext_ref[...]).astype(BF16)


def _ffn(h, g_pre, g_post, wg, wu, wd, g_next, tm, tf):
    emit_next = g_next is not None
    m, d = h.shape
    f_dim = wg.shape[1]
    nf = f_dim // tf
    assert m % tm == 0 and f_dim % tf == 0
    vec = pl.BlockSpec((1, d), lambda i, f: (0, 0))
    row = pl.BlockSpec((tm, d), lambda i, f: (i, 0))
    in_specs = [row, vec, vec,
                pl.BlockSpec((d, tf), lambda i, f: (0, f)),
                pl.BlockSpec((d, tf), lambda i, f: (0, f)),
                pl.BlockSpec((tf, d), lambda i, f: (f, 0))]
    args = [h, g_pre, g_post, wg, wu, wd]
    out_shape = [jax.ShapeDtypeStruct((m, d), F32)]
    out_specs = [row]
    if emit_next:
        in_specs.append(vec)
        args.append(g_next)
        out_shape.append(jax.ShapeDtypeStruct((m, d), BF16))
        out_specs.append(row)
    res = pl.pallas_call(
        functools.partial(_ffn_kernel, nf=nf, emit_next=emit_next),
        grid=(m // tm, nf),
        in_specs=in_specs,
        out_specs=out_specs,
        out_shape=out_shape,
        scratch_shapes=[pltpu.VMEM((tm, d), BF16)],
        compiler_params=_params(("parallel", "arbitrary"), 56),
        name="ffn",
    )(*args)
    return res if emit_next else res[0]


def _mm_kernel(x_ref, w_ref, o_ref, *, tn):
    x = x_ref[...]
    for j in range(o_ref.shape[1] // tn):
        cs = slice(j * tn, (j + 1) * tn)
        o_ref[:, cs] = _dot(x, w_ref[:, cs]).astype(o_ref.dtype)


def _matmul(x, w, out_dtype, tm, m_out=None, row_map=None, name="proj"):
    m, k = x.shape
    n = w.shape[1]
    m_out = m if m_out is None else m_out
    row_map = (lambda i: i) if row_map is None else row_map
    tn = min(n, 1024)
    assert m_out % tm == 0 and n % tn == 0
    return pl.pallas_call(
        functools.partial(_mm_kernel, tn=tn),
        grid=(m_out // tm,),
        in_specs=[pl.BlockSpec((tm, k), lambda i: (row_map(i), 0)),
                  pl.BlockSpec((k, n), lambda i: (0, 0))],
        out_specs=pl.BlockSpec((tm, n), lambda i: (i, 0)),
        out_shape=jax.ShapeDtypeStruct((m_out, n), out_dtype),
        compiler_params=_params(("parallel",), 56),
        name=name,
    )(x, w)


def _norm_mm_kernel(x_ref, g_ref, w_ref, o_ref):
    xn = _rms(x_ref[...], g_ref[...]).astype(BF16)
    o_ref[...] = _dot(xn, w_ref[...]).astype(o_ref.dtype)


def _norm_matmul(x, g, w, tm, tn):
    m, k = x.shape
    n = w.shape[1]
    tm = min(tm, m)
    assert m % tm == 0 and n % tn == 0
    return pl.pallas_call(
        _norm_mm_kernel,
        grid=(m // tm, n // tn),
        in_specs=[pl.BlockSpec((tm, k), lambda i, j: (i, 0)),
                  pl.BlockSpec((1, k), lambda i, j: (0, 0)),
                  pl.BlockSpec((k, tn), lambda i, j: (0, j))],
        out_specs=pl.BlockSpec((tm, tn), lambda i, j: (i, j)),
        out_shape=jax.ShapeDtypeStruct((m, n), F32),
        compiler_params=_params(("parallel", "arbitrary"), 48),
        name="memory_kv",
    )(x, g, w)


def _ssm_inproj_kernel(x_ref, w_ref, o_ref, nat_ref):
    r = _dot(x_ref[...], w_ref[...])
    tc = nat_ref.shape[1] // SSM_CHUNK
    for m in range(OCTETS):
        nat_ref[m] = r[:, m * LANES:(m + 1) * LANES]
    for j in range(SSM_CHUNK):
        for m in range(OCTETS):
            o_ref[m, :, j * LANES:(j + 1) * LANES] = nat_ref[m, pl.ds(j, tc, stride=SSM_CHUNK), :]


def _ssm_inproj(u, w_ssm, tm):
    n_tok = u.shape[0]
    assert n_tok % tm == 0 and tm % SSM_CHUNK == 0
    tc = tm // SSM_CHUNK
    return pl.pallas_call(
        _ssm_inproj_kernel,
        grid=(n_tok // tm,),
        in_specs=[pl.BlockSpec((tm, D_MODEL), lambda i: (i, 0)),
                  pl.BlockSpec((D_MODEL, SSM_WIDTH), lambda i: (0, 0))],
        out_specs=pl.BlockSpec((OCTETS, tc, OCT_COLS), lambda i: (0, i, 0)),
        out_shape=jax.ShapeDtypeStruct((OCTETS, n_tok // SSM_CHUNK, OCT_COLS), F32),
        scratch_shapes=[pltpu.VMEM((OCTETS, tm, LANES), F32)],
        compiler_params=_params(("parallel",), 48),
        name="ssm_inproj",
    )(u, w_ssm)


def _ssm_params(a_re, a_im, log_dt, b_re, b_im, c_re, c_im):
    dt = jnp.exp(log_dt)[:, None]
    mag = jnp.exp(a_re * dt)
    ab_re = mag * jnp.cos(a_im * dt)
    ab_im = mag * jnp.sin(a_im * dt)
    den = a_re * a_re + a_im * a_im
    n_re = ab_re - 1.0
    n_im = ab_im
    k_re = (n_re * a_re + n_im * a_im) / den
    k_im = (n_im * a_re - n_re * a_im) / den
    bb_re = k_re[..., None] * b_re - k_im[..., None] * b_im
    bb_im = k_re[..., None] * b_im + k_im[..., None] * b_re
    pr = [jnp.ones_like(ab_re)]
    pi = [jnp.zeros_like(ab_re)]
    for _ in range(SSM_CHUNK):
        pr.append(pr[-1] * ab_re - pi[-1] * ab_im)
        pi.append(pr[-2] * ab_im + pi[-1] * ab_re)
    n_pw = SSM_CHUNK + 1
    pw = jnp.concatenate([jnp.stack(pr).reshape(n_pw, OCTETS, OCT_HALF),
                          jnp.stack(pi).reshape(n_pw, OCTETS, OCT_HALF)], axis=2)
    pw = jnp.transpose(pw, (1, 0, 2))
    eye = jnp.eye(OCT_GROUPS, dtype=F32)

    def expand(x):
        x4 = x.reshape(OCTETS, OCT_GROUPS, x.shape[1], SSM_STATE)
        out = eye[None, :, None, :, None] * x4[:, :, :, None, :]
        return out.reshape(OCTETS, OCT_GROUPS * x.shape[1], OCT_HALF)

    bd = jnp.concatenate([expand(jnp.transpose(bb_re, (0, 2, 1))),
                          expand(jnp.transpose(bb_im, (0, 2, 1)))], axis=2)
    cd = jnp.concatenate([expand(c_re), expand(c_im)], axis=2)
    return bd, cd, pw


def _state_to_octets(s_re, s_im):
    b = s_re.shape[0]
    s = jnp.stack([s_re, s_im], 0).reshape(2, b, OCTETS, OCT_GROUPS, SSM_STATE)
    return jnp.transpose(s, (2, 1, 0, 3, 4)).reshape(OCTETS, b, OCT_STATE)


def _state_from_octets(s):
    b = s.shape[1]
    s = s.reshape(OCTETS, b, 2, OCT_GROUPS, SSM_STATE)
    s = jnp.transpose(s, (2, 1, 0, 3, 4)).reshape(2, b, SSM_GROUPS, SSM_STATE)
    return s[0], s[1]


def _ssm_build_operators(bd_ref, cd_ref, pw_ref, t8_ref, wus_ref, wso_ref):
    bd_re, bd_im = bd_ref[0, :, :OCT_HALF], bd_ref[0, :, OCT_HALF:]
    cd_re, cd_im = cd_ref[0, :, :OCT_HALF], cd_ref[0, :, OCT_HALF:]
    cdm = jnp.concatenate([cd_re, -cd_im], axis=1)
    blk = lambda i: slice(i * LANES, (i + 1) * LANES)
    for a in range(SSM_CHUNK // 2):
        t8_ref[blk(2 * a + 1), blk(2 * a)] = jnp.zeros((LANES, LANES), BF16)
    for k in range(SSM_CHUNK + 1):
        p_re = pw_ref[0, k:k + 1, :OCT_HALF]
        p_im = pw_ref[0, k:k + 1, OCT_HALF:]
        if k < SSM_CHUNK:
            e = jnp.concatenate([bd_re * p_re - bd_im * p_im, bd_re * p_im + bd_im * p_re], axis=1)
            wus_ref[blk(SSM_CHUNK - 1 - k), :] = e.astype(BF16)
            lag = _dot_nt(e, cdm, precision=lax.Precision.HIGHEST).astype(BF16)
            for j in range(SSM_CHUNK - k):
                t8_ref[blk(j), blk(j + k)] = lag
        if k >= 1:
            g = jnp.concatenate([cd_re * p_re - cd_im * p_im, -(cd_re * p_im + cd_im * p_re)], axis=1)
            wso_ref[blk(k - 1), :] = g.astype(BF16)


def _ssm_core_kernel(us_ref, s0_ref, bd_ref, cd_ref, pw_ref, y_ref, sfin_ref,
                     t8_ref, wus_ref, wso_ref, ds_ref, sp_ref, *, nb, n_chunks):
    @pl.when(pl.program_id(1) == 0)
    def _():
        _ssm_build_operators(bd_ref, cd_ref, pw_ref, t8_ref, wus_ref, wso_ref)

    ub = us_ref[0].astype(BF16)
    ds_ref[...] = _dot(ub, wus_ref[...])
    a_re = pw_ref[0, SSM_CHUNK:SSM_CHUNK + 1, :OCT_HALF]
    a_im = pw_ref[0, SSM_CHUNK:SSM_CHUNK + 1, OCT_HALF:]

    def advance(s, d):
        s_re, s_im = s[:, :OCT_HALF], s[:, OCT_HALF:]
        n_re = a_re * s_re - a_im * s_im + d[:, :OCT_HALF]
        n_im = a_re * s_im + a_im * s_re + d[:, OCT_HALF:]
        return jnp.concatenate([n_re, n_im], axis=1)

    if n_chunks == 1:
        s0 = s0_ref[0, 0]
        sp_ref[...] = s0
        sfin_ref[0, 0] = advance(s0, ds_ref[...])
    else:
        def body(c, carry):
            new = []
            for bl in range(nb):
                row = bl * n_chunks + c
                sp_ref[pl.ds(row, 1), :] = carry[bl]
                new.append(advance(carry[bl], ds_ref[pl.ds(row, 1), :]))
            return tuple(new)

        init = tuple(s0_ref[0, 0, bl:bl + 1, :] for bl in range(nb))
        fin = lax.fori_loop(0, n_chunks, body, init, unroll=2)
        for bl in range(nb):
            sfin_ref[0, 0, bl:bl + 1, :] = fin[bl]

    spb = sp_ref[...].astype(BF16)
    for nt in range(OCT_COLS // MXU_DIM):
        k_hi = (nt + 1) * MXU_DIM
        cs = slice(nt * MXU_DIM, k_hi)
        y_ref[0, :, cs] = _dot(ub[:, :k_hi], t8_ref[:k_hi, cs]) + _dot_nt(spb, wso_ref[cs, :])


def _ssm_core(us_r, s0, bd, cd, pw, n_batch, n_chunks, nb):
    nc = us_r.shape[1]
    rows = nb * n_chunks
    assert n_batch % nb == 0 and nc == n_batch * n_chunks
    nr = n_batch // nb
    s0 = s0.reshape(OCTETS, nr, nb, OCT_STATE)
    par = lambda rows_: pl.BlockSpec((1, rows_, OCT_STATE), lambda m, r: (m, 0, 0))
    y, sfin = pl.pallas_call(
        functools.partial(_ssm_core_kernel, nb=nb, n_chunks=n_chunks),
        grid=(OCTETS, nr),
        in_specs=[pl.BlockSpec((1, rows, OCT_COLS), lambda m, r: (m, r, 0)),
                  pl.BlockSpec((1, 1, nb, OCT_STATE), lambda m, r: (m, r, 0, 0)),
                  par(LANES), par(LANES), par(SSM_CHUNK + 1)],
        out_specs=[pl.BlockSpec((1, rows, OCT_COLS), lambda m, r: (m, r, 0)),
                   pl.BlockSpec((1, 1, nb, OCT_STATE), lambda m, r: (m, r, 0, 0))],
        out_shape=[jax.ShapeDtypeStruct((OCTETS, nc, OCT_COLS), F32),
                   jax.ShapeDtypeStruct((OCTETS, nr, nb, OCT_STATE), F32)],
        scratch_shapes=[pltpu.VMEM((OCT_COLS, OCT_COLS), BF16),
                        pltpu.VMEM((OCT_COLS, OCT_STATE), BF16),
                        pltpu.VMEM((OCT_COLS, OCT_STATE), BF16),
                        pltpu.VMEM((rows, OCT_STATE), F32),
                        pltpu.VMEM((rows, OCT_STATE), F32)],
        compiler_params=_params(("arbitrary", "arbitrary"), 56),
        name="ssm_core",
    )(us_r, s0, bd, cd, pw)
    return y, sfin.reshape(OCTETS, n_batch, OCT_STATE)


def _gelu_tanh(x):
    c = math.sqrt(2.0 / math.pi)
    return x * (0.5 * (1.0 + jnp.tanh(c * (x + 0.044715 * (x * x * x)))))


def _ssm_glu_kernel(y_ref, us_ref, d_ref, w_ref, b_ref, o_ref, nat_ref):
    tc = y_ref.shape[1]
    for i in range(SSM_CHUNK):
        cs = slice(i * LANES, (i + 1) * LANES)
        for m in range(OCTETS):
            d = d_ref[:, m * LANES:(m + 1) * LANES]
            nat_ref[m, pl.ds(i, tc, stride=SSM_CHUNK), :] = y_ref[m, :, cs] + d * us_ref[m, :, cs]
    yg = _gelu_tanh(jnp.concatenate([nat_ref[m] for m in range(OCTETS)], axis=1))
    z = _dot(yg.astype(BF16), w_ref[...]) + b_ref[...]
    o_ref[...] = (yg * jax.nn.sigmoid(z)).astype(o_ref.dtype)


def _ssm_glu(y_r, us_r, d, w_glu, b_glu, tc):
    nc = y_r.shape[1]
    assert nc % tc == 0
    oct_spec = pl.BlockSpec((OCTETS, tc, OCT_COLS), lambda c: (0, c, 0))
    vec = pl.BlockSpec((1, SSM_WIDTH), lambda c: (0, 0))
    return pl.pallas_call(
        _ssm_glu_kernel,
        grid=(nc // tc,),
        in_specs=[oct_spec, oct_spec, vec,
                  pl.BlockSpec((SSM_WIDTH, SSM_WIDTH), lambda c: (0, 0)), vec],
        out_specs=pl.BlockSpec((tc * SSM_CHUNK, SSM_WIDTH), lambda c: (c, 0)),
        out_shape=jax.ShapeDtypeStruct((nc * SSM_CHUNK, SSM_WIDTH), BF16),
        scratch_shapes=[pltpu.VMEM((OCTETS, tc * SSM_CHUNK, LANES), F32)],
        compiler_params=_params(("parallel",), 48),
        name="ssm_glu",
    )(y_r, us_r, d, w_glu, b_glu)


N_PAIRS = ATT_HEADS // 2


def _head_masks():
    lane = lax.broadcasted_iota(jnp.int32, (1, LANES), 1)
    return (lane < ATT_HEAD_DIM, lane >= ATT_HEAD_DIM)


def _pair_scores(q2, kw, bias2, masks):
    qq = jnp.concatenate([jnp.where(m, q2, jnp.zeros_like(q2)) for m in masks], axis=0)
    return _dot_nt(qq, kw) * (ATT_HEAD_DIM ** -0.5 * LOG2E) + bias2


def _softmax_parts(sc):
    mx = jnp.max(sc, axis=1, keepdims=True)
    p = jnp.exp2(sc - mx)
    l = jnp.sum(p, axis=1, keepdims=True)
    return p.astype(BF16), jnp.broadcast_to(1.0 / l, (sc.shape[0], LANES))


def _pair_output(p, rl, vw, masks):
    o2 = _dot(p, vw) * rl
    n_q = o2.shape[0] // 2
    return jnp.where(masks[0], o2[:n_q], o2[n_q:])


def _band_attn_kernel(q_ref, kp_ref, kc_ref, vp_ref, vc_ref, bias_ref, o_ref,
                      kw_ref, vw_ref, sc_ref, p_ref, rl_ref):
    kw_ref[0:Q_TILE] = kp_ref[...]
    kw_ref[Q_TILE:2 * Q_TILE] = kc_ref[...]
    vw_ref[0:Q_TILE] = vp_ref[...]
    vw_ref[Q_TILE:2 * Q_TILE] = vc_ref[...]
    masks = _head_masks()
    cs = lambda hp: slice(hp * LANES, (hp + 1) * LANES)

    def sub(s, carry):
        r0 = pl.multiple_of(s * Q_SUB, Q_SUB)
        rows = pl.ds(r0, Q_SUB)
        win = pl.ds(r0, K_WIN)
        for hp in range(N_PAIRS):
            sc_ref[hp] = _pair_scores(q_ref[rows, cs(hp)], kw_ref[win, cs(hp)], bias_ref[hp], masks)

        @pl.when(pl.program_id(1) == 0)
        def _():
            col = lax.broadcasted_iota(jnp.int32, (1, K_WIN), 1)
            extra = jnp.where(col + r0 < Q_TILE, NEG_INF, 0.0)
            for hp in range(N_PAIRS):
                sc_ref[hp] = sc_ref[hp] + extra

        for hp in range(N_PAIRS):
            p_ref[hp], rl_ref[hp] = _softmax_parts(sc_ref[hp])
        for hp in range(N_PAIRS):
            o = _pair_output(p_ref[hp], rl_ref[hp], vw_ref[win, cs(hp)], masks)
            o_ref[rows, cs(hp)] = o.astype(o_ref.dtype)
        return carry

    lax.fori_loop(0, Q_TILE // Q_SUB, sub, 0)


def _rel_bias_tile(rel_bias, n_q, n_k, offset, ok):
    r = np.arange(n_q - 1 + n_k)
    idx = np.clip(offset + n_q - 1 - r, -REL_CLIP, REL_CLIP) + REL_CLIP
    v = jnp.pad(rel_bias[:, idx] * LOG2E, ((0, 0), (0, 1)))
    w = n_q + n_k
    flat = jnp.tile(v, (1, n_q))[:, :n_q * (w - 1)]
    toep = flat.reshape(rel_bias.shape[0], n_q, w - 1)[:, :, n_q - 1:]
    return jnp.where(ok[None], toep, NEG_INF).reshape(N_PAIRS, 2 * n_q, n_k)


def _band_attention_prompt(qkvm, rel_bias, n_batch, seq):
    assert seq % Q_TILE == 0
    nt = seq // Q_TILE
    qc = np.arange(Q_SUB)[:, None] // CHUNK
    kc = np.arange(K_WIN)[None, :] // CHUNK
    bias = _rel_bias_tile(rel_bias, Q_SUB, K_WIN, BAND_PAST, (kc >= qc) & (kc <= qc + BAND_PAST_CHUNKS))
    blk = (Q_TILE, ATT_WIDTH)
    cur = lambda col: pl.BlockSpec(blk, lambda b, t: (b * nt + t, col))
    prev = lambda col: pl.BlockSpec(blk, lambda b, t: (b * nt + jnp.maximum(t - 1, 0), col))
    return pl.pallas_call(
        _band_attn_kernel,
        grid=(n_batch, nt),
        in_specs=[cur(0), prev(1), cur(1), prev(2), cur(2),
                  pl.BlockSpec((N_PAIRS, 2 * Q_SUB, K_WIN), lambda b, t: (0, 0, 0))],
        out_specs=pl.BlockSpec(blk, lambda b, t: (b * nt + t, 0)),
        out_shape=jax.ShapeDtypeStruct((n_batch * seq, ATT_WIDTH), BF16),
        scratch_shapes=[pltpu.VMEM((2 * Q_TILE, ATT_WIDTH), BF16),
                        pltpu.VMEM((2 * Q_TILE, ATT_WIDTH), BF16),
                        pltpu.VMEM((N_PAIRS, 2 * Q_SUB, K_WIN), F32),
                        pltpu.VMEM((N_PAIRS, 2 * Q_SUB, K_WIN), BF16),
                        pltpu.VMEM((N_PAIRS, 2 * Q_SUB, LANES), F32)],
        compiler_params=_params(("parallel", "arbitrary"), 48),
        name="band_attn",
    )(qkvm, qkvm, qkvm, qkvm, qkvm, bias)


def _band_attn_sample_kernel(q_ref, kn_ref, vn_ref, ck_ref, cv_ref, bias_ref, o_ref, kw_ref, vw_ref,
                             *, n_cache, n_new):
    kw_ref[...] = jnp.zeros_like(kw_ref)
    vw_ref[...] = jnp.zeros_like(vw_ref)
    kw_ref[0:n_cache] = ck_ref[0].astype(BF16)
    vw_ref[0:n_cache] = cv_ref[0].astype(BF16)
    kw_ref[n_cache:n_cache + n_new] = kn_ref[...]
    vw_ref[n_cache:n_cache + n_new] = vn_ref[...]
    masks = _head_masks()
    for hp in range(N_PAIRS):
        cs = slice(hp * LANES, (hp + 1) * LANES)
        p, rl = _softmax_parts(_pair_scores(q_ref[:, cs], kw_ref[:, cs], bias_ref[hp], masks))
        o_ref[:, cs] = _pair_output(p, rl, vw_ref[:, cs], masks).astype(o_ref.dtype)


def _band_attention_sample(qkvm, rel_bias, cache_k, cache_v, n_batch, n_new):
    n_cache = cache_k.shape[1]
    n_keys = -(-(n_cache + n_new) // LANES) * LANES
    q_pos = PAST_LEN + np.arange(n_new)[:, None]
    j = np.arange(n_keys)[None, :]
    k_pos = PAST_LEN - n_cache + j
    ok = ((j < n_cache + n_new) & (k_pos >= 0) & (k_pos // CHUNK <= q_pos // CHUNK)
          & (k_pos // CHUNK >= q_pos // CHUNK - BAND_PAST_CHUNKS))
    bias = _rel_bias_tile(rel_bias, n_new, n_keys, n_cache, ok)
    new = lambda col: pl.BlockSpec((n_new, ATT_WIDTH), lambda b: (b, col))
    cache = pl.BlockSpec((1, n_cache, ATT_WIDTH), lambda b: (b, 0, 0))
    return pl.pallas_call(
        functools.partial(_band_attn_sample_kernel, n_cache=n_cache, n_new=n_new),
        grid=(n_batch,),
        in_specs=[new(0), new(1), new(2), cache, cache,
                  pl.BlockSpec((N_PAIRS, 2 * n_new, n_keys), lambda b: (0, 0, 0))],
        out_specs=pl.BlockSpec((n_new, ATT_WIDTH), lambda b: (b, 0)),
        out_shape=jax.ShapeDtypeStruct((n_batch * n_new, ATT_WIDTH), BF16),
        scratch_shapes=[pltpu.VMEM((n_keys, ATT_WIDTH), BF16),
                        pltpu.VMEM((n_keys, ATT_WIDTH), BF16)],
        compiler_params=_params(("parallel",), 48),
        name="band_attn_sample",
    )(qkvm, qkvm, qkvm, cache_k, cache_v, bias)


def _mem_attn_kernel(q_ref, k_ref, v_ref, o_ref):
    k = k_ref[0].astype(BF16)
    v = v_ref[0].astype(BF16)
    for h in range(MEM_HEADS):
        cs = slice(h * MEM_HEAD_DIM, (h + 1) * MEM_HEAD_DIM)
        sc = _dot_nt(q_ref[:, cs], k[:, cs]) * (MEM_HEAD_DIM ** -0.5 * LOG2E)
        mx = jnp.max(sc, axis=1, keepdims=True)
        p = jnp.exp2(sc - mx)
        l = jnp.sum(p, axis=1, keepdims=True)
        o = _dot(p.astype(BF16), v[:, cs]) * (1.0 / l)
        o_ref[:, cs] = o.astype(o_ref.dtype)


def _memory_attention(qkvm, mem_k, mem_v, n_batch, seq, tq):
    nt = seq // tq
    mem = pl.BlockSpec((1, N_MEM, MEM_WIDTH), lambda b, t: (b, 0, 0))
    return pl.pallas_call(
        _mem_attn_kernel,
        grid=(n_batch, nt),
        in_specs=[pl.BlockSpec((tq, MEM_WIDTH), lambda b, t: (b * nt + t, 3)), mem, mem],
        out_specs=pl.BlockSpec((tq, MEM_WIDTH), lambda b, t: (b * nt + t, 0)),
        out_shape=jax.ShapeDtypeStruct((n_batch * seq, MEM_WIDTH), BF16),
        compiler_params=_params(("parallel", "arbitrary"), 48),
        name="mem_attn",
    )(qkvm, mem_k, mem_v)


def _gate_merge_kernel(u_ref, os_ref, oa_ref, om_ref, wgs_ref, wga_ref, wgm_ref,
                       wbs_ref, wba_ref, wbm_ref, out_ref):
    u = u_ref[...]

    def branch(o_ref, wg_ref, wb_ref):
        return jax.nn.sigmoid(_dot(u, wg_ref[...])) * _dot(o_ref[...], wb_ref[...])

    merged = (branch(os_ref, wgs_ref, wbs_ref) + branch(oa_ref, wga_ref, wba_ref)
              + branch(om_ref, wgm_ref, wbm_ref))
    out_ref[...] = merged.astype(out_ref.dtype)


def _gate_merge(u, o_s, o_a, o_m, w_gate, wb_s, wb_a, wb_m, tm, tn):
    m, d = u.shape
    gate = lambda b: pl.BlockSpec((d, tn), lambda i, n: (0, b * (d // tn) + n))
    wb = pl.BlockSpec((SSM_WIDTH, tn), lambda i, n: (0, n))
    ob = pl.BlockSpec((tm, SSM_WIDTH), lambda i, n: (i, 0))
    return pl.pallas_call(
        _gate_merge_kernel,
        grid=(m // tm, d // tn),
        in_specs=[pl.BlockSpec((tm, d), lambda i, n: (i, 0)), ob, ob, ob,
                  gate(0), gate(1), gate(2), wb, wb, wb],
        out_specs=pl.BlockSpec((tm, tn), lambda i, n: (i, n)),
        out_shape=jax.ShapeDtypeStruct((m, d), BF16),
        compiler_params=_params(("parallel", "arbitrary"), 56),
        name="gate_merge",
    )(u, o_s, o_a, o_m, w_gate, w_gate, w_gate, wb_s, wb_a, wb_m)


def _out_proj_kernel(x_ref, w_ref, h_ref, g_ref, o_ref):
    o_ref[...] = h_ref[...] + _rms(_dot(x_ref[...], w_ref[...]), g_ref[...])


def _out_proj(x, w_out, h, g_post, tm):
    m, d = h.shape
    row = lambda: pl.BlockSpec((tm, d), lambda i: (i, 0))
    return pl.pallas_call(
        _out_proj_kernel,
        grid=(m // tm,),
        in_specs=[row(), pl.BlockSpec((d, d), lambda i: (0, 0)), row(),
                  pl.BlockSpec((1, d), lambda i: (0, 0))],
        out_specs=row(),
        out_shape=jax.ShapeDtypeStruct((m, d), F32),
        compiler_params=_params(("parallel",), 56),
        name="out_proj",
    )(x, w_out, h, g_post)


def _layer(x, w, *, n_batch, seq, s0, mem_k, mem_v, cache_k, cache_v, tm, keep):
    n_tok = n_batch * seq
    n_chunks = seq // SSM_CHUNK
    h1, u = _ffn(x, w["ffn1_norm_pre"], w["ffn1_norm_post"], w["ffn1_w_gate"], w["ffn1_w_up"],
                 w["ffn1_w_down"], w["mix_norm_pre"], tm, 512)

    qkvm = _matmul(u, w["w_qkvm"], BF16, tm, name="qkvm")
    if keep == seq:
        kv = _matmul(u, w["w_kv"], F32, tm, name="kv_tail")
    else:
        n_keep_blocks = seq // keep
        kv = _matmul(u, w["w_kv"], F32, keep, m_out=n_batch * keep,
                     row_map=lambda i: i * n_keep_blocks + (n_keep_blocks - 1), name="kv_tail")

    us_r = _ssm_inproj(u, w["w_ssm"], tm)
    nb = 2 if (n_chunks > 1 and n_batch % 2 == 0) else (1 if n_chunks > 1 else n_batch)
    y_r, s_fin = _ssm_core(us_r, s0, w["ssm_bd"], w["ssm_cd"], w["ssm_pw"], n_batch, n_chunks, nb)
    o_s = _ssm_glu(y_r, us_r, w["ssm_d"], w["ssm_w_glu"], w["ssm_b_glu"], min(32, n_tok // SSM_CHUNK))

    if cache_k is None:
        o_a = _band_attention_prompt(qkvm, w["att_rel_bias"], n_batch, seq)
    else:
        o_a = _band_attention_sample(qkvm, w["att_rel_bias"], cache_k, cache_v, n_batch, seq)
    o_m = _memory_attention(qkvm, mem_k, mem_v, n_batch, seq, min(512, seq))

    merged = _gate_merge(u, o_s, o_a, o_m, w["w_gate"], w["w_branch_ssm"], w["w_branch_att"],
                         w["w_branch_mem"], min(1024, n_tok), 512)
    h2 = _out_proj(merged, w["w_out"], h1, w["mix_norm_post"], tm)
    y = _ffn(h2, w["ffn2_norm_pre"], w["ffn2_norm_post"], w["ffn2_w_gate"], w["ffn2_w_up"],
             w["ffn2_w_down"], None, tm, 512)
    return y, kv, s_fin


def kernel(x_prompt, x_sample, mem_prompt, cache_att_k, cache_att_v, cache_mem_k, cache_mem_v, state_ssm_re, state_ssm_im, ffn1_norm_pre, ffn1_norm_post, ffn1_w_gate, ffn1_w_up, ffn1_w_down, mix_norm_pre, mix_norm_post, w_in, ssm_a_re, ssm_a_im, ssm_log_dt, ssm_b_re, ssm_b_im, ssm_c_re, ssm_c_im, ssm_d, ssm_w_glu, ssm_b_glu, att_rel_bias, mem_norm, w_mem_k, w_mem_v, w_branch_ssm, w_branch_att, w_branch_mem, w_out, ffn2_norm_pre, ffn2_norm_post, ffn2_w_gate, ffn2_w_up, ffn2_w_down):
    n_bp, t_p, d = x_prompt.shape
    n_bs, t_s, _ = x_sample.shape
    depth = ffn1_norm_pre.shape[0]
    assert depth == 1 and d == D_MODEL
    keep = min(BAND_PAST, t_p)
    l = 0

    vec = lambda a: a[l].reshape(1, -1).astype(F32)
    mat = lambda a: a[l].astype(BF16)
    bd, cd, pw = _ssm_params(ssm_a_re[l], ssm_a_im[l], ssm_log_dt[l], ssm_b_re[l],
                             ssm_b_im[l], ssm_c_re[l], ssm_c_im[l])
    w = {
        "ffn1_norm_pre": vec(ffn1_norm_pre), "ffn1_norm_post": vec(ffn1_norm_post),
        "ffn1_w_gate": mat(ffn1_w_gate), "ffn1_w_up": mat(ffn1_w_up), "ffn1_w_down": mat(ffn1_w_down),
        "mix_norm_pre": vec(mix_norm_pre), "mix_norm_post": vec(mix_norm_post),
        "w_ssm": w_in[l, :, :COL_Q].astype(BF16),
        "w_qkvm": w_in[l, :, COL_Q:COL_GATE].astype(BF16),
        "w_kv": w_in[l, :, COL_Q + ATT_WIDTH:COL_Q + 3 * ATT_WIDTH].astype(BF16),
        "w_gate": w_in[l, :, COL_GATE:].astype(BF16),
        "ssm_bd": bd, "ssm_cd": cd, "ssm_pw": pw,
        "ssm_d": vec(ssm_d), "ssm_w_glu": mat(ssm_w_glu), "ssm_b_glu": vec(ssm_b_glu),
        "att_rel_bias": att_rel_bias[l].astype(F32),
        "w_branch_ssm": mat(w_branch_ssm), "w_branch_att": mat(w_branch_att),
        "w_branch_mem": mat(w_branch_mem), "w_out": mat(w_out),
        "ffn2_norm_pre": vec(ffn2_norm_pre), "ffn2_norm_post": vec(ffn2_norm_post),
        "ffn2_w_gate": mat(ffn2_w_gate), "ffn2_w_up": mat(ffn2_w_up), "ffn2_w_down": mat(ffn2_w_down),
    }

    w_mem = jnp.concatenate([w_mem_k[l], w_mem_v[l]], axis=1).astype(BF16)
    mkv = _norm_matmul(mem_prompt.reshape(n_bp * N_MEM, d), vec(mem_norm), w_mem, 512, 1024)
    mk_p = mkv[:, :MEM_WIDTH].reshape(n_bp, N_MEM, MEM_WIDTH)
    mv_p = mkv[:, MEM_WIDTH:].reshape(n_bp, N_MEM, MEM_WIDTH)
    zero_state = jnp.zeros((OCTETS, n_bp, OCT_STATE), F32)
    y_p, kv_p, sfin_p = _layer(x_prompt.reshape(n_bp * t_p, d), w, n_batch=n_bp, seq=t_p, s0=zero_state,
                               mem_k=mk_p, mem_v=mv_p, cache_k=None, cache_v=None, tm=512, keep=keep)
    sre_p, sim_p = _state_from_octets(sfin_p)

    n_cache = cache_att_k.shape[2]
    s0_s = _state_to_octets(state_ssm_re[l].astype(F32), state_ssm_im[l].astype(F32))
    y_s, kv_s, sfin_s = _layer(x_sample.reshape(n_bs * t_s, d), w, n_batch=n_bs, seq=t_s, s0=s0_s,
                               mem_k=cache_mem_k[l].reshape(n_bs, N_MEM, MEM_WIDTH),
                               mem_v=cache_mem_v[l].reshape(n_bs, N_MEM, MEM_WIDTH),
                               cache_k=cache_att_k[l].reshape(n_bs, n_cache, ATT_WIDTH),
                               cache_v=cache_att_v[l].reshape(n_bs, n_cache, ATT_WIDTH),
                               tm=n_bs * t_s, keep=t_s)
    sre_s, sim_s = _state_from_octets(sfin_s)

    heads = lambda a, nb, t: a.reshape(1, nb, t, ATT_HEADS, ATT_HEAD_DIM)
    memh = lambda a: a.reshape(1, n_bp, N_MEM, MEM_HEADS, MEM_HEAD_DIM)
    return (y_p.reshape(n_bp, t_p, d), y_s.reshape(n_bs, t_s, d),
            heads(kv_p[:, :ATT_WIDTH], n_bp, keep), heads(kv_p[:, ATT_WIDTH:], n_bp, keep),
            memh(mk_p), memh(mv_p), sre_p[None], sim_p[None],
            heads(kv_s[:, :ATT_WIDTH], n_bs, t_s), heads(kv_s[:, ATT_WIDTH:], n_bs, t_s),
            sre_s[None], sim_s[None])
```

```python
import functools
import math

import numpy as np
import jax
import jax.numpy as jnp
from jax import lax
from jax.experimental import pallas as pl
from jax.experimental.pallas import tpu as pltpu

F32 = jnp.float32
BF16 = jnp.bfloat16

D_MODEL = 2048
CHUNK = 64
BAND_PAST_CHUNKS = 8
BAND_PAST = BAND_PAST_CHUNKS * CHUNK
ATT_HEADS = 16
ATT_HEAD_DIM = 64
ATT_WIDTH = ATT_HEADS * ATT_HEAD_DIM
REL_CLIP = 128
SSM_GROUP = 16
SSM_WIDTH = 1024
SSM_GROUPS = SSM_WIDTH // SSM_GROUP
SSM_STATE = 64
N_MEM = 256
MEM_HEADS = 4
MEM_HEAD_DIM = 256
MEM_WIDTH = MEM_HEADS * MEM_HEAD_DIM
N_BRANCH = 3
EPS = 1e-6
NEG_INF = -1e30
PAST_LEN = 4096
LOG2E = math.log2(math.e)

COL_Q = SSM_WIDTH
COL_GATE = SSM_WIDTH + 3 * ATT_WIDTH + MEM_WIDTH

SSM_CHUNK = 16
OCTETS = 8
OCT_GROUPS = SSM_GROUPS // OCTETS
LANES = 128
MXU_DIM = 256
OCT_COLS = SSM_CHUNK * LANES
OCT_HALF = OCT_GROUPS * SSM_STATE
OCT_STATE = 2 * OCT_HALF

Q_TILE = 512
Q_SUB = 128
K_WIN = Q_SUB + BAND_PAST

MIB = 1024 * 1024


def _params(sem, vmem_mib):
    return pltpu.CompilerParams(dimension_semantics=sem, vmem_limit_bytes=vmem_mib * MIB)


def _dot(a, b):
    return jnp.dot(a, b, preferred_element_type=F32)


def _dot_nt(a, b, precision=None):
    return lax.dot_general(a, b, (((1,), (1,)), ((), ())), precision=precision,
                           preferred_element_type=F32)


def _rms(xf, g):
    y = xf * lax.rsqrt(jnp.mean(xf * xf, axis=-1, keepdims=True) + EPS)
    return y * g


FFN_SLICES = 8


def _ffn_kernel(hp_ref, hn_ref, gpre_ref, gpost_ref, wg_ref, wu_ref, wd_ref, *rest,
                n_tiles, emit_next):
    if emit_next:
        gnext_ref, out_ref, nxt_ref, *scratch = rest
    else:
        out_ref, *scratch = rest
    xn_refs, acc_refs = scratch[:2], scratch[2:]
    r = pl.program_id(0)
    f = pl.program_id(1)
    rs = hp_ref.shape[0]
    rows = pl.ds(pl.multiple_of(jnp.minimum(f, FFN_SLICES - 1) * rs, rs), rs)

    def pre_norm_slice(slot):
        xn_refs[slot][rows, :] = _rms(hn_ref[...], gpre_ref[...]).astype(BF16)

    def finish_slice(slot):
        hn = hp_ref[...] + 0.5 * _rms(acc_refs[slot][rows, :], gpost_ref[...])
        out_ref[...] = hn
        if emit_next:
            nxt_ref[...] = _rms(hn, gnext_ref[...]).astype(BF16)

    def matmul_chunk(slot):
        xn = xn_refs[slot][...]
        g = _dot(xn, wg_ref[...])
        u = _dot(xn, wu_ref[...])
        a = (g * jax.nn.sigmoid(g)) * u
        d = _dot(a.astype(BF16), wd_ref[...])
        acc_refs[slot][...] = jnp.where(f == 0, d, acc_refs[slot][...] + d)

    @pl.when((r == 0) & (f == 0))
    def _():
        acc_refs[1][...] = jnp.zeros_like(acc_refs[1])

    @pl.when(r == 0)
    def _():
        pre_norm_slice(0)

    for parity in range(2):
        @pl.when((r >= 1) & (r <= n_tiles) & (lax.rem(r, 2) == parity))
        def _():
            finish_slice(parity)
            pre_norm_slice(parity)
            matmul_chunk(1 - parity)

    @pl.when(r == n_tiles + 1)
    def _():
        finish_slice((n_tiles + 1) % 2)


def _ffn(h, g_pre, g_post, wg, wu, wd, g_next, tm, tf):
    emit_next = g_next is not None
    m, d = h.shape
    f_dim = wg.shape[1]
    nf = f_dim // tf
    n = m // tm
    rs = tm // FFN_SLICES
    assert m % tm == 0 and f_dim % tf == 0 and tm % FFN_SLICES == 0 and nf >= FFN_SLICES
    sl = lambda f: jnp.minimum(f, FFN_SLICES - 1)
    done = lambda r, f: (jnp.maximum(r - 2, 0) * FFN_SLICES + jnp.where(r >= 2, sl(f), 0), 0)
    ahead = lambda r, f: (jnp.minimum(r, n - 1) * FFN_SLICES + sl(f), 0)
    chunk = lambda r, f: jnp.where((r >= 1) & (r <= n), f, 0)
    vec = pl.BlockSpec((1, d), lambda r, f: (0, 0))
    in_specs = [pl.BlockSpec((rs, d), done), pl.BlockSpec((rs, d), ahead), vec, vec,
                pl.BlockSpec((d, tf), lambda r, f: (0, chunk(r, f))),
                pl.BlockSpec((d, tf), lambda r, f: (0, chunk(r, f))),
                pl.BlockSpec((tf, d), lambda r, f: (chunk(r, f), 0))]
    args = [h, h, g_pre, g_post, wg, wu, wd]
    out_shape = [jax.ShapeDtypeStruct((m, d), F32)]
    out_specs = [pl.BlockSpec((rs, d), done)]
    if emit_next:
        in_specs.append(vec)
        args.append(g_next)
        out_shape.append(jax.ShapeDtypeStruct((m, d), BF16))
        out_specs.append(pl.BlockSpec((rs, d), done))
    res = pl.pallas_call(
        functools.partial(_ffn_kernel, n_tiles=n, emit_next=emit_next),
        grid=(n + 2, nf),
        in_specs=in_specs,
        out_specs=out_specs,
        out_shape=out_shape,
        scratch_shapes=[pltpu.VMEM((tm, d), BF16), pltpu.VMEM((tm, d), BF16),
                        pltpu.VMEM((tm, d), F32), pltpu.VMEM((tm, d), F32)],
        compiler_params=_params(("arbitrary", "arbitrary"), 56),
        name="ffn",
    )(*args)
    return res if emit_next else res[0]


PROJ_COLS = 1024


def _mm_kernel(*refs, normed):
    if normed:
        x_ref, g_ref, w_ref, *o_refs = refs
        x = _rms(x_ref[...], g_ref[...]).astype(BF16)
    else:
        x_ref, w_ref, *o_refs = refs
        x = x_ref[...]
    col = 0
    for o_ref in o_refs:
        for j in range(0, o_ref.shape[1], PROJ_COLS):
            o_ref[:, j:j + PROJ_COLS] = _dot(x, w_ref[:, col + j:col + j + PROJ_COLS]).astype(o_ref.dtype)
        col += o_ref.shape[1]


def _matmul(x, w, out_dtype, tm, n_out=1, gain=None, m_out=None, row_map=None, name="proj"):
    m, k = x.shape
    n = w.shape[1] // n_out
    m_out = m if m_out is None else m_out
    row_map = (lambda i: i) if row_map is None else row_map
    tm = min(tm, m_out)
    assert m_out % tm == 0 and n % PROJ_COLS == 0 and n * n_out == w.shape[1]
    in_specs = [pl.BlockSpec((tm, k), lambda i: (row_map(i), 0))]
    args = [x]
    if gain is not None:
        in_specs.append(pl.BlockSpec((1, k), lambda i: (0, 0)))
        args.append(gain)
    in_specs.append(pl.BlockSpec((k, n * n_out), lambda i: (0, 0)))
    args.append(w)
    res = pl.pallas_call(
        functools.partial(_mm_kernel, normed=gain is not None),
        grid=(m_out // tm,),
        in_specs=in_specs,
        out_specs=[pl.BlockSpec((tm, n), lambda i: (i, 0))] * n_out,
        out_shape=[jax.ShapeDtypeStruct((m_out, n), out_dtype)] * n_out,
        compiler_params=_params(("parallel",), 56),
        name=name,
    )(*args)
    return res[0] if n_out == 1 else res


def _ssm_inproj_kernel(x_ref, w_ref, o_ref, nat_ref):
    r = _dot(x_ref[...], w_ref[...])
    tc = nat_ref.shape[1] // SSM_CHUNK
    for m in range(OCTETS):
        nat_ref[m] = r[:, m * LANES:(m + 1) * LANES]
    for j in range(SSM_CHUNK):
        for m in range(OCTETS):
            o_ref[m, :, j * LANES:(j + 1) * LANES] = nat_ref[m, pl.ds(j, tc, stride=SSM_CHUNK), :]


def _ssm_inproj(u, w_ssm, tm):
    n_tok = u.shape[0]
    assert n_tok % tm == 0 and tm % SSM_CHUNK == 0
    tc = tm // SSM_CHUNK
    return pl.pallas_call(
        _ssm_inproj_kernel,
        grid=(n_tok // tm,),
        in_specs=[pl.BlockSpec((tm, D_MODEL), lambda i: (i, 0)),
                  pl.BlockSpec((D_MODEL, SSM_WIDTH), lambda i: (0, 0))],
        out_specs=pl.BlockSpec((OCTETS, tc, OCT_COLS), lambda i: (0, i, 0)),
        out_shape=jax.ShapeDtypeStruct((OCTETS, n_tok // SSM_CHUNK, OCT_COLS), F32),
        scratch_shapes=[pltpu.VMEM((OCTETS, tm, LANES), F32)],
        compiler_params=_params(("parallel",), 48),
        name="ssm_inproj",
    )(u, w_ssm)


def _ssm_params(a_re, a_im, log_dt, b_re, b_im, c_re, c_im):
    dt = jnp.exp(log_dt)[:, None]
    mag = jnp.exp(a_re * dt)
    ab_re = mag * jnp.cos(a_im * dt)
    ab_im = mag * jnp.sin(a_im * dt)
    den = a_re * a_re + a_im * a_im
    n_re = ab_re - 1.0
    n_im = ab_im
    k_re = (n_re * a_re + n_im * a_im) / den
    k_im = (n_im * a_re - n_re * a_im) / den
    bb_re = k_re[..., None] * b_re - k_im[..., None] * b_im
    bb_im = k_re[..., None] * b_im + k_im[..., None] * b_re
    pr = [jnp.ones_like(ab_re)]
    pi = [jnp.zeros_like(ab_re)]
    for _ in range(SSM_CHUNK):
        pr.append(pr[-1] * ab_re - pi[-1] * ab_im)
        pi.append(pr[-2] * ab_im + pi[-1] * ab_re)
    n_pw = SSM_CHUNK + 1
    pw = jnp.concatenate([jnp.stack(pr).reshape(n_pw, OCTETS, OCT_HALF),
                          jnp.stack(pi).reshape(n_pw, OCTETS, OCT_HALF)], axis=2)
    pw = jnp.transpose(pw, (1, 0, 2))
    eye = jnp.eye(OCT_GROUPS, dtype=F32)

    def expand(x):
        x4 = x.reshape(OCTETS, OCT_GROUPS, x.shape[1], SSM_STATE)
        out = eye[None, :, None, :, None] * x4[:, :, :, None, :]
        return out.reshape(OCTETS, OCT_GROUPS * x.shape[1], OCT_HALF)

    bd = jnp.concatenate([expand(jnp.transpose(bb_re, (0, 2, 1))),
                          expand(jnp.transpose(bb_im, (0, 2, 1)))], axis=2)
    cd = jnp.concatenate([expand(c_re), expand(c_im)], axis=2)
    return bd, cd, pw


def _state_to_octets(s_re, s_im):
    b = s_re.shape[0]
    s = jnp.stack([s_re, s_im], 0).reshape(2, b, OCTETS, OCT_GROUPS, SSM_STATE)
    return jnp.transpose(s, (2, 1, 0, 3, 4)).reshape(OCTETS, b, OCT_STATE)


def _state_from_octets(s):
    b = s.shape[1]
    s = s.reshape(OCTETS, b, 2, OCT_GROUPS, SSM_STATE)
    s = jnp.transpose(s, (2, 1, 0, 3, 4)).reshape(2, b, SSM_GROUPS, SSM_STATE)
    return s[0], s[1]


def _ssm_build_operators(bd_ref, cd_ref, pw_ref, t8_ref, wus_ref, wso_ref):
    bd_re, bd_im = bd_ref[0, :, :OCT_HALF], bd_ref[0, :, OCT_HALF:]
    cd_re, cd_im = cd_ref[0, :, :OCT_HALF], cd_ref[0, :, OCT_HALF:]
    cdm = jnp.concatenate([cd_re, -cd_im], axis=1)
    blk = lambda i: slice(i * LANES, (i + 1) * LANES)
    for a in range(SSM_CHUNK // 2):
        t8_ref[blk(2 * a + 1), blk(2 * a)] = jnp.zeros((LANES, LANES), BF16)
    for k in range(SSM_CHUNK + 1):
        p_re = pw_ref[0, k:k + 1, :OCT_HALF]
        p_im = pw_ref[0, k:k + 1, OCT_HALF:]
        if k < SSM_CHUNK:
            e = jnp.concatenate([bd_re * p_re - bd_im * p_im, bd_re * p_im + bd_im * p_re], axis=1)
            wus_ref[blk(SSM_CHUNK - 1 - k), :] = e.astype(BF16)
            lag = _dot_nt(e, cdm, precision=lax.Precision.HIGHEST).astype(BF16)
            for j in range(SSM_CHUNK - k):
                t8_ref[blk(j), blk(j + k)] = lag
        if k >= 1:
            g = jnp.concatenate([cd_re * p_re - cd_im * p_im, -(cd_re * p_im + cd_im * p_re)], axis=1)
            wso_ref[blk(k - 1), :] = g.astype(BF16)


def _ssm_core_kernel(us_ref, s0_ref, bd_ref, cd_ref, pw_ref, y_ref, sfin_ref,
                     t8_ref, wus_ref, wso_ref, ds_ref, sp_ref, *, nb, n_chunks):
    @pl.when(pl.program_id(1) == 0)
    def _():
        _ssm_build_operators(bd_ref, cd_ref, pw_ref, t8_ref, wus_ref, wso_ref)

    ub = us_ref[0].astype(BF16)
    ds_ref[...] = _dot(ub, wus_ref[...])
    a_re = pw_ref[0, SSM_CHUNK:SSM_CHUNK + 1, :OCT_HALF]
    a_im = pw_ref[0, SSM_CHUNK:SSM_CHUNK + 1, OCT_HALF:]

    def advance(s, d):
        s_re, s_im = s[:, :OCT_HALF], s[:, OCT_HALF:]
        n_re = a_re * s_re - a_im * s_im + d[:, :OCT_HALF]
        n_im = a_re * s_im + a_im * s_re + d[:, OCT_HALF:]
        return jnp.concatenate([n_re, n_im], axis=1)

    if n_chunks == 1:
        s0 = s0_ref[0, 0]
        sp_ref[...] = s0
        sfin_ref[0, 0] = advance(s0, ds_ref[...])
    else:
        def body(c, carry):
            new = []
            for bl in range(nb):
                row = bl * n_chunks + c
                sp_ref[pl.ds(row, 1), :] = carry[bl]
                new.append(advance(carry[bl], ds_ref[pl.ds(row, 1), :]))
            return tuple(new)

        init = tuple(s0_ref[0, 0, bl:bl + 1, :] for bl in range(nb))
        fin = lax.fori_loop(0, n_chunks, body, init, unroll=2)
        for bl in range(nb):
            sfin_ref[0, 0, bl:bl + 1, :] = fin[bl]

    spb = sp_ref[...].astype(BF16)
    for nt in range(OCT_COLS // MXU_DIM):
        k_hi = (nt + 1) * MXU_DIM
        cs = slice(nt * MXU_DIM, k_hi)
        y_ref[0, :, cs] = _dot(ub[:, :k_hi], t8_ref[:k_hi, cs]) + _dot_nt(spb, wso_ref[cs, :])


def _ssm_core(us_r, s0, bd, cd, pw, n_batch, n_chunks, nb):
    nc = us_r.shape[1]
    rows = nb * n_chunks
    assert n_batch % nb == 0 and nc == n_batch * n_chunks
    nr = n_batch // nb
    s0 = s0.reshape(OCTETS, nr, nb, OCT_STATE)
    par = lambda rows_: pl.BlockSpec((1, rows_, OCT_STATE), lambda m, r: (m, 0, 0))
    y, sfin = pl.pallas_call(
        functools.partial(_ssm_core_kernel, nb=nb, n_chunks=n_chunks),
        grid=(OCTETS, nr),
        in_specs=[pl.BlockSpec((1, rows, OCT_COLS), lambda m, r: (m, r, 0)),
                  pl.BlockSpec((1, 1, nb, OCT_STATE), lambda m, r: (m, r, 0, 0)),
                  par(LANES), par(LANES), par(SSM_CHUNK + 1)],
        out_specs=[pl.BlockSpec((1, rows, OCT_COLS), lambda m, r: (m, r, 0)),
                   pl.BlockSpec((1, 1, nb, OCT_STATE), lambda m, r: (m, r, 0, 0))],
        out_shape=[jax.ShapeDtypeStruct((OCTETS, nc, OCT_COLS), F32),
                   jax.ShapeDtypeStruct((OCTETS, nr, nb, OCT_STATE), F32)],
        scratch_shapes=[pltpu.VMEM((OCT_COLS, OCT_COLS), BF16),
                        pltpu.VMEM((OCT_COLS, OCT_STATE), BF16),
                        pltpu.VMEM((OCT_COLS, OCT_STATE), BF16),
                        pltpu.VMEM((rows, OCT_STATE), F32),
                        pltpu.VMEM((rows, OCT_STATE), F32)],
        compiler_params=_params(("arbitrary", "arbitrary"), 56),
        name="ssm_core",
    )(us_r, s0, bd, cd, pw)
    return y, sfin.reshape(OCTETS, n_batch, OCT_STATE)


def _gelu_tanh(x):
    c = math.sqrt(2.0 / math.pi)
    return x * (0.5 * (1.0 + jnp.tanh(c * (x + 0.044715 * (x * x * x)))))


def _ssm_glu_kernel(y_ref, us_ref, d_ref, w_ref, b_ref, o_ref, nat_ref):
    tc = y_ref.shape[1]
    for i in range(SSM_CHUNK):
        cs = slice(i * LANES, (i + 1) * LANES)
        for m in range(OCTETS):
            d = d_ref[:, m * LANES:(m + 1) * LANES]
            nat_ref[m, pl.ds(i, tc, stride=SSM_CHUNK), :] = y_ref[m, :, cs] + d * us_ref[m, :, cs]
    yg = _gelu_tanh(jnp.concatenate([nat_ref[m] for m in range(OCTETS)], axis=1))
    z = _dot(yg.astype(BF16), w_ref[...]) + b_ref[...]
    o_ref[...] = (yg * jax.nn.sigmoid(z)).astype(o_ref.dtype)


def _ssm_glu(y_r, us_r, d, w_glu, b_glu, tc):
    nc = y_r.shape[1]
    assert nc % tc == 0
    oct_spec = pl.BlockSpec((OCTETS, tc, OCT_COLS), lambda c: (0, c, 0))
    vec = pl.BlockSpec((1, SSM_WIDTH), lambda c: (0, 0))
    return pl.pallas_call(
        _ssm_glu_kernel,
        grid=(nc // tc,),
        in_specs=[oct_spec, oct_spec, vec,
                  pl.BlockSpec((SSM_WIDTH, SSM_WIDTH), lambda c: (0, 0)), vec],
        out_specs=pl.BlockSpec((tc * SSM_CHUNK, SSM_WIDTH), lambda c: (c, 0)),
        out_shape=jax.ShapeDtypeStruct((nc * SSM_CHUNK, SSM_WIDTH), BF16),
        scratch_shapes=[pltpu.VMEM((OCTETS, tc * SSM_CHUNK, LANES), F32)],
        compiler_params=_params(("parallel",), 48),
        name="ssm_glu",
    )(y_r, us_r, d, w_glu, b_glu)


N_PAIRS = ATT_HEADS // 2
SOFTMAX_ROWS = 64


def _head_masks():
    lane = lax.broadcasted_iota(jnp.int32, (1, LANES), 1)
    return (lane < ATT_HEAD_DIM, lane >= ATT_HEAD_DIM)


def _pair_scores(q2, kw, bias2, masks):
    qq = jnp.concatenate([jnp.where(m, q2, jnp.zeros_like(q2)) for m in masks], axis=0)
    return _dot_nt(qq, kw) * (ATT_HEAD_DIM ** -0.5 * LOG2E) + bias2


def _softmax_parts(sc):
    mx = jnp.max(sc, axis=1, keepdims=True)
    p = jnp.exp2(sc - mx)
    l = jnp.sum(p, axis=1, keepdims=True)
    return p.astype(BF16), jnp.broadcast_to(1.0 / l, (sc.shape[0], LANES))


def _pair_output(p, rl, vw, masks):
    o2 = _dot(p, vw) * rl
    n_q = o2.shape[0] // 2
    return jnp.where(masks[0], o2[:n_q], o2[n_q:])


def _band_attn_kernel(q_ref, kp_ref, kc_ref, vp_ref, vc_ref, bias_ref, o_ref,
                      kw_ref, vw_ref, sc_ref, p_ref, rl_ref):
    kw_ref[0:Q_TILE] = kp_ref[...]
    kw_ref[Q_TILE:2 * Q_TILE] = kc_ref[...]
    vw_ref[0:Q_TILE] = vp_ref[...]
    vw_ref[Q_TILE:2 * Q_TILE] = vc_ref[...]
    masks = _head_masks()
    cs = lambda hp: slice(hp * LANES, (hp + 1) * LANES)

    def sub(s, carry):
        r0 = pl.multiple_of(s * Q_SUB, Q_SUB)
        rows = pl.ds(r0, Q_SUB)
        win = pl.ds(r0, K_WIN)
        for hp in range(N_PAIRS):
            sc_ref[hp] = _pair_scores(q_ref[rows, cs(hp)], kw_ref[win, cs(hp)], bias_ref[hp], masks)

        @pl.when(pl.program_id(1) == 0)
        def _():
            col = lax.broadcasted_iota(jnp.int32, (1, K_WIN), 1)
            extra = jnp.where(col + r0 < Q_TILE, NEG_INF, 0.0)
            for hp in range(N_PAIRS):
                sc_ref[hp] = sc_ref[hp] + extra

        for hp in range(N_PAIRS):
            for g0 in range(0, 2 * Q_SUB, SOFTMAX_ROWS):
                gs = slice(g0, g0 + SOFTMAX_ROWS)
                p_ref[hp, gs, :], rl_ref[hp, gs, :] = _softmax_parts(sc_ref[hp, gs, :])
        for hp in range(N_PAIRS):
            o = _pair_output(p_ref[hp], rl_ref[hp], vw_ref[win, cs(hp)], masks)
            o_ref[rows, cs(hp)] = o.astype(o_ref.dtype)
        return carry

    lax.fori_loop(0, Q_TILE // Q_SUB, sub, 0)


def _rel_bias_tile(rel_bias, n_q, n_k, offset, ok):
    r = np.arange(n_q - 1 + n_k)
    idx = np.clip(offset + n_q - 1 - r, -REL_CLIP, REL_CLIP) + REL_CLIP
    v = jnp.pad(rel_bias[:, idx] * LOG2E, ((0, 0), (0, 1)))
    w = n_q + n_k
    flat = jnp.tile(v, (1, n_q))[:, :n_q * (w - 1)]
    toep = flat.reshape(rel_bias.shape[0], n_q, w - 1)[:, :, n_q - 1:]
    return jnp.where(ok[None], toep, NEG_INF).reshape(N_PAIRS, 2 * n_q, n_k)


def _band_attention_prompt(qkvm, rel_bias, n_batch, seq):
    assert seq % Q_TILE == 0
    nt = seq // Q_TILE
    qc = np.arange(Q_SUB)[:, None] // CHUNK
    kc = np.arange(K_WIN)[None, :] // CHUNK
    bias = _rel_bias_tile(rel_bias, Q_SUB, K_WIN, BAND_PAST, (kc >= qc) & (kc <= qc + BAND_PAST_CHUNKS))
    blk = (Q_TILE, ATT_WIDTH)
    cur = lambda col: pl.BlockSpec(blk, lambda b, t: (b * nt + t, col))
    prev = lambda col: pl.BlockSpec(blk, lambda b, t: (b * nt + jnp.maximum(t - 1, 0), col))
    return pl.pallas_call(
        _band_attn_kernel,
        grid=(n_batch, nt),
        in_specs=[cur(0), prev(1), cur(1), prev(2), cur(2),
                  pl.BlockSpec((N_PAIRS, 2 * Q_SUB, K_WIN), lambda b, t: (0, 0, 0))],
        out_specs=pl.BlockSpec(blk, lambda b, t: (b * nt + t, 0)),
        out_shape=jax.ShapeDtypeStruct((n_batch * seq, ATT_WIDTH), BF16),
        scratch_shapes=[pltpu.VMEM((2 * Q_TILE, ATT_WIDTH), BF16),
                        pltpu.VMEM((2 * Q_TILE, ATT_WIDTH), BF16),
                        pltpu.VMEM((N_PAIRS, 2 * Q_SUB, K_WIN), F32),
                        pltpu.VMEM((N_PAIRS, 2 * Q_SUB, K_WIN), BF16),
                        pltpu.VMEM((N_PAIRS, 2 * Q_SUB, LANES), F32)],
        compiler_params=_params(("parallel", "arbitrary"), 48),
        name="band_attn",
    )(qkvm, qkvm, qkvm, qkvm, qkvm, bias)


def _band_attn_sample_kernel(q_ref, kn_ref, vn_ref, ck_ref, cv_ref, bias_ref, o_ref, kw_ref, vw_ref,
                             *, n_cache, n_new):
    kw_ref[...] = jnp.zeros_like(kw_ref)
    vw_ref[...] = jnp.zeros_like(vw_ref)
    kw_ref[0:n_cache] = ck_ref[0].astype(BF16)
    vw_ref[0:n_cache] = cv_ref[0].astype(BF16)
    kw_ref[n_cache:n_cache + n_new] = kn_ref[...]
    vw_ref[n_cache:n_cache + n_new] = vn_ref[...]
    masks = _head_masks()
    for hp in range(N_PAIRS):
        cs = slice(hp * LANES, (hp + 1) * LANES)
        p, rl = _softmax_parts(_pair_scores(q_ref[:, cs], kw_ref[:, cs], bias_ref[hp], masks))
        o_ref[:, cs] = _pair_output(p, rl, vw_ref[:, cs], masks).astype(o_ref.dtype)


def _band_attention_sample(qkvm, rel_bias, cache_k, cache_v, n_batch, n_new):
    n_cache = cache_k.shape[1]
    n_keys = -(-(n_cache + n_new) // LANES) * LANES
    q_pos = PAST_LEN + np.arange(n_new)[:, None]
    j = np.arange(n_keys)[None, :]
    k_pos = PAST_LEN - n_cache + j
    ok = ((j < n_cache + n_new) & (k_pos >= 0) & (k_pos // CHUNK <= q_pos // CHUNK)
          & (k_pos // CHUNK >= q_pos // CHUNK - BAND_PAST_CHUNKS))
    bias = _rel_bias_tile(rel_bias, n_new, n_keys, n_cache, ok)
    new = lambda col: pl.BlockSpec((n_new, ATT_WIDTH), lambda b: (b, col))
    cache = pl.BlockSpec((1, n_cache, ATT_WIDTH), lambda b: (b, 0, 0))
    return pl.pallas_call(
        functools.partial(_band_attn_sample_kernel, n_cache=n_cache, n_new=n_new),
        grid=(n_batch,),
        in_specs=[new(0), new(1), new(2), cache, cache,
                  pl.BlockSpec((N_PAIRS, 2 * n_new, n_keys), lambda b: (0, 0, 0))],
        out_specs=pl.BlockSpec((n_new, ATT_WIDTH), lambda b: (b, 0)),
        out_shape=jax.ShapeDtypeStruct((n_batch * n_new, ATT_WIDTH), BF16),
        scratch_shapes=[pltpu.VMEM((n_keys, ATT_WIDTH), BF16),
                        pltpu.VMEM((n_keys, ATT_WIDTH), BF16)],
        compiler_params=_params(("parallel",), 48),
        name="band_attn_sample",
    )(qkvm, qkvm, qkvm, cache_k, cache_v, bias)


def _mem_attn_kernel(q_ref, k_ref, v_ref, o_ref):
    k = k_ref[0].astype(BF16)
    v = v_ref[0].astype(BF16)
    for h in range(MEM_HEADS):
        cs = slice(h * MEM_HEAD_DIM, (h + 1) * MEM_HEAD_DIM)
        sc = _dot_nt(q_ref[:, cs], k[:, cs]) * (MEM_HEAD_DIM ** -0.5 * LOG2E)
        mx = jnp.max(sc, axis=1, keepdims=True)
        p = jnp.exp2(sc - mx)
        l = jnp.sum(p, axis=1, keepdims=True)
        o = _dot(p.astype(BF16), v[:, cs]) * (1.0 / l)
        o_ref[:, cs] = o.astype(o_ref.dtype)


def _memory_attention(qkvm, mem_k, mem_v, n_batch, seq, tq):
    nt = seq // tq
    mem = pl.BlockSpec((1, N_MEM, MEM_WIDTH), lambda b, t: (b, 0, 0))
    return pl.pallas_call(
        _mem_attn_kernel,
        grid=(n_batch, nt),
        in_specs=[pl.BlockSpec((tq, MEM_WIDTH), lambda b, t: (b * nt + t, 3)), mem, mem],
        out_specs=pl.BlockSpec((tq, MEM_WIDTH), lambda b, t: (b * nt + t, 0)),
        out_shape=jax.ShapeDtypeStruct((n_batch * seq, MEM_WIDTH), BF16),
        compiler_params=_params(("parallel", "arbitrary"), 48),
        name="mem_attn",
    )(qkvm, mem_k, mem_v)


def _gate_merge_kernel(u_ref, os_ref, oa_ref, om_ref, wgs_ref, wga_ref, wgm_ref,
                       wbs_ref, wba_ref, wbm_ref, out_ref):
    u = u_ref[...]

    def branch(o_ref, wg_ref, wb_ref):
        return jax.nn.sigmoid(_dot(u, wg_ref[...])) * _dot(o_ref[...], wb_ref[...])

    merged = (branch(os_ref, wgs_ref, wbs_ref) + branch(oa_ref, wga_ref, wba_ref)
              + branch(om_ref, wgm_ref, wbm_ref))
    out_ref[...] = merged.astype(out_ref.dtype)


def _gate_merge(u, o_s, o_a, o_m, w_gate, wb_s, wb_a, wb_m, tm, tn):
    m, d = u.shape
    gate = lambda b: pl.BlockSpec((d, tn), lambda i, n: (0, b * (d // tn) + n))
    wb = pl.BlockSpec((SSM_WIDTH, tn), lambda i, n: (0, n))
    ob = pl.BlockSpec((tm, SSM_WIDTH), lambda i, n: (i, 0))
    return pl.pallas_call(
        _gate_merge_kernel,
        grid=(m // tm, d // tn),
        in_specs=[pl.BlockSpec((tm, d), lambda i, n: (i, 0)), ob, ob, ob,
                  gate(0), gate(1), gate(2), wb, wb, wb],
        out_specs=pl.BlockSpec((tm, tn), lambda i, n: (i, n)),
        out_shape=jax.ShapeDtypeStruct((m, d), BF16),
        compiler_params=_params(("parallel", "arbitrary"), 56),
        name="gate_merge",
    )(u, o_s, o_a, o_m, w_gate, w_gate, w_gate, wb_s, wb_a, wb_m)


def _out_proj_kernel(x_ref, w_ref, h_ref, g_ref, o_ref):
    o_ref[...] = h_ref[...] + _rms(_dot(x_ref[...], w_ref[...]), g_ref[...])


def _out_proj(x, w_out, h, g_post, tm):
    m, d = h.shape
    row = lambda: pl.BlockSpec((tm, d), lambda i: (i, 0))
    return pl.pallas_call(
        _out_proj_kernel,
        grid=(m // tm,),
        in_specs=[row(), pl.BlockSpec((d, d), lambda i: (0, 0)), row(),
                  pl.BlockSpec((1, d), lambda i: (0, 0))],
        out_specs=row(),
        out_shape=jax.ShapeDtypeStruct((m, d), F32),
        compiler_params=_params(("parallel",), 56),
        name="out_proj",
    )(x, w_out, h, g_post)


def _layer(x, w, *, n_batch, seq, s0, mem_k, mem_v, cache_k, cache_v, tm, keep):
    n_tok = n_batch * seq
    n_chunks = seq // SSM_CHUNK
    big_tm = min(1024, n_tok)
    h1, u = _ffn(x, w["ffn1_norm_pre"], w["ffn1_norm_post"], w["ffn1_w_gate"], w["ffn1_w_up"],
                 w["ffn1_w_down"], w["mix_norm_pre"], big_tm, 512)

    qkvm = _matmul(u, w["w_qkvm"], BF16, tm, name="qkvm")
    if keep == seq:
        kv = _matmul(u, w["w_kv"], F32, tm, n_out=2, name="kv_tail")
    else:
        n_keep_blocks = seq // keep
        kv = _matmul(u, w["w_kv"], F32, keep, n_out=2, m_out=n_batch * keep,
                     row_map=lambda i: i * n_keep_blocks + (n_keep_blocks - 1), name="kv_tail")

    us_r = _ssm_inproj(u, w["w_ssm"], tm)
    nb = 2 if (n_chunks > 1 and n_batch % 2 == 0) else (1 if n_chunks > 1 else n_batch)
    y_r, s_fin = _ssm_core(us_r, s0, w["ssm_bd"], w["ssm_cd"], w["ssm_pw"], n_batch, n_chunks, nb)
    o_s = _ssm_glu(y_r, us_r, w["ssm_d"], w["ssm_w_glu"], w["ssm_b_glu"], min(32, n_tok // SSM_CHUNK))

    if cache_k is None:
        o_a = _band_attention_prompt(qkvm, w["att_rel_bias"], n_batch, seq)
    else:
        o_a = _band_attention_sample(qkvm, w["att_rel_bias"], cache_k, cache_v, n_batch, seq)
    o_m = _memory_attention(qkvm, mem_k, mem_v, n_batch, seq, min(512, seq))

    merged = _gate_merge(u, o_s, o_a, o_m, w["w_gate"], w["w_branch_ssm"], w["w_branch_att"],
                         w["w_branch_mem"], big_tm, 512)
    h2 = _out_proj(merged, w["w_out"], h1, w["mix_norm_post"], tm)
    y = _ffn(h2, w["ffn2_norm_pre"], w["ffn2_norm_post"], w["ffn2_w_gate"], w["ffn2_w_up"],
             w["ffn2_w_down"], None, big_tm, 512)
    return y, kv, s_fin


def kernel(x_prompt, x_sample, mem_prompt, cache_att_k, cache_att_v, cache_mem_k, cache_mem_v, state_ssm_re, state_ssm_im, ffn1_norm_pre, ffn1_norm_post, ffn1_w_gate, ffn1_w_up, ffn1_w_down, mix_norm_pre, mix_norm_post, w_in, ssm_a_re, ssm_a_im, ssm_log_dt, ssm_b_re, ssm_b_im, ssm_c_re, ssm_c_im, ssm_d, ssm_w_glu, ssm_b_glu, att_rel_bias, mem_norm, w_mem_k, w_mem_v, w_branch_ssm, w_branch_att, w_branch_mem, w_out, ffn2_norm_pre, ffn2_norm_post, ffn2_w_gate, ffn2_w_up, ffn2_w_down):
    n_bp, t_p, d = x_prompt.shape
    n_bs, t_s, _ = x_sample.shape
    depth = ffn1_norm_pre.shape[0]
    assert depth == 1 and d == D_MODEL
    keep = min(BAND_PAST, t_p)
    l = 0

    vec = lambda a: a[l].reshape(1, -1).astype(F32)
    mat = lambda a: a[l].astype(BF16)
    bd, cd, pw = _ssm_params(ssm_a_re[l], ssm_a_im[l], ssm_log_dt[l], ssm_b_re[l],
                             ssm_b_im[l], ssm_c_re[l], ssm_c_im[l])
    w = {
        "ffn1_norm_pre": vec(ffn1_norm_pre), "ffn1_norm_post": vec(ffn1_norm_post),
        "ffn1_w_gate": mat(ffn1_w_gate), "ffn1_w_up": mat(ffn1_w_up), "ffn1_w_down": mat(ffn1_w_down),
        "mix_norm_pre": vec(mix_norm_pre), "mix_norm_post": vec(mix_norm_post),
        "w_ssm": w_in[l, :, :COL_Q].astype(BF16),
        "w_qkvm": w_in[l, :, COL_Q:COL_GATE].astype(BF16),
        "w_kv": w_in[l, :, COL_Q + ATT_WIDTH:COL_Q + 3 * ATT_WIDTH].astype(BF16),
        "w_gate": w_in[l, :, COL_GATE:].astype(BF16),
        "ssm_bd": bd, "ssm_cd": cd, "ssm_pw": pw,
        "ssm_d": vec(ssm_d), "ssm_w_glu": mat(ssm_w_glu), "ssm_b_glu": vec(ssm_b_glu),
        "att_rel_bias": att_rel_bias[l].astype(F32),
        "w_branch_ssm": mat(w_branch_ssm), "w_branch_att": mat(w_branch_att),
        "w_branch_mem": mat(w_branch_mem), "w_out": mat(w_out),
        "ffn2_norm_pre": vec(ffn2_norm_pre), "ffn2_norm_post": vec(ffn2_norm_post),
        "ffn2_w_gate": mat(ffn2_w_gate), "ffn2_w_up": mat(ffn2_w_up), "ffn2_w_down": mat(ffn2_w_down),
    }

    w_mem = jnp.concatenate([w_mem_k[l], w_mem_v[l]], axis=1).astype(BF16)
    mk_p, mv_p = _matmul(mem_prompt.reshape(n_bp * N_MEM, d), w_mem, F32, 512, n_out=2,
                         gain=vec(mem_norm), name="memory_kv")
    mk_p = mk_p.reshape(n_bp, N_MEM, MEM_WIDTH)
    mv_p = mv_p.reshape(n_bp, N_MEM, MEM_WIDTH)
    zero_state = jnp.zeros((OCTETS, n_bp, OCT_STATE), F32)
    y_p, kv_p, sfin_p = _layer(x_prompt.reshape(n_bp * t_p, d), w, n_batch=n_bp, seq=t_p, s0=zero_state,
                               mem_k=mk_p, mem_v=mv_p, cache_k=None, cache_v=None, tm=512, keep=keep)
    sre_p, sim_p = _state_from_octets(sfin_p)

    n_cache = cache_att_k.shape[2]
    s0_s = _state_to_octets(state_ssm_re[l].astype(F32), state_ssm_im[l].astype(F32))
    y_s, kv_s, sfin_s = _layer(x_sample.reshape(n_bs * t_s, d), w, n_batch=n_bs, seq=t_s, s0=s0_s,
                               mem_k=cache_mem_k[l].reshape(n_bs, N_MEM, MEM_WIDTH),
                               mem_v=cache_mem_v[l].reshape(n_bs, N_MEM, MEM_WIDTH),
                               cache_k=cache_att_k[l].reshape(n_bs, n_cache, ATT_WIDTH),
                               cache_v=cache_att_v[l].reshape(n_bs, n_cache, ATT_WIDTH),
                               tm=n_bs * t_s, keep=t_s)
    sre_s, sim_s = _state_from_octets(sfin_s)

    heads = lambda a, nb, t: a.reshape(1, nb, t, ATT_HEADS, ATT_HEAD_DIM)
    memh = lambda a: a.reshape(1, n_bp, N_MEM, MEM_HEADS, MEM_HEAD_DIM)
    return (y_p.reshape(n_bp, t_p, d), y_s.reshape(n_bs, t_s, d),
            heads(kv_p[0], n_bp, keep), heads(kv_p[1], n_bp, keep),
            memh(mk_p), memh(mv_p), sre_p[None], sim_p[None],
            heads(kv_s[0], n_bs, t_s), heads(kv_s[1], n_bs, t_s),
            sre_s[None], sim_s[None])
```

```python
import functools
import math

import numpy as np
import jax
import jax.numpy as jnp
from jax import lax
from jax.experimental import pallas as pl
from jax.experimental.pallas import tpu as pltpu

F32 = jnp.float32
BF16 = jnp.bfloat16

D_MODEL = 2048
CHUNK = 64
BAND_PAST_CHUNKS = 8
BAND_PAST = BAND_PAST_CHUNKS * CHUNK
ATT_HEADS = 16
ATT_HEAD_DIM = 64
ATT_WIDTH = ATT_HEADS * ATT_HEAD_DIM
REL_CLIP = 128
SSM_GROUP = 16
SSM_WIDTH = 1024
SSM_GROUPS = SSM_WIDTH // SSM_GROUP
SSM_STATE = 64
N_MEM = 256
MEM_HEADS = 4
MEM_HEAD_DIM = 256
MEM_WIDTH = MEM_HEADS * MEM_HEAD_DIM
N_BRANCH = 3
EPS = 1e-6
NEG_INF = -1e30
PAST_LEN = 4096
LOG2E = math.log2(math.e)

COL_Q = SSM_WIDTH
COL_GATE = SSM_WIDTH + 3 * ATT_WIDTH + MEM_WIDTH

SSM_CHUNK = 16
OCTETS = 8
OCT_GROUPS = SSM_GROUPS // OCTETS
LANES = 128
MXU_DIM = 256
OCT_COLS = SSM_CHUNK * LANES
OCT_HALF = OCT_GROUPS * SSM_STATE
OCT_STATE = 2 * OCT_HALF

Q_TILE = 512
Q_SUB = 128
K_WIN = Q_SUB + BAND_PAST

MIB = 1024 * 1024


def _params(sem, vmem_mib):
    return pltpu.CompilerParams(dimension_semantics=sem, vmem_limit_bytes=vmem_mib * MIB)


def _dot(a, b):
    return jnp.dot(a, b, preferred_element_type=F32)


def _dot_nt(a, b, precision=None):
    return lax.dot_general(a, b, (((1,), (1,)), ((), ())), precision=precision,
                           preferred_element_type=F32)


def _rms(xf, g):
    y = xf * lax.rsqrt(jnp.mean(xf * xf, axis=-1, keepdims=True) + EPS)
    return y * g


FFN_SLICES = 8
FFN_CHUNK = 512


def _ffn_kernel(hp_ref, hn_ref, gpre_ref, gpost_ref, wg_ref, wu_ref, wd_ref, *rest,
                n_tiles, emit_next):
    if emit_next:
        gnext_ref, out_ref, nxt_ref, *scratch = rest
    else:
        out_ref, *scratch = rest
    xn_refs, acc_refs = scratch[:2], scratch[2:]
    r = pl.program_id(0)
    f = pl.program_id(1)
    rs = hp_ref.shape[0]
    rows = pl.ds(pl.multiple_of(jnp.minimum(f, FFN_SLICES - 1) * rs, rs), rs)

    def pre_norm_slice(slot):
        xn_refs[slot][rows, :] = _rms(hn_ref[...], gpre_ref[...]).astype(BF16)

    def finish_slice(slot):
        hn = hp_ref[...] + 0.5 * _rms(acc_refs[slot][rows, :], gpost_ref[...])
        out_ref[...] = hn
        if emit_next:
            nxt_ref[...] = _rms(hn, gnext_ref[...]).astype(BF16)

    def matmul_chunk(slot):
        xn = xn_refs[slot][...]
        g = _dot(xn, wg_ref[...])
        u = _dot(xn, wu_ref[...])
        a = (g * jax.nn.sigmoid(g)) * u
        d = _dot(a.astype(BF16), wd_ref[...])
        acc_refs[slot][...] = jnp.where(f == 0, d, acc_refs[slot][...] + d)

    @pl.when((r == 0) & (f == 0))
    def _():
        acc_refs[1][...] = jnp.zeros_like(acc_refs[1])

    @pl.when(r == 0)
    def _():
        pre_norm_slice(0)

    for parity in range(2):
        @pl.when((r >= 1) & (r <= n_tiles) & (lax.rem(r, 2) == parity))
        def _():
            finish_slice(parity)
            pre_norm_slice(parity)
            matmul_chunk(1 - parity)

    @pl.when(r == n_tiles + 1)
    def _():
        finish_slice((n_tiles + 1) % 2)


def _ffn(h, g_pre, g_post, wg, wu, wd, g_next, tm):
    emit_next = g_next is not None
    m, d = h.shape
    nf, _, tf = wg.shape
    n = m // tm
    rs = tm // FFN_SLICES
    assert m % tm == 0 and wd.shape[0] == nf * tf and tm % FFN_SLICES == 0 and nf >= FFN_SLICES
    sl = lambda f: jnp.minimum(f, FFN_SLICES - 1)
    done = lambda r, f: (jnp.maximum(r - 2, 0) * FFN_SLICES + jnp.where(r >= 2, sl(f), 0), 0)
    ahead = lambda r, f: (jnp.minimum(r, n - 1) * FFN_SLICES + sl(f), 0)
    chunk = lambda r, f: jnp.where((r >= 1) & (r <= n), f, 0)
    vec = pl.BlockSpec((1, d), lambda r, f: (0, 0))
    in_specs = [pl.BlockSpec((rs, d), done), pl.BlockSpec((rs, d), ahead), vec, vec,
                pl.BlockSpec((None, d, tf), lambda r, f: (chunk(r, f), 0, 0)),
                pl.BlockSpec((None, d, tf), lambda r, f: (chunk(r, f), 0, 0)),
                pl.BlockSpec((tf, d), lambda r, f: (chunk(r, f), 0))]
    args = [h, h, g_pre, g_post, wg, wu, wd]
    out_shape = [jax.ShapeDtypeStruct((m, d), F32)]
    out_specs = [pl.BlockSpec((rs, d), done)]
    if emit_next:
        in_specs.append(vec)
        args.append(g_next)
        out_shape.append(jax.ShapeDtypeStruct((m, d), BF16))
        out_specs.append(pl.BlockSpec((rs, d), done))
    res = pl.pallas_call(
        functools.partial(_ffn_kernel, n_tiles=n, emit_next=emit_next),
        grid=(n + 2, nf),
        in_specs=in_specs,
        out_specs=out_specs,
        out_shape=out_shape,
        scratch_shapes=[pltpu.VMEM((tm, d), BF16), pltpu.VMEM((tm, d), BF16),
                        pltpu.VMEM((tm, d), F32), pltpu.VMEM((tm, d), F32)],
        compiler_params=_params(("arbitrary", "arbitrary"), 56),
        name="ffn",
    )(*args)
    return res if emit_next else res[0]


PROJ_COLS = 1024


def _mm_kernel(*refs, normed):
    if normed:
        x_ref, g_ref, w_ref, *o_refs = refs
        x = _rms(x_ref[...], g_ref[...]).astype(BF16)
    else:
        x_ref, w_ref, *o_refs = refs
        x = x_ref[...]
    col = 0
    for o_ref in o_refs:
        for j in range(0, o_ref.shape[1], PROJ_COLS):
            o_ref[:, j:j + PROJ_COLS] = _dot(x, w_ref[:, col + j:col + j + PROJ_COLS]).astype(o_ref.dtype)
        col += o_ref.shape[1]


def _matmul(x, w, out_dtype, tm, n_out=1, gain=None, m_out=None, row_map=None, name="proj"):
    m, k = x.shape
    n = w.shape[1] // n_out
    m_out = m if m_out is None else m_out
    row_map = (lambda i: i) if row_map is None else row_map
    tm = min(tm, m_out)
    assert m_out % tm == 0 and n % PROJ_COLS == 0 and n * n_out == w.shape[1]
    in_specs = [pl.BlockSpec((tm, k), lambda i: (row_map(i), 0))]
    args = [x]
    if gain is not None:
        in_specs.append(pl.BlockSpec((1, k), lambda i: (0, 0)))
        args.append(gain)
    in_specs.append(pl.BlockSpec((k, n * n_out), lambda i: (0, 0)))
    args.append(w)
    res = pl.pallas_call(
        functools.partial(_mm_kernel, normed=gain is not None),
        grid=(m_out // tm,),
        in_specs=in_specs,
        out_specs=[pl.BlockSpec((tm, n), lambda i: (i, 0))] * n_out,
        out_shape=[jax.ShapeDtypeStruct((m_out, n), out_dtype)] * n_out,
        compiler_params=_params(("parallel",), 56),
        name=name,
    )(*args)
    return res[0] if n_out == 1 else res


def _ssm_inproj_kernel(x_ref, w_ref, o_ref, nat_ref):
    r = _dot(x_ref[...], w_ref[...])
    tc = nat_ref.shape[1] // SSM_CHUNK
    for m in range(OCTETS):
        nat_ref[m] = r[:, m * LANES:(m + 1) * LANES]
    for j in range(SSM_CHUNK):
        for m in range(OCTETS):
            o_ref[m, :, j * LANES:(j + 1) * LANES] = nat_ref[m, pl.ds(j, tc, stride=SSM_CHUNK), :]


def _ssm_inproj(u, w_ssm, tm):
    n_tok = u.shape[0]
    assert n_tok % tm == 0 and tm % SSM_CHUNK == 0
    tc = tm // SSM_CHUNK
    return pl.pallas_call(
        _ssm_inproj_kernel,
        grid=(n_tok // tm,),
        in_specs=[pl.BlockSpec((tm, D_MODEL), lambda i: (i, 0)),
                  pl.BlockSpec((D_MODEL, SSM_WIDTH), lambda i: (0, 0))],
        out_specs=pl.BlockSpec((OCTETS, tc, OCT_COLS), lambda i: (0, i, 0)),
        out_shape=jax.ShapeDtypeStruct((OCTETS, n_tok // SSM_CHUNK, OCT_COLS), F32),
        scratch_shapes=[pltpu.VMEM((OCTETS, tm, LANES), F32)],
        compiler_params=_params(("parallel",), 48),
        name="ssm_inproj",
    )(u, w_ssm)


def _ssm_params(a_re, a_im, log_dt, b_re, b_im, c_re, c_im):
    dt = jnp.exp(log_dt)[:, None]
    mag = jnp.exp(a_re * dt)
    ab_re = mag * jnp.cos(a_im * dt)
    ab_im = mag * jnp.sin(a_im * dt)
    den = a_re * a_re + a_im * a_im
    n_re = ab_re - 1.0
    n_im = ab_im
    k_re = (n_re * a_re + n_im * a_im) / den
    k_im = (n_im * a_re - n_re * a_im) / den
    bb_re = k_re[..., None] * b_re - k_im[..., None] * b_im
    bb_im = k_re[..., None] * b_im + k_im[..., None] * b_re
    pr = [jnp.ones_like(ab_re)]
    pi = [jnp.zeros_like(ab_re)]
    for _ in range(SSM_CHUNK):
        pr.append(pr[-1] * ab_re - pi[-1] * ab_im)
        pi.append(pr[-2] * ab_im + pi[-1] * ab_re)
    n_pw = SSM_CHUNK + 1
    pw = jnp.concatenate([jnp.stack(pr).reshape(n_pw, OCTETS, OCT_HALF),
                          jnp.stack(pi).reshape(n_pw, OCTETS, OCT_HALF)], axis=2)
    pw = jnp.transpose(pw, (1, 0, 2))
    eye = jnp.eye(OCT_GROUPS, dtype=F32)

    def expand(x):
        x4 = x.reshape(OCTETS, OCT_GROUPS, x.shape[1], SSM_STATE)
        out = eye[None, :, None, :, None] * x4[:, :, :, None, :]
        return out.reshape(OCTETS, OCT_GROUPS * x.shape[1], OCT_HALF)

    bd = jnp.concatenate([expand(jnp.transpose(bb_re, (0, 2, 1))),
                          expand(jnp.transpose(bb_im, (0, 2, 1)))], axis=2)
    cd = jnp.concatenate([expand(c_re), expand(c_im)], axis=2)
    return bd, cd, pw


def _state_to_octets(s_re, s_im):
    b = s_re.shape[0]
    s = jnp.stack([s_re, s_im], 0).reshape(2, b, OCTETS, OCT_GROUPS, SSM_STATE)
    return jnp.transpose(s, (2, 1, 0, 3, 4)).reshape(OCTETS, b, OCT_STATE)


def _state_from_octets(s):
    b = s.shape[1]
    s = s.reshape(OCTETS, b, 2, OCT_GROUPS, SSM_STATE)
    s = jnp.transpose(s, (2, 1, 0, 3, 4)).reshape(2, b, SSM_GROUPS, SSM_STATE)
    return s[0], s[1]


def _split_bf16(x):
    hi = x.astype(BF16)
    return hi, (x - hi.astype(F32)).astype(BF16)


def _ssm_build_operators(bd_ref, cd_ref, pw_ref, t8_ref, wus_ref, wso_ref):
    bd_re, bd_im = bd_ref[0, :, :OCT_HALF], bd_ref[0, :, OCT_HALF:]
    cd_re, cd_im = cd_ref[0, :, :OCT_HALF], cd_ref[0, :, OCT_HALF:]
    c_hi, c_lo = _split_bf16(jnp.concatenate([cd_re, -cd_im], axis=1))
    blk = lambda i: slice(i * LANES, (i + 1) * LANES)
    for a in range(SSM_CHUNK // 2):
        t8_ref[blk(2 * a + 1), blk(2 * a)] = jnp.zeros((LANES, LANES), BF16)
    for k in range(SSM_CHUNK + 1):
        p_re = pw_ref[0, k:k + 1, :OCT_HALF]
        p_im = pw_ref[0, k:k + 1, OCT_HALF:]
        if k < SSM_CHUNK:
            e = jnp.concatenate([bd_re * p_re - bd_im * p_im, bd_re * p_im + bd_im * p_re], axis=1)
            wus_ref[blk(SSM_CHUNK - 1 - k), :] = e.astype(BF16)
            e_hi, e_lo = _split_bf16(e)
            lag = (_dot_nt(e_hi, c_hi) + _dot_nt(e_hi, c_lo) + _dot_nt(e_lo, c_hi)).astype(BF16)
            for j in range(SSM_CHUNK - k):
                t8_ref[blk(j), blk(j + k)] = lag
        if k >= 1:
            g = jnp.concatenate([cd_re * p_re - cd_im * p_im, -(cd_re * p_im + cd_im * p_re)], axis=1)
            wso_ref[blk(k - 1), :] = g.astype(BF16)


def _ssm_core_kernel(us_ref, s0_ref, bd_ref, cd_ref, pw_ref, y_ref, sfin_ref,
                     t8_ref, wus_ref, wso_ref, ds_ref, sp_ref, *, nb, n_chunks):
    @pl.when(pl.program_id(1) == 0)
    def _():
        _ssm_build_operators(bd_ref, cd_ref, pw_ref, t8_ref, wus_ref, wso_ref)

    ub = us_ref[0].astype(BF16)
    ds_ref[...] = _dot(ub, wus_ref[...])
    a_re = pw_ref[0, SSM_CHUNK:SSM_CHUNK + 1, :OCT_HALF]
    a_im = pw_ref[0, SSM_CHUNK:SSM_CHUNK + 1, OCT_HALF:]

    def advance(s, d):
        s_re, s_im = s[:, :OCT_HALF], s[:, OCT_HALF:]
        n_re = a_re * s_re - a_im * s_im + d[:, :OCT_HALF]
        n_im = a_re * s_im + a_im * s_re + d[:, OCT_HALF:]
        return jnp.concatenate([n_re, n_im], axis=1)

    if n_chunks == 1:
        s0 = s0_ref[0, 0]
        sp_ref[...] = s0
        sfin_ref[0, 0] = advance(s0, ds_ref[...])
    else:
        def body(c, carry):
            new = []
            for bl in range(nb):
                row = bl * n_chunks + c
                sp_ref[pl.ds(row, 1), :] = carry[bl]
                new.append(advance(carry[bl], ds_ref[pl.ds(row, 1), :]))
            return tuple(new)

        init = tuple(s0_ref[0, 0, bl:bl + 1, :] for bl in range(nb))
        fin = lax.fori_loop(0, n_chunks, body, init, unroll=2)
        for bl in range(nb):
            sfin_ref[0, 0, bl:bl + 1, :] = fin[bl]

    spb = sp_ref[...].astype(BF16)
    for nt in range(OCT_COLS // MXU_DIM):
        k_hi = (nt + 1) * MXU_DIM
        cs = slice(nt * MXU_DIM, k_hi)
        y_ref[0, :, cs] = _dot(ub[:, :k_hi], t8_ref[:k_hi, cs]) + _dot_nt(spb, wso_ref[cs, :])


def _ssm_core(us_r, s0, bd, cd, pw, n_batch, n_chunks, nb):
    nc = us_r.shape[1]
    rows = nb * n_chunks
    assert n_batch % nb == 0 and nc == n_batch * n_chunks
    nr = n_batch // nb
    s0 = s0.reshape(OCTETS, nr, nb, OCT_STATE)
    par = lambda rows_: pl.BlockSpec((1, rows_, OCT_STATE), lambda m, r: (m, 0, 0))
    y, sfin = pl.pallas_call(
        functools.partial(_ssm_core_kernel, nb=nb, n_chunks=n_chunks),
        grid=(OCTETS, nr),
        in_specs=[pl.BlockSpec((1, rows, OCT_COLS), lambda m, r: (m, r, 0)),
                  pl.BlockSpec((1, 1, nb, OCT_STATE), lambda m, r: (m, r, 0, 0)),
                  par(LANES), par(LANES), par(SSM_CHUNK + 1)],
        out_specs=[pl.BlockSpec((1, rows, OCT_COLS), lambda m, r: (m, r, 0)),
                   pl.BlockSpec((1, 1, nb, OCT_STATE), lambda m, r: (m, r, 0, 0))],
        out_shape=[jax.ShapeDtypeStruct((OCTETS, nc, OCT_COLS), F32),
                   jax.ShapeDtypeStruct((OCTETS, nr, nb, OCT_STATE), F32)],
        scratch_shapes=[pltpu.VMEM((OCT_COLS, OCT_COLS), BF16),
                        pltpu.VMEM((OCT_COLS, OCT_STATE), BF16),
                        pltpu.VMEM((OCT_COLS, OCT_STATE), BF16),
                        pltpu.VMEM((rows, OCT_STATE), F32),
                        pltpu.VMEM((rows, OCT_STATE), F32)],
        compiler_params=_params(("arbitrary", "arbitrary"), 56),
        name="ssm_core",
    )(us_r, s0, bd, cd, pw)
    return y, sfin.reshape(OCTETS, n_batch, OCT_STATE)


def _gelu_tanh(x):
    c = math.sqrt(2.0 / math.pi)
    return x * (0.5 * (1.0 + jnp.tanh(c * (x + 0.044715 * (x * x * x)))))


def _ssm_glu_kernel(y_ref, us_ref, d_ref, w_ref, b_ref, o_ref, nat_ref):
    tc = y_ref.shape[1]
    for i in range(SSM_CHUNK):
        cs = slice(i * LANES, (i + 1) * LANES)
        for m in range(OCTETS):
            d = d_ref[:, m * LANES:(m + 1) * LANES]
            nat_ref[m, pl.ds(i, tc, stride=SSM_CHUNK), :] = y_ref[m, :, cs] + d * us_ref[m, :, cs]
    yg = _gelu_tanh(jnp.concatenate([nat_ref[m] for m in range(OCTETS)], axis=1))
    z = _dot(yg.astype(BF16), w_ref[...]) + b_ref[...]
    o_ref[...] = (yg * jax.nn.sigmoid(z)).astype(o_ref.dtype)


def _ssm_glu(y_r, us_r, d, w_glu, b_glu, tc):
    nc = y_r.shape[1]
    assert nc % tc == 0
    oct_spec = pl.BlockSpec((OCTETS, tc, OCT_COLS), lambda c: (0, c, 0))
    vec = pl.BlockSpec((1, SSM_WIDTH), lambda c: (0, 0))
    return pl.pallas_call(
        _ssm_glu_kernel,
        grid=(nc // tc,),
        in_specs=[oct_spec, oct_spec, vec,
                  pl.BlockSpec((SSM_WIDTH, SSM_WIDTH), lambda c: (0, 0)), vec],
        out_specs=pl.BlockSpec((tc * SSM_CHUNK, SSM_WIDTH), lambda c: (c, 0)),
        out_shape=jax.ShapeDtypeStruct((nc * SSM_CHUNK, SSM_WIDTH), BF16),
        scratch_shapes=[pltpu.VMEM((OCTETS, tc * SSM_CHUNK, LANES), F32)],
        compiler_params=_params(("parallel",), 48),
        name="ssm_glu",
    )(y_r, us_r, d, w_glu, b_glu)


N_PAIRS = ATT_HEADS // 2
SOFTMAX_ROWS = 64


def _head_masks():
    lane = lax.broadcasted_iota(jnp.int32, (1, LANES), 1)
    return (lane < ATT_HEAD_DIM, lane >= ATT_HEAD_DIM)


def _pair_scores(q2, kw, bias2, masks):
    qq = jnp.concatenate([jnp.where(m, q2, jnp.zeros_like(q2)) for m in masks], axis=0)
    return _dot_nt(qq, kw) * (ATT_HEAD_DIM ** -0.5 * LOG2E) + bias2


def _softmax_parts(sc):
    mx = jnp.max(sc, axis=1, keepdims=True)
    p = jnp.exp2(sc - mx)
    l = jnp.sum(p, axis=1, keepdims=True)
    return p.astype(BF16), jnp.broadcast_to(1.0 / l, (sc.shape[0], LANES))


def _pair_output(p, rl, vw, masks):
    o2 = _dot(p, vw) * rl
    n_q = o2.shape[0] // 2
    return jnp.where(masks[0], o2[:n_q], o2[n_q:])


def _band_attn_kernel(q_ref, kp_ref, kc_ref, vp_ref, vc_ref, bias_ref, o_ref,
                      kw_ref, vw_ref, sc_ref, p_ref, rl_ref):
    kw_ref[0:Q_TILE] = kp_ref[...]
    kw_ref[Q_TILE:2 * Q_TILE] = kc_ref[...]
    vw_ref[0:Q_TILE] = vp_ref[...]
    vw_ref[Q_TILE:2 * Q_TILE] = vc_ref[...]
    masks = _head_masks()
    cs = lambda hp: slice(hp * LANES, (hp + 1) * LANES)

    def sub(s, carry):
        r0 = pl.multiple_of(s * Q_SUB, Q_SUB)
        rows = pl.ds(r0, Q_SUB)
        win = pl.ds(r0, K_WIN)
        for hp in range(N_PAIRS):
            sc_ref[hp] = _pair_scores(q_ref[rows, cs(hp)], kw_ref[win, cs(hp)], bias_ref[hp], masks)

        @pl.when(pl.program_id(1) == 0)
        def _():
            col = lax.broadcasted_iota(jnp.int32, (1, K_WIN), 1)
            extra = jnp.where(col + r0 < Q_TILE, NEG_INF, 0.0)
            for hp in range(N_PAIRS):
                sc_ref[hp] = sc_ref[hp] + extra

        for hp in range(N_PAIRS):
            for g0 in range(0, 2 * Q_SUB, SOFTMAX_ROWS):
                gs = slice(g0, g0 + SOFTMAX_ROWS)
                p_ref[hp, gs, :], rl_ref[hp, gs, :] = _softmax_parts(sc_ref[hp, gs, :])
        for hp in range(N_PAIRS):
            o = _pair_output(p_ref[hp], rl_ref[hp], vw_ref[win, cs(hp)], masks)
            o_ref[rows, cs(hp)] = o.astype(o_ref.dtype)
        return carry

    lax.fori_loop(0, Q_TILE // Q_SUB, sub, 0)


def _rel_bias_tile(rel_bias, n_q, n_k, offset, ok):
    r = np.arange(n_q - 1 + n_k)
    idx = np.clip(offset + n_q - 1 - r, -REL_CLIP, REL_CLIP) + REL_CLIP
    v = jnp.pad(rel_bias[:, idx] * LOG2E, ((0, 0), (0, 1)))
    w = n_q + n_k
    flat = jnp.tile(v, (1, n_q))[:, :n_q * (w - 1)]
    toep = flat.reshape(rel_bias.shape[0], n_q, w - 1)[:, :, n_q - 1:]
    return jnp.where(ok[None], toep, NEG_INF).reshape(N_PAIRS, 2 * n_q, n_k)


def _band_attention_prompt(qkvm, rel_bias, n_batch, seq):
    assert seq % Q_TILE == 0
    nt = seq // Q_TILE
    qc = np.arange(Q_SUB)[:, None] // CHUNK
    kc = np.arange(K_WIN)[None, :] // CHUNK
    bias = _rel_bias_tile(rel_bias, Q_SUB, K_WIN, BAND_PAST, (kc >= qc) & (kc <= qc + BAND_PAST_CHUNKS))
    blk = (Q_TILE, ATT_WIDTH)
    cur = lambda col: pl.BlockSpec(blk, lambda b, t: (b * nt + t, col))
    prev = lambda col: pl.BlockSpec(blk, lambda b, t: (b * nt + jnp.maximum(t - 1, 0), col))
    return pl.pallas_call(
        _band_attn_kernel,
        grid=(n_batch, nt),
        in_specs=[cur(0), prev(1), cur(1), prev(2), cur(2),
                  pl.BlockSpec((N_PAIRS, 2 * Q_SUB, K_WIN), lambda b, t: (0, 0, 0))],
        out_specs=pl.BlockSpec(blk, lambda b, t: (b * nt + t, 0)),
        out_shape=jax.ShapeDtypeStruct((n_batch * seq, ATT_WIDTH), BF16),
        scratch_shapes=[pltpu.VMEM((2 * Q_TILE, ATT_WIDTH), BF16),
                        pltpu.VMEM((2 * Q_TILE, ATT_WIDTH), BF16),
                        pltpu.VMEM((N_PAIRS, 2 * Q_SUB, K_WIN), F32),
                        pltpu.VMEM((N_PAIRS, 2 * Q_SUB, K_WIN), BF16),
                        pltpu.VMEM((N_PAIRS, 2 * Q_SUB, LANES), F32)],
        compiler_params=_params(("parallel", "arbitrary"), 48),
        name="band_attn",
    )(qkvm, qkvm, qkvm, qkvm, qkvm, bias)


def _band_attn_sample_kernel(q_ref, kn_ref, vn_ref, ck_ref, cv_ref, bias_ref, o_ref, kw_ref, vw_ref,
                             *, n_cache, n_new):
    kw_ref[...] = jnp.zeros_like(kw_ref)
    vw_ref[...] = jnp.zeros_like(vw_ref)
    kw_ref[0:n_cache] = ck_ref[0].astype(BF16)
    vw_ref[0:n_cache] = cv_ref[0].astype(BF16)
    kw_ref[n_cache:n_cache + n_new] = kn_ref[...]
    vw_ref[n_cache:n_cache + n_new] = vn_ref[...]
    masks = _head_masks()
    for hp in range(N_PAIRS):
        cs = slice(hp * LANES, (hp + 1) * LANES)
        p, rl = _softmax_parts(_pair_scores(q_ref[:, cs], kw_ref[:, cs], bias_ref[hp], masks))
        o_ref[:, cs] = _pair_output(p, rl, vw_ref[:, cs], masks).astype(o_ref.dtype)


def _band_attention_sample(qkvm, rel_bias, cache_k, cache_v, n_batch, n_new):
    n_cache = cache_k.shape[1]
    n_keys = -(-(n_cache + n_new) // LANES) * LANES
    q_pos = PAST_LEN + np.arange(n_new)[:, None]
    j = np.arange(n_keys)[None, :]
    k_pos = PAST_LEN - n_cache + j
    ok = ((j < n_cache + n_new) & (k_pos >= 0) & (k_pos // CHUNK <= q_pos // CHUNK)
          & (k_pos // CHUNK >= q_pos // CHUNK - BAND_PAST_CHUNKS))
    bias = _rel_bias_tile(rel_bias, n_new, n_keys, n_cache, ok)
    new = lambda col: pl.BlockSpec((n_new, ATT_WIDTH), lambda b: (b, col))
    cache = pl.BlockSpec((1, n_cache, ATT_WIDTH), lambda b: (b, 0, 0))
    return pl.pallas_call(
        functools.partial(_band_attn_sample_kernel, n_cache=n_cache, n_new=n_new),
        grid=(n_batch,),
        in_specs=[new(0), new(1), new(2), cache, cache,
                  pl.BlockSpec((N_PAIRS, 2 * n_new, n_keys), lambda b: (0, 0, 0))],
        out_specs=pl.BlockSpec((n_new, ATT_WIDTH), lambda b: (b, 0)),
        out_shape=jax.ShapeDtypeStruct((n_batch * n_new, ATT_WIDTH), BF16),
        scratch_shapes=[pltpu.VMEM((n_keys, ATT_WIDTH), BF16),
                        pltpu.VMEM((n_keys, ATT_WIDTH), BF16)],
        compiler_params=_params(("parallel",), 48),
        name="band_attn_sample",
    )(qkvm, qkvm, qkvm, cache_k, cache_v, bias)


def _mem_attn_kernel(q_ref, k_ref, v_ref, o_ref):
    k = k_ref[0].astype(BF16)
    v = v_ref[0].astype(BF16)
    for h in range(MEM_HEADS):
        cs = slice(h * MEM_HEAD_DIM, (h + 1) * MEM_HEAD_DIM)
        sc = _dot_nt(q_ref[:, cs], k[:, cs]) * (MEM_HEAD_DIM ** -0.5 * LOG2E)
        mx = jnp.max(sc, axis=1, keepdims=True)
        p = jnp.exp2(sc - mx)
        l = jnp.sum(p, axis=1, keepdims=True)
        o = _dot(p.astype(BF16), v[:, cs]) * (1.0 / l)
        o_ref[:, cs] = o.astype(o_ref.dtype)


def _memory_attention(qkvm, mem_k, mem_v, n_batch, seq, tq):
    nt = seq // tq
    mem = pl.BlockSpec((1, N_MEM, MEM_WIDTH), lambda b, t: (b, 0, 0))
    return pl.pallas_call(
        _mem_attn_kernel,
        grid=(n_batch, nt),
        in_specs=[pl.BlockSpec((tq, MEM_WIDTH), lambda b, t: (b * nt + t, 3)), mem, mem],
        out_specs=pl.BlockSpec((tq, MEM_WIDTH), lambda b, t: (b * nt + t, 0)),
        out_shape=jax.ShapeDtypeStruct((n_batch * seq, MEM_WIDTH), BF16),
        compiler_params=_params(("parallel", "arbitrary"), 48),
        name="mem_attn",
    )(qkvm, mem_k, mem_v)


def _gate_merge_kernel(u_ref, os_ref, oa_ref, om_ref, wgs_ref, wga_ref, wgm_ref,
                       wbs_ref, wba_ref, wbm_ref, out_ref):
    u = u_ref[...]

    def branch(o_ref, wg_ref, wb_ref):
        return jax.nn.sigmoid(_dot(u, wg_ref[...])) * _dot(o_ref[...], wb_ref[...])

    merged = (branch(os_ref, wgs_ref, wbs_ref) + branch(oa_ref, wga_ref, wba_ref)
              + branch(om_ref, wgm_ref, wbm_ref))
    out_ref[...] = merged.astype(out_ref.dtype)


def _gate_merge(u, o_s, o_a, o_m, w_gate, wb_s, wb_a, wb_m, tm, tn):
    m, d = u.shape
    gate = lambda b: pl.BlockSpec((d, tn), lambda i, n: (0, b * (d // tn) + n))
    wb = pl.BlockSpec((SSM_WIDTH, tn), lambda i, n: (0, n))
    ob = pl.BlockSpec((tm, SSM_WIDTH), lambda i, n: (i, 0))
    return pl.pallas_call(
        _gate_merge_kernel,
        grid=(m // tm, d // tn),
        in_specs=[pl.BlockSpec((tm, d), lambda i, n: (i, 0)), ob, ob, ob,
                  gate(0), gate(1), gate(2), wb, wb, wb],
        out_specs=pl.BlockSpec((tm, tn), lambda i, n: (i, n)),
        out_shape=jax.ShapeDtypeStruct((m, d), BF16),
        compiler_params=_params(("parallel", "arbitrary"), 56),
        name="gate_merge",
    )(u, o_s, o_a, o_m, w_gate, w_gate, w_gate, wb_s, wb_a, wb_m)


def _out_proj_kernel(x_ref, w_ref, h_ref, g_ref, o_ref):
    o_ref[...] = h_ref[...] + _rms(_dot(x_ref[...], w_ref[...]), g_ref[...])


def _out_proj(x, w_out, h, g_post, tm):
    m, d = h.shape
    row = lambda: pl.BlockSpec((tm, d), lambda i: (i, 0))
    return pl.pallas_call(
        _out_proj_kernel,
        grid=(m // tm,),
        in_specs=[row(), pl.BlockSpec((d, d), lambda i: (0, 0)), row(),
                  pl.BlockSpec((1, d), lambda i: (0, 0))],
        out_specs=row(),
        out_shape=jax.ShapeDtypeStruct((m, d), F32),
        compiler_params=_params(("parallel",), 56),
        name="out_proj",
    )(x, w_out, h, g_post)


def _layer(x, w, *, n_batch, seq, s0, mem_k, mem_v, cache_k, cache_v, tm, keep):
    n_tok = n_batch * seq
    n_chunks = seq // SSM_CHUNK
    big_tm = min(1024, n_tok)
    h1, u = _ffn(x, w["ffn1_norm_pre"], w["ffn1_norm_post"], w["ffn1_w_gate"], w["ffn1_w_up"],
                 w["ffn1_w_down"], w["mix_norm_pre"], big_tm)

    qkvm = _matmul(u, w["w_qkvm"], BF16, tm, name="qkvm")
    if keep == seq:
        kv = _matmul(u, w["w_kv"], F32, tm, n_out=2, name="kv_tail")
    else:
        n_keep_blocks = seq // keep
        kv = _matmul(u, w["w_kv"], F32, keep, n_out=2, m_out=n_batch * keep,
                     row_map=lambda i: i * n_keep_blocks + (n_keep_blocks - 1), name="kv_tail")

    us_r = _ssm_inproj(u, w["w_ssm"], tm)
    nb = 2 if (n_chunks > 1 and n_batch % 2 == 0) else (1 if n_chunks > 1 else n_batch)
    y_r, s_fin = _ssm_core(us_r, s0, w["ssm_bd"], w["ssm_cd"], w["ssm_pw"], n_batch, n_chunks, nb)
    o_s = _ssm_glu(y_r, us_r, w["ssm_d"], w["ssm_w_glu"], w["ssm_b_glu"], min(32, n_tok // SSM_CHUNK))

    if cache_k is None:
        o_a = _band_attention_prompt(qkvm, w["att_rel_bias"], n_batch, seq)
    else:
        o_a = _band_attention_sample(qkvm, w["att_rel_bias"], cache_k, cache_v, n_batch, seq)
    o_m = _memory_attention(qkvm, mem_k, mem_v, n_batch, seq, min(512, seq))

    merged = _gate_merge(u, o_s, o_a, o_m, w["w_gate"], w["w_branch_ssm"], w["w_branch_att"],
                         w["w_branch_mem"], big_tm, 512)
    h2 = _out_proj(merged, w["w_out"], h1, w["mix_norm_post"], tm)
    y = _ffn(h2, w["ffn2_norm_pre"], w["ffn2_norm_post"], w["ffn2_w_gate"], w["ffn2_w_up"],
             w["ffn2_w_down"], None, big_tm)
    return y, kv, s_fin


def kernel(x_prompt, x_sample, mem_prompt, cache_att_k, cache_att_v, cache_mem_k, cache_mem_v, state_ssm_re, state_ssm_im, ffn1_norm_pre, ffn1_norm_post, ffn1_w_gate, ffn1_w_up, ffn1_w_down, mix_norm_pre, mix_norm_post, w_in, ssm_a_re, ssm_a_im, ssm_log_dt, ssm_b_re, ssm_b_im, ssm_c_re, ssm_c_im, ssm_d, ssm_w_glu, ssm_b_glu, att_rel_bias, mem_norm, w_mem_k, w_mem_v, w_branch_ssm, w_branch_att, w_branch_mem, w_out, ffn2_norm_pre, ffn2_norm_post, ffn2_w_gate, ffn2_w_up, ffn2_w_down):
    n_bp, t_p, d = x_prompt.shape
    n_bs, t_s, _ = x_sample.shape
    depth = ffn1_norm_pre.shape[0]
    assert depth == 1 and d == D_MODEL
    keep = min(BAND_PAST, t_p)
    l = 0

    vec = lambda a: a[l].reshape(1, -1).astype(F32)
    mat = lambda a: a[l].astype(BF16)
    chunked = lambda a: jnp.transpose(a[l].astype(BF16).reshape(d, -1, FFN_CHUNK), (1, 0, 2))
    bd, cd, pw = _ssm_params(ssm_a_re[l], ssm_a_im[l], ssm_log_dt[l], ssm_b_re[l],
                             ssm_b_im[l], ssm_c_re[l], ssm_c_im[l])
    w = {
        "ffn1_norm_pre": vec(ffn1_norm_pre), "ffn1_norm_post": vec(ffn1_norm_post),
        "ffn1_w_gate": chunked(ffn1_w_gate), "ffn1_w_up": chunked(ffn1_w_up),
        "ffn1_w_down": mat(ffn1_w_down),
        "mix_norm_pre": vec(mix_norm_pre), "mix_norm_post": vec(mix_norm_post),
        "w_ssm": w_in[l, :, :COL_Q].astype(BF16),
        "w_qkvm": w_in[l, :, COL_Q:COL_GATE].astype(BF16),
        "w_kv": w_in[l, :, COL_Q + ATT_WIDTH:COL_Q + 3 * ATT_WIDTH].astype(BF16),
        "w_gate": w_in[l, :, COL_GATE:].astype(BF16),
        "ssm_bd": bd, "ssm_cd": cd, "ssm_pw": pw,
        "ssm_d": vec(ssm_d), "ssm_w_glu": mat(ssm_w_glu), "ssm_b_glu": vec(ssm_b_glu),
        "att_rel_bias": att_rel_bias[l].astype(F32),
        "w_branch_ssm": mat(w_branch_ssm), "w_branch_att": mat(w_branch_att),
        "w_branch_mem": mat(w_branch_mem), "w_out": mat(w_out),
        "ffn2_norm_pre": vec(ffn2_norm_pre), "ffn2_norm_post": vec(ffn2_norm_post),
        "ffn2_w_gate": chunked(ffn2_w_gate), "ffn2_w_up": chunked(ffn2_w_up),
        "ffn2_w_down": mat(ffn2_w_down),
    }

    w_mem = jnp.concatenate([w_mem_k[l], w_mem_v[l]], axis=1).astype(BF16)
    mk_p, mv_p = _matmul(mem_prompt.reshape(n_bp * N_MEM, d), w_mem, F32, 512, n_out=2,
                         gain=vec(mem_norm), name="memory_kv")
    mk_p = mk_p.reshape(n_bp, N_MEM, MEM_WIDTH)
    mv_p = mv_p.reshape(n_bp, N_MEM, MEM_WIDTH)
    zero_state = jnp.zeros((OCTETS, n_bp, OCT_STATE), F32)
    y_p, kv_p, sfin_p = _layer(x_prompt.reshape(n_bp * t_p, d), w, n_batch=n_bp, seq=t_p, s0=zero_state,
                               mem_k=mk_p, mem_v=mv_p, cache_k=None, cache_v=None, tm=512, keep=keep)
    sre_p, sim_p = _state_from_octets(sfin_p)

    n_cache = cache_att_k.shape[2]
    s0_s = _state_to_octets(state_ssm_re[l].astype(F32), state_ssm_im[l].astype(F32))
    y_s, kv_s, sfin_s = _layer(x_sample.reshape(n_bs * t_s, d), w, n_batch=n_bs, seq=t_s, s0=s0_s,
                               mem_k=cache_mem_k[l].reshape(n_bs, N_MEM, MEM_WIDTH),
                               mem_v=cache_mem_v[l].reshape(n_bs, N_MEM, MEM_WIDTH),
                               cache_k=cache_att_k[l].reshape(n_bs, n_cache, ATT_WIDTH),
                               cache_v=cache_att_v[l].reshape(n_bs, n_cache, ATT_WIDTH),
                               tm=n_bs * t_s, keep=t_s)
    sre_s, sim_s = _state_from_octets(sfin_s)

    heads = lambda a, nb, t: a.reshape(1, nb, t, ATT_HEADS, ATT_HEAD_DIM)
    memh = lambda a: a.reshape(1, n_bp, N_MEM, MEM_HEADS, MEM_HEAD_DIM)
    return (y_p.reshape(n_bp, t_p, d), y_s.reshape(n_bs, t_s, d),
            heads(kv_p[0], n_bp, keep), heads(kv_p[1], n_bp, keep),
            memh(mk_p), memh(mv_p), sre_p[None], sim_p[None],
            heads(kv_s[0], n_bs, t_s), heads(kv_s[1], n_bs, t_s),
            sre_s[None], sim_s[None])
```

```python
import functools
import math

import numpy as np
import jax
import jax.numpy as jnp
from jax import lax
from jax.experimental import pallas as pl
from jax.experimental.pallas import tpu as pltpu

F32 = jnp.float32
BF16 = jnp.bfloat16

D_MODEL = 2048
CHUNK = 64
BAND_PAST_CHUNKS = 8
BAND_PAST = BAND_PAST_CHUNKS * CHUNK
ATT_HEADS = 16
ATT_HEAD_DIM = 64
ATT_WIDTH = ATT_HEADS * ATT_HEAD_DIM
REL_CLIP = 128
SSM_GROUP = 16
SSM_WIDTH = 1024
SSM_GROUPS = SSM_WIDTH // SSM_GROUP
SSM_STATE = 64
N_MEM = 256
MEM_HEADS = 4
MEM_HEAD_DIM = 256
MEM_WIDTH = MEM_HEADS * MEM_HEAD_DIM
N_BRANCH = 3
EPS = 1e-6
NEG_INF = -1e30
PAST_LEN = 4096
LOG2E = math.log2(math.e)

COL_Q = SSM_WIDTH
COL_GATE = SSM_WIDTH + 3 * ATT_WIDTH + MEM_WIDTH

SSM_CHUNK = 16
OCTETS = 8
OCT_GROUPS = SSM_GROUPS // OCTETS
LANES = 128
MXU_DIM = 256
OCT_COLS = SSM_CHUNK * LANES
OCT_HALF = OCT_GROUPS * SSM_STATE
OCT_STATE = 2 * OCT_HALF

Q_TILE = 512
Q_SUB = 128
K_WIN = Q_SUB + BAND_PAST

MIB = 1024 * 1024


def _params(sem, vmem_mib):
    return pltpu.CompilerParams(dimension_semantics=sem, vmem_limit_bytes=vmem_mib * MIB)


def _dot(a, b):
    return jnp.dot(a, b, preferred_element_type=F32)


def _dot_nt(a, b, precision=None):
    return lax.dot_general(a, b, (((1,), (1,)), ((), ())), precision=precision,
                           preferred_element_type=F32)


def _rms(xf, g):
    y = xf * lax.rsqrt(jnp.mean(xf * xf, axis=-1, keepdims=True) + EPS)
    return y * g


CAST_BLOCK_BYTES = 8 * MIB


def _cast_kernel(x_ref, o_ref):
    o_ref[...] = x_ref[...].astype(o_ref.dtype)


def _cast_bf16(w):
    rows, cols = w.shape
    tr = rows
    while tr * cols * 4 > CAST_BLOCK_BYTES and tr % 32 == 0:
        tr //= 2
    return pl.pallas_call(
        _cast_kernel,
        grid=(rows // tr,),
        in_specs=[pl.BlockSpec((tr, cols), lambda i: (i, 0))],
        out_specs=pl.BlockSpec((tr, cols), lambda i: (i, 0)),
        out_shape=jax.ShapeDtypeStruct((rows, cols), BF16),
        compiler_params=_params(("parallel",), 32),
        name="cast_bf16",
    )(w)


FFN_SLICES = 8


def _ffn_kernel(hp_ref, hn_ref, gpre_ref, gpost_ref, wg_ref, wu_ref, wd_ref, *rest,
                n_tiles, emit_next):
    if emit_next:
        gnext_ref, out_ref, nxt_ref, *scratch = rest
    else:
        out_ref, *scratch = rest
    xn_refs, acc_refs = scratch[:2], scratch[2:]
    r = pl.program_id(0)
    f = pl.program_id(1)
    rs = hp_ref.shape[0]
    rows = pl.ds(pl.multiple_of(jnp.minimum(f, FFN_SLICES - 1) * rs, rs), rs)

    def pre_norm_slice(slot):
        xn_refs[slot][rows, :] = _rms(hn_ref[...], gpre_ref[...]).astype(BF16)

    def finish_slice(slot):
        hn = hp_ref[...] + 0.5 * _rms(acc_refs[slot][rows, :], gpost_ref[...])
        out_ref[...] = hn
        if emit_next:
            nxt_ref[...] = _rms(hn, gnext_ref[...]).astype(BF16)

    def matmul_chunk(slot):
        xn = xn_refs[slot][...]
        g = _dot(xn, wg_ref[...])
        u = _dot(xn, wu_ref[...])
        a = (g * jax.nn.sigmoid(g)) * u
        d = _dot(a.astype(BF16), wd_ref[...])
        acc_refs[slot][...] = jnp.where(f == 0, d, acc_refs[slot][...] + d)

    @pl.when((r == 0) & (f == 0))
    def _():
        acc_refs[1][...] = jnp.zeros_like(acc_refs[1])

    @pl.when(r == 0)
    def _():
        pre_norm_slice(0)

    for parity in range(2):
        @pl.when((r >= 1) & (r <= n_tiles) & (lax.rem(r, 2) == parity))
        def _():
            finish_slice(parity)
            pre_norm_slice(parity)
            matmul_chunk(1 - parity)

    @pl.when(r == n_tiles + 1)
    def _():
        finish_slice((n_tiles + 1) % 2)


def _ffn(h, g_pre, g_post, wg, wu, wd, g_next, tm, tf):
    emit_next = g_next is not None
    m, d = h.shape
    f_dim = wg.shape[1]
    nf = f_dim // tf
    n = m // tm
    rs = tm // FFN_SLICES
    assert m % tm == 0 and f_dim % tf == 0 and tm % FFN_SLICES == 0 and nf >= FFN_SLICES
    sl = lambda f: jnp.minimum(f, FFN_SLICES - 1)
    done = lambda r, f: (jnp.maximum(r - 2, 0) * FFN_SLICES + jnp.where(r >= 2, sl(f), 0), 0)
    ahead = lambda r, f: (jnp.minimum(r, n - 1) * FFN_SLICES + sl(f), 0)
    chunk = lambda r, f: jnp.where((r >= 1) & (r <= n), f, 0)
    vec = pl.BlockSpec((1, d), lambda r, f: (0, 0))
    in_specs = [pl.BlockSpec((rs, d), done), pl.BlockSpec((rs, d), ahead), vec, vec,
                pl.BlockSpec((d, tf), lambda r, f: (0, chunk(r, f))),
                pl.BlockSpec((d, tf), lambda r, f: (0, chunk(r, f))),
                pl.BlockSpec((tf, d), lambda r, f: (chunk(r, f), 0))]
    args = [h, h, g_pre, g_post, wg, wu, wd]
    out_shape = [jax.ShapeDtypeStruct((m, d), F32)]
    out_specs = [pl.BlockSpec((rs, d), done)]
    if emit_next:
        in_specs.append(vec)
        args.append(g_next)
        out_shape.append(jax.ShapeDtypeStruct((m, d), BF16))
        out_specs.append(pl.BlockSpec((rs, d), done))
    res = pl.pallas_call(
        functools.partial(_ffn_kernel, n_tiles=n, emit_next=emit_next),
        grid=(n + 2, nf),
        in_specs=in_specs,
        out_specs=out_specs,
        out_shape=out_shape,
        scratch_shapes=[pltpu.VMEM((tm, d), BF16), pltpu.VMEM((tm, d), BF16),
                        pltpu.VMEM((tm, d), F32), pltpu.VMEM((tm, d), F32)],
        compiler_params=_params(("arbitrary", "arbitrary"), 56),
        name="ffn",
    )(*args)
    return res if emit_next else res[0]


PROJ_COLS = 1024


def _mm_kernel(*refs, normed):
    if normed:
        x_ref, g_ref, w_ref, *o_refs = refs
        x = _rms(x_ref[...], g_ref[...]).astype(BF16)
    else:
        x_ref, w_ref, *o_refs = refs
        x = x_ref[...]
    col = 0
    for o_ref in o_refs:
        for j in range(0, o_ref.shape[1], PROJ_COLS):
            o_ref[:, j:j + PROJ_COLS] = _dot(x, w_ref[:, col + j:col + j + PROJ_COLS]).astype(o_ref.dtype)
        col += o_ref.shape[1]


def _matmul(x, w, out_dtype, tm, n_out=1, gain=None, m_out=None, row_map=None, name="proj"):
    m, k = x.shape
    n = w.shape[1] // n_out
    m_out = m if m_out is None else m_out
    row_map = (lambda i: i) if row_map is None else row_map
    tm = min(tm, m_out)
    assert m_out % tm == 0 and n % PROJ_COLS == 0 and n * n_out == w.shape[1]
    in_specs = [pl.BlockSpec((tm, k), lambda i: (row_map(i), 0))]
    args = [x]
    if gain is not None:
        in_specs.append(pl.BlockSpec((1, k), lambda i: (0, 0)))
        args.append(gain)
    in_specs.append(pl.BlockSpec((k, n * n_out), lambda i: (0, 0)))
    args.append(w)
    res = pl.pallas_call(
        functools.partial(_mm_kernel, normed=gain is not None),
        grid=(m_out // tm,),
        in_specs=in_specs,
        out_specs=[pl.BlockSpec((tm, n), lambda i: (i, 0))] * n_out,
        out_shape=[jax.ShapeDtypeStruct((m_out, n), out_dtype)] * n_out,
        compiler_params=_params(("parallel",), 56),
        name=name,
    )(*args)
    return res[0] if n_out == 1 else res


def _ssm_inproj_kernel(x_ref, w_ref, o_ref, nat_ref):
    r = _dot(x_ref[...], w_ref[...])
    tc = nat_ref.shape[1] // SSM_CHUNK
    for m in range(OCTETS):
        nat_ref[m] = r[:, m * LANES:(m + 1) * LANES]
    for j in range(SSM_CHUNK):
        for m in range(OCTETS):
            o_ref[m, :, j * LANES:(j + 1) * LANES] = nat_ref[m, pl.ds(j, tc, stride=SSM_CHUNK), :]


def _ssm_inproj(u, w_ssm, tm):
    n_tok = u.shape[0]
    assert n_tok % tm == 0 and tm % SSM_CHUNK == 0
    tc = tm // SSM_CHUNK
    return pl.pallas_call(
        _ssm_inproj_kernel,
        grid=(n_tok // tm,),
        in_specs=[pl.BlockSpec((tm, D_MODEL), lambda i: (i, 0)),
                  pl.BlockSpec((D_MODEL, SSM_WIDTH), lambda i: (0, 0))],
        out_specs=pl.BlockSpec((OCTETS, tc, OCT_COLS), lambda i: (0, i, 0)),
        out_shape=jax.ShapeDtypeStruct((OCTETS, n_tok // SSM_CHUNK, OCT_COLS), F32),
        scratch_shapes=[pltpu.VMEM((OCTETS, tm, LANES), F32)],
        compiler_params=_params(("parallel",), 48),
        name="ssm_inproj",
    )(u, w_ssm)


def _ssm_params(a_re, a_im, log_dt, b_re, b_im, c_re, c_im):
    dt = jnp.exp(log_dt)[:, None]
    mag = jnp.exp(a_re * dt)
    ab_re = mag * jnp.cos(a_im * dt)
    ab_im = mag * jnp.sin(a_im * dt)
    den = a_re * a_re + a_im * a_im
    n_re = ab_re - 1.0
    n_im = ab_im
    k_re = (n_re * a_re + n_im * a_im) / den
    k_im = (n_im * a_re - n_re * a_im) / den
    bb_re = k_re[..., None] * b_re - k_im[..., None] * b_im
    bb_im = k_re[..., None] * b_im + k_im[..., None] * b_re
    pr = [jnp.ones_like(ab_re)]
    pi = [jnp.zeros_like(ab_re)]
    for _ in range(SSM_CHUNK):
        pr.append(pr[-1] * ab_re - pi[-1] * ab_im)
        pi.append(pr[-2] * ab_im + pi[-1] * ab_re)
    n_pw = SSM_CHUNK + 1
    pw = jnp.concatenate([jnp.stack(pr).reshape(n_pw, OCTETS, OCT_HALF),
                          jnp.stack(pi).reshape(n_pw, OCTETS, OCT_HALF)], axis=2)
    pw = jnp.transpose(pw, (1, 0, 2))
    eye = jnp.eye(OCT_GROUPS, dtype=F32)

    def expand(x):
        x4 = x.reshape(OCTETS, OCT_GROUPS, x.shape[1], SSM_STATE)
        out = eye[None, :, None, :, None] * x4[:, :, :, None, :]
        return out.reshape(OCTETS, OCT_GROUPS * x.shape[1], OCT_HALF)

    bd = jnp.concatenate([expand(jnp.transpose(bb_re, (0, 2, 1))),
                          expand(jnp.transpose(bb_im, (0, 2, 1)))], axis=2)
    cd = jnp.concatenate([expand(c_re), expand(c_im)], axis=2)
    return bd, cd, pw


def _state_to_octets(s_re, s_im):
    b = s_re.shape[0]
    s = jnp.stack([s_re, s_im], 0).reshape(2, b, OCTETS, OCT_GROUPS, SSM_STATE)
    return jnp.transpose(s, (2, 1, 0, 3, 4)).reshape(OCTETS, b, OCT_STATE)


def _state_from_octets(s):
    b = s.shape[1]
    s = s.reshape(OCTETS, b, 2, OCT_GROUPS, SSM_STATE)
    s = jnp.transpose(s, (2, 1, 0, 3, 4)).reshape(2, b, SSM_GROUPS, SSM_STATE)
    return s[0], s[1]


def _split_bf16(x):
    hi = x.astype(BF16)
    return hi, (x - hi.astype(F32)).astype(BF16)


def _ssm_build_operators(bd_ref, cd_ref, pw_ref, t8_ref, wus_ref, wso_ref):
    bd_re, bd_im = bd_ref[0, :, :OCT_HALF], bd_ref[0, :, OCT_HALF:]
    cd_re, cd_im = cd_ref[0, :, :OCT_HALF], cd_ref[0, :, OCT_HALF:]
    c_hi, c_lo = _split_bf16(jnp.concatenate([cd_re, -cd_im], axis=1))
    blk = lambda i: slice(i * LANES, (i + 1) * LANES)
    for a in range(SSM_CHUNK // 2):
        t8_ref[blk(2 * a + 1), blk(2 * a)] = jnp.zeros((LANES, LANES), BF16)
    for k in range(SSM_CHUNK + 1):
        p_re = pw_ref[0, k:k + 1, :OCT_HALF]
        p_im = pw_ref[0, k:k + 1, OCT_HALF:]
        if k < SSM_CHUNK:
            e = jnp.concatenate([bd_re * p_re - bd_im * p_im, bd_re * p_im + bd_im * p_re], axis=1)
            wus_ref[blk(SSM_CHUNK - 1 - k), :] = e.astype(BF16)
            e_hi, e_lo = _split_bf16(e)
            lag = (_dot_nt(e_hi, c_hi) + _dot_nt(e_hi, c_lo) + _dot_nt(e_lo, c_hi)).astype(BF16)
            for j in range(SSM_CHUNK - k):
                t8_ref[blk(j), blk(j + k)] = lag
        if k >= 1:
            g = jnp.concatenate([cd_re * p_re - cd_im * p_im, -(cd_re * p_im + cd_im * p_re)], axis=1)
            wso_ref[blk(k - 1), :] = g.astype(BF16)


def _ssm_core_kernel(us_ref, s0_ref, bd_ref, cd_ref, pw_ref, y_ref, sfin_ref,
                     t8_ref, wus_ref, wso_ref, ds_ref, sp_ref, *, nb, n_chunks):
    @pl.when(pl.program_id(1) == 0)
    def _():
        _ssm_build_operators(bd_ref, cd_ref, pw_ref, t8_ref, wus_ref, wso_ref)

    ub = us_ref[0].astype(BF16)
    ds_ref[...] = _dot(ub, wus_ref[...])
    a_re = pw_ref[0, SSM_CHUNK:SSM_CHUNK + 1, :OCT_HALF]
    a_im = pw_ref[0, SSM_CHUNK:SSM_CHUNK + 1, OCT_HALF:]

    def advance(s, d):
        s_re, s_im = s[:, :OCT_HALF], s[:, OCT_HALF:]
        n_re = a_re * s_re - a_im * s_im + d[:, :OCT_HALF]
        n_im = a_re * s_im + a_im * s_re + d[:, OCT_HALF:]
        return jnp.concatenate([n_re, n_im], axis=1)

    if n_chunks == 1:
        s0 = s0_ref[0, 0]
        sp_ref[...] = s0
        sfin_ref[0, 0] = advance(s0, ds_ref[...])
    else:
        def body(c, carry):
            new = []
            for bl in range(nb):
                row = bl * n_chunks + c
                sp_ref[pl.ds(row, 1), :] = carry[bl]
                new.append(advance(carry[bl], ds_ref[pl.ds(row, 1), :]))
            return tuple(new)

        init = tuple(s0_ref[0, 0, bl:bl + 1, :] for bl in range(nb))
        fin = lax.fori_loop(0, n_chunks, body, init, unroll=2)
        for bl in range(nb):
            sfin_ref[0, 0, bl:bl + 1, :] = fin[bl]

    spb = sp_ref[...].astype(BF16)
    for nt in range(OCT_COLS // MXU_DIM):
        k_hi = (nt + 1) * MXU_DIM
        cs = slice(nt * MXU_DIM, k_hi)
        y_ref[0, :, cs] = _dot(ub[:, :k_hi], t8_ref[:k_hi, cs]) + _dot_nt(spb, wso_ref[cs, :])


def _ssm_core(us_r, s0, bd, cd, pw, n_batch, n_chunks, nb):
    nc = us_r.shape[1]
    rows = nb * n_chunks
    assert n_batch % nb == 0 and nc == n_batch * n_chunks
    nr = n_batch // nb
    s0 = s0.reshape(OCTETS, nr, nb, OCT_STATE)
    par = lambda rows_: pl.BlockSpec((1, rows_, OCT_STATE), lambda m, r: (m, 0, 0))
    y, sfin = pl.pallas_call(
        functools.partial(_ssm_core_kernel, nb=nb, n_chunks=n_chunks),
        grid=(OCTETS, nr),
        in_specs=[pl.BlockSpec((1, rows, OCT_COLS), lambda m, r: (m, r, 0)),
                  pl.BlockSpec((1, 1, nb, OCT_STATE), lambda m, r: (m, r, 0, 0)),
                  par(LANES), par(LANES), par(SSM_CHUNK + 1)],
        out_specs=[pl.BlockSpec((1, rows, OCT_COLS), lambda m, r: (m, r, 0)),
                   pl.BlockSpec((1, 1, nb, OCT_STATE), lambda m, r: (m, r, 0, 0))],
        out_shape=[jax.ShapeDtypeStruct((OCTETS, nc, OCT_COLS), F32),
                   jax.ShapeDtypeStruct((OCTETS, nr, nb, OCT_STATE), F32)],
        scratch_shapes=[pltpu.VMEM((OCT_COLS, OCT_COLS), BF16),
                        pltpu.VMEM((OCT_COLS, OCT_STATE), BF16),
                        pltpu.VMEM((OCT_COLS, OCT_STATE), BF16),
                        pltpu.VMEM((rows, OCT_STATE), F32),
                        pltpu.VMEM((rows, OCT_STATE), F32)],
        compiler_params=_params(("arbitrary", "arbitrary"), 56),
        name="ssm_core",
    )(us_r, s0, bd, cd, pw)
    return y, sfin.reshape(OCTETS, n_batch, OCT_STATE)


def _gelu_tanh(x):
    c = math.sqrt(2.0 / math.pi)
    return x * (0.5 * (1.0 + jnp.tanh(c * (x + 0.044715 * (x * x * x)))))


def _ssm_glu_kernel(y_ref, us_ref, d_ref, w_ref, b_ref, o_ref, nat_ref):
    tc = y_ref.shape[1]
    for i in range(SSM_CHUNK):
        cs = slice(i * LANES, (i + 1) * LANES)
        for m in range(OCTETS):
            d = d_ref[:, m * LANES:(m + 1) * LANES]
            nat_ref[m, pl.ds(i, tc, stride=SSM_CHUNK), :] = y_ref[m, :, cs] + d * us_ref[m, :, cs]
    yg = _gelu_tanh(jnp.concatenate([nat_ref[m] for m in range(OCTETS)], axis=1))
    z = _dot(yg.astype(BF16), w_ref[...]) + b_ref[...]
    o_ref[...] = (yg * jax.nn.sigmoid(z)).astype(o_ref.dtype)


def _ssm_glu(y_r, us_r, d, w_glu, b_glu, tc):
    nc = y_r.shape[1]
    assert nc % tc == 0
    oct_spec = pl.BlockSpec((OCTETS, tc, OCT_COLS), lambda c: (0, c, 0))
    vec = pl.BlockSpec((1, SSM_WIDTH), lambda c: (0, 0))
    return pl.pallas_call(
        _ssm_glu_kernel,
        grid=(nc // tc,),
        in_specs=[oct_spec, oct_spec, vec,
                  pl.BlockSpec((SSM_WIDTH, SSM_WIDTH), lambda c: (0, 0)), vec],
        out_specs=pl.BlockSpec((tc * SSM_CHUNK, SSM_WIDTH), lambda c: (c, 0)),
        out_shape=jax.ShapeDtypeStruct((nc * SSM_CHUNK, SSM_WIDTH), BF16),
        scratch_shapes=[pltpu.VMEM((OCTETS, tc * SSM_CHUNK, LANES), F32)],
        compiler_params=_params(("parallel",), 48),
        name="ssm_glu",
    )(y_r, us_r, d, w_glu, b_glu)


N_PAIRS = ATT_HEADS // 2
SOFTMAX_ROWS = 64


def _head_masks():
    lane = lax.broadcasted_iota(jnp.int32, (1, LANES), 1)
    return (lane < ATT_HEAD_DIM, lane >= ATT_HEAD_DIM)


def _pair_scores(q2, kw, bias2, masks):
    qq = jnp.concatenate([jnp.where(m, q2, jnp.zeros_like(q2)) for m in masks], axis=0)
    return _dot_nt(qq, kw) * (ATT_HEAD_DIM ** -0.5 * LOG2E) + bias2


def _softmax_parts(sc):
    mx = jnp.max(sc, axis=1, keepdims=True)
    p = jnp.exp2(sc - mx)
    l = jnp.sum(p, axis=1, keepdims=True)
    return p.astype(BF16), jnp.broadcast_to(1.0 / l, (sc.shape[0], LANES))


def _pair_output(p, rl, vw, masks):
    o2 = _dot(p, vw) * rl
    n_q = o2.shape[0] // 2
    return jnp.where(masks[0], o2[:n_q], o2[n_q:])


def _band_attn_kernel(q_ref, kp_ref, kc_ref, vp_ref, vc_ref, bias_ref, o_ref,
                      kw_ref, vw_ref, sc_ref, p_ref, rl_ref):
    kw_ref[0:Q_TILE] = kp_ref[...]
    kw_ref[Q_TILE:2 * Q_TILE] = kc_ref[...]
    vw_ref[0:Q_TILE] = vp_ref[...]
    vw_ref[Q_TILE:2 * Q_TILE] = vc_ref[...]
    masks = _head_masks()
    cs = lambda hp: slice(hp * LANES, (hp + 1) * LANES)

    def sub(s, carry):
        r0 = pl.multiple_of(s * Q_SUB, Q_SUB)
        rows = pl.ds(r0, Q_SUB)
        win = pl.ds(r0, K_WIN)
        for hp in range(N_PAIRS):
            sc_ref[hp] = _pair_scores(q_ref[rows, cs(hp)], kw_ref[win, cs(hp)], bias_ref[hp], masks)

        @pl.when(pl.program_id(1) == 0)
        def _():
            col = lax.broadcasted_iota(jnp.int32, (1, K_WIN), 1)
            extra = jnp.where(col + r0 < Q_TILE, NEG_INF, 0.0)
            for hp in range(N_PAIRS):
                sc_ref[hp] = sc_ref[hp] + extra

        for hp in range(N_PAIRS):
            for g0 in range(0, 2 * Q_SUB, SOFTMAX_ROWS):
                gs = slice(g0, g0 + SOFTMAX_ROWS)
                p_ref[hp, gs, :], rl_ref[hp, gs, :] = _softmax_parts(sc_ref[hp, gs, :])
        for hp in range(N_PAIRS):
            o = _pair_output(p_ref[hp], rl_ref[hp], vw_ref[win, cs(hp)], masks)
            o_ref[rows, cs(hp)] = o.astype(o_ref.dtype)
        return carry

    lax.fori_loop(0, Q_TILE // Q_SUB, sub, 0)


def _rel_bias_tile(rel_bias, n_q, n_k, offset, ok):
    r = np.arange(n_q - 1 + n_k)
    idx = np.clip(offset + n_q - 1 - r, -REL_CLIP, REL_CLIP) + REL_CLIP
    v = jnp.pad(rel_bias[:, idx] * LOG2E, ((0, 0), (0, 1)))
    w = n_q + n_k
    flat = jnp.tile(v, (1, n_q))[:, :n_q * (w - 1)]
    toep = flat.reshape(rel_bias.shape[0], n_q, w - 1)[:, :, n_q - 1:]
    return jnp.where(ok[None], toep, NEG_INF).reshape(N_PAIRS, 2 * n_q, n_k)


def _band_attention_prompt(qkvm, rel_bias, n_batch, seq):
    assert seq % Q_TILE == 0
    nt = seq // Q_TILE
    qc = np.arange(Q_SUB)[:, None] // CHUNK
    kc = np.arange(K_WIN)[None, :] // CHUNK
    bias = _rel_bias_tile(rel_bias, Q_SUB, K_WIN, BAND_PAST, (kc >= qc) & (kc <= qc + BAND_PAST_CHUNKS))
    blk = (Q_TILE, ATT_WIDTH)
    cur = lambda col: pl.BlockSpec(blk, lambda b, t: (b * nt + t, col))
    prev = lambda col: pl.BlockSpec(blk, lambda b, t: (b * nt + jnp.maximum(t - 1, 0), col))
    return pl.pallas_call(
        _band_attn_kernel,
        grid=(n_batch, nt),
        in_specs=[cur(0), prev(1), cur(1), prev(2), cur(2),
                  pl.BlockSpec((N_PAIRS, 2 * Q_SUB, K_WIN), lambda b, t: (0, 0, 0))],
        out_specs=pl.BlockSpec(blk, lambda b, t: (b * nt + t, 0)),
        out_shape=jax.ShapeDtypeStruct((n_batch * seq, ATT_WIDTH), BF16),
        scratch_shapes=[pltpu.VMEM((2 * Q_TILE, ATT_WIDTH), BF16),
                        pltpu.VMEM((2 * Q_TILE, ATT_WIDTH), BF16),
                        pltpu.VMEM((N_PAIRS, 2 * Q_SUB, K_WIN), F32),
                        pltpu.VMEM((N_PAIRS, 2 * Q_SUB, K_WIN), BF16),
                        pltpu.VMEM((N_PAIRS, 2 * Q_SUB, LANES), F32)],
        compiler_params=_params(("parallel", "arbitrary"), 48),
        name="band_attn",
    )(qkvm, qkvm, qkvm, qkvm, qkvm, bias)


def _band_attn_sample_kernel(q_ref, kn_ref, vn_ref, ck_ref, cv_ref, bias_ref, o_ref, kw_ref, vw_ref,
                             *, n_cache, n_new):
    kw_ref[...] = jnp.zeros_like(kw_ref)
    vw_ref[...] = jnp.zeros_like(vw_ref)
    kw_ref[0:n_cache] = ck_ref[0].astype(BF16)
    vw_ref[0:n_cache] = cv_ref[0].astype(BF16)
    kw_ref[n_cache:n_cache + n_new] = kn_ref[...]
    vw_ref[n_cache:n_cache + n_new] = vn_ref[...]
    masks = _head_masks()
    for hp in range(N_PAIRS):
        cs = slice(hp * LANES, (hp + 1) * LANES)
        p, rl = _softmax_parts(_pair_scores(q_ref[:, cs], kw_ref[:, cs], bias_ref[hp], masks))
        o_ref[:, cs] = _pair_output(p, rl, vw_ref[:, cs], masks).astype(o_ref.dtype)


def _band_attention_sample(qkvm, rel_bias, cache_k, cache_v, n_batch, n_new):
    n_cache = cache_k.shape[1]
    n_keys = -(-(n_cache + n_new) // LANES) * LANES
    q_pos = PAST_LEN + np.arange(n_new)[:, None]
    j = np.arange(n_keys)[None, :]
    k_pos = PAST_LEN - n_cache + j
    ok = ((j < n_cache + n_new) & (k_pos >= 0) & (k_pos // CHUNK <= q_pos // CHUNK)
          & (k_pos // CHUNK >= q_pos // CHUNK - BAND_PAST_CHUNKS))
    bias = _rel_bias_tile(rel_bias, n_new, n_keys, n_cache, ok)
    new = lambda col: pl.BlockSpec((n_new, ATT_WIDTH), lambda b: (b, col))
    cache = pl.BlockSpec((1, n_cache, ATT_WIDTH), lambda b: (b, 0, 0))
    return pl.pallas_call(
        functools.partial(_band_attn_sample_kernel, n_cache=n_cache, n_new=n_new),
        grid=(n_batch,),
        in_specs=[new(0), new(1), new(2), cache, cache,
                  pl.BlockSpec((N_PAIRS, 2 * n_new, n_keys), lambda b: (0, 0, 0))],
        out_specs=pl.BlockSpec((n_new, ATT_WIDTH), lambda b: (b, 0)),
        out_shape=jax.ShapeDtypeStruct((n_batch * n_new, ATT_WIDTH), BF16),
        scratch_shapes=[pltpu.VMEM((n_keys, ATT_WIDTH), BF16),
                        pltpu.VMEM((n_keys, ATT_WIDTH), BF16)],
        compiler_params=_params(("parallel",), 48),
        name="band_attn_sample",
    )(qkvm, qkvm, qkvm, cache_k, cache_v, bias)


def _mem_attn_kernel(q_ref, k_ref, v_ref, o_ref):
    k = k_ref[0].astype(BF16)
    v = v_ref[0].astype(BF16)
    for h in range(MEM_HEADS):
        cs = slice(h * MEM_HEAD_DIM, (h + 1) * MEM_HEAD_DIM)
        sc = _dot_nt(q_ref[:, cs], k[:, cs]) * (MEM_HEAD_DIM ** -0.5 * LOG2E)
        mx = jnp.max(sc, axis=1, keepdims=True)
        p = jnp.exp2(sc - mx)
        l = jnp.sum(p, axis=1, keepdims=True)
        o = _dot(p.astype(BF16), v[:, cs]) * (1.0 / l)
        o_ref[:, cs] = o.astype(o_ref.dtype)


def _memory_attention(qkvm, mem_k, mem_v, n_batch, seq, tq):
    nt = seq // tq
    mem = pl.BlockSpec((1, N_MEM, MEM_WIDTH), lambda b, t: (b, 0, 0))
    return pl.pallas_call(
        _mem_attn_kernel,
        grid=(n_batch, nt),
        in_specs=[pl.BlockSpec((tq, MEM_WIDTH), lambda b, t: (b * nt + t, 3)), mem, mem],
        out_specs=pl.BlockSpec((tq, MEM_WIDTH), lambda b, t: (b * nt + t, 0)),
        out_shape=jax.ShapeDtypeStruct((n_batch * seq, MEM_WIDTH), BF16),
        compiler_params=_params(("parallel", "arbitrary"), 48),
        name="mem_attn",
    )(qkvm, mem_k, mem_v)


def _gate_merge_kernel(u_ref, os_ref, oa_ref, om_ref, wgs_ref, wga_ref, wgm_ref,
                       wbs_ref, wba_ref, wbm_ref, out_ref):
    u = u_ref[...]

    def branch(o_ref, wg_ref, wb_ref):
        return jax.nn.sigmoid(_dot(u, wg_ref[...])) * _dot(o_ref[...], wb_ref[...])

    merged = (branch(os_ref, wgs_ref, wbs_ref) + branch(oa_ref, wga_ref, wba_ref)
              + branch(om_ref, wgm_ref, wbm_ref))
    out_ref[...] = merged.astype(out_ref.dtype)


def _gate_merge(u, o_s, o_a, o_m, w_gate, wb_s, wb_a, wb_m, tm, tn):
    m, d = u.shape
    gate = lambda b: pl.BlockSpec((d, tn), lambda i, n: (0, b * (d // tn) + n))
    wb = pl.BlockSpec((SSM_WIDTH, tn), lambda i, n: (0, n))
    ob = pl.BlockSpec((tm, SSM_WIDTH), lambda i, n: (i, 0))
    return pl.pallas_call(
        _gate_merge_kernel,
        grid=(m // tm, d // tn),
        in_specs=[pl.BlockSpec((tm, d), lambda i, n: (i, 0)), ob, ob, ob,
                  gate(0), gate(1), gate(2), wb, wb, wb],
        out_specs=pl.BlockSpec((tm, tn), lambda i, n: (i, n)),
        out_shape=jax.ShapeDtypeStruct((m, d), BF16),
        compiler_params=_params(("parallel", "arbitrary"), 56),
        name="gate_merge",
    )(u, o_s, o_a, o_m, w_gate, w_gate, w_gate, wb_s, wb_a, wb_m)


def _out_proj_kernel(x_ref, w_ref, h_ref, g_ref, o_ref):
    o_ref[...] = h_ref[...] + _rms(_dot(x_ref[...], w_ref[...]), g_ref[...])


def _out_proj(x, w_out, h, g_post, tm):
    m, d = h.shape
    row = lambda: pl.BlockSpec((tm, d), lambda i: (i, 0))
    return pl.pallas_call(
        _out_proj_kernel,
        grid=(m // tm,),
        in_specs=[row(), pl.BlockSpec((d, d), lambda i: (0, 0)), row(),
                  pl.BlockSpec((1, d), lambda i: (0, 0))],
        out_specs=row(),
        out_shape=jax.ShapeDtypeStruct((m, d), F32),
        compiler_params=_params(("parallel",), 56),
        name="out_proj",
    )(x, w_out, h, g_post)


def _layer(x, w, *, n_batch, seq, s0, mem_k, mem_v, cache_k, cache_v, tm, keep):
    n_tok = n_batch * seq
    n_chunks = seq // SSM_CHUNK
    big_tm = min(1024, n_tok)
    h1, u = _ffn(x, w["ffn1_norm_pre"], w["ffn1_norm_post"], w["ffn1_w_gate"], w["ffn1_w_up"],
                 w["ffn1_w_down"], w["mix_norm_pre"], big_tm, 512)

    qkvm = _matmul(u, w["w_qkvm"], BF16, tm, name="qkvm")
    if keep == seq:
        kv = _matmul(u, w["w_kv"], F32, tm, n_out=2, name="kv_tail")
    else:
        n_keep_blocks = seq // keep
        kv = _matmul(u, w["w_kv"], F32, keep, n_out=2, m_out=n_batch * keep,
                     row_map=lambda i: i * n_keep_blocks + (n_keep_blocks - 1), name="kv_tail")

    us_r = _ssm_inproj(u, w["w_ssm"], tm)
    nb = 2 if (n_chunks > 1 and n_batch % 2 == 0) else (1 if n_chunks > 1 else n_batch)
    y_r, s_fin = _ssm_core(us_r, s0, w["ssm_bd"], w["ssm_cd"], w["ssm_pw"], n_batch, n_chunks, nb)
    o_s = _ssm_glu(y_r, us_r, w["ssm_d"], w["ssm_w_glu"], w["ssm_b_glu"], min(32, n_tok // SSM_CHUNK))

    if cache_k is None:
        o_a = _band_attention_prompt(qkvm, w["att_rel_bias"], n_batch, seq)
    else:
        o_a = _band_attention_sample(qkvm, w["att_rel_bias"], cache_k, cache_v, n_batch, seq)
    o_m = _memory_attention(qkvm, mem_k, mem_v, n_batch, seq, min(512, seq))

    merged = _gate_merge(u, o_s, o_a, o_m, w["w_gate"], w["w_branch_ssm"], w["w_branch_att"],
                         w["w_branch_mem"], big_tm, 512)
    h2 = _out_proj(merged, w["w_out"], h1, w["mix_norm_post"], tm)
    y = _ffn(h2, w["ffn2_norm_pre"], w["ffn2_norm_post"], w["ffn2_w_gate"], w["ffn2_w_up"],
             w["ffn2_w_down"], None, big_tm, 512)
    return y, kv, s_fin


def kernel(x_prompt, x_sample, mem_prompt, cache_att_k, cache_att_v, cache_mem_k, cache_mem_v, state_ssm_re, state_ssm_im, ffn1_norm_pre, ffn1_norm_post, ffn1_w_gate, ffn1_w_up, ffn1_w_down, mix_norm_pre, mix_norm_post, w_in, ssm_a_re, ssm_a_im, ssm_log_dt, ssm_b_re, ssm_b_im, ssm_c_re, ssm_c_im, ssm_d, ssm_w_glu, ssm_b_glu, att_rel_bias, mem_norm, w_mem_k, w_mem_v, w_branch_ssm, w_branch_att, w_branch_mem, w_out, ffn2_norm_pre, ffn2_norm_post, ffn2_w_gate, ffn2_w_up, ffn2_w_down):
    n_bp, t_p, d = x_prompt.shape
    n_bs, t_s, _ = x_sample.shape
    depth = ffn1_norm_pre.shape[0]
    assert depth == 1 and d == D_MODEL
    keep = min(BAND_PAST, t_p)
    l = 0

    vec = lambda a: a[l].reshape(1, -1).astype(F32)
    mat = lambda a: _cast_bf16(a[l])
    bd, cd, pw = _ssm_params(ssm_a_re[l], ssm_a_im[l], ssm_log_dt[l], ssm_b_re[l],
                             ssm_b_im[l], ssm_c_re[l], ssm_c_im[l])
    w = {
        "ffn1_norm_pre": vec(ffn1_norm_pre), "ffn1_norm_post": vec(ffn1_norm_post),
        "ffn1_w_gate": mat(ffn1_w_gate), "ffn1_w_up": mat(ffn1_w_up), "ffn1_w_down": mat(ffn1_w_down),
        "mix_norm_pre": vec(mix_norm_pre), "mix_norm_post": vec(mix_norm_post),
        "w_ssm": w_in[l, :, :COL_Q].astype(BF16),
        "w_qkvm": w_in[l, :, COL_Q:COL_GATE].astype(BF16),
        "w_kv": w_in[l, :, COL_Q + ATT_WIDTH:COL_Q + 3 * ATT_WIDTH].astype(BF16),
        "w_gate": w_in[l, :, COL_GATE:].astype(BF16),
        "ssm_bd": bd, "ssm_cd": cd, "ssm_pw": pw,
        "ssm_d": vec(ssm_d), "ssm_w_glu": mat(ssm_w_glu), "ssm_b_glu": vec(ssm_b_glu),
        "att_rel_bias": att_rel_bias[l].astype(F32),
        "w_branch_ssm": mat(w_branch_ssm), "w_branch_att": mat(w_branch_att),
        "w_branch_mem": mat(w_branch_mem), "w_out": mat(w_out),
        "ffn2_norm_pre": vec(ffn2_norm_pre), "ffn2_norm_post": vec(ffn2_norm_post),
        "ffn2_w_gate": mat(ffn2_w_gate), "ffn2_w_up": mat(ffn2_w_up), "ffn2_w_down": mat(ffn2_w_down),
    }

    w_mem = jnp.concatenate([w_mem_k[l], w_mem_v[l]], axis=1).astype(BF16)
    mk_p, mv_p = _matmul(mem_prompt.reshape(n_bp * N_MEM, d), w_mem, F32, 512, n_out=2,
                         gain=vec(mem_norm), name="memory_kv")
    mk_p = mk_p.reshape(n_bp, N_MEM, MEM_WIDTH)
    mv_p = mv_p.reshape(n_bp, N_MEM, MEM_WIDTH)
    zero_state = jnp.zeros((OCTETS, n_bp, OCT_STATE), F32)
    y_p, kv_p, sfin_p = _layer(x_prompt.reshape(n_bp * t_p, d), w, n_batch=n_bp, seq=t_p, s0=zero_state,
                               mem_k=mk_p, mem_v=mv_p, cache_k=None, cache_v=None, tm=512, keep=keep)
    sre_p, sim_p = _state_from_octets(sfin_p)

    n_cache = cache_att_k.shape[2]
    s0_s = _state_to_octets(state_ssm_re[l].astype(F32), state_ssm_im[l].astype(F32))
    y_s, kv_s, sfin_s = _layer(x_sample.reshape(n_bs * t_s, d), w, n_batch=n_bs, seq=t_s, s0=s0_s,
                               mem_k=cache_mem_k[l].reshape(n_bs, N_MEM, MEM_WIDTH),
                               mem_v=cache_mem_v[l].reshape(n_bs, N_MEM, MEM_WIDTH),
                               cache_k=cache_att_k[l].reshape(n_bs, n_cache, ATT_WIDTH),
                               cache_v=cache_att_v[l].reshape(n_bs, n_cache, ATT_WIDTH),
                               tm=n_bs * t_s, keep=t_s)
    sre_s, sim_s = _state_from_octets(sfin_s)

    heads = lambda a, nb, t: a.reshape(1, nb, t, ATT_HEADS, ATT_HEAD_DIM)
    memh = lambda a: a.reshape(1, n_bp, N_MEM, MEM_HEADS, MEM_HEAD_DIM)
    return (y_p.reshape(n_bp, t_p, d), y_s.reshape(n_bs, t_s, d),
            heads(kv_p[0], n_bp, keep), heads(kv_p[1], n_bp, keep),
            memh(mk_p), memh(mv_p), sre_p[None], sim_p[None],
            heads(kv_s[0], n_bs, t_s), heads(kv_s[1], n_bs, t_s),
            sre_s[None], sim_s[None])
```

```python
import functools
import math

import numpy as np
import jax
import jax.numpy as jnp
from jax import lax
from jax.experimental import pallas as pl
from jax.experimental.pallas import tpu as pltpu

F32 = jnp.float32
BF16 = jnp.bfloat16

D_MODEL = 2048
CHUNK = 64
BAND_PAST_CHUNKS = 8
BAND_PAST = BAND_PAST_CHUNKS * CHUNK
ATT_HEADS = 16
ATT_HEAD_DIM = 64
ATT_WIDTH = ATT_HEADS * ATT_HEAD_DIM
REL_CLIP = 128
SSM_GROUP = 16
SSM_WIDTH = 1024
SSM_GROUPS = SSM_WIDTH // SSM_GROUP
SSM_STATE = 64
N_MEM = 256
MEM_HEADS = 4
MEM_HEAD_DIM = 256
MEM_WIDTH = MEM_HEADS * MEM_HEAD_DIM
N_BRANCH = 3
EPS = 1e-6
NEG_INF = -1e30
PAST_LEN = 4096
LOG2E = math.log2(math.e)

COL_Q = SSM_WIDTH
COL_GATE = SSM_WIDTH + 3 * ATT_WIDTH + MEM_WIDTH

SSM_CHUNK = 16
OCTETS = 8
OCT_GROUPS = SSM_GROUPS // OCTETS
LANES = 128
MXU_DIM = 256
OCT_COLS = SSM_CHUNK * LANES
OCT_HALF = OCT_GROUPS * SSM_STATE
OCT_STATE = 2 * OCT_HALF

Q_TILE = 512
Q_SUB = 128
K_WIN = Q_SUB + BAND_PAST

MIB = 1024 * 1024


def _params(sem, vmem_mib):
    return pltpu.CompilerParams(dimension_semantics=sem, vmem_limit_bytes=vmem_mib * MIB)


def _dot(a, b):
    return jnp.dot(a, b, preferred_element_type=F32)


def _dot_nt(a, b, precision=None):
    return lax.dot_general(a, b, (((1,), (1,)), ((), ())), precision=precision,
                           preferred_element_type=F32)


def _rms(xf, g):
    y = xf * lax.rsqrt(jnp.mean(xf * xf, axis=-1, keepdims=True) + EPS)
    return y * g


CAST_BLOCK_BYTES = 8 * MIB


def _cast_kernel(x_ref, o_ref):
    o_ref[...] = x_ref[...].astype(o_ref.dtype)


def _cast_bf16(w, col0=0, n_cols=None):
    rows, cols = w.shape
    n_cols = cols - col0 if n_cols is None else n_cols
    cw = math.gcd(col0, n_cols)
    assert cw % LANES == 0 and col0 + n_cols <= cols
    tr = rows
    while tr * cw * 4 > CAST_BLOCK_BYTES and tr % 32 == 0:
        tr //= 2
    cb0 = col0 // cw
    return pl.pallas_call(
        _cast_kernel,
        grid=(rows // tr, n_cols // cw),
        in_specs=[pl.BlockSpec((tr, cw), lambda i, j: (i, cb0 + j))],
        out_specs=pl.BlockSpec((tr, cw), lambda i, j: (i, j)),
        out_shape=jax.ShapeDtypeStruct((rows, n_cols), BF16),
        compiler_params=_params(("parallel", "parallel"), 32),
        name="cast_bf16",
    )(w)


FFN_SLICES = 8


def _ffn_kernel(hp_ref, hn_ref, gpre_ref, gpost_ref, wg_ref, wu_ref, wd_ref, *rest,
                n_tiles, emit_next):
    if emit_next:
        gnext_ref, out_ref, nxt_ref, *scratch = rest
    else:
        out_ref, *scratch = rest
    xn_refs, acc_refs = scratch[:2], scratch[2:]
    r = pl.program_id(0)
    f = pl.program_id(1)
    rs = hp_ref.shape[0]
    rows = pl.ds(pl.multiple_of(jnp.minimum(f, FFN_SLICES - 1) * rs, rs), rs)

    def pre_norm_slice(slot):
        xn_refs[slot][rows, :] = _rms(hn_ref[...], gpre_ref[...]).astype(BF16)

    def finish_slice(slot):
        hn = hp_ref[...] + 0.5 * _rms(acc_refs[slot][rows, :], gpost_ref[...])
        out_ref[...] = hn
        if emit_next:
            nxt_ref[...] = _rms(hn, gnext_ref[...]).astype(BF16)

    def matmul_chunk(slot):
        xn = xn_refs[slot][...]
        g = _dot(xn, wg_ref[...])
        u = _dot(xn, wu_ref[...])
        a = (g * jax.nn.sigmoid(g)) * u
        d = _dot(a.astype(BF16), wd_ref[...])
        acc_refs[slot][...] = jnp.where(f == 0, d, acc_refs[slot][...] + d)

    @pl.when((r == 0) & (f == 0))
    def _():
        acc_refs[1][...] = jnp.zeros_like(acc_refs[1])

    @pl.when(r == 0)
    def _():
        pre_norm_slice(0)

    for parity in range(2):
        @pl.when((r >= 1) & (r <= n_tiles) & (lax.rem(r, 2) == parity))
        def _():
            finish_slice(parity)
            pre_norm_slice(parity)
            matmul_chunk(1 - parity)

    @pl.when(r == n_tiles + 1)
    def _():
        finish_slice((n_tiles + 1) % 2)


def _ffn(h, g_pre, g_post, wg, wu, wd, g_next, tm, tf):
    emit_next = g_next is not None
    m, d = h.shape
    f_dim = wg.shape[1]
    nf = f_dim // tf
    n = m // tm
    rs = tm // FFN_SLICES
    assert m % tm == 0 and f_dim % tf == 0 and tm % FFN_SLICES == 0 and nf >= FFN_SLICES
    sl = lambda f: jnp.minimum(f, FFN_SLICES - 1)
    done = lambda r, f: (jnp.maximum(r - 2, 0) * FFN_SLICES + jnp.where(r >= 2, sl(f), 0), 0)
    ahead = lambda r, f: (jnp.minimum(r, n - 1) * FFN_SLICES + sl(f), 0)
    chunk = lambda r, f: jnp.where((r >= 1) & (r <= n), f, 0)
    vec = pl.BlockSpec((1, d), lambda r, f: (0, 0))
    in_specs = [pl.BlockSpec((rs, d), done), pl.BlockSpec((rs, d), ahead), vec, vec,
                pl.BlockSpec((d, tf), lambda r, f: (0, chunk(r, f))),
                pl.BlockSpec((d, tf), lambda r, f: (0, chunk(r, f))),
                pl.BlockSpec((tf, d), lambda r, f: (chunk(r, f), 0))]
    args = [h, h, g_pre, g_post, wg, wu, wd]
    out_shape = [jax.ShapeDtypeStruct((m, d), F32)]
    out_specs = [pl.BlockSpec((rs, d), done)]
    if emit_next:
        in_specs.append(vec)
        args.append(g_next)
        out_shape.append(jax.ShapeDtypeStruct((m, d), BF16))
        out_specs.append(pl.BlockSpec((rs, d), done))
    res = pl.pallas_call(
        functools.partial(_ffn_kernel, n_tiles=n, emit_next=emit_next),
        grid=(n + 2, nf),
        in_specs=in_specs,
        out_specs=out_specs,
        out_shape=out_shape,
        scratch_shapes=[pltpu.VMEM((tm, d), BF16), pltpu.VMEM((tm, d), BF16),
                        pltpu.VMEM((tm, d), F32), pltpu.VMEM((tm, d), F32)],
        compiler_params=_params(("arbitrary", "arbitrary"), 56),
        name="ffn",
    )(*args)
    return res if emit_next else res[0]


PROJ_COLS = 1024


def _mm_kernel(*refs, normed):
    if normed:
        x_ref, g_ref, w_ref, *o_refs = refs
        x = _rms(x_ref[...], g_ref[...]).astype(BF16)
    else:
        x_ref, w_ref, *o_refs = refs
        x = x_ref[...]
    col = 0
    for o_ref in o_refs:
        for j in range(0, o_ref.shape[1], PROJ_COLS):
            o_ref[:, j:j + PROJ_COLS] = _dot(x, w_ref[:, col + j:col + j + PROJ_COLS]).astype(o_ref.dtype)
        col += o_ref.shape[1]


def _matmul(x, w, out_dtype, tm, n_out=1, gain=None, m_out=None, row_map=None, name="proj"):
    m, k = x.shape
    n = w.shape[1] // n_out
    m_out = m if m_out is None else m_out
    row_map = (lambda i: i) if row_map is None else row_map
    tm = min(tm, m_out)
    assert m_out % tm == 0 and n % PROJ_COLS == 0 and n * n_out == w.shape[1]
    in_specs = [pl.BlockSpec((tm, k), lambda i: (row_map(i), 0))]
    args = [x]
    if gain is not None:
        in_specs.append(pl.BlockSpec((1, k), lambda i: (0, 0)))
        args.append(gain)
    in_specs.append(pl.BlockSpec((k, n * n_out), lambda i: (0, 0)))
    args.append(w)
    res = pl.pallas_call(
        functools.partial(_mm_kernel, normed=gain is not None),
        grid=(m_out // tm,),
        in_specs=in_specs,
        out_specs=[pl.BlockSpec((tm, n), lambda i: (i, 0))] * n_out,
        out_shape=[jax.ShapeDtypeStruct((m_out, n), out_dtype)] * n_out,
        compiler_params=_params(("parallel",), 56),
        name=name,
    )(*args)
    return res[0] if n_out == 1 else res


def _ssm_inproj_kernel(x_ref, w_ref, o_ref, nat_ref):
    r = _dot(x_ref[...], w_ref[...])
    tc = nat_ref.shape[1] // SSM_CHUNK
    for m in range(OCTETS):
        nat_ref[m] = r[:, m * LANES:(m + 1) * LANES]
    for j in range(SSM_CHUNK):
        for m in range(OCTETS):
            o_ref[m, :, j * LANES:(j + 1) * LANES] = nat_ref[m, pl.ds(j, tc, stride=SSM_CHUNK), :]


def _ssm_inproj(u, w_ssm, tm):
    n_tok = u.shape[0]
    assert n_tok % tm == 0 and tm % SSM_CHUNK == 0
    tc = tm // SSM_CHUNK
    return pl.pallas_call(
        _ssm_inproj_kernel,
        grid=(n_tok // tm,),
        in_specs=[pl.BlockSpec((tm, D_MODEL), lambda i: (i, 0)),
                  pl.BlockSpec((D_MODEL, SSM_WIDTH), lambda i: (0, 0))],
        out_specs=pl.BlockSpec((OCTETS, tc, OCT_COLS), lambda i: (0, i, 0)),
        out_shape=jax.ShapeDtypeStruct((OCTETS, n_tok // SSM_CHUNK, OCT_COLS), F32),
        scratch_shapes=[pltpu.VMEM((OCTETS, tm, LANES), F32)],
        compiler_params=_params(("parallel",), 48),
        name="ssm_inproj",
    )(u, w_ssm)


def _ssm_params(a_re, a_im, log_dt, b_re, b_im, c_re, c_im):
    dt = jnp.exp(log_dt)[:, None]
    mag = jnp.exp(a_re * dt)
    ab_re = mag * jnp.cos(a_im * dt)
    ab_im = mag * jnp.sin(a_im * dt)
    den = a_re * a_re + a_im * a_im
    n_re = ab_re - 1.0
    n_im = ab_im
    k_re = (n_re * a_re + n_im * a_im) / den
    k_im = (n_im * a_re - n_re * a_im) / den
    bb_re = k_re[..., None] * b_re - k_im[..., None] * b_im
    bb_im = k_re[..., None] * b_im + k_im[..., None] * b_re
    pr = [jnp.ones_like(ab_re)]
    pi = [jnp.zeros_like(ab_re)]
    for _ in range(SSM_CHUNK):
        pr.append(pr[-1] * ab_re - pi[-1] * ab_im)
        pi.append(pr[-2] * ab_im + pi[-1] * ab_re)
    n_pw = SSM_CHUNK + 1
    pw = jnp.concatenate([jnp.stack(pr).reshape(n_pw, OCTETS, OCT_HALF),
                          jnp.stack(pi).reshape(n_pw, OCTETS, OCT_HALF)], axis=2)
    pw = jnp.transpose(pw, (1, 0, 2))
    eye = jnp.eye(OCT_GROUPS, dtype=F32)

    def expand(x):
        x4 = x.reshape(OCTETS, OCT_GROUPS, x.shape[1], SSM_STATE)
        out = eye[None, :, None, :, None] * x4[:, :, :, None, :]
        return out.reshape(OCTETS, OCT_GROUPS * x.shape[1], OCT_HALF)

    bd = jnp.concatenate([expand(jnp.transpose(bb_re, (0, 2, 1))),
                          expand(jnp.transpose(bb_im, (0, 2, 1)))], axis=2)
    cd = jnp.concatenate([expand(c_re), expand(c_im)], axis=2)
    return bd, cd, pw


def _state_to_octets(s_re, s_im):
    b = s_re.shape[0]
    s = jnp.stack([s_re, s_im], 0).reshape(2, b, OCTETS, OCT_GROUPS, SSM_STATE)
    return jnp.transpose(s, (2, 1, 0, 3, 4)).reshape(OCTETS, b, OCT_STATE)


def _state_from_octets(s):
    b = s.shape[1]
    s = s.reshape(OCTETS, b, 2, OCT_GROUPS, SSM_STATE)
    s = jnp.transpose(s, (2, 1, 0, 3, 4)).reshape(2, b, SSM_GROUPS, SSM_STATE)
    return s[0], s[1]


def _split_bf16(x):
    hi = x.astype(BF16)
    return hi, (x - hi.astype(F32)).astype(BF16)


def _ssm_build_operators(bd_ref, cd_ref, pw_ref, t8_ref, wus_ref, wso_ref):
    bd_re, bd_im = bd_ref[0, :, :OCT_HALF], bd_ref[0, :, OCT_HALF:]
    cd_re, cd_im = cd_ref[0, :, :OCT_HALF], cd_ref[0, :, OCT_HALF:]
    c_hi, c_lo = _split_bf16(jnp.concatenate([cd_re, -cd_im], axis=1))
    blk = lambda i: slice(i * LANES, (i + 1) * LANES)
    for a in range(SSM_CHUNK // 2):
        t8_ref[blk(2 * a + 1), blk(2 * a)] = jnp.zeros((LANES, LANES), BF16)
    for k in range(SSM_CHUNK + 1):
        p_re = pw_ref[0, k:k + 1, :OCT_HALF]
        p_im = pw_ref[0, k:k + 1, OCT_HALF:]
        if k < SSM_CHUNK:
            e = jnp.concatenate([bd_re * p_re - bd_im * p_im, bd_re * p_im + bd_im * p_re], axis=1)
            wus_ref[blk(SSM_CHUNK - 1 - k), :] = e.astype(BF16)
            e_hi, e_lo = _split_bf16(e)
            lag = (_dot_nt(e_hi, c_hi) + _dot_nt(e_hi, c_lo) + _dot_nt(e_lo, c_hi)).astype(BF16)
            for j in range(SSM_CHUNK - k):
                t8_ref[blk(j), blk(j + k)] = lag
        if k >= 1:
            g = jnp.concatenate([cd_re * p_re - cd_im * p_im, -(cd_re * p_im + cd_im * p_re)], axis=1)
            wso_ref[blk(k - 1), :] = g.astype(BF16)


def _ssm_core_kernel(us_ref, s0_ref, bd_ref, cd_ref, pw_ref, y_ref, sfin_ref,
                     t8_ref, wus_ref, wso_ref, ds_ref, sp_ref, *, nb, n_chunks):
    @pl.when(pl.program_id(1) == 0)
    def _():
        _ssm_build_operators(bd_ref, cd_ref, pw_ref, t8_ref, wus_ref, wso_ref)

    ub = us_ref[0].astype(BF16)
    ds = _dot(ub, wus_ref[...])
    a16 = pw_ref[0, SSM_CHUNK:SSM_CHUNK + 1, :]
    n_blk = OCT_HALF // LANES
    blk = lambda k: slice(k * LANES, (k + 1) * LANES)

    def advance(s_re, s_im, d_re, d_im, a_re, a_im):
        return a_re * s_re - a_im * s_im + d_re, a_re * s_im + a_im * s_re + d_im

    if n_chunks == 1:
        s0 = s0_ref[0, 0]
        n_re, n_im = advance(s0[:, :OCT_HALF], s0[:, OCT_HALF:], ds[:, :OCT_HALF], ds[:, OCT_HALF:],
                             a16[:, :OCT_HALF], a16[:, OCT_HALF:])
        sfin_ref[0, 0] = jnp.concatenate([n_re, n_im], axis=1)
        spb = s0.astype(BF16)
    else:
        for k in range(2 * n_blk):
            ds_ref[k] = ds[:, blk(k)]

        def body(c, carry):
            at = pl.ds(c, nb, stride=n_chunks)
            new = list(carry)
            for k in range(n_blk):
                sp_ref[k, at, :] = carry[k]
                sp_ref[n_blk + k, at, :] = carry[n_blk + k]
                new[k], new[n_blk + k] = advance(carry[k], carry[n_blk + k], ds_ref[k, at, :],
                                                 ds_ref[n_blk + k, at, :], a16[:, blk(k)],
                                                 a16[:, blk(n_blk + k)])
            return tuple(new)

        init = tuple(s0_ref[0, 0, :, blk(k)] for k in range(2 * n_blk))
        fin = lax.fori_loop(0, n_chunks, body, init, unroll=2)
        for k in range(2 * n_blk):
            sfin_ref[0, 0, :, blk(k)] = fin[k]
        spb = jnp.concatenate([sp_ref[k] for k in range(2 * n_blk)], axis=1).astype(BF16)
    for nt in range(OCT_COLS // MXU_DIM):
        k_hi = (nt + 1) * MXU_DIM
        cs = slice(nt * MXU_DIM, k_hi)
        y_ref[0, :, cs] = _dot(ub[:, :k_hi], t8_ref[:k_hi, cs]) + _dot_nt(spb, wso_ref[cs, :])


def _ssm_core(us_r, s0, bd, cd, pw, n_batch, n_chunks, nb):
    nc = us_r.shape[1]
    rows = nb * n_chunks
    assert n_batch % nb == 0 and nc == n_batch * n_chunks
    nr = n_batch // nb
    s0 = s0.reshape(OCTETS, nr, nb, OCT_STATE)
    par = lambda rows_: pl.BlockSpec((1, rows_, OCT_STATE), lambda m, r: (m, 0, 0))
    y, sfin = pl.pallas_call(
        functools.partial(_ssm_core_kernel, nb=nb, n_chunks=n_chunks),
        grid=(OCTETS, nr),
        in_specs=[pl.BlockSpec((1, rows, OCT_COLS), lambda m, r: (m, r, 0)),
                  pl.BlockSpec((1, 1, nb, OCT_STATE), lambda m, r: (m, r, 0, 0)),
                  par(LANES), par(LANES), par(SSM_CHUNK + 1)],
        out_specs=[pl.BlockSpec((1, rows, OCT_COLS), lambda m, r: (m, r, 0)),
                   pl.BlockSpec((1, 1, nb, OCT_STATE), lambda m, r: (m, r, 0, 0))],
        out_shape=[jax.ShapeDtypeStruct((OCTETS, nc, OCT_COLS), F32),
                   jax.ShapeDtypeStruct((OCTETS, nr, nb, OCT_STATE), F32)],
        scratch_shapes=[pltpu.VMEM((OCT_COLS, OCT_COLS), BF16),
                        pltpu.VMEM((OCT_COLS, OCT_STATE), BF16),
                        pltpu.VMEM((OCT_COLS, OCT_STATE), BF16),
                        pltpu.VMEM((OCT_STATE // LANES, rows, LANES), F32),
                        pltpu.VMEM((OCT_STATE // LANES, rows, LANES), F32)],
        compiler_params=_params(("arbitrary", "arbitrary"), 56),
        name="ssm_core",
    )(us_r, s0, bd, cd, pw)
    return y, sfin.reshape(OCTETS, n_batch, OCT_STATE)


def _gelu_tanh(x):
    c = math.sqrt(2.0 / math.pi)
    return x * (0.5 * (1.0 + jnp.tanh(c * (x + 0.044715 * (x * x * x)))))


def _ssm_glu_kernel(y_ref, us_ref, d_ref, w_ref, b_ref, o_ref, nat_ref):
    tc = y_ref.shape[1]
    for i in range(SSM_CHUNK):
        cs = slice(i * LANES, (i + 1) * LANES)
        for m in range(OCTETS):
            d = d_ref[:, m * LANES:(m + 1) * LANES]
            nat_ref[m, pl.ds(i, tc, stride=SSM_CHUNK), :] = y_ref[m, :, cs] + d * us_ref[m, :, cs]
    yg = _gelu_tanh(jnp.concatenate([nat_ref[m] for m in range(OCTETS)], axis=1))
    z = _dot(yg.astype(BF16), w_ref[...]) + b_ref[...]
    o_ref[...] = (yg * jax.nn.sigmoid(z)).astype(o_ref.dtype)


def _ssm_glu(y_r, us_r, d, w_glu, b_glu, tc):
    nc = y_r.shape[1]
    assert nc % tc == 0
    oct_spec = pl.BlockSpec((OCTETS, tc, OCT_COLS), lambda c: (0, c, 0))
    vec = pl.BlockSpec((1, SSM_WIDTH), lambda c: (0, 0))
    return pl.pallas_call(
        _ssm_glu_kernel,
        grid=(nc // tc,),
        in_specs=[oct_spec, oct_spec, vec,
                  pl.BlockSpec((SSM_WIDTH, SSM_WIDTH), lambda c: (0, 0)), vec],
        out_specs=pl.BlockSpec((tc * SSM_CHUNK, SSM_WIDTH), lambda c: (c, 0)),
        out_shape=jax.ShapeDtypeStruct((nc * SSM_CHUNK, SSM_WIDTH), BF16),
        scratch_shapes=[pltpu.VMEM((OCTETS, tc * SSM_CHUNK, LANES), F32)],
        compiler_params=_params(("parallel",), 48),
        name="ssm_glu",
    )(y_r, us_r, d, w_glu, b_glu)


N_PAIRS = ATT_HEADS // 2
SOFTMAX_ROWS = 64


def _head_masks():
    lane = lax.broadcasted_iota(jnp.int32, (1, LANES), 1)
    return (lane < ATT_HEAD_DIM, lane >= ATT_HEAD_DIM)


def _pair_scores(q2, kw, bias2, masks):
    qq = jnp.concatenate([jnp.where(m, q2, jnp.zeros_like(q2)) for m in masks], axis=0)
    return _dot_nt(qq, kw) * (ATT_HEAD_DIM ** -0.5 * LOG2E) + bias2


def _softmax_parts(sc):
    mx = jnp.max(sc, axis=1, keepdims=True)
    p = jnp.exp2(sc - mx)
    l = jnp.sum(p, axis=1, keepdims=True)
    return p.astype(BF16), jnp.broadcast_to(1.0 / l, (sc.shape[0], LANES))


def _pair_output(p, rl, vw, masks):
    o2 = _dot(p, vw) * rl
    n_q = o2.shape[0] // 2
    return jnp.where(masks[0], o2[:n_q], o2[n_q:])


def _band_attn_kernel(q_ref, kp_ref, kc_ref, vp_ref, vc_ref, bias_ref, o_ref,
                      kw_ref, vw_ref, sc_ref, p_ref, rl_ref):
    kw_ref[0:Q_TILE] = kp_ref[...]
    kw_ref[Q_TILE:2 * Q_TILE] = kc_ref[...]
    vw_ref[0:Q_TILE] = vp_ref[...]
    vw_ref[Q_TILE:2 * Q_TILE] = vc_ref[...]
    masks = _head_masks()
    cs = lambda hp: slice(hp * LANES, (hp + 1) * LANES)

    def sub(s, carry):
        r0 = pl.multiple_of(s * Q_SUB, Q_SUB)
        rows = pl.ds(r0, Q_SUB)
        win = pl.ds(r0, K_WIN)
        for hp in range(N_PAIRS):
            sc_ref[hp] = _pair_scores(q_ref[rows, cs(hp)], kw_ref[win, cs(hp)], bias_ref[hp], masks)

        @pl.when(pl.program_id(1) == 0)
        def _():
            col = lax.broadcasted_iota(jnp.int32, (1, K_WIN), 1)
            extra = jnp.where(col + r0 < Q_TILE, NEG_INF, 0.0)
            for hp in range(N_PAIRS):
                sc_ref[hp] = sc_ref[hp] + extra

        for hp in range(N_PAIRS):
            for g0 in range(0, 2 * Q_SUB, SOFTMAX_ROWS):
                gs = slice(g0, g0 + SOFTMAX_ROWS)
                p_ref[hp, gs, :], rl_ref[hp, gs, :] = _softmax_parts(sc_ref[hp, gs, :])
        for hp in range(N_PAIRS):
            o = _pair_output(p_ref[hp], rl_ref[hp], vw_ref[win, cs(hp)], masks)
            o_ref[rows, cs(hp)] = o.astype(o_ref.dtype)
        return carry

    lax.fori_loop(0, Q_TILE // Q_SUB, sub, 0)


def _rel_bias_tile(rel_bias, n_q, n_k, offset, ok):
    r = np.arange(n_q - 1 + n_k)
    idx = np.clip(offset + n_q - 1 - r, -REL_CLIP, REL_CLIP) + REL_CLIP
    v = jnp.pad(rel_bias[:, idx] * LOG2E, ((0, 0), (0, 1)))
    w = n_q + n_k
    flat = jnp.tile(v, (1, n_q))[:, :n_q * (w - 1)]
    toep = flat.reshape(rel_bias.shape[0], n_q, w - 1)[:, :, n_q - 1:]
    return jnp.where(ok[None], toep, NEG_INF).reshape(N_PAIRS, 2 * n_q, n_k)


def _band_attention_prompt(qkvm, rel_bias, n_batch, seq):
    assert seq % Q_TILE == 0
    nt = seq // Q_TILE
    qc = np.arange(Q_SUB)[:, None] // CHUNK
    kc = np.arange(K_WIN)[None, :] // CHUNK
    bias = _rel_bias_tile(rel_bias, Q_SUB, K_WIN, BAND_PAST, (kc >= qc) & (kc <= qc + BAND_PAST_CHUNKS))
    blk = (Q_TILE, ATT_WIDTH)
    cur = lambda col: pl.BlockSpec(blk, lambda b, t: (b * nt + t, col))
    prev = lambda col: pl.BlockSpec(blk, lambda b, t: (b * nt + jnp.maximum(t - 1, 0), col))
    return pl.pallas_call(
        _band_attn_kernel,
        grid=(n_batch, nt),
        in_specs=[cur(0), prev(1), cur(1), prev(2), cur(2),
                  pl.BlockSpec((N_PAIRS, 2 * Q_SUB, K_WIN), lambda b, t: (0, 0, 0))],
        out_specs=pl.BlockSpec(blk, lambda b, t: (b * nt + t, 0)),
        out_shape=jax.ShapeDtypeStruct((n_batch * seq, ATT_WIDTH), BF16),
        scratch_shapes=[pltpu.VMEM((2 * Q_TILE, ATT_WIDTH), BF16),
                        pltpu.VMEM((2 * Q_TILE, ATT_WIDTH), BF16),
                        pltpu.VMEM((N_PAIRS, 2 * Q_SUB, K_WIN), F32),
                        pltpu.VMEM((N_PAIRS, 2 * Q_SUB, K_WIN), BF16),
                        pltpu.VMEM((N_PAIRS, 2 * Q_SUB, LANES), F32)],
        compiler_params=_params(("parallel", "arbitrary"), 48),
        name="band_attn",
    )(qkvm, qkvm, qkvm, qkvm, qkvm, bias)


def _band_attn_sample_kernel(q_ref, kn_ref, vn_ref, ck_ref, cv_ref, bias_ref, o_ref, kw_ref, vw_ref,
                             *, n_cache, n_new):
    kw_ref[...] = jnp.zeros_like(kw_ref)
    vw_ref[...] = jnp.zeros_like(vw_ref)
    kw_ref[0:n_cache] = ck_ref[0].astype(BF16)
    vw_ref[0:n_cache] = cv_ref[0].astype(BF16)
    kw_ref[n_cache:n_cache + n_new] = kn_ref[...]
    vw_ref[n_cache:n_cache + n_new] = vn_ref[...]
    masks = _head_masks()
    for hp in range(N_PAIRS):
        cs = slice(hp * LANES, (hp + 1) * LANES)
        p, rl = _softmax_parts(_pair_scores(q_ref[:, cs], kw_ref[:, cs], bias_ref[hp], masks))
        o_ref[:, cs] = _pair_output(p, rl, vw_ref[:, cs], masks).astype(o_ref.dtype)


def _band_attention_sample(qkvm, rel_bias, cache_k, cache_v, n_batch, n_new):
    n_cache = cache_k.shape[1]
    n_keys = -(-(n_cache + n_new) // LANES) * LANES
    q_pos = PAST_LEN + np.arange(n_new)[:, None]
    j = np.arange(n_keys)[None, :]
    k_pos = PAST_LEN - n_cache + j
    ok = ((j < n_cache + n_new) & (k_pos >= 0) & (k_pos // CHUNK <= q_pos // CHUNK)
          & (k_pos // CHUNK >= q_pos // CHUNK - BAND_PAST_CHUNKS))
    bias = _rel_bias_tile(rel_bias, n_new, n_keys, n_cache, ok)
    new = lambda col: pl.BlockSpec((n_new, ATT_WIDTH), lambda b: (b, col))
    cache = pl.BlockSpec((1, n_cache, ATT_WIDTH), lambda b: (b, 0, 0))
    return pl.pallas_call(
        functools.partial(_band_attn_sample_kernel, n_cache=n_cache, n_new=n_new),
        grid=(n_batch,),
        in_specs=[new(0), new(1), new(2), cache, cache,
                  pl.BlockSpec((N_PAIRS, 2 * n_new, n_keys), lambda b: (0, 0, 0))],
        out_specs=pl.BlockSpec((n_new, ATT_WIDTH), lambda b: (b, 0)),
        out_shape=jax.ShapeDtypeStruct((n_batch * n_new, ATT_WIDTH), BF16),
        scratch_shapes=[pltpu.VMEM((n_keys, ATT_WIDTH), BF16),
                        pltpu.VMEM((n_keys, ATT_WIDTH), BF16)],
        compiler_params=_params(("parallel",), 48),
        name="band_attn_sample",
    )(qkvm, qkvm, qkvm, cache_k, cache_v, bias)


def _mem_attn_kernel(q_ref, k_ref, v_ref, o_ref):
    k = k_ref[0].astype(BF16)
    v = v_ref[0].astype(BF16)
    for h in range(MEM_HEADS):
        cs = slice(h * MEM_HEAD_DIM, (h + 1) * MEM_HEAD_DIM)
        sc = _dot_nt(q_ref[:, cs], k[:, cs]) * (MEM_HEAD_DIM ** -0.5 * LOG2E)
        mx = jnp.max(sc, axis=1, keepdims=True)
        p = jnp.exp2(sc - mx)
        l = jnp.sum(p, axis=1, keepdims=True)
        o = _dot(p.astype(BF16), v[:, cs]) * (1.0 / l)
        o_ref[:, cs] = o.astype(o_ref.dtype)


def _memory_attention(qkvm, mem_k, mem_v, n_batch, seq, tq):
    nt = seq // tq
    mem = pl.BlockSpec((1, N_MEM, MEM_WIDTH), lambda b, t: (b, 0, 0))
    return pl.pallas_call(
        _mem_attn_kernel,
        grid=(n_batch, nt),
        in_specs=[pl.BlockSpec((tq, MEM_WIDTH), lambda b, t: (b * nt + t, 3)), mem, mem],
        out_specs=pl.BlockSpec((tq, MEM_WIDTH), lambda b, t: (b * nt + t, 0)),
        out_shape=jax.ShapeDtypeStruct((n_batch * seq, MEM_WIDTH), BF16),
        compiler_params=_params(("parallel", "arbitrary"), 48),
        name="mem_attn",
    )(qkvm, mem_k, mem_v)


def _gate_merge_kernel(u_ref, os_ref, oa_ref, om_ref, wgs_ref, wga_ref, wgm_ref,
                       wbs_ref, wba_ref, wbm_ref, out_ref):
    u = u_ref[...]

    def branch(o_ref, wg_ref, wb_ref):
        return jax.nn.sigmoid(_dot(u, wg_ref[...])) * _dot(o_ref[...], wb_ref[...])

    merged = (branch(os_ref, wgs_ref, wbs_ref) + branch(oa_ref, wga_ref, wba_ref)
              + branch(om_ref, wgm_ref, wbm_ref))
    out_ref[...] = merged.astype(out_ref.dtype)


def _gate_merge(u, o_s, o_a, o_m, w_gate, wb_s, wb_a, wb_m, tm, tn):
    m, d = u.shape
    gate = lambda b: pl.BlockSpec((d, tn), lambda i, n: (0, b * (d // tn) + n))
    wb = pl.BlockSpec((SSM_WIDTH, tn), lambda i, n: (0, n))
    ob = pl.BlockSpec((tm, SSM_WIDTH), lambda i, n: (i, 0))
    return pl.pallas_call(
        _gate_merge_kernel,
        grid=(m // tm, d // tn),
        in_specs=[pl.BlockSpec((tm, d), lambda i, n: (i, 0)), ob, ob, ob,
                  gate(0), gate(1), gate(2), wb, wb, wb],
        out_specs=pl.BlockSpec((tm, tn), lambda i, n: (i, n)),
        out_shape=jax.ShapeDtypeStruct((m, d), BF16),
        compiler_params=_params(("parallel", "arbitrary"), 56),
        name="gate_merge",
    )(u, o_s, o_a, o_m, w_gate, w_gate, w_gate, wb_s, wb_a, wb_m)


def _out_proj_kernel(x_ref, w_ref, h_ref, g_ref, o_ref):
    o_ref[...] = h_ref[...] + _rms(_dot(x_ref[...], w_ref[...]), g_ref[...])


def _out_proj(x, w_out, h, g_post, tm):
    m, d = h.shape
    row = lambda: pl.BlockSpec((tm, d), lambda i: (i, 0))
    return pl.pallas_call(
        _out_proj_kernel,
        grid=(m // tm,),
        in_specs=[row(), pl.BlockSpec((d, d), lambda i: (0, 0)), row(),
                  pl.BlockSpec((1, d), lambda i: (0, 0))],
        out_specs=row(),
        out_shape=jax.ShapeDtypeStruct((m, d), F32),
        compiler_params=_params(("parallel",), 56),
        name="out_proj",
    )(x, w_out, h, g_post)


def _layer(x, w, *, n_batch, seq, s0, mem_k, mem_v, cache_k, cache_v, tm, keep):
    n_tok = n_batch * seq
    n_chunks = seq // SSM_CHUNK
    big_tm = min(1024, n_tok)
    h1, u = _ffn(x, w["ffn1_norm_pre"], w["ffn1_norm_post"], w["ffn1_w_gate"], w["ffn1_w_up"],
                 w["ffn1_w_down"], w["mix_norm_pre"], big_tm, 512)

    qkvm = _matmul(u, w["w_qkvm"], BF16, tm, name="qkvm")
    if keep == seq:
        kv = _matmul(u, w["w_kv"], F32, tm, n_out=2, name="kv_tail")
    else:
        n_keep_blocks = seq // keep
        kv = _matmul(u, w["w_kv"], F32, keep, n_out=2, m_out=n_batch * keep,
                     row_map=lambda i: i * n_keep_blocks + (n_keep_blocks - 1), name="kv_tail")

    us_r = _ssm_inproj(u, w["w_ssm"], tm)
    nb = 2 if (n_chunks > 1 and n_batch % 2 == 0) else (1 if n_chunks > 1 else n_batch)
    y_r, s_fin = _ssm_core(us_r, s0, w["ssm_bd"], w["ssm_cd"], w["ssm_pw"], n_batch, n_chunks, nb)
    o_s = _ssm_glu(y_r, us_r, w["ssm_d"], w["ssm_w_glu"], w["ssm_b_glu"], min(32, n_tok // SSM_CHUNK))

    if cache_k is None:
        o_a = _band_attention_prompt(qkvm, w["att_rel_bias"], n_batch, seq)
    else:
        o_a = _band_attention_sample(qkvm, w["att_rel_bias"], cache_k, cache_v, n_batch, seq)
    o_m = _memory_attention(qkvm, mem_k, mem_v, n_batch, seq, min(512, seq))

    merged = _gate_merge(u, o_s, o_a, o_m, w["w_gate"], w["w_branch_ssm"], w["w_branch_att"],
                         w["w_branch_mem"], big_tm, 512)
    h2 = _out_proj(merged, w["w_out"], h1, w["mix_norm_post"], tm)
    y = _ffn(h2, w["ffn2_norm_pre"], w["ffn2_norm_post"], w["ffn2_w_gate"], w["ffn2_w_up"],
             w["ffn2_w_down"], None, big_tm, 512)
    return y, kv, s_fin


def kernel(x_prompt, x_sample, mem_prompt, cache_att_k, cache_att_v, cache_mem_k, cache_mem_v, state_ssm_re, state_ssm_im, ffn1_norm_pre, ffn1_norm_post, ffn1_w_gate, ffn1_w_up, ffn1_w_down, mix_norm_pre, mix_norm_post, w_in, ssm_a_re, ssm_a_im, ssm_log_dt, ssm_b_re, ssm_b_im, ssm_c_re, ssm_c_im, ssm_d, ssm_w_glu, ssm_b_glu, att_rel_bias, mem_norm, w_mem_k, w_mem_v, w_branch_ssm, w_branch_att, w_branch_mem, w_out, ffn2_norm_pre, ffn2_norm_post, ffn2_w_gate, ffn2_w_up, ffn2_w_down):
    n_bp, t_p, d = x_prompt.shape
    n_bs, t_s, _ = x_sample.shape
    depth = ffn1_norm_pre.shape[0]
    assert depth == 1 and d == D_MODEL
    keep = min(BAND_PAST, t_p)
    l = 0

    vec = lambda a: a[l].reshape(1, -1).astype(F32)
    mat = lambda a: _cast_bf16(a[l])
    bd, cd, pw = _ssm_params(ssm_a_re[l], ssm_a_im[l], ssm_log_dt[l], ssm_b_re[l],
                             ssm_b_im[l], ssm_c_re[l], ssm_c_im[l])
    w = {
        "ffn1_norm_pre": vec(ffn1_norm_pre), "ffn1_norm_post": vec(ffn1_norm_post),
        "ffn1_w_gate": mat(ffn1_w_gate), "ffn1_w_up": mat(ffn1_w_up), "ffn1_w_down": mat(ffn1_w_down),
        "mix_norm_pre": vec(mix_norm_pre), "mix_norm_post": vec(mix_norm_post),
        "w_ssm": _cast_bf16(w_in[l], 0, COL_Q),
        "w_qkvm": _cast_bf16(w_in[l], COL_Q, COL_GATE - COL_Q),
        "w_kv": _cast_bf16(w_in[l], COL_Q + ATT_WIDTH, 2 * ATT_WIDTH),
        "w_gate": _cast_bf16(w_in[l], COL_GATE),
        "ssm_bd": bd, "ssm_cd": cd, "ssm_pw": pw,
        "ssm_d": vec(ssm_d), "ssm_w_glu": mat(ssm_w_glu), "ssm_b_glu": vec(ssm_b_glu),
        "att_rel_bias": att_rel_bias[l].astype(F32),
        "w_branch_ssm": mat(w_branch_ssm), "w_branch_att": mat(w_branch_att),
        "w_branch_mem": mat(w_branch_mem), "w_out": mat(w_out),
        "ffn2_norm_pre": vec(ffn2_norm_pre), "ffn2_norm_post": vec(ffn2_norm_post),
        "ffn2_w_gate": mat(ffn2_w_gate), "ffn2_w_up": mat(ffn2_w_up), "ffn2_w_down": mat(ffn2_w_down),
    }

    w_mem = jnp.concatenate([w_mem_k[l], w_mem_v[l]], axis=1).astype(BF16)
    mk_p, mv_p = _matmul(mem_prompt.reshape(n_bp * N_MEM, d), w_mem, F32, 512, n_out=2,
                         gain=vec(mem_norm), name="memory_kv")
    mk_p = mk_p.reshape(n_bp, N_MEM, MEM_WIDTH)
    mv_p = mv_p.reshape(n_bp, N_MEM, MEM_WIDTH)
    zero_state = jnp.zeros((OCTETS, n_bp, OCT_STATE), F32)
    y_p, kv_p, sfin_p = _layer(x_prompt.reshape(n_bp * t_p, d), w, n_batch=n_bp, seq=t_p, s0=zero_state,
                               mem_k=mk_p, mem_v=mv_p, cache_k=None, cache_v=None, tm=512, keep=keep)
    sre_p, sim_p = _state_from_octets(sfin_p)

    n_cache = cache_att_k.shape[2]
    s0_s = _state_to_octets(state_ssm_re[l].astype(F32), state_ssm_im[l].astype(F32))
    y_s, kv_s, sfin_s = _layer(x_sample.reshape(n_bs * t_s, d), w, n_batch=n_bs, seq=t_s, s0=s0_s,
                               mem_k=cache_mem_k[l].reshape(n_bs, N_MEM, MEM_WIDTH),
                               mem_v=cache_mem_v[l].reshape(n_bs, N_MEM, MEM_WIDTH),
                               cache_k=cache_att_k[l].reshape(n_bs, n_cache, ATT_WIDTH),
                               cache_v=cache_att_v[l].reshape(n_bs, n_cache, ATT_WIDTH),
                               tm=n_bs * t_s, keep=t_s)
    sre_s, sim_s = _state_from_octets(sfin_s)

    heads = lambda a, nb, t: a.reshape(1, nb, t, ATT_HEADS, ATT_HEAD_DIM)
    memh = lambda a: a.reshape(1, n_bp, N_MEM, MEM_HEADS, MEM_HEAD_DIM)
    return (y_p.reshape(n_bp, t_p, d), y_s.reshape(n_bs, t_s, d),
            heads(kv_p[0], n_bp, keep), heads(kv_p[1], n_bp, keep),
            memh(mk_p), memh(mv_p), sre_p[None], sim_p[None],
            heads(kv_s[0], n_bs, t_s), heads(kv_s[1], n_bs, t_s),
            sre_s[None], sim_s[None])
```

```python
import functools
import math

import numpy as np
import jax
import jax.numpy as jnp
from jax import lax
from jax.experimental import pallas as pl
from jax.experimental.pallas import tpu as pltpu

F32 = jnp.float32
BF16 = jnp.bfloat16

D_MODEL = 2048
CHUNK = 64
BAND_PAST_CHUNKS = 8
BAND_PAST = BAND_PAST_CHUNKS * CHUNK
ATT_HEADS = 16
ATT_HEAD_DIM = 64
ATT_WIDTH = ATT_HEADS * ATT_HEAD_DIM
REL_CLIP = 128
SSM_GROUP = 16
SSM_WIDTH = 1024
SSM_GROUPS = SSM_WIDTH // SSM_GROUP
SSM_STATE = 64
N_MEM = 256
MEM_HEADS = 4
MEM_HEAD_DIM = 256
MEM_WIDTH = MEM_HEADS * MEM_HEAD_DIM
N_BRANCH = 3
EPS = 1e-6
NEG_INF = -1e30
PAST_LEN = 4096
LOG2E = math.log2(math.e)

COL_Q = SSM_WIDTH
COL_GATE = SSM_WIDTH + 3 * ATT_WIDTH + MEM_WIDTH

SSM_CHUNK = 16
OCTETS = 8
OCT_GROUPS = SSM_GROUPS // OCTETS
LANES = 128
MXU_DIM = 256
OCT_COLS = SSM_CHUNK * LANES
OCT_HALF = OCT_GROUPS * SSM_STATE
OCT_STATE = 2 * OCT_HALF

Q_TILE = 1024
Q_SUB = 128
K_WIN = Q_SUB + BAND_PAST

MIB = 1024 * 1024


def _params(sem, vmem_mib):
    return pltpu.CompilerParams(dimension_semantics=sem, vmem_limit_bytes=vmem_mib * MIB)


def _dot(a, b):
    return jnp.dot(a, b, preferred_element_type=F32)


def _dot_nt(a, b, precision=None):
    return lax.dot_general(a, b, (((1,), (1,)), ((), ())), precision=precision,
                           preferred_element_type=F32)


def _rms(xf, g):
    y = xf * lax.rsqrt(jnp.mean(xf * xf, axis=-1, keepdims=True) + EPS)
    return y * g


CAST_BLOCK_BYTES = 8 * MIB


def _cast_kernel(x_ref, o_ref):
    o_ref[...] = x_ref[...].astype(o_ref.dtype)


def _cast_bf16(w, col0=0, n_cols=None):
    rows, cols = w.shape
    n_cols = cols - col0 if n_cols is None else n_cols
    cw = math.gcd(col0, n_cols)
    assert cw % LANES == 0 and col0 + n_cols <= cols
    tr = rows
    while tr * cw * 4 > CAST_BLOCK_BYTES and tr % 32 == 0:
        tr //= 2
    cb0 = col0 // cw
    return pl.pallas_call(
        _cast_kernel,
        grid=(rows // tr, n_cols // cw),
        in_specs=[pl.BlockSpec((tr, cw), lambda i, j: (i, cb0 + j))],
        out_specs=pl.BlockSpec((tr, cw), lambda i, j: (i, j)),
        out_shape=jax.ShapeDtypeStruct((rows, n_cols), BF16),
        compiler_params=_params(("parallel", "parallel"), 32),
        name="cast_bf16",
    )(w)


FFN_SLICES = 8


def _ffn_kernel(hp_ref, hn_ref, gpre_ref, gpost_ref, wg_ref, wu_ref, wd_ref, *rest,
                n_tiles, emit_next):
    if emit_next:
        gnext_ref, out_ref, nxt_ref, *scratch = rest
    else:
        out_ref, *scratch = rest
    xn_refs, acc_refs = scratch[:2], scratch[2:]
    r = pl.program_id(0)
    f = pl.program_id(1)
    rs = hp_ref.shape[0]
    rows = pl.ds(pl.multiple_of(jnp.minimum(f, FFN_SLICES - 1) * rs, rs), rs)

    def pre_norm_slice(slot):
        xn_refs[slot][rows, :] = _rms(hn_ref[...], gpre_ref[...]).astype(BF16)

    def finish_slice(slot):
        hn = hp_ref[...] + 0.5 * _rms(acc_refs[slot][rows, :], gpost_ref[...])
        out_ref[...] = hn
        if emit_next:
            nxt_ref[...] = _rms(hn, gnext_ref[...]).astype(BF16)

    def matmul_chunk(slot):
        xn = xn_refs[slot][...]
        g = _dot(xn, wg_ref[...])
        u = _dot(xn, wu_ref[...])
        a = (g * jax.nn.sigmoid(g)) * u
        d = _dot(a.astype(BF16), wd_ref[...])
        acc_refs[slot][...] = jnp.where(f == 0, d, acc_refs[slot][...] + d)

    @pl.when((r == 0) & (f == 0))
    def _():
        acc_refs[1][...] = jnp.zeros_like(acc_refs[1])

    @pl.when(r == 0)
    def _():
        pre_norm_slice(0)

    for parity in range(2):
        @pl.when((r >= 1) & (r <= n_tiles) & (lax.rem(r, 2) == parity))
        def _():
            finish_slice(parity)
            pre_norm_slice(parity)
            matmul_chunk(1 - parity)

    @pl.when(r == n_tiles + 1)
    def _():
        finish_slice((n_tiles + 1) % 2)


def _ffn(h, g_pre, g_post, wg, wu, wd, g_next, tm, tf):
    emit_next = g_next is not None
    m, d = h.shape
    f_dim = wg.shape[1]
    nf = f_dim // tf
    n = m // tm
    rs = tm // FFN_SLICES
    assert m % tm == 0 and f_dim % tf == 0 and tm % FFN_SLICES == 0 and nf >= FFN_SLICES
    sl = lambda f: jnp.minimum(f, FFN_SLICES - 1)
    done = lambda r, f: (jnp.maximum(r - 2, 0) * FFN_SLICES + jnp.where(r >= 2, sl(f), 0), 0)
    ahead = lambda r, f: (jnp.minimum(r, n - 1) * FFN_SLICES + sl(f), 0)
    chunk = lambda r, f: jnp.where((r >= 1) & (r <= n), f, 0)
    vec = pl.BlockSpec((1, d), lambda r, f: (0, 0))
    in_specs = [pl.BlockSpec((rs, d), done), pl.BlockSpec((rs, d), ahead), vec, vec,
                pl.BlockSpec((d, tf), lambda r, f: (0, chunk(r, f))),
                pl.BlockSpec((d, tf), lambda r, f: (0, chunk(r, f))),
                pl.BlockSpec((tf, d), lambda r, f: (chunk(r, f), 0))]
    args = [h, h, g_pre, g_post, wg, wu, wd]
    out_shape = [jax.ShapeDtypeStruct((m, d), F32)]
    out_specs = [pl.BlockSpec((rs, d), done)]
    if emit_next:
        in_specs.append(vec)
        args.append(g_next)
        out_shape.append(jax.ShapeDtypeStruct((m, d), BF16))
        out_specs.append(pl.BlockSpec((rs, d), done))
    res = pl.pallas_call(
        functools.partial(_ffn_kernel, n_tiles=n, emit_next=emit_next),
        grid=(n + 2, nf),
        in_specs=in_specs,
        out_specs=out_specs,
        out_shape=out_shape,
        scratch_shapes=[pltpu.VMEM((tm, d), BF16), pltpu.VMEM((tm, d), BF16),
                        pltpu.VMEM((tm, d), F32), pltpu.VMEM((tm, d), F32)],
        compiler_params=_params(("arbitrary", "arbitrary"), 56),
        name="ffn",
    )(*args)
    return res if emit_next else res[0]


PROJ_COLS = 1024


def _mm_kernel(*refs, normed):
    if normed:
        x_ref, g_ref, w_ref, *o_refs = refs
        x = _rms(x_ref[...], g_ref[...]).astype(BF16)
    else:
        x_ref, w_ref, *o_refs = refs
        x = x_ref[...]
    col = 0
    for o_ref in o_refs:
        for j in range(0, o_ref.shape[1], PROJ_COLS):
            o_ref[:, j:j + PROJ_COLS] = _dot(x, w_ref[:, col + j:col + j + PROJ_COLS]).astype(o_ref.dtype)
        col += o_ref.shape[1]


def _matmul(x, w, out_dtype, tm, n_out=1, gain=None, m_out=None, row_map=None, name="proj"):
    m, k = x.shape
    n = w.shape[1] // n_out
    m_out = m if m_out is None else m_out
    row_map = (lambda i: i) if row_map is None else row_map
    tm = min(tm, m_out)
    assert m_out % tm == 0 and n % PROJ_COLS == 0 and n * n_out == w.shape[1]
    in_specs = [pl.BlockSpec((tm, k), lambda i: (row_map(i), 0))]
    args = [x]
    if gain is not None:
        in_specs.append(pl.BlockSpec((1, k), lambda i: (0, 0)))
        args.append(gain)
    in_specs.append(pl.BlockSpec((k, n * n_out), lambda i: (0, 0)))
    args.append(w)
    res = pl.pallas_call(
        functools.partial(_mm_kernel, normed=gain is not None),
        grid=(m_out // tm,),
        in_specs=in_specs,
        out_specs=[pl.BlockSpec((tm, n), lambda i: (i, 0))] * n_out,
        out_shape=[jax.ShapeDtypeStruct((m_out, n), out_dtype)] * n_out,
        compiler_params=_params(("parallel",), 56),
        name=name,
    )(*args)
    return res[0] if n_out == 1 else res


def _ssm_inproj_kernel(x_ref, w_ref, o_ref, nat_ref):
    r = _dot(x_ref[...], w_ref[...])
    tc = nat_ref.shape[1] // SSM_CHUNK
    for m in range(OCTETS):
        nat_ref[m] = r[:, m * LANES:(m + 1) * LANES]
    for j in range(SSM_CHUNK):
        for m in range(OCTETS):
            o_ref[m, :, j * LANES:(j + 1) * LANES] = nat_ref[m, pl.ds(j, tc, stride=SSM_CHUNK), :]


def _ssm_inproj(u, w_ssm, tm):
    n_tok = u.shape[0]
    assert n_tok % tm == 0 and tm % SSM_CHUNK == 0
    tc = tm // SSM_CHUNK
    return pl.pallas_call(
        _ssm_inproj_kernel,
        grid=(n_tok // tm,),
        in_specs=[pl.BlockSpec((tm, D_MODEL), lambda i: (i, 0)),
                  pl.BlockSpec((D_MODEL, SSM_WIDTH), lambda i: (0, 0))],
        out_specs=pl.BlockSpec((OCTETS, tc, OCT_COLS), lambda i: (0, i, 0)),
        out_shape=jax.ShapeDtypeStruct((OCTETS, n_tok // SSM_CHUNK, OCT_COLS), F32),
        scratch_shapes=[pltpu.VMEM((OCTETS, tm, LANES), F32)],
        compiler_params=_params(("parallel",), 48),
        name="ssm_inproj",
    )(u, w_ssm)


def _ssm_params(a_re, a_im, log_dt, b_re, b_im, c_re, c_im):
    dt = jnp.exp(log_dt)[:, None]
    mag = jnp.exp(a_re * dt)
    ab_re = mag * jnp.cos(a_im * dt)
    ab_im = mag * jnp.sin(a_im * dt)
    den = a_re * a_re + a_im * a_im
    n_re = ab_re - 1.0
    n_im = ab_im
    k_re = (n_re * a_re + n_im * a_im) / den
    k_im = (n_im * a_re - n_re * a_im) / den
    bb_re = k_re[..., None] * b_re - k_im[..., None] * b_im
    bb_im = k_re[..., None] * b_im + k_im[..., None] * b_re
    pr = [jnp.ones_like(ab_re)]
    pi = [jnp.zeros_like(ab_re)]
    for _ in range(SSM_CHUNK):
        pr.append(pr[-1] * ab_re - pi[-1] * ab_im)
        pi.append(pr[-2] * ab_im + pi[-1] * ab_re)
    n_pw = SSM_CHUNK + 1
    pw = jnp.concatenate([jnp.stack(pr).reshape(n_pw, OCTETS, OCT_HALF),
                          jnp.stack(pi).reshape(n_pw, OCTETS, OCT_HALF)], axis=2)
    pw = jnp.transpose(pw, (1, 0, 2))
    eye = jnp.eye(OCT_GROUPS, dtype=F32)

    def expand(x):
        x4 = x.reshape(OCTETS, OCT_GROUPS, x.shape[1], SSM_STATE)
        out = eye[None, :, None, :, None] * x4[:, :, :, None, :]
        return out.reshape(OCTETS, OCT_GROUPS * x.shape[1], OCT_HALF)

    bd = jnp.concatenate([expand(jnp.transpose(bb_re, (0, 2, 1))),
                          expand(jnp.transpose(bb_im, (0, 2, 1)))], axis=2)
    cd = jnp.concatenate([expand(c_re), expand(c_im)], axis=2)
    return bd, cd, pw


def _state_to_octets(s_re, s_im):
    b = s_re.shape[0]
    s = jnp.stack([s_re, s_im], 0).reshape(2, b, OCTETS, OCT_GROUPS, SSM_STATE)
    return jnp.transpose(s, (2, 1, 0, 3, 4)).reshape(OCTETS, b, OCT_STATE)


def _state_from_octets(s):
    b = s.shape[1]
    s = s.reshape(OCTETS, b, 2, OCT_GROUPS, SSM_STATE)
    s = jnp.transpose(s, (2, 1, 0, 3, 4)).reshape(2, b, SSM_GROUPS, SSM_STATE)
    return s[0], s[1]


def _split_bf16(x):
    hi = x.astype(BF16)
    return hi, (x - hi.astype(F32)).astype(BF16)


def _ssm_build_operators(bd_ref, cd_ref, pw_ref, t8_ref, wus_ref, wso_ref):
    bd_re, bd_im = bd_ref[0, :, :OCT_HALF], bd_ref[0, :, OCT_HALF:]
    cd_re, cd_im = cd_ref[0, :, :OCT_HALF], cd_ref[0, :, OCT_HALF:]
    c_hi, c_lo = _split_bf16(jnp.concatenate([cd_re, -cd_im], axis=1))
    blk = lambda i: slice(i * LANES, (i + 1) * LANES)
    for a in range(SSM_CHUNK // 2):
        t8_ref[blk(2 * a + 1), blk(2 * a)] = jnp.zeros((LANES, LANES), BF16)
    for k in range(SSM_CHUNK + 1):
        p_re = pw_ref[0, k:k + 1, :OCT_HALF]
        p_im = pw_ref[0, k:k + 1, OCT_HALF:]
        if k < SSM_CHUNK:
            e = jnp.concatenate([bd_re * p_re - bd_im * p_im, bd_re * p_im + bd_im * p_re], axis=1)
            wus_ref[blk(SSM_CHUNK - 1 - k), :] = e.astype(BF16)
            e_hi, e_lo = _split_bf16(e)
            lag = (_dot_nt(e_hi, c_hi) + _dot_nt(e_hi, c_lo) + _dot_nt(e_lo, c_hi)).astype(BF16)
            for j in range(SSM_CHUNK - k):
                t8_ref[blk(j), blk(j + k)] = lag
        if k >= 1:
            g = jnp.concatenate([cd_re * p_re - cd_im * p_im, -(cd_re * p_im + cd_im * p_re)], axis=1)
            wso_ref[blk(k - 1), :] = g.astype(BF16)


def _ssm_core_kernel(us_ref, s0_ref, bd_ref, cd_ref, pw_ref, y_ref, sfin_ref,
                     t8_ref, wus_ref, wso_ref, ds_ref, sp_ref, *, nb, n_chunks):
    @pl.when(pl.program_id(1) == 0)
    def _():
        _ssm_build_operators(bd_ref, cd_ref, pw_ref, t8_ref, wus_ref, wso_ref)

    ub = us_ref[0].astype(BF16)
    ds = _dot(ub, wus_ref[...])
    a16 = pw_ref[0, SSM_CHUNK:SSM_CHUNK + 1, :]
    n_blk = OCT_HALF // LANES
    blk = lambda k: slice(k * LANES, (k + 1) * LANES)

    def advance(s_re, s_im, d_re, d_im, a_re, a_im):
        return a_re * s_re - a_im * s_im + d_re, a_re * s_im + a_im * s_re + d_im

    if n_chunks == 1:
        s0 = s0_ref[0, 0]
        n_re, n_im = advance(s0[:, :OCT_HALF], s0[:, OCT_HALF:], ds[:, :OCT_HALF], ds[:, OCT_HALF:],
                             a16[:, :OCT_HALF], a16[:, OCT_HALF:])
        sfin_ref[0, 0] = jnp.concatenate([n_re, n_im], axis=1)
        spb = s0.astype(BF16)
    else:
        for k in range(2 * n_blk):
            ds_ref[k] = ds[:, blk(k)]

        def body(c, carry):
            at = pl.ds(c, nb, stride=n_chunks)
            new = list(carry)
            for k in range(n_blk):
                sp_ref[k, at, :] = carry[k]
                sp_ref[n_blk + k, at, :] = carry[n_blk + k]
                new[k], new[n_blk + k] = advance(carry[k], carry[n_blk + k], ds_ref[k, at, :],
                                                 ds_ref[n_blk + k, at, :], a16[:, blk(k)],
                                                 a16[:, blk(n_blk + k)])
            return tuple(new)

        init = tuple(s0_ref[0, 0, :, blk(k)] for k in range(2 * n_blk))
        fin = lax.fori_loop(0, n_chunks, body, init, unroll=2)
        for k in range(2 * n_blk):
            sfin_ref[0, 0, :, blk(k)] = fin[k]
        spb = jnp.concatenate([sp_ref[k] for k in range(2 * n_blk)], axis=1).astype(BF16)
    for nt in range(OCT_COLS // MXU_DIM):
        k_hi = (nt + 1) * MXU_DIM
        cs = slice(nt * MXU_DIM, k_hi)
        y_ref[0, :, cs] = _dot(ub[:, :k_hi], t8_ref[:k_hi, cs]) + _dot_nt(spb, wso_ref[cs, :])


def _ssm_core(us_r, s0, bd, cd, pw, n_batch, n_chunks, nb):
    nc = us_r.shape[1]
    rows = nb * n_chunks
    assert n_batch % nb == 0 and nc == n_batch * n_chunks
    nr = n_batch // nb
    s0 = s0.reshape(OCTETS, nr, nb, OCT_STATE)
    par = lambda rows_: pl.BlockSpec((1, rows_, OCT_STATE), lambda m, r: (m, 0, 0))
    y, sfin = pl.pallas_call(
        functools.partial(_ssm_core_kernel, nb=nb, n_chunks=n_chunks),
        grid=(OCTETS, nr),
        in_specs=[pl.BlockSpec((1, rows, OCT_COLS), lambda m, r: (m, r, 0)),
                  pl.BlockSpec((1, 1, nb, OCT_STATE), lambda m, r: (m, r, 0, 0)),
                  par(LANES), par(LANES), par(SSM_CHUNK + 1)],
        out_specs=[pl.BlockSpec((1, rows, OCT_COLS), lambda m, r: (m, r, 0)),
                   pl.BlockSpec((1, 1, nb, OCT_STATE), lambda m, r: (m, r, 0, 0))],
        out_shape=[jax.ShapeDtypeStruct((OCTETS, nc, OCT_COLS), F32),
                   jax.ShapeDtypeStruct((OCTETS, nr, nb, OCT_STATE), F32)],
        scratch_shapes=[pltpu.VMEM((OCT_COLS, OCT_COLS), BF16),
                        pltpu.VMEM((OCT_COLS, OCT_STATE), BF16),
                        pltpu.VMEM((OCT_COLS, OCT_STATE), BF16),
                        pltpu.VMEM((OCT_STATE // LANES, rows, LANES), F32),
                        pltpu.VMEM((OCT_STATE // LANES, rows, LANES), F32)],
        compiler_params=_params(("arbitrary", "arbitrary"), 56),
        name="ssm_core",
    )(us_r, s0, bd, cd, pw)
    return y, sfin.reshape(OCTETS, n_batch, OCT_STATE)


def _gelu_tanh(x):
    c = math.sqrt(2.0 / math.pi)
    return x * (0.5 * (1.0 + jnp.tanh(c * (x + 0.044715 * (x * x * x)))))


def _ssm_glu_kernel(y_ref, us_ref, d_ref, w_ref, b_ref, o_ref, nat_ref):
    tc = y_ref.shape[1]
    for i in range(SSM_CHUNK):
        cs = slice(i * LANES, (i + 1) * LANES)
        for m in range(OCTETS):
            d = d_ref[:, m * LANES:(m + 1) * LANES]
            nat_ref[m, pl.ds(i, tc, stride=SSM_CHUNK), :] = y_ref[m, :, cs] + d * us_ref[m, :, cs]
    yg = _gelu_tanh(jnp.concatenate([nat_ref[m] for m in range(OCTETS)], axis=1))
    z = _dot(yg.astype(BF16), w_ref[...]) + b_ref[...]
    o_ref[...] = (yg * jax.nn.sigmoid(z)).astype(o_ref.dtype)


def _ssm_glu(y_r, us_r, d, w_glu, b_glu, tc):
    nc = y_r.shape[1]
    assert nc % tc == 0
    oct_spec = pl.BlockSpec((OCTETS, tc, OCT_COLS), lambda c: (0, c, 0))
    vec = pl.BlockSpec((1, SSM_WIDTH), lambda c: (0, 0))
    return pl.pallas_call(
        _ssm_glu_kernel,
        grid=(nc // tc,),
        in_specs=[oct_spec, oct_spec, vec,
                  pl.BlockSpec((SSM_WIDTH, SSM_WIDTH), lambda c: (0, 0)), vec],
        out_specs=pl.BlockSpec((tc * SSM_CHUNK, SSM_WIDTH), lambda c: (c, 0)),
        out_shape=jax.ShapeDtypeStruct((nc * SSM_CHUNK, SSM_WIDTH), BF16),
        scratch_shapes=[pltpu.VMEM((OCTETS, tc * SSM_CHUNK, LANES), F32)],
        compiler_params=_params(("parallel",), 48),
        name="ssm_glu",
    )(y_r, us_r, d, w_glu, b_glu)


N_PAIRS = ATT_HEADS // 2
SOFTMAX_ROWS = 64


def _head_masks():
    lane = lax.broadcasted_iota(jnp.int32, (1, LANES), 1)
    return (lane < ATT_HEAD_DIM, lane >= ATT_HEAD_DIM)


def _pair_scores(q2, kw, bias2, masks):
    qq = jnp.concatenate([jnp.where(m, q2, jnp.zeros_like(q2)) for m in masks], axis=0)
    return _dot_nt(qq, kw) * (ATT_HEAD_DIM ** -0.5 * LOG2E) + bias2


def _softmax_parts(sc):
    mx = jnp.max(sc, axis=1, keepdims=True)
    p = jnp.exp2(sc - mx)
    l = jnp.sum(p, axis=1, keepdims=True)
    return p.astype(BF16), jnp.broadcast_to(1.0 / l, (sc.shape[0], LANES))


def _pair_output(p, rl, vw, masks):
    o2 = _dot(p, vw) * rl
    n_q = o2.shape[0] // 2
    return jnp.where(masks[0], o2[:n_q], o2[n_q:])


def _band_attn_kernel(q_ref, kp_ref, kc_ref, vp_ref, vc_ref, bias_ref, o_ref,
                      kw_ref, vw_ref, sc_ref, p_ref, rl_ref):
    kw_ref[0:BAND_PAST] = kp_ref[...]
    kw_ref[BAND_PAST:BAND_PAST + Q_TILE] = kc_ref[...]
    vw_ref[0:BAND_PAST] = vp_ref[...]
    vw_ref[BAND_PAST:BAND_PAST + Q_TILE] = vc_ref[...]
    masks = _head_masks()
    cs = lambda hp: slice(hp * LANES, (hp + 1) * LANES)

    def sub(s, carry):
        r0 = pl.multiple_of(s * Q_SUB, Q_SUB)
        rows = pl.ds(r0, Q_SUB)
        win = pl.ds(r0, K_WIN)
        for hp in range(N_PAIRS):
            sc_ref[hp] = _pair_scores(q_ref[rows, cs(hp)], kw_ref[win, cs(hp)], bias_ref[hp], masks)

        @pl.when(pl.program_id(1) == 0)
        def _():
            col = lax.broadcasted_iota(jnp.int32, (1, K_WIN), 1)
            extra = jnp.where(col + r0 < BAND_PAST, NEG_INF, 0.0)
            for hp in range(N_PAIRS):
                sc_ref[hp] = sc_ref[hp] + extra

        for hp in range(N_PAIRS):
            for g0 in range(0, 2 * Q_SUB, SOFTMAX_ROWS):
                gs = slice(g0, g0 + SOFTMAX_ROWS)
                p_ref[hp, gs, :], rl_ref[hp, gs, :] = _softmax_parts(sc_ref[hp, gs, :])
        for hp in range(N_PAIRS):
            o = _pair_output(p_ref[hp], rl_ref[hp], vw_ref[win, cs(hp)], masks)
            o_ref[rows, cs(hp)] = o.astype(o_ref.dtype)
        return carry

    lax.fori_loop(0, Q_TILE // Q_SUB, sub, 0)


def _rel_bias_tile(rel_bias, n_q, n_k, offset, ok):
    r = np.arange(n_q - 1 + n_k)
    idx = np.clip(offset + n_q - 1 - r, -REL_CLIP, REL_CLIP) + REL_CLIP
    v = jnp.pad(rel_bias[:, idx] * LOG2E, ((0, 0), (0, 1)))
    w = n_q + n_k
    flat = jnp.tile(v, (1, n_q))[:, :n_q * (w - 1)]
    toep = flat.reshape(rel_bias.shape[0], n_q, w - 1)[:, :, n_q - 1:]
    return jnp.where(ok[None], toep, NEG_INF).reshape(N_PAIRS, 2 * n_q, n_k)


def _band_attention_prompt(qkvm, rel_bias, n_batch, seq):
    assert seq % Q_TILE == 0 and Q_TILE % BAND_PAST == 0
    nt = seq // Q_TILE
    qc = np.arange(Q_SUB)[:, None] // CHUNK
    kc = np.arange(K_WIN)[None, :] // CHUNK
    bias = _rel_bias_tile(rel_bias, Q_SUB, K_WIN, BAND_PAST, (kc >= qc) & (kc <= qc + BAND_PAST_CHUNKS))
    blk = (Q_TILE, ATT_WIDTH)
    ratio = Q_TILE // BAND_PAST
    cur = lambda col: pl.BlockSpec(blk, lambda b, t: (b * nt + t, col))
    prev = lambda col: pl.BlockSpec(
        (BAND_PAST, ATT_WIDTH), lambda b, t: (jnp.maximum((b * nt + t) * ratio - 1, b * nt * ratio), col))
    return pl.pallas_call(
        _band_attn_kernel,
        grid=(n_batch, nt),
        in_specs=[cur(0), prev(1), cur(1), prev(2), cur(2),
                  pl.BlockSpec((N_PAIRS, 2 * Q_SUB, K_WIN), lambda b, t: (0, 0, 0))],
        out_specs=pl.BlockSpec(blk, lambda b, t: (b * nt + t, 0)),
        out_shape=jax.ShapeDtypeStruct((n_batch * seq, ATT_WIDTH), BF16),
        scratch_shapes=[pltpu.VMEM((BAND_PAST + Q_TILE, ATT_WIDTH), BF16),
                        pltpu.VMEM((BAND_PAST + Q_TILE, ATT_WIDTH), BF16),
                        pltpu.VMEM((N_PAIRS, 2 * Q_SUB, K_WIN), F32),
                        pltpu.VMEM((N_PAIRS, 2 * Q_SUB, K_WIN), BF16),
                        pltpu.VMEM((N_PAIRS, 2 * Q_SUB, LANES), F32)],
        compiler_params=_params(("parallel", "arbitrary"), 48),
        name="band_attn",
    )(qkvm, qkvm, qkvm, qkvm, qkvm, bias)


def _band_attn_sample_kernel(q_ref, kn_ref, vn_ref, ck_ref, cv_ref, bias_ref, o_ref, kw_ref, vw_ref,
                             *, n_cache, n_new):
    kw_ref[...] = jnp.zeros_like(kw_ref)
    vw_ref[...] = jnp.zeros_like(vw_ref)
    kw_ref[0:n_cache] = ck_ref[0].astype(BF16)
    vw_ref[0:n_cache] = cv_ref[0].astype(BF16)
    kw_ref[n_cache:n_cache + n_new] = kn_ref[...]
    vw_ref[n_cache:n_cache + n_new] = vn_ref[...]
    masks = _head_masks()
    for hp in range(N_PAIRS):
        cs = slice(hp * LANES, (hp + 1) * LANES)
        p, rl = _softmax_parts(_pair_scores(q_ref[:, cs], kw_ref[:, cs], bias_ref[hp], masks))
        o_ref[:, cs] = _pair_output(p, rl, vw_ref[:, cs], masks).astype(o_ref.dtype)


def _band_attention_sample(qkvm, rel_bias, cache_k, cache_v, n_batch, n_new):
    n_cache = cache_k.shape[1]
    n_keys = -(-(n_cache + n_new) // LANES) * LANES
    q_pos = PAST_LEN + np.arange(n_new)[:, None]
    j = np.arange(n_keys)[None, :]
    k_pos = PAST_LEN - n_cache + j
    ok = ((j < n_cache + n_new) & (k_pos >= 0) & (k_pos // CHUNK <= q_pos // CHUNK)
          & (k_pos // CHUNK >= q_pos // CHUNK - BAND_PAST_CHUNKS))
    bias = _rel_bias_tile(rel_bias, n_new, n_keys, n_cache, ok)
    new = lambda col: pl.BlockSpec((n_new, ATT_WIDTH), lambda b: (b, col))
    cache = pl.BlockSpec((1, n_cache, ATT_WIDTH), lambda b: (b, 0, 0))
    return pl.pallas_call(
        functools.partial(_band_attn_sample_kernel, n_cache=n_cache, n_new=n_new),
        grid=(n_batch,),
        in_specs=[new(0), new(1), new(2), cache, cache,
                  pl.BlockSpec((N_PAIRS, 2 * n_new, n_keys), lambda b: (0, 0, 0))],
        out_specs=pl.BlockSpec((n_new, ATT_WIDTH), lambda b: (b, 0)),
        out_shape=jax.ShapeDtypeStruct((n_batch * n_new, ATT_WIDTH), BF16),
        scratch_shapes=[pltpu.VMEM((n_keys, ATT_WIDTH), BF16),
                        pltpu.VMEM((n_keys, ATT_WIDTH), BF16)],
        compiler_params=_params(("parallel",), 48),
        name="band_attn_sample",
    )(qkvm, qkvm, qkvm, cache_k, cache_v, bias)


def _mem_attn_kernel(q_ref, k_ref, v_ref, o_ref):
    k = k_ref[0].astype(BF16)
    v = v_ref[0].astype(BF16)
    for h in range(MEM_HEADS):
        cs = slice(h * MEM_HEAD_DIM, (h + 1) * MEM_HEAD_DIM)
        sc = _dot_nt(q_ref[:, cs], k[:, cs]) * (MEM_HEAD_DIM ** -0.5 * LOG2E)
        mx = jnp.max(sc, axis=1, keepdims=True)
        p = jnp.exp2(sc - mx)
        l = jnp.sum(p, axis=1, keepdims=True)
        o = _dot(p.astype(BF16), v[:, cs]) * (1.0 / l)
        o_ref[:, cs] = o.astype(o_ref.dtype)


def _memory_attention(qkvm, mem_k, mem_v, n_batch, seq, tq):
    nt = seq // tq
    mem = pl.BlockSpec((1, N_MEM, MEM_WIDTH), lambda b, t: (b, 0, 0))
    return pl.pallas_call(
        _mem_attn_kernel,
        grid=(n_batch, nt),
        in_specs=[pl.BlockSpec((tq, MEM_WIDTH), lambda b, t: (b * nt + t, 3)), mem, mem],
        out_specs=pl.BlockSpec((tq, MEM_WIDTH), lambda b, t: (b * nt + t, 0)),
        out_shape=jax.ShapeDtypeStruct((n_batch * seq, MEM_WIDTH), BF16),
        compiler_params=_params(("parallel", "arbitrary"), 48),
        name="mem_attn",
    )(qkvm, mem_k, mem_v)


def _gate_merge_kernel(u_ref, os_ref, oa_ref, om_ref, wgs_ref, wga_ref, wgm_ref,
                       wbs_ref, wba_ref, wbm_ref, out_ref):
    u = u_ref[...]

    def branch(o_ref, wg_ref, wb_ref):
        return jax.nn.sigmoid(_dot(u, wg_ref[...])) * _dot(o_ref[...], wb_ref[...])

    merged = (branch(os_ref, wgs_ref, wbs_ref) + branch(oa_ref, wga_ref, wba_ref)
              + branch(om_ref, wgm_ref, wbm_ref))
    out_ref[...] = merged.astype(out_ref.dtype)


def _gate_merge(u, o_s, o_a, o_m, w_gate, wb_s, wb_a, wb_m, tm, tn):
    m, d = u.shape
    gate = lambda b: pl.BlockSpec((d, tn), lambda i, n: (0, b * (d // tn) + n))
    wb = pl.BlockSpec((SSM_WIDTH, tn), lambda i, n: (0, n))
    ob = pl.BlockSpec((tm, SSM_WIDTH), lambda i, n: (i, 0))
    return pl.pallas_call(
        _gate_merge_kernel,
        grid=(m // tm, d // tn),
        in_specs=[pl.BlockSpec((tm, d), lambda i, n: (i, 0)), ob, ob, ob,
                  gate(0), gate(1), gate(2), wb, wb, wb],
        out_specs=pl.BlockSpec((tm, tn), lambda i, n: (i, n)),
        out_shape=jax.ShapeDtypeStruct((m, d), BF16),
        compiler_params=_params(("parallel", "arbitrary"), 56),
        name="gate_merge",
    )(u, o_s, o_a, o_m, w_gate, w_gate, w_gate, wb_s, wb_a, wb_m)


def _out_proj_kernel(x_ref, w_ref, h_ref, g_ref, o_ref):
    o_ref[...] = h_ref[...] + _rms(_dot(x_ref[...], w_ref[...]), g_ref[...])


def _out_proj(x, w_out, h, g_post, tm):
    m, d = h.shape
    row = lambda: pl.BlockSpec((tm, d), lambda i: (i, 0))
    return pl.pallas_call(
        _out_proj_kernel,
        grid=(m // tm,),
        in_specs=[row(), pl.BlockSpec((d, d), lambda i: (0, 0)), row(),
                  pl.BlockSpec((1, d), lambda i: (0, 0))],
        out_specs=row(),
        out_shape=jax.ShapeDtypeStruct((m, d), F32),
        compiler_params=_params(("parallel",), 56),
        name="out_proj",
    )(x, w_out, h, g_post)


def _layer(x, w, *, n_batch, seq, s0, mem_k, mem_v, cache_k, cache_v, tm, keep):
    n_tok = n_batch * seq
    n_chunks = seq // SSM_CHUNK
    big_tm = min(1024, n_tok)
    h1, u = _ffn(x, w["ffn1_norm_pre"], w["ffn1_norm_post"], w["ffn1_w_gate"], w["ffn1_w_up"],
                 w["ffn1_w_down"], w["mix_norm_pre"], big_tm, 512)

    qkvm = _matmul(u, w["w_qkvm"], BF16, tm, name="qkvm")
    if keep == seq:
        kv = _matmul(u, w["w_kv"], F32, tm, n_out=2, name="kv_tail")
    else:
        n_keep_blocks = seq // keep
        kv = _matmul(u, w["w_kv"], F32, keep, n_out=2, m_out=n_batch * keep,
                     row_map=lambda i: i * n_keep_blocks + (n_keep_blocks - 1), name="kv_tail")

    us_r = _ssm_inproj(u, w["w_ssm"], big_tm)
    nb = 2 if (n_chunks > 1 and n_batch % 2 == 0) else (1 if n_chunks > 1 else n_batch)
    y_r, s_fin = _ssm_core(us_r, s0, w["ssm_bd"], w["ssm_cd"], w["ssm_pw"], n_batch, n_chunks, nb)
    o_s = _ssm_glu(y_r, us_r, w["ssm_d"], w["ssm_w_glu"], w["ssm_b_glu"], min(64, n_tok // SSM_CHUNK))

    if cache_k is None:
        o_a = _band_attention_prompt(qkvm, w["att_rel_bias"], n_batch, seq)
    else:
        o_a = _band_attention_sample(qkvm, w["att_rel_bias"], cache_k, cache_v, n_batch, seq)
    o_m = _memory_attention(qkvm, mem_k, mem_v, n_batch, seq, min(1024, seq))

    merged = _gate_merge(u, o_s, o_a, o_m, w["w_gate"], w["w_branch_ssm"], w["w_branch_att"],
                         w["w_branch_mem"], big_tm, 512)
    h2 = _out_proj(merged, w["w_out"], h1, w["mix_norm_post"], tm)
    y = _ffn(h2, w["ffn2_norm_pre"], w["ffn2_norm_post"], w["ffn2_w_gate"], w["ffn2_w_up"],
             w["ffn2_w_down"], None, big_tm, 512)
    return y, kv, s_fin


def kernel(x_prompt, x_sample, mem_prompt, cache_att_k, cache_att_v, cache_mem_k, cache_mem_v, state_ssm_re, state_ssm_im, ffn1_norm_pre, ffn1_norm_post, ffn1_w_gate, ffn1_w_up, ffn1_w_down, mix_norm_pre, mix_norm_post, w_in, ssm_a_re, ssm_a_im, ssm_log_dt, ssm_b_re, ssm_b_im, ssm_c_re, ssm_c_im, ssm_d, ssm_w_glu, ssm_b_glu, att_rel_bias, mem_norm, w_mem_k, w_mem_v, w_branch_ssm, w_branch_att, w_branch_mem, w_out, ffn2_norm_pre, ffn2_norm_post, ffn2_w_gate, ffn2_w_up, ffn2_w_down):
    n_bp, t_p, d = x_prompt.shape
    n_bs, t_s, _ = x_sample.shape
    depth = ffn1_norm_pre.shape[0]
    assert depth == 1 and d == D_MODEL
    keep = min(BAND_PAST, t_p)
    l = 0

    vec = lambda a: a[l].reshape(1, -1).astype(F32)
    mat = lambda a: _cast_bf16(a[l])
    bd, cd, pw = _ssm_params(ssm_a_re[l], ssm_a_im[l], ssm_log_dt[l], ssm_b_re[l],
                             ssm_b_im[l], ssm_c_re[l], ssm_c_im[l])
    w = {
        "ffn1_norm_pre": vec(ffn1_norm_pre), "ffn1_norm_post": vec(ffn1_norm_post),
        "ffn1_w_gate": mat(ffn1_w_gate), "ffn1_w_up": mat(ffn1_w_up), "ffn1_w_down": mat(ffn1_w_down),
        "mix_norm_pre": vec(mix_norm_pre), "mix_norm_post": vec(mix_norm_post),
        "w_ssm": _cast_bf16(w_in[l], 0, COL_Q),
        "w_qkvm": _cast_bf16(w_in[l], COL_Q, COL_GATE - COL_Q),
        "w_kv": _cast_bf16(w_in[l], COL_Q + ATT_WIDTH, 2 * ATT_WIDTH),
        "w_gate": _cast_bf16(w_in[l], COL_GATE),
        "ssm_bd": bd, "ssm_cd": cd, "ssm_pw": pw,
        "ssm_d": vec(ssm_d), "ssm_w_glu": mat(ssm_w_glu), "ssm_b_glu": vec(ssm_b_glu),
        "att_rel_bias": att_rel_bias[l].astype(F32),
        "w_branch_ssm": mat(w_branch_ssm), "w_branch_att": mat(w_branch_att),
        "w_branch_mem": mat(w_branch_mem), "w_out": mat(w_out),
        "ffn2_norm_pre": vec(ffn2_norm_pre), "ffn2_norm_post": vec(ffn2_norm_post),
        "ffn2_w_gate": mat(ffn2_w_gate), "ffn2_w_up": mat(ffn2_w_up), "ffn2_w_down": mat(ffn2_w_down),
    }

    w_mem = jnp.concatenate([w_mem_k[l], w_mem_v[l]], axis=1).astype(BF16)
    mk_p, mv_p = _matmul(mem_prompt.reshape(n_bp * N_MEM, d), w_mem, F32, 512, n_out=2,
                         gain=vec(mem_norm), name="memory_kv")
    mk_p = mk_p.reshape(n_bp, N_MEM, MEM_WIDTH)
    mv_p = mv_p.reshape(n_bp, N_MEM, MEM_WIDTH)
    zero_state = jnp.zeros((OCTETS, n_bp, OCT_STATE), F32)
    y_p, kv_p, sfin_p = _layer(x_prompt.reshape(n_bp * t_p, d), w, n_batch=n_bp, seq=t_p, s0=zero_state,
                               mem_k=mk_p, mem_v=mv_p, cache_k=None, cache_v=None, tm=512, keep=keep)
    sre_p, sim_p = _state_from_octets(sfin_p)

    n_cache = cache_att_k.shape[2]
    s0_s = _state_to_octets(state_ssm_re[l].astype(F32), state_ssm_im[l].astype(F32))
    y_s, kv_s, sfin_s = _layer(x_sample.reshape(n_bs * t_s, d), w, n_batch=n_bs, seq=t_s, s0=s0_s,
                               mem_k=cache_mem_k[l].reshape(n_bs, N_MEM, MEM_WIDTH),
                               mem_v=cache_mem_v[l].reshape(n_bs, N_MEM, MEM_WIDTH),
                               cache_k=cache_att_k[l].reshape(n_bs, n_cache, ATT_WIDTH),
                               cache_v=cache_att_v[l].reshape(n_bs, n_cache, ATT_WIDTH),
                               tm=n_bs * t_s, keep=t_s)
    sre_s, sim_s = _state_from_octets(sfin_s)

    heads = lambda a, nb, t: a.reshape(1, nb, t, ATT_HEADS, ATT_HEAD_DIM)
    memh = lambda a: a.reshape(1, n_bp, N_MEM, MEM_HEADS, MEM_HEAD_DIM)
    return (y_p.reshape(n_bp, t_p, d), y_s.reshape(n_bs, t_s, d),
            heads(kv_p[0], n_bp, keep), heads(kv_p[1], n_bp, keep),
            memh(mk_p), memh(mv_p), sre_p[None], sim_p[None],
            heads(kv_s[0], n_bs, t_s), heads(kv_s[1], n_bs, t_s),
            sre_s[None], sim_s[None])
```

```python
import functools
import math

import numpy as np
import jax
import jax.numpy as jnp
from jax import lax
from jax.experimental import pallas as pl
from jax.experimental.pallas import tpu as pltpu

F32 = jnp.float32
BF16 = jnp.bfloat16

D_MODEL = 2048
CHUNK = 64
BAND_PAST_CHUNKS = 8
BAND_PAST = BAND_PAST_CHUNKS * CHUNK
ATT_HEADS = 16
ATT_HEAD_DIM = 64
ATT_WIDTH = ATT_HEADS * ATT_HEAD_DIM
REL_CLIP = 128
SSM_GROUP = 16
SSM_WIDTH = 1024
SSM_GROUPS = SSM_WIDTH // SSM_GROUP
SSM_STATE = 64
N_MEM = 256
MEM_HEADS = 4
MEM_HEAD_DIM = 256
MEM_WIDTH = MEM_HEADS * MEM_HEAD_DIM
N_BRANCH = 3
EPS = 1e-6
NEG_INF = -1e30
PAST_LEN = 4096
LOG2E = math.log2(math.e)

COL_Q = SSM_WIDTH
COL_GATE = SSM_WIDTH + 3 * ATT_WIDTH + MEM_WIDTH

SSM_CHUNK = 16
OCTETS = 8
OCT_GROUPS = SSM_GROUPS // OCTETS
LANES = 128
MXU_DIM = 256
OCT_COLS = SSM_CHUNK * LANES
OCT_HALF = OCT_GROUPS * SSM_STATE
OCT_STATE = 2 * OCT_HALF

Q_TILE = 512
Q_SUB = 128
K_WIN = Q_SUB + BAND_PAST

MIB = 1024 * 1024


def _params(sem, vmem_mib):
    return pltpu.CompilerParams(dimension_semantics=sem, vmem_limit_bytes=vmem_mib * MIB)


def _dot(a, b):
    return jnp.dot(a, b, preferred_element_type=F32)


def _dot_nt(a, b, precision=None):
    return lax.dot_general(a, b, (((1,), (1,)), ((), ())), precision=precision,
                           preferred_element_type=F32)


def _rms(xf, g):
    y = xf * lax.rsqrt(jnp.mean(xf * xf, axis=-1, keepdims=True) + EPS)
    return y * g


CAST_BLOCK_BYTES = 8 * MIB


def _cast_kernel(x_ref, o_ref):
    o_ref[...] = x_ref[...].astype(o_ref.dtype)


def _cast_bf16(w, col0=0, n_cols=None):
    rows, cols = w.shape
    n_cols = cols - col0 if n_cols is None else n_cols
    cw = math.gcd(col0, n_cols)
    assert cw % LANES == 0 and col0 + n_cols <= cols
    tr = rows
    while tr * cw * 4 > CAST_BLOCK_BYTES and tr % 32 == 0:
        tr //= 2
    cb0 = col0 // cw
    return pl.pallas_call(
        _cast_kernel,
        grid=(rows // tr, n_cols // cw),
        in_specs=[pl.BlockSpec((tr, cw), lambda i, j: (i, cb0 + j))],
        out_specs=pl.BlockSpec((tr, cw), lambda i, j: (i, j)),
        out_shape=jax.ShapeDtypeStruct((rows, n_cols), BF16),
        compiler_params=_params(("parallel", "parallel"), 32),
        name="cast_bf16",
    )(w)


FFN_SLICES = 8


def _ffn_kernel(hp_ref, hn_ref, gpre_ref, gpost_ref, wg_ref, wu_ref, wd_ref, *rest,
                n_tiles, emit_next):
    if emit_next:
        gnext_ref, out_ref, nxt_ref, *scratch = rest
    else:
        out_ref, *scratch = rest
    xn_refs, acc_refs = scratch[:2], scratch[2:]
    r = pl.program_id(0)
    f = pl.program_id(1)
    rs = hp_ref.shape[0]
    rows = pl.ds(pl.multiple_of(jnp.minimum(f, FFN_SLICES - 1) * rs, rs), rs)

    def pre_norm_slice(slot):
        xn_refs[slot][rows, :] = _rms(hn_ref[...], gpre_ref[...]).astype(BF16)

    def finish_slice(slot):
        hn = hp_ref[...] + 0.5 * _rms(acc_refs[slot][rows, :], gpost_ref[...])
        out_ref[...] = hn
        if emit_next:
            nxt_ref[...] = _rms(hn, gnext_ref[...]).astype(BF16)

    def matmul_chunk(slot):
        xn = xn_refs[slot][...]
        g = _dot(xn, wg_ref[...])
        u = _dot(xn, wu_ref[...])
        a = (g * jax.nn.sigmoid(g)) * u
        d = _dot(a.astype(BF16), wd_ref[...])
        acc_refs[slot][...] = jnp.where(f == 0, d, acc_refs[slot][...] + d)

    @pl.when((r == 0) & (f == 0))
    def _():
        acc_refs[1][...] = jnp.zeros_like(acc_refs[1])

    @pl.when(r == 0)
    def _():
        pre_norm_slice(0)

    for parity in range(2):
        @pl.when((r >= 1) & (r <= n_tiles) & (lax.rem(r, 2) == parity))
        def _():
            finish_slice(parity)
            pre_norm_slice(parity)
            matmul_chunk(1 - parity)

    @pl.when(r == n_tiles + 1)
    def _():
        finish_slice((n_tiles + 1) % 2)


def _ffn(h, g_pre, g_post, wg, wu, wd, g_next, tm, tf):
    emit_next = g_next is not None
    m, d = h.shape
    f_dim = wg.shape[1]
    nf = f_dim // tf
    n = m // tm
    rs = tm // FFN_SLICES
    assert m % tm == 0 and f_dim % tf == 0 and tm % FFN_SLICES == 0 and nf >= FFN_SLICES
    sl = lambda f: jnp.minimum(f, FFN_SLICES - 1)
    done = lambda r, f: (jnp.maximum(r - 2, 0) * FFN_SLICES + jnp.where(r >= 2, sl(f), 0), 0)
    ahead = lambda r, f: (jnp.minimum(r, n - 1) * FFN_SLICES + sl(f), 0)
    chunk = lambda r, f: jnp.where((r >= 1) & (r <= n), f, 0)
    vec = pl.BlockSpec((1, d), lambda r, f: (0, 0))
    in_specs = [pl.BlockSpec((rs, d), done), pl.BlockSpec((rs, d), ahead), vec, vec,
                pl.BlockSpec((d, tf), lambda r, f: (0, chunk(r, f))),
                pl.BlockSpec((d, tf), lambda r, f: (0, chunk(r, f))),
                pl.BlockSpec((tf, d), lambda r, f: (chunk(r, f), 0))]
    args = [h, h, g_pre, g_post, wg, wu, wd]
    out_shape = [jax.ShapeDtypeStruct((m, d), F32)]
    out_specs = [pl.BlockSpec((rs, d), done)]
    if emit_next:
        in_specs.append(vec)
        args.append(g_next)
        out_shape.append(jax.ShapeDtypeStruct((m, d), BF16))
        out_specs.append(pl.BlockSpec((rs, d), done))
    res = pl.pallas_call(
        functools.partial(_ffn_kernel, n_tiles=n, emit_next=emit_next),
        grid=(n + 2, nf),
        in_specs=in_specs,
        out_specs=out_specs,
        out_shape=out_shape,
        scratch_shapes=[pltpu.VMEM((tm, d), BF16), pltpu.VMEM((tm, d), BF16),
                        pltpu.VMEM((tm, d), F32), pltpu.VMEM((tm, d), F32)],
        compiler_params=_params(("arbitrary", "arbitrary"), 56),
        name="ffn",
    )(*args)
    return res if emit_next else res[0]


PROJ_COLS = 1024


def _mm_kernel(*refs, normed):
    if normed:
        x_ref, g_ref, w_ref, *o_refs = refs
        x = _rms(x_ref[...], g_ref[...]).astype(BF16)
    else:
        x_ref, w_ref, *o_refs = refs
        x = x_ref[...]
    col = 0
    for o_ref in o_refs:
        for j in range(0, o_ref.shape[1], PROJ_COLS):
            o_ref[:, j:j + PROJ_COLS] = _dot(x, w_ref[:, col + j:col + j + PROJ_COLS]).astype(o_ref.dtype)
        col += o_ref.shape[1]


def _matmul(x, w, out_dtype, tm, n_out=1, gain=None, m_out=None, row_map=None, name="proj"):
    m, k = x.shape
    n = w.shape[1] // n_out
    m_out = m if m_out is None else m_out
    row_map = (lambda i: i) if row_map is None else row_map
    tm = min(tm, m_out)
    assert m_out % tm == 0 and n % PROJ_COLS == 0 and n * n_out == w.shape[1]
    in_specs = [pl.BlockSpec((tm, k), lambda i: (row_map(i), 0))]
    args = [x]
    if gain is not None:
        in_specs.append(pl.BlockSpec((1, k), lambda i: (0, 0)))
        args.append(gain)
    in_specs.append(pl.BlockSpec((k, n * n_out), lambda i: (0, 0)))
    args.append(w)
    res = pl.pallas_call(
        functools.partial(_mm_kernel, normed=gain is not None),
        grid=(m_out // tm,),
        in_specs=in_specs,
        out_specs=[pl.BlockSpec((tm, n), lambda i: (i, 0))] * n_out,
        out_shape=[jax.ShapeDtypeStruct((m_out, n), out_dtype)] * n_out,
        compiler_params=_params(("parallel",), 56),
        name=name,
    )(*args)
    return res[0] if n_out == 1 else res


def _ssm_inproj_kernel(x_ref, w_ref, o_ref, nat_ref):
    r = _dot(x_ref[...], w_ref[...])
    tc = nat_ref.shape[1] // SSM_CHUNK
    for m in range(OCTETS):
        nat_ref[m] = r[:, m * LANES:(m + 1) * LANES]
    for j in range(SSM_CHUNK):
        for m in range(OCTETS):
            o_ref[m, :, j * LANES:(j + 1) * LANES] = nat_ref[m, pl.ds(j, tc, stride=SSM_CHUNK), :]


def _ssm_inproj(u, w_ssm, tm):
    n_tok = u.shape[0]
    assert n_tok % tm == 0 and tm % SSM_CHUNK == 0
    tc = tm // SSM_CHUNK
    return pl.pallas_call(
        _ssm_inproj_kernel,
        grid=(n_tok // tm,),
        in_specs=[pl.BlockSpec((tm, D_MODEL), lambda i: (i, 0)),
                  pl.BlockSpec((D_MODEL, SSM_WIDTH), lambda i: (0, 0))],
        out_specs=pl.BlockSpec((OCTETS, tc, OCT_COLS), lambda i: (0, i, 0)),
        out_shape=jax.ShapeDtypeStruct((OCTETS, n_tok // SSM_CHUNK, OCT_COLS), F32),
        scratch_shapes=[pltpu.VMEM((OCTETS, tm, LANES), F32)],
        compiler_params=_params(("parallel",), 48),
        name="ssm_inproj",
    )(u, w_ssm)


def _ssm_params(a_re, a_im, log_dt, b_re, b_im, c_re, c_im):
    dt = jnp.exp(log_dt)[:, None]
    mag = jnp.exp(a_re * dt)
    ab_re = mag * jnp.cos(a_im * dt)
    ab_im = mag * jnp.sin(a_im * dt)
    den = a_re * a_re + a_im * a_im
    n_re = ab_re - 1.0
    n_im = ab_im
    k_re = (n_re * a_re + n_im * a_im) / den
    k_im = (n_im * a_re - n_re * a_im) / den
    bb_re = k_re[..., None] * b_re - k_im[..., None] * b_im
    bb_im = k_re[..., None] * b_im + k_im[..., None] * b_re
    pr = [jnp.ones_like(ab_re)]
    pi = [jnp.zeros_like(ab_re)]
    for _ in range(SSM_CHUNK):
        pr.append(pr[-1] * ab_re - pi[-1] * ab_im)
        pi.append(pr[-2] * ab_im + pi[-1] * ab_re)
    n_pw = SSM_CHUNK + 1
    pw = jnp.concatenate([jnp.stack(pr).reshape(n_pw, OCTETS, OCT_HALF),
                          jnp.stack(pi).reshape(n_pw, OCTETS, OCT_HALF)], axis=2)
    pw = jnp.transpose(pw, (1, 0, 2))
    eye = jnp.eye(OCT_GROUPS, dtype=F32)

    def expand(x):
        x4 = x.reshape(OCTETS, OCT_GROUPS, x.shape[1], SSM_STATE)
        out = eye[None, :, None, :, None] * x4[:, :, :, None, :]
        return out.reshape(OCTETS, OCT_GROUPS * x.shape[1], OCT_HALF)

    bd = jnp.concatenate([expand(jnp.transpose(bb_re, (0, 2, 1))),
                          expand(jnp.transpose(bb_im, (0, 2, 1)))], axis=2)
    cd = jnp.concatenate([expand(c_re), expand(c_im)], axis=2)
    return bd, cd, pw


def _state_to_octets(s_re, s_im):
    b = s_re.shape[0]
    s = jnp.stack([s_re, s_im], 0).reshape(2, b, OCTETS, OCT_GROUPS, SSM_STATE)
    return jnp.transpose(s, (2, 1, 0, 3, 4)).reshape(OCTETS, b, OCT_STATE)


def _state_from_octets(s):
    b = s.shape[1]
    s = s.reshape(OCTETS, b, 2, OCT_GROUPS, SSM_STATE)
    s = jnp.transpose(s, (2, 1, 0, 3, 4)).reshape(2, b, SSM_GROUPS, SSM_STATE)
    return s[0], s[1]


def _split_bf16(x):
    hi = x.astype(BF16)
    return hi, (x - hi.astype(F32)).astype(BF16)


def _ssm_build_operators(bd_ref, cd_ref, pw_ref, t8_ref, wus_ref, wso_ref):
    bd_re, bd_im = bd_ref[0, :, :OCT_HALF], bd_ref[0, :, OCT_HALF:]
    cd_re, cd_im = cd_ref[0, :, :OCT_HALF], cd_ref[0, :, OCT_HALF:]
    c_hi, c_lo = _split_bf16(jnp.concatenate([cd_re, -cd_im], axis=1))
    blk = lambda i: slice(i * LANES, (i + 1) * LANES)
    for a in range(SSM_CHUNK // 2):
        t8_ref[blk(2 * a + 1), blk(2 * a)] = jnp.zeros((LANES, LANES), BF16)
    for k in range(SSM_CHUNK + 1):
        p_re = pw_ref[0, k:k + 1, :OCT_HALF]
        p_im = pw_ref[0, k:k + 1, OCT_HALF:]
        if k < SSM_CHUNK:
            e = jnp.concatenate([bd_re * p_re - bd_im * p_im, bd_re * p_im + bd_im * p_re], axis=1)
            wus_ref[blk(SSM_CHUNK - 1 - k), :] = e.astype(BF16)
            e_hi, e_lo = _split_bf16(e)
            lag = (_dot_nt(e_hi, c_hi) + _dot_nt(e_hi, c_lo) + _dot_nt(e_lo, c_hi)).astype(BF16)
            for j in range(SSM_CHUNK - k):
                t8_ref[blk(j), blk(j + k)] = lag
        if k >= 1:
            g = jnp.concatenate([cd_re * p_re - cd_im * p_im, -(cd_re * p_im + cd_im * p_re)], axis=1)
            wso_ref[blk(k - 1), :] = g.astype(BF16)


def _ssm_core_kernel(us_ref, s0_ref, bd_ref, cd_ref, pw_ref, y_ref, sfin_ref,
                     t8_ref, wus_ref, wso_ref, ds_ref, sp_ref, *, nb, n_chunks):
    @pl.when(pl.program_id(1) == 0)
    def _():
        _ssm_build_operators(bd_ref, cd_ref, pw_ref, t8_ref, wus_ref, wso_ref)

    ub = us_ref[0].astype(BF16)
    ds = _dot(ub, wus_ref[...])
    a16 = pw_ref[0, SSM_CHUNK:SSM_CHUNK + 1, :]
    n_blk = OCT_HALF // LANES
    blk = lambda k: slice(k * LANES, (k + 1) * LANES)

    def advance(s_re, s_im, d_re, d_im, a_re, a_im):
        return a_re * s_re - a_im * s_im + d_re, a_re * s_im + a_im * s_re + d_im

    if n_chunks == 1:
        s0 = s0_ref[0, 0]
        n_re, n_im = advance(s0[:, :OCT_HALF], s0[:, OCT_HALF:], ds[:, :OCT_HALF], ds[:, OCT_HALF:],
                             a16[:, :OCT_HALF], a16[:, OCT_HALF:])
        sfin_ref[0, 0] = jnp.concatenate([n_re, n_im], axis=1)
        spb = s0.astype(BF16)
    else:
        for k in range(2 * n_blk):
            ds_ref[k] = ds[:, blk(k)]

        def body(c, carry):
            at = pl.ds(c, nb, stride=n_chunks)
            new = list(carry)
            for k in range(n_blk):
                sp_ref[k, at, :] = carry[k]
                sp_ref[n_blk + k, at, :] = carry[n_blk + k]
                new[k], new[n_blk + k] = advance(carry[k], carry[n_blk + k], ds_ref[k, at, :],
                                                 ds_ref[n_blk + k, at, :], a16[:, blk(k)],
                                                 a16[:, blk(n_blk + k)])
            return tuple(new)

        init = tuple(s0_ref[0, 0, :, blk(k)] for k in range(2 * n_blk))
        fin = lax.fori_loop(0, n_chunks, body, init, unroll=2)
        for k in range(2 * n_blk):
            sfin_ref[0, 0, :, blk(k)] = fin[k]
        spb = jnp.concatenate([sp_ref[k] for k in range(2 * n_blk)], axis=1).astype(BF16)
    for nt in range(OCT_COLS // MXU_DIM):
        k_hi = (nt + 1) * MXU_DIM
        cs = slice(nt * MXU_DIM, k_hi)
        y_ref[0, :, cs] = _dot(ub[:, :k_hi], t8_ref[:k_hi, cs]) + _dot_nt(spb, wso_ref[cs, :])


def _ssm_core(us_r, s0, bd, cd, pw, n_batch, n_chunks, nb):
    nc = us_r.shape[1]
    rows = nb * n_chunks
    assert n_batch % nb == 0 and nc == n_batch * n_chunks
    nr = n_batch // nb
    s0 = s0.reshape(OCTETS, nr, nb, OCT_STATE)
    par = lambda rows_: pl.BlockSpec((1, rows_, OCT_STATE), lambda m, r: (m, 0, 0))
    y, sfin = pl.pallas_call(
        functools.partial(_ssm_core_kernel, nb=nb, n_chunks=n_chunks),
        grid=(OCTETS, nr),
        in_specs=[pl.BlockSpec((1, rows, OCT_COLS), lambda m, r: (m, r, 0)),
                  pl.BlockSpec((1, 1, nb, OCT_STATE), lambda m, r: (m, r, 0, 0)),
                  par(LANES), par(LANES), par(SSM_CHUNK + 1)],
        out_specs=[pl.BlockSpec((1, rows, OCT_COLS), lambda m, r: (m, r, 0)),
                   pl.BlockSpec((1, 1, nb, OCT_STATE), lambda m, r: (m, r, 0, 0))],
        out_shape=[jax.ShapeDtypeStruct((OCTETS, nc, OCT_COLS), F32),
                   jax.ShapeDtypeStruct((OCTETS, nr, nb, OCT_STATE), F32)],
        scratch_shapes=[pltpu.VMEM((OCT_COLS, OCT_COLS), BF16),
                        pltpu.VMEM((OCT_COLS, OCT_STATE), BF16),
                        pltpu.VMEM((OCT_COLS, OCT_STATE), BF16),
                        pltpu.VMEM((OCT_STATE // LANES, rows, LANES), F32),
                        pltpu.VMEM((OCT_STATE // LANES, rows, LANES), F32)],
        compiler_params=_params(("arbitrary", "arbitrary"), 56),
        name="ssm_core",
    )(us_r, s0, bd, cd, pw)
    return y, sfin.reshape(OCTETS, n_batch, OCT_STATE)


def _gelu_tanh(x):
    c = math.sqrt(2.0 / math.pi)
    return x * (0.5 * (1.0 + jnp.tanh(c * (x + 0.044715 * (x * x * x)))))


def _ssm_glu_kernel(y_ref, us_ref, d_ref, w_ref, b_ref, o_ref, nat_ref):
    tc = y_ref.shape[1]
    for i in range(SSM_CHUNK):
        cs = slice(i * LANES, (i + 1) * LANES)
        for m in range(OCTETS):
            d = d_ref[:, m * LANES:(m + 1) * LANES]
            nat_ref[m, pl.ds(i, tc, stride=SSM_CHUNK), :] = y_ref[m, :, cs] + d * us_ref[m, :, cs]
    yg = _gelu_tanh(jnp.concatenate([nat_ref[m] for m in range(OCTETS)], axis=1))
    z = _dot(yg.astype(BF16), w_ref[...]) + b_ref[...]
    o_ref[...] = (yg * jax.nn.sigmoid(z)).astype(o_ref.dtype)


def _ssm_glu(y_r, us_r, d, w_glu, b_glu, tc):
    nc = y_r.shape[1]
    assert nc % tc == 0
    oct_spec = pl.BlockSpec((OCTETS, tc, OCT_COLS), lambda c: (0, c, 0))
    vec = pl.BlockSpec((1, SSM_WIDTH), lambda c: (0, 0))
    return pl.pallas_call(
        _ssm_glu_kernel,
        grid=(nc // tc,),
        in_specs=[oct_spec, oct_spec, vec,
                  pl.BlockSpec((SSM_WIDTH, SSM_WIDTH), lambda c: (0, 0)), vec],
        out_specs=pl.BlockSpec((tc * SSM_CHUNK, SSM_WIDTH), lambda c: (c, 0)),
        out_shape=jax.ShapeDtypeStruct((nc * SSM_CHUNK, SSM_WIDTH), BF16),
        scratch_shapes=[pltpu.VMEM((OCTETS, tc * SSM_CHUNK, LANES), F32)],
        compiler_params=_params(("parallel",), 48),
        name="ssm_glu",
    )(y_r, us_r, d, w_glu, b_glu)


N_PAIRS = ATT_HEADS // 2


def _head_masks():
    lane = lax.broadcasted_iota(jnp.int32, (1, LANES), 1)
    return (lane < ATT_HEAD_DIM, lane >= ATT_HEAD_DIM)


def _pair_scores(q2, kw, bias2, masks):
    qq = jnp.concatenate([jnp.where(m, q2, jnp.zeros_like(q2)) for m in masks], axis=0)
    return _dot_nt(qq, kw) * (ATT_HEAD_DIM ** -0.5 * LOG2E) + bias2


def _row_max(sc):
    return jnp.broadcast_to(jnp.max(sc, axis=1, keepdims=True), (sc.shape[0], LANES))


def _softmax_parts(sc, mx):
    p = jnp.exp2(sc - jnp.concatenate([mx] * (sc.shape[1] // LANES), axis=1))
    l = jnp.sum(p, axis=1, keepdims=True)
    return p.astype(BF16), jnp.broadcast_to(1.0 / l, (sc.shape[0], LANES))


def _pair_output(p, rl, vw, masks):
    o2 = _dot(p, vw) * rl
    n_q = o2.shape[0] // 2
    return jnp.where(masks[0], o2[:n_q], o2[n_q:])


def _band_attn_kernel(q_ref, kp_ref, kc_ref, vp_ref, vc_ref, bias_ref, o_ref,
                      kw_ref, vw_ref, sc_ref, mx_ref, p_ref, rl_ref):
    kw_ref[0:BAND_PAST] = kp_ref[...]
    kw_ref[BAND_PAST:BAND_PAST + Q_TILE] = kc_ref[...]
    vw_ref[0:BAND_PAST] = vp_ref[...]
    vw_ref[BAND_PAST:BAND_PAST + Q_TILE] = vc_ref[...]
    masks = _head_masks()
    cs = lambda hp: slice(hp * LANES, (hp + 1) * LANES)

    def sub(s, carry):
        r0 = pl.multiple_of(s * Q_SUB, Q_SUB)
        rows = pl.ds(r0, Q_SUB)
        win = pl.ds(r0, K_WIN)
        for hp in range(N_PAIRS):
            sc = _pair_scores(q_ref[rows, cs(hp)], kw_ref[win, cs(hp)], bias_ref[hp], masks)
            sc_ref[hp] = sc
            mx_ref[hp] = _row_max(sc)

        @pl.when(pl.program_id(1) == 0)
        def _():
            col = lax.broadcasted_iota(jnp.int32, (1, K_WIN), 1)
            extra = jnp.where(col + r0 < BAND_PAST, NEG_INF, 0.0)
            for hp in range(N_PAIRS):
                sc = sc_ref[hp] + extra
                sc_ref[hp] = sc
                mx_ref[hp] = _row_max(sc)

        for hp in range(N_PAIRS):
            p_ref[hp], rl_ref[hp] = _softmax_parts(sc_ref[hp], mx_ref[hp])
        for hp in range(N_PAIRS):
            o = _pair_output(p_ref[hp], rl_ref[hp], vw_ref[win, cs(hp)], masks)
            o_ref[rows, cs(hp)] = o.astype(o_ref.dtype)
        return carry

    lax.fori_loop(0, Q_TILE // Q_SUB, sub, 0)


def _rel_bias_tile(rel_bias, n_q, n_k, offset, ok):
    r = np.arange(n_q - 1 + n_k)
    idx = np.clip(offset + n_q - 1 - r, -REL_CLIP, REL_CLIP) + REL_CLIP
    v = jnp.pad(rel_bias[:, idx] * LOG2E, ((0, 0), (0, 1)))
    w = n_q + n_k
    flat = jnp.tile(v, (1, n_q))[:, :n_q * (w - 1)]
    toep = flat.reshape(rel_bias.shape[0], n_q, w - 1)[:, :, n_q - 1:]
    return jnp.where(ok[None], toep, NEG_INF).reshape(N_PAIRS, 2 * n_q, n_k)


def _band_attention_prompt(qkvm, rel_bias, n_batch, seq):
    assert seq % Q_TILE == 0 and Q_TILE % BAND_PAST == 0
    nt = seq // Q_TILE
    qc = np.arange(Q_SUB)[:, None] // CHUNK
    kc = np.arange(K_WIN)[None, :] // CHUNK
    bias = _rel_bias_tile(rel_bias, Q_SUB, K_WIN, BAND_PAST, (kc >= qc) & (kc <= qc + BAND_PAST_CHUNKS))
    blk = (Q_TILE, ATT_WIDTH)
    ratio = Q_TILE // BAND_PAST
    cur = lambda col: pl.BlockSpec(blk, lambda b, t: (b * nt + t, col))
    prev = lambda col: pl.BlockSpec(
        (BAND_PAST, ATT_WIDTH), lambda b, t: (jnp.maximum((b * nt + t) * ratio - 1, b * nt * ratio), col))
    return pl.pallas_call(
        _band_attn_kernel,
        grid=(n_batch, nt),
        in_specs=[cur(0), prev(1), cur(1), prev(2), cur(2),
                  pl.BlockSpec((N_PAIRS, 2 * Q_SUB, K_WIN), lambda b, t: (0, 0, 0))],
        out_specs=pl.BlockSpec(blk, lambda b, t: (b * nt + t, 0)),
        out_shape=jax.ShapeDtypeStruct((n_batch * seq, ATT_WIDTH), BF16),
        scratch_shapes=[pltpu.VMEM((BAND_PAST + Q_TILE, ATT_WIDTH), BF16),
                        pltpu.VMEM((BAND_PAST + Q_TILE, ATT_WIDTH), BF16),
                        pltpu.VMEM((N_PAIRS, 2 * Q_SUB, K_WIN), F32),
                        pltpu.VMEM((N_PAIRS, 2 * Q_SUB, LANES), F32),
                        pltpu.VMEM((N_PAIRS, 2 * Q_SUB, K_WIN), BF16),
                        pltpu.VMEM((N_PAIRS, 2 * Q_SUB, LANES), F32)],
        compiler_params=_params(("parallel", "arbitrary"), 48),
        name="band_attn",
    )(qkvm, qkvm, qkvm, qkvm, qkvm, bias)


def _band_attn_sample_kernel(q_ref, kn_ref, vn_ref, ck_ref, cv_ref, bias_ref, o_ref, kw_ref, vw_ref,
                             *, n_cache, n_new):
    kw_ref[...] = jnp.zeros_like(kw_ref)
    vw_ref[...] = jnp.zeros_like(vw_ref)
    kw_ref[0:n_cache] = ck_ref[0].astype(BF16)
    vw_ref[0:n_cache] = cv_ref[0].astype(BF16)
    kw_ref[n_cache:n_cache + n_new] = kn_ref[...]
    vw_ref[n_cache:n_cache + n_new] = vn_ref[...]
    masks = _head_masks()
    for hp in range(N_PAIRS):
        cs = slice(hp * LANES, (hp + 1) * LANES)
        sc = _pair_scores(q_ref[:, cs], kw_ref[:, cs], bias_ref[hp], masks)
        p, rl = _softmax_parts(sc, _row_max(sc))
        o_ref[:, cs] = _pair_output(p, rl, vw_ref[:, cs], masks).astype(o_ref.dtype)


def _band_attention_sample(qkvm, rel_bias, cache_k, cache_v, n_batch, n_new):
    n_cache = cache_k.shape[1]
    n_keys = -(-(n_cache + n_new) // LANES) * LANES
    q_pos = PAST_LEN + np.arange(n_new)[:, None]
    j = np.arange(n_keys)[None, :]
    k_pos = PAST_LEN - n_cache + j
    ok = ((j < n_cache + n_new) & (k_pos >= 0) & (k_pos // CHUNK <= q_pos // CHUNK)
          & (k_pos // CHUNK >= q_pos // CHUNK - BAND_PAST_CHUNKS))
    bias = _rel_bias_tile(rel_bias, n_new, n_keys, n_cache, ok)
    new = lambda col: pl.BlockSpec((n_new, ATT_WIDTH), lambda b: (b, col))
    cache = pl.BlockSpec((1, n_cache, ATT_WIDTH), lambda b: (b, 0, 0))
    return pl.pallas_call(
        functools.partial(_band_attn_sample_kernel, n_cache=n_cache, n_new=n_new),
        grid=(n_batch,),
        in_specs=[new(0), new(1), new(2), cache, cache,
                  pl.BlockSpec((N_PAIRS, 2 * n_new, n_keys), lambda b: (0, 0, 0))],
        out_specs=pl.BlockSpec((n_new, ATT_WIDTH), lambda b: (b, 0)),
        out_shape=jax.ShapeDtypeStruct((n_batch * n_new, ATT_WIDTH), BF16),
        scratch_shapes=[pltpu.VMEM((n_keys, ATT_WIDTH), BF16),
                        pltpu.VMEM((n_keys, ATT_WIDTH), BF16)],
        compiler_params=_params(("parallel",), 48),
        name="band_attn_sample",
    )(qkvm, qkvm, qkvm, cache_k, cache_v, bias)


def _mem_attn_kernel(q_ref, k_ref, v_ref, o_ref):
    k = k_ref[0].astype(BF16)
    v = v_ref[0].astype(BF16)
    for h in range(MEM_HEADS):
        cs = slice(h * MEM_HEAD_DIM, (h + 1) * MEM_HEAD_DIM)
        sc = _dot_nt(q_ref[:, cs], k[:, cs]) * (MEM_HEAD_DIM ** -0.5 * LOG2E)
        mx = jnp.max(sc, axis=1, keepdims=True)
        p = jnp.exp2(sc - mx)
        l = jnp.sum(p, axis=1, keepdims=True)
        o = _dot(p.astype(BF16), v[:, cs]) * (1.0 / l)
        o_ref[:, cs] = o.astype(o_ref.dtype)


def _memory_attention(qkvm, mem_k, mem_v, n_batch, seq, tq):
    nt = seq // tq
    mem = pl.BlockSpec((1, N_MEM, MEM_WIDTH), lambda b, t: (b, 0, 0))
    return pl.pallas_call(
        _mem_attn_kernel,
        grid=(n_batch, nt),
        in_specs=[pl.BlockSpec((tq, MEM_WIDTH), lambda b, t: (b * nt + t, 3)), mem, mem],
        out_specs=pl.BlockSpec((tq, MEM_WIDTH), lambda b, t: (b * nt + t, 0)),
        out_shape=jax.ShapeDtypeStruct((n_batch * seq, MEM_WIDTH), BF16),
        compiler_params=_params(("parallel", "arbitrary"), 48),
        name="mem_attn",
    )(qkvm, mem_k, mem_v)


def _gate_merge_kernel(u_ref, os_ref, oa_ref, om_ref, wgs_ref, wga_ref, wgm_ref,
                       wbs_ref, wba_ref, wbm_ref, out_ref):
    u = u_ref[...]

    def branch(o_ref, wg_ref, wb_ref):
        return jax.nn.sigmoid(_dot(u, wg_ref[...])) * _dot(o_ref[...], wb_ref[...])

    merged = (branch(os_ref, wgs_ref, wbs_ref) + branch(oa_ref, wga_ref, wba_ref)
              + branch(om_ref, wgm_ref, wbm_ref))
    out_ref[...] = merged.astype(out_ref.dtype)


def _gate_merge(u, o_s, o_a, o_m, w_gate, wb_s, wb_a, wb_m, tm, tn):
    m, d = u.shape
    gate = lambda b: pl.BlockSpec((d, tn), lambda i, n: (0, b * (d // tn) + n))
    wb = pl.BlockSpec((SSM_WIDTH, tn), lambda i, n: (0, n))
    ob = pl.BlockSpec((tm, SSM_WIDTH), lambda i, n: (i, 0))
    return pl.pallas_call(
        _gate_merge_kernel,
        grid=(m // tm, d // tn),
        in_specs=[pl.BlockSpec((tm, d), lambda i, n: (i, 0)), ob, ob, ob,
                  gate(0), gate(1), gate(2), wb, wb, wb],
        out_specs=pl.BlockSpec((tm, tn), lambda i, n: (i, n)),
        out_shape=jax.ShapeDtypeStruct((m, d), BF16),
        compiler_params=_params(("parallel", "arbitrary"), 56),
        name="gate_merge",
    )(u, o_s, o_a, o_m, w_gate, w_gate, w_gate, wb_s, wb_a, wb_m)


def _out_proj_kernel(x_ref, w_ref, h_ref, g_ref, o_ref):
    o_ref[...] = h_ref[...] + _rms(_dot(x_ref[...], w_ref[...]), g_ref[...])


def _out_proj(x, w_out, h, g_post, tm):
    m, d = h.shape
    row = lambda: pl.BlockSpec((tm, d), lambda i: (i, 0))
    return pl.pallas_call(
        _out_proj_kernel,
        grid=(m // tm,),
        in_specs=[row(), pl.BlockSpec((d, d), lambda i: (0, 0)), row(),
                  pl.BlockSpec((1, d), lambda i: (0, 0))],
        out_specs=row(),
        out_shape=jax.ShapeDtypeStruct((m, d), F32),
        compiler_params=_params(("parallel",), 56),
        name="out_proj",
    )(x, w_out, h, g_post)


def _layer(x, w, *, n_batch, seq, s0, mem_k, mem_v, cache_k, cache_v, tm, keep):
    n_tok = n_batch * seq
    n_chunks = seq // SSM_CHUNK
    big_tm = min(1024, n_tok)
    h1, u = _ffn(x, w["ffn1_norm_pre"], w["ffn1_norm_post"], w["ffn1_w_gate"], w["ffn1_w_up"],
                 w["ffn1_w_down"], w["mix_norm_pre"], big_tm, 512)

    qkvm = _matmul(u, w["w_qkvm"], BF16, tm, name="qkvm")
    if keep == seq:
        kv = _matmul(u, w["w_kv"], F32, tm, n_out=2, name="kv_tail")
    else:
        n_keep_blocks = seq // keep
        kv = _matmul(u, w["w_kv"], F32, keep, n_out=2, m_out=n_batch * keep,
                     row_map=lambda i: i * n_keep_blocks + (n_keep_blocks - 1), name="kv_tail")

    us_r = _ssm_inproj(u, w["w_ssm"], big_tm)
    nb = 2 if (n_chunks > 1 and n_batch % 2 == 0) else (1 if n_chunks > 1 else n_batch)
    y_r, s_fin = _ssm_core(us_r, s0, w["ssm_bd"], w["ssm_cd"], w["ssm_pw"], n_batch, n_chunks, nb)
    o_s = _ssm_glu(y_r, us_r, w["ssm_d"], w["ssm_w_glu"], w["ssm_b_glu"], min(64, n_tok // SSM_CHUNK))

    if cache_k is None:
        o_a = _band_attention_prompt(qkvm, w["att_rel_bias"], n_batch, seq)
    else:
        o_a = _band_attention_sample(qkvm, w["att_rel_bias"], cache_k, cache_v, n_batch, seq)
    o_m = _memory_attention(qkvm, mem_k, mem_v, n_batch, seq, min(2048, seq))

    merged = _gate_merge(u, o_s, o_a, o_m, w["w_gate"], w["w_branch_ssm"], w["w_branch_att"],
                         w["w_branch_mem"], big_tm, 512)
    h2 = _out_proj(merged, w["w_out"], h1, w["mix_norm_post"], tm)
    y = _ffn(h2, w["ffn2_norm_pre"], w["ffn2_norm_post"], w["ffn2_w_gate"], w["ffn2_w_up"],
             w["ffn2_w_down"], None, big_tm, 512)
    return y, kv, s_fin


def kernel(x_prompt, x_sample, mem_prompt, cache_att_k, cache_att_v, cache_mem_k, cache_mem_v, state_ssm_re, state_ssm_im, ffn1_norm_pre, ffn1_norm_post, ffn1_w_gate, ffn1_w_up, ffn1_w_down, mix_norm_pre, mix_norm_post, w_in, ssm_a_re, ssm_a_im, ssm_log_dt, ssm_b_re, ssm_b_im, ssm_c_re, ssm_c_im, ssm_d, ssm_w_glu, ssm_b_glu, att_rel_bias, mem_norm, w_mem_k, w_mem_v, w_branch_ssm, w_branch_att, w_branch_mem, w_out, ffn2_norm_pre, ffn2_norm_post, ffn2_w_gate, ffn2_w_up, ffn2_w_down):
    n_bp, t_p, d = x_prompt.shape
    n_bs, t_s, _ = x_sample.shape
    depth = ffn1_norm_pre.shape[0]
    assert depth == 1 and d == D_MODEL
    keep = min(BAND_PAST, t_p)
    l = 0

    vec = lambda a: a[l].reshape(1, -1).astype(F32)
    mat = lambda a: _cast_bf16(a[l])
    bd, cd, pw = _ssm_params(ssm_a_re[l], ssm_a_im[l], ssm_log_dt[l], ssm_b_re[l],
                             ssm_b_im[l], ssm_c_re[l], ssm_c_im[l])
    w = {
        "ffn1_norm_pre": vec(ffn1_norm_pre), "ffn1_norm_post": vec(ffn1_norm_post),
        "ffn1_w_gate": mat(ffn1_w_gate), "ffn1_w_up": mat(ffn1_w_up), "ffn1_w_down": mat(ffn1_w_down),
        "mix_norm_pre": vec(mix_norm_pre), "mix_norm_post": vec(mix_norm_post),
        "w_ssm": _cast_bf16(w_in[l], 0, COL_Q),
        "w_qkvm": _cast_bf16(w_in[l], COL_Q, COL_GATE - COL_Q),
        "w_kv": _cast_bf16(w_in[l], COL_Q + ATT_WIDTH, 2 * ATT_WIDTH),
        "w_gate": _cast_bf16(w_in[l], COL_GATE),
        "ssm_bd": bd, "ssm_cd": cd, "ssm_pw": pw,
        "ssm_d": vec(ssm_d), "ssm_w_glu": mat(ssm_w_glu), "ssm_b_glu": vec(ssm_b_glu),
        "att_rel_bias": att_rel_bias[l].astype(F32),
        "w_branch_ssm": mat(w_branch_ssm), "w_branch_att": mat(w_branch_att),
        "w_branch_mem": mat(w_branch_mem), "w_out": mat(w_out),
        "ffn2_norm_pre": vec(ffn2_norm_pre), "ffn2_norm_post": vec(ffn2_norm_post),
        "ffn2_w_gate": mat(ffn2_w_gate), "ffn2_w_up": mat(ffn2_w_up), "ffn2_w_down": mat(ffn2_w_down),
    }

    w_mem = jnp.concatenate([w_mem_k[l], w_mem_v[l]], axis=1).astype(BF16)
    mk_p, mv_p = _matmul(mem_prompt.reshape(n_bp * N_MEM, d), w_mem, F32, 512, n_out=2,
                         gain=vec(mem_norm), name="memory_kv")
    mk_p = mk_p.reshape(n_bp, N_MEM, MEM_WIDTH)
    mv_p = mv_p.reshape(n_bp, N_MEM, MEM_WIDTH)
    zero_state = jnp.zeros((OCTETS, n_bp, OCT_STATE), F32)
    y_p, kv_p, sfin_p = _layer(x_prompt.reshape(n_bp * t_p, d), w, n_batch=n_bp, seq=t_p, s0=zero_state,
                               mem_k=mk_p, mem_v=mv_p, cache_k=None, cache_v=None, tm=512, keep=keep)
    sre_p, sim_p = _state_from_octets(sfin_p)

    n_cache = cache_att_k.shape[2]
    s0_s = _state_to_octets(state_ssm_re[l].astype(F32), state_ssm_im[l].astype(F32))
    y_s, kv_s, sfin_s = _layer(x_sample.reshape(n_bs * t_s, d), w, n_batch=n_bs, seq=t_s, s0=s0_s,
                               mem_k=cache_mem_k[l].reshape(n_bs, N_MEM, MEM_WIDTH),
                               mem_v=cache_mem_v[l].reshape(n_bs, N_MEM, MEM_WIDTH),
                               cache_k=cache_att_k[l].reshape(n_bs, n_cache, ATT_WIDTH),
                               cache_v=cache_att_v[l].reshape(n_bs, n_cache, ATT_WIDTH),
                               tm=n_bs * t_s, keep=t_s)
    sre_s, sim_s = _state_from_octets(sfin_s)

    heads = lambda a, nb, t: a.reshape(1, nb, t, ATT_HEADS, ATT_HEAD_DIM)
    memh = lambda a: a.reshape(1, n_bp, N_MEM, MEM_HEADS, MEM_HEAD_DIM)
    return (y_p.reshape(n_bp, t_p, d), y_s.reshape(n_bs, t_s, d),
            heads(kv_p[0], n_bp, keep), heads(kv_p[1], n_bp, keep),
            memh(mk_p), memh(mv_p), sre_p[None], sim_p[None],
            heads(kv_s[0], n_bs, t_s), heads(kv_s[1], n_bs, t_s),
            sre_s[None], sim_s[None])
```

```python
import functools
import math

import numpy as np
import jax
import jax.numpy as jnp
from jax import lax
from jax.experimental import pallas as pl
from jax.experimental.pallas import tpu as pltpu

F32 = jnp.float32
BF16 = jnp.bfloat16

D_MODEL = 2048
CHUNK = 64
BAND_PAST_CHUNKS = 8
BAND_PAST = BAND_PAST_CHUNKS * CHUNK
ATT_HEADS = 16
ATT_HEAD_DIM = 64
ATT_WIDTH = ATT_HEADS * ATT_HEAD_DIM
REL_CLIP = 128
SSM_GROUP = 16
SSM_WIDTH = 1024
SSM_GROUPS = SSM_WIDTH // SSM_GROUP
SSM_STATE = 64
N_MEM = 256
MEM_HEADS = 4
MEM_HEAD_DIM = 256
MEM_WIDTH = MEM_HEADS * MEM_HEAD_DIM
EPS = 1e-6
NEG_INF = -1e30
PAST_LEN = 4096
LOG2E = math.log2(math.e)

COL_Q = SSM_WIDTH
COL_GATE = SSM_WIDTH + 3 * ATT_WIDTH + MEM_WIDTH

SSM_CHUNK = 16
OCTETS = 8
OCT_GROUPS = SSM_GROUPS // OCTETS
LANES = 128
MXU_DIM = 256
OCT_COLS = SSM_CHUNK * LANES
OCT_HALF = OCT_GROUPS * SSM_STATE
OCT_STATE = 2 * OCT_HALF

Q_TILE = 512
Q_SUB = 128
K_WIN = Q_SUB + BAND_PAST

MIB = 1024 * 1024
VMEM_BIG_MIB = 56
VMEM_MID_MIB = 48
VMEM_SMALL_MIB = 32


def _params(sem, vmem_mib):
    return pltpu.CompilerParams(dimension_semantics=sem, vmem_limit_bytes=vmem_mib * MIB)


def _dot(a, b):
    return jnp.dot(a, b, preferred_element_type=F32)


def _dot_nt(a, b):
    return lax.dot_general(a, b, (((1,), (1,)), ((), ())), preferred_element_type=F32)


def _rms(xf, g):
    y = xf * lax.rsqrt(jnp.mean(xf * xf, axis=-1, keepdims=True) + EPS)
    return y * g


CAST_BLOCK_BYTES = 8 * MIB


def _cast_kernel(x_ref, o_ref):
    o_ref[...] = x_ref[...].astype(o_ref.dtype)


def _cast_bf16(w, col0=0, n_cols=None):
    rows, cols = w.shape
    n_cols = cols - col0 if n_cols is None else n_cols
    cw = math.gcd(col0, n_cols)
    assert cw % LANES == 0 and col0 + n_cols <= cols
    tr = rows
    while tr * cw * 4 > CAST_BLOCK_BYTES and tr % 32 == 0:
        tr //= 2
    cb0 = col0 // cw
    return pl.pallas_call(
        _cast_kernel,
        grid=(rows // tr, n_cols // cw),
        in_specs=[pl.BlockSpec((tr, cw), lambda i, j: (i, cb0 + j))],
        out_specs=pl.BlockSpec((tr, cw), lambda i, j: (i, j)),
        out_shape=jax.ShapeDtypeStruct((rows, n_cols), BF16),
        compiler_params=_params(("parallel", "parallel"), VMEM_SMALL_MIB),
        name="cast_bf16",
    )(w)


FFN_SLICES = 8


def _ffn_kernel(hp_ref, hn_ref, gpre_ref, gpost_ref, wg_ref, wu_ref, wd_ref, *rest,
                n_tiles, emit_next):
    if emit_next:
        gnext_ref, out_ref, nxt_ref, *scratch = rest
    else:
        out_ref, *scratch = rest
    xn_refs, acc_refs = scratch[:2], scratch[2:]
    r = pl.program_id(0)
    f = pl.program_id(1)
    rs = hp_ref.shape[0]
    rows = pl.ds(pl.multiple_of(jnp.minimum(f, FFN_SLICES - 1) * rs, rs), rs)

    def pre_norm_slice(slot):
        xn_refs[slot][rows, :] = _rms(hn_ref[...], gpre_ref[...]).astype(BF16)

    def finish_slice(slot):
        hn = hp_ref[...] + 0.5 * _rms(acc_refs[slot][rows, :], gpost_ref[...])
        out_ref[...] = hn
        if emit_next:
            nxt_ref[...] = _rms(hn, gnext_ref[...]).astype(BF16)

    def matmul_chunk(slot):
        xn = xn_refs[slot][...]
        g = _dot(xn, wg_ref[...])
        u = _dot(xn, wu_ref[...])
        a = (g * jax.nn.sigmoid(g)) * u
        d = _dot(a.astype(BF16), wd_ref[...])
        acc_refs[slot][...] = jnp.where(f == 0, d, acc_refs[slot][...] + d)

    @pl.when((r == 0) & (f == 0))
    def _():
        acc_refs[1][...] = jnp.zeros_like(acc_refs[1])

    @pl.when(r == 0)
    def _():
        pre_norm_slice(0)

    for parity in range(2):
        @pl.when((r >= 1) & (r <= n_tiles) & (lax.rem(r, 2) == parity))
        def _():
            finish_slice(parity)
            pre_norm_slice(parity)
            matmul_chunk(1 - parity)

    @pl.when(r == n_tiles + 1)
    def _():
        finish_slice((n_tiles + 1) % 2)


def _ffn(h, g_pre, g_post, wg, wu, wd, g_next, tm, tf):
    emit_next = g_next is not None
    m, d = h.shape
    f_dim = wg.shape[1]
    nf = f_dim // tf
    n = m // tm
    rs = tm // FFN_SLICES
    assert m % tm == 0 and f_dim % tf == 0 and tm % FFN_SLICES == 0 and nf >= FFN_SLICES
    sl = lambda f: jnp.minimum(f, FFN_SLICES - 1)
    done = lambda r, f: (jnp.maximum(r - 2, 0) * FFN_SLICES + jnp.where(r >= 2, sl(f), 0), 0)
    ahead = lambda r, f: (jnp.minimum(r, n - 1) * FFN_SLICES + sl(f), 0)
    chunk = lambda r, f: jnp.where((r >= 1) & (r <= n), f, 0)
    vec = pl.BlockSpec((1, d), lambda r, f: (0, 0))
    in_specs = [pl.BlockSpec((rs, d), done), pl.BlockSpec((rs, d), ahead), vec, vec,
                pl.BlockSpec((d, tf), lambda r, f: (0, chunk(r, f))),
                pl.BlockSpec((d, tf), lambda r, f: (0, chunk(r, f))),
                pl.BlockSpec((tf, d), lambda r, f: (chunk(r, f), 0))]
    args = [h, h, g_pre, g_post, wg, wu, wd]
    out_shape = [jax.ShapeDtypeStruct((m, d), F32)]
    out_specs = [pl.BlockSpec((rs, d), done)]
    if emit_next:
        in_specs.append(vec)
        args.append(g_next)
        out_shape.append(jax.ShapeDtypeStruct((m, d), BF16))
        out_specs.append(pl.BlockSpec((rs, d), done))
    res = pl.pallas_call(
        functools.partial(_ffn_kernel, n_tiles=n, emit_next=emit_next),
        grid=(n + 2, nf),
        in_specs=in_specs,
        out_specs=out_specs,
        out_shape=out_shape,
        scratch_shapes=[pltpu.VMEM((tm, d), BF16), pltpu.VMEM((tm, d), BF16),
                        pltpu.VMEM((tm, d), F32), pltpu.VMEM((tm, d), F32)],
        compiler_params=_params(("arbitrary", "arbitrary"), VMEM_BIG_MIB),
        name="ffn",
    )(*args)
    return res if emit_next else res[0]


def _memory_kv_kernel(x_ref, g_ref, w_ref, k_ref, v_ref):
    x = _rms(x_ref[...], g_ref[...]).astype(BF16)
    k_ref[...] = _dot(x, w_ref[:, :MEM_WIDTH])
    v_ref[...] = _dot(x, w_ref[:, MEM_WIDTH:])


def _memory_kv(x, gain, w, tm):
    m, k = x.shape
    tm = min(tm, m)
    assert m % tm == 0 and w.shape == (k, 2 * MEM_WIDTH)
    out = pl.BlockSpec((tm, MEM_WIDTH), lambda i: (i, 0))
    return pl.pallas_call(
        _memory_kv_kernel,
        grid=(m // tm,),
        in_specs=[pl.BlockSpec((tm, k), lambda i: (i, 0)), pl.BlockSpec((1, k), lambda i: (0, 0)),
                  pl.BlockSpec((k, 2 * MEM_WIDTH), lambda i: (0, 0))],
        out_specs=[out, out],
        out_shape=[jax.ShapeDtypeStruct((m, MEM_WIDTH), F32)] * 2,
        compiler_params=_params(("parallel",), VMEM_MID_MIB),
        name="memory_kv",
    )(x, gain, w)


def _in_proj_kernel(x_ref, wq_ref, ws_ref, o_ref, k_ref, v_ref, us_ref, nat_ref):
    x = x_ref[...]
    for j, keep_ref in enumerate((None, k_ref, v_ref, None)):
        cs = slice(j * ATT_WIDTH, (j + 1) * ATT_WIDTH)
        r = _dot(x, wq_ref[:, cs])
        o_ref[:, cs] = r.astype(o_ref.dtype)
        if keep_ref is not None:
            keep_ref[...] = r
    r = _dot(x, ws_ref[...])
    tc = nat_ref.shape[1] // SSM_CHUNK
    for m in range(OCTETS):
        nat_ref[m] = r[:, m * LANES:(m + 1) * LANES]
    for j in range(SSM_CHUNK):
        for m in range(OCTETS):
            us_ref[m, :, j * LANES:(j + 1) * LANES] = nat_ref[m, pl.ds(j, tc, stride=SSM_CHUNK), :]


def _in_proj(u, w_qkvm, w_ssm, tm, seq, keep):
    n_tok = u.shape[0]
    assert n_tok % tm == 0 and tm % SSM_CHUNK == 0 and MEM_WIDTH == ATT_WIDTH
    tc = tm // SSM_CHUNK
    if keep == seq:
        kept_rows, kept = n_tok, pl.BlockSpec((tm, ATT_WIDTH), lambda i: (i, 0))
    else:
        assert keep == tm and seq % tm == 0
        kept_rows, kept = (n_tok // seq) * keep, pl.BlockSpec((tm, ATT_WIDTH), lambda i: (i // (seq // tm), 0))
    return pl.pallas_call(
        _in_proj_kernel,
        grid=(n_tok // tm,),
        in_specs=[pl.BlockSpec((tm, D_MODEL), lambda i: (i, 0)),
                  pl.BlockSpec((D_MODEL, 4 * ATT_WIDTH), lambda i: (0, 0)),
                  pl.BlockSpec((D_MODEL, SSM_WIDTH), lambda i: (0, 0))],
        out_specs=[pl.BlockSpec((tm, 4 * ATT_WIDTH), lambda i: (i, 0)), kept, kept,
                   pl.BlockSpec((OCTETS, tc, OCT_COLS), lambda i: (0, i, 0))],
        out_shape=[jax.ShapeDtypeStruct((n_tok, 4 * ATT_WIDTH), BF16),
                   jax.ShapeDtypeStruct((kept_rows, ATT_WIDTH), F32),
                   jax.ShapeDtypeStruct((kept_rows, ATT_WIDTH), F32),
                   jax.ShapeDtypeStruct((OCTETS, n_tok // SSM_CHUNK, OCT_COLS), F32)],
        scratch_shapes=[pltpu.VMEM((OCTETS, tm, LANES), F32)],
        compiler_params=_params(("arbitrary",), VMEM_BIG_MIB),
        name="in_proj",
    )(u, w_qkvm, w_ssm)


def _ssm_params(a_re, a_im, log_dt, b_re, b_im, c_re, c_im):
    dt = jnp.exp(log_dt)[:, None]
    mag = jnp.exp(a_re * dt)
    ab_re = mag * jnp.cos(a_im * dt)
    ab_im = mag * jnp.sin(a_im * dt)
    den = a_re * a_re + a_im * a_im
    n_re = ab_re - 1.0
    n_im = ab_im
    k_re = (n_re * a_re + n_im * a_im) / den
    k_im = (n_im * a_re - n_re * a_im) / den
    bb_re = k_re[..., None] * b_re - k_im[..., None] * b_im
    bb_im = k_re[..., None] * b_im + k_im[..., None] * b_re
    pr = [jnp.ones_like(ab_re)]
    pi = [jnp.zeros_like(ab_re)]
    for _ in range(SSM_CHUNK):
        pr.append(pr[-1] * ab_re - pi[-1] * ab_im)
        pi.append(pr[-2] * ab_im + pi[-1] * ab_re)
    n_pw = SSM_CHUNK + 1
    pw = jnp.concatenate([jnp.stack(pr).reshape(n_pw, OCTETS, OCT_HALF),
                          jnp.stack(pi).reshape(n_pw, OCTETS, OCT_HALF)], axis=2)
    pw = jnp.transpose(pw, (1, 0, 2))
    eye = jnp.eye(OCT_GROUPS, dtype=F32)

    def expand(x):
        x4 = x.reshape(OCTETS, OCT_GROUPS, x.shape[1], SSM_STATE)
        out = eye[None, :, None, :, None] * x4[:, :, :, None, :]
        return out.reshape(OCTETS, OCT_GROUPS * x.shape[1], OCT_HALF)

    bd = jnp.concatenate([expand(jnp.transpose(bb_re, (0, 2, 1))),
                          expand(jnp.transpose(bb_im, (0, 2, 1)))], axis=2)
    cd = jnp.concatenate([expand(c_re), expand(c_im)], axis=2)
    return bd, cd, pw


def _state_to_octets(s_re, s_im):
    b = s_re.shape[0]
    s = jnp.stack([s_re, s_im], 0).reshape(2, b, OCTETS, OCT_GROUPS, SSM_STATE)
    return jnp.transpose(s, (2, 1, 0, 3, 4)).reshape(OCTETS, b, OCT_STATE)


def _state_from_octets(s):
    b = s.shape[1]
    s = s.reshape(OCTETS, b, 2, OCT_GROUPS, SSM_STATE)
    s = jnp.transpose(s, (2, 1, 0, 3, 4)).reshape(2, b, SSM_GROUPS, SSM_STATE)
    return s[0], s[1]


def _split_bf16(x):
    hi = x.astype(BF16)
    return hi, (x - hi.astype(F32)).astype(BF16)


def _ssm_build_operators(bd_ref, cd_ref, pw_ref, t8_ref, wus_ref, wso_ref):
    bd_re, bd_im = bd_ref[0, :, :OCT_HALF], bd_ref[0, :, OCT_HALF:]
    cd_re, cd_im = cd_ref[0, :, :OCT_HALF], cd_ref[0, :, OCT_HALF:]
    c_hi, c_lo = _split_bf16(jnp.concatenate([cd_re, -cd_im], axis=1))
    blk = lambda i: slice(i * LANES, (i + 1) * LANES)
    for a in range(SSM_CHUNK // 2):
        t8_ref[blk(2 * a + 1), blk(2 * a)] = jnp.zeros((LANES, LANES), BF16)
    for k in range(SSM_CHUNK + 1):
        p_re = pw_ref[0, k:k + 1, :OCT_HALF]
        p_im = pw_ref[0, k:k + 1, OCT_HALF:]
        if k < SSM_CHUNK:
            e = jnp.concatenate([bd_re * p_re - bd_im * p_im, bd_re * p_im + bd_im * p_re], axis=1)
            wus_ref[blk(SSM_CHUNK - 1 - k), :] = e.astype(BF16)
            e_hi, e_lo = _split_bf16(e)
            lag = (_dot_nt(e_hi, c_hi) + _dot_nt(e_hi, c_lo) + _dot_nt(e_lo, c_hi)).astype(BF16)
            for j in range(SSM_CHUNK - k):
                t8_ref[blk(j), blk(j + k)] = lag
        if k >= 1:
            g = jnp.concatenate([cd_re * p_re - cd_im * p_im, -(cd_re * p_im + cd_im * p_re)], axis=1)
            wso_ref[blk(k - 1), :] = g.astype(BF16)


def _ssm_core_kernel(us_ref, s0_ref, bd_ref, cd_ref, pw_ref, y_ref, sfin_ref,
                     t8_ref, wus_ref, wso_ref, ds_ref, sp_ref, *, nb, n_chunks):
    @pl.when(pl.program_id(1) == 0)
    def _():
        _ssm_build_operators(bd_ref, cd_ref, pw_ref, t8_ref, wus_ref, wso_ref)

    ub = us_ref[0].astype(BF16)
    ds = _dot(ub, wus_ref[...])
    a16 = pw_ref[0, SSM_CHUNK:SSM_CHUNK + 1, :]
    n_blk = OCT_HALF // LANES
    blk = lambda k: slice(k * LANES, (k + 1) * LANES)

    def advance(s_re, s_im, d_re, d_im, a_re, a_im):
        return a_re * s_re - a_im * s_im + d_re, a_re * s_im + a_im * s_re + d_im

    if n_chunks == 1:
        s0 = s0_ref[0, 0]
        n_re, n_im = advance(s0[:, :OCT_HALF], s0[:, OCT_HALF:], ds[:, :OCT_HALF], ds[:, OCT_HALF:],
                             a16[:, :OCT_HALF], a16[:, OCT_HALF:])
        sfin_ref[0, 0] = jnp.concatenate([n_re, n_im], axis=1)
        spb = s0.astype(BF16)
    else:
        for k in range(2 * n_blk):
            ds_ref[k] = ds[:, blk(k)]

        def body(c, carry):
            at = pl.ds(c, nb, stride=n_chunks)
            new = list(carry)
            for k in range(n_blk):
                sp_ref[k, at, :] = carry[k]
                sp_ref[n_blk + k, at, :] = carry[n_blk + k]
                new[k], new[n_blk + k] = advance(carry[k], carry[n_blk + k], ds_ref[k, at, :],
                                                 ds_ref[n_blk + k, at, :], a16[:, blk(k)],
                                                 a16[:, blk(n_blk + k)])
            return tuple(new)

        init = tuple(s0_ref[0, 0, :, blk(k)] for k in range(2 * n_blk))
        fin = lax.fori_loop(0, n_chunks, body, init, unroll=2)
        for k in range(2 * n_blk):
            sfin_ref[0, 0, :, blk(k)] = fin[k]
        spb = jnp.concatenate([sp_ref[k] for k in range(2 * n_blk)], axis=1).astype(BF16)
    for nt in range(OCT_COLS // MXU_DIM):
        k_hi = (nt + 1) * MXU_DIM
        cs = slice(nt * MXU_DIM, k_hi)
        y_ref[0, :, cs] = _dot(ub[:, :k_hi], t8_ref[:k_hi, cs]) + _dot_nt(spb, wso_ref[cs, :])


def _ssm_core(us_r, s0, bd, cd, pw, n_batch, n_chunks, nb):
    nc = us_r.shape[1]
    rows = nb * n_chunks
    assert n_batch % nb == 0 and nc == n_batch * n_chunks
    nr = n_batch // nb
    s0 = s0.reshape(OCTETS, nr, nb, OCT_STATE)
    par = lambda rows_: pl.BlockSpec((1, rows_, OCT_STATE), lambda m, r: (m, 0, 0))
    y, sfin = pl.pallas_call(
        functools.partial(_ssm_core_kernel, nb=nb, n_chunks=n_chunks),
        grid=(OCTETS, nr),
        in_specs=[pl.BlockSpec((1, rows, OCT_COLS), lambda m, r: (m, r, 0)),
                  pl.BlockSpec((1, 1, nb, OCT_STATE), lambda m, r: (m, r, 0, 0)),
                  par(LANES), par(LANES), par(SSM_CHUNK + 1)],
        out_specs=[pl.BlockSpec((1, rows, OCT_COLS), lambda m, r: (m, r, 0)),
                   pl.BlockSpec((1, 1, nb, OCT_STATE), lambda m, r: (m, r, 0, 0))],
        out_shape=[jax.ShapeDtypeStruct((OCTETS, nc, OCT_COLS), F32),
                   jax.ShapeDtypeStruct((OCTETS, nr, nb, OCT_STATE), F32)],
        scratch_shapes=[pltpu.VMEM((OCT_COLS, OCT_COLS), BF16),
                        pltpu.VMEM((OCT_COLS, OCT_STATE), BF16),
                        pltpu.VMEM((OCT_COLS, OCT_STATE), BF16),
                        pltpu.VMEM((OCT_STATE // LANES, rows, LANES), F32),
                        pltpu.VMEM((OCT_STATE // LANES, rows, LANES), F32)],
        compiler_params=_params(("arbitrary", "arbitrary"), VMEM_BIG_MIB),
        name="ssm_core",
    )(us_r, s0, bd, cd, pw)
    return y, sfin.reshape(OCTETS, n_batch, OCT_STATE)


def _gelu_tanh(x):
    c = math.sqrt(2.0 / math.pi)
    return x * (0.5 * (1.0 + jnp.tanh(c * (x + 0.044715 * (x * x * x)))))


def _ssm_glu_kernel(y_ref, us_ref, d_ref, w_ref, b_ref, o_ref, nat_ref):
    tc = y_ref.shape[1]
    for i in range(SSM_CHUNK):
        cs = slice(i * LANES, (i + 1) * LANES)
        for m in range(OCTETS):
            d = d_ref[:, m * LANES:(m + 1) * LANES]
            nat_ref[m, pl.ds(i, tc, stride=SSM_CHUNK), :] = y_ref[m, :, cs] + d * us_ref[m, :, cs]
    yg = _gelu_tanh(jnp.concatenate([nat_ref[m] for m in range(OCTETS)], axis=1))
    z = _dot(yg.astype(BF16), w_ref[...]) + b_ref[...]
    o_ref[...] = (yg * jax.nn.sigmoid(z)).astype(o_ref.dtype)


def _ssm_glu(y_r, us_r, d, w_glu, b_glu, tc):
    nc = y_r.shape[1]
    assert nc % tc == 0
    oct_spec = pl.BlockSpec((OCTETS, tc, OCT_COLS), lambda c: (0, c, 0))
    vec = pl.BlockSpec((1, SSM_WIDTH), lambda c: (0, 0))
    return pl.pallas_call(
        _ssm_glu_kernel,
        grid=(nc // tc,),
        in_specs=[oct_spec, oct_spec, vec,
                  pl.BlockSpec((SSM_WIDTH, SSM_WIDTH), lambda c: (0, 0)), vec],
        out_specs=pl.BlockSpec((tc * SSM_CHUNK, SSM_WIDTH), lambda c: (c, 0)),
        out_shape=jax.ShapeDtypeStruct((nc * SSM_CHUNK, SSM_WIDTH), BF16),
        scratch_shapes=[pltpu.VMEM((OCTETS, tc * SSM_CHUNK, LANES), F32)],
        compiler_params=_params(("parallel",), VMEM_MID_MIB),
        name="ssm_glu",
    )(y_r, us_r, d, w_glu, b_glu)


N_PAIRS = ATT_HEADS // 2


def _head_masks():
    lane = lax.broadcasted_iota(jnp.int32, (1, LANES), 1)
    return (lane < ATT_HEAD_DIM, lane >= ATT_HEAD_DIM)


def _pair_scores(q2, kw, bias2, masks):
    qq = jnp.concatenate([jnp.where(m, q2, jnp.zeros_like(q2)) for m in masks], axis=0)
    return _dot_nt(qq, kw) * (ATT_HEAD_DIM ** -0.5 * LOG2E) + bias2


def _row_max(sc):
    return jnp.broadcast_to(jnp.max(sc, axis=1, keepdims=True), (sc.shape[0], LANES))


def _softmax_parts(sc, mx):
    p = jnp.exp2(sc - jnp.concatenate([mx] * (sc.shape[1] // LANES), axis=1))
    l = jnp.sum(p, axis=1, keepdims=True)
    return p.astype(BF16), jnp.broadcast_to(1.0 / l, (sc.shape[0], LANES))


def _pair_output(p, rl, vw, masks):
    o2 = _dot(p, vw) * rl
    n_q = o2.shape[0] // 2
    return jnp.where(masks[0], o2[:n_q], o2[n_q:])


def _band_attn_kernel(q_ref, kp_ref, kc_ref, vp_ref, vc_ref, bias_ref, o_ref,
                      kw_ref, vw_ref, sc_ref, mx_ref, p_ref, rl_ref):
    kw_ref[0:BAND_PAST] = kp_ref[...]
    kw_ref[BAND_PAST:BAND_PAST + Q_TILE] = kc_ref[...]
    vw_ref[0:BAND_PAST] = vp_ref[...]
    vw_ref[BAND_PAST:BAND_PAST + Q_TILE] = vc_ref[...]
    masks = _head_masks()
    cs = lambda hp: slice(hp * LANES, (hp + 1) * LANES)

    def sub(s, carry):
        r0 = pl.multiple_of(s * Q_SUB, Q_SUB)
        rows = pl.ds(r0, Q_SUB)
        win = pl.ds(r0, K_WIN)
        for hp in range(N_PAIRS):
            sc = _pair_scores(q_ref[rows, cs(hp)], kw_ref[win, cs(hp)], bias_ref[hp], masks)
            sc_ref[hp] = sc
            mx_ref[hp] = _row_max(sc)

        @pl.when(pl.program_id(1) == 0)
        def _():
            col = lax.broadcasted_iota(jnp.int32, (1, K_WIN), 1)
            extra = jnp.where(col + r0 < BAND_PAST, NEG_INF, 0.0)
            for hp in range(N_PAIRS):
                sc = sc_ref[hp] + extra
                sc_ref[hp] = sc
                mx_ref[hp] = _row_max(sc)

        for hp in range(N_PAIRS):
            p_ref[hp], rl_ref[hp] = _softmax_parts(sc_ref[hp], mx_ref[hp])
        for hp in range(N_PAIRS):
            o = _pair_output(p_ref[hp], rl_ref[hp], vw_ref[win, cs(hp)], masks)
            o_ref[rows, cs(hp)] = o.astype(o_ref.dtype)
        return carry

    lax.fori_loop(0, Q_TILE // Q_SUB, sub, 0)


def _rel_bias_tile(rel_bias, n_q, n_k, offset, ok):
    r = np.arange(n_q - 1 + n_k)
    idx = np.clip(offset + n_q - 1 - r, -REL_CLIP, REL_CLIP) + REL_CLIP
    v = jnp.pad(rel_bias[:, idx] * LOG2E, ((0, 0), (0, 1)))
    w = n_q + n_k
    flat = jnp.tile(v, (1, n_q))[:, :n_q * (w - 1)]
    toep = flat.reshape(rel_bias.shape[0], n_q, w - 1)[:, :, n_q - 1:]
    return jnp.where(ok[None], toep, NEG_INF).reshape(N_PAIRS, 2 * n_q, n_k)


def _band_attention_prompt(qkvm, rel_bias, n_batch, seq):
    assert seq % Q_TILE == 0 and Q_TILE % BAND_PAST == 0
    nt = seq // Q_TILE
    qc = np.arange(Q_SUB)[:, None] // CHUNK
    kc = np.arange(K_WIN)[None, :] // CHUNK
    bias = _rel_bias_tile(rel_bias, Q_SUB, K_WIN, BAND_PAST, (kc >= qc) & (kc <= qc + BAND_PAST_CHUNKS))
    blk = (Q_TILE, ATT_WIDTH)
    ratio = Q_TILE // BAND_PAST
    cur = lambda col: pl.BlockSpec(blk, lambda b, t: (b * nt + t, col))
    prev = lambda col: pl.BlockSpec(
        (BAND_PAST, ATT_WIDTH), lambda b, t: (jnp.maximum((b * nt + t) * ratio - 1, b * nt * ratio), col))
    return pl.pallas_call(
        _band_attn_kernel,
        grid=(n_batch, nt),
        in_specs=[cur(0), prev(1), cur(1), prev(2), cur(2),
                  pl.BlockSpec((N_PAIRS, 2 * Q_SUB, K_WIN), lambda b, t: (0, 0, 0))],
        out_specs=pl.BlockSpec(blk, lambda b, t: (b * nt + t, 0)),
        out_shape=jax.ShapeDtypeStruct((n_batch * seq, ATT_WIDTH), BF16),
        scratch_shapes=[pltpu.VMEM((BAND_PAST + Q_TILE, ATT_WIDTH), BF16),
                        pltpu.VMEM((BAND_PAST + Q_TILE, ATT_WIDTH), BF16),
                        pltpu.VMEM((N_PAIRS, 2 * Q_SUB, K_WIN), F32),
                        pltpu.VMEM((N_PAIRS, 2 * Q_SUB, LANES), F32),
                        pltpu.VMEM((N_PAIRS, 2 * Q_SUB, K_WIN), BF16),
                        pltpu.VMEM((N_PAIRS, 2 * Q_SUB, LANES), F32)],
        compiler_params=_params(("parallel", "arbitrary"), VMEM_MID_MIB),
        name="band_attn",
    )(qkvm, qkvm, qkvm, qkvm, qkvm, bias)


def _band_attn_sample_kernel(q_ref, kn_ref, vn_ref, ck_ref, cv_ref, bias_ref, o_ref, kw_ref, vw_ref,
                             *, n_cache, n_new):
    kw_ref[...] = jnp.zeros_like(kw_ref)
    vw_ref[...] = jnp.zeros_like(vw_ref)
    kw_ref[0:n_cache] = ck_ref[0].astype(BF16)
    vw_ref[0:n_cache] = cv_ref[0].astype(BF16)
    kw_ref[n_cache:n_cache + n_new] = kn_ref[...]
    vw_ref[n_cache:n_cache + n_new] = vn_ref[...]
    masks = _head_masks()
    for hp in range(N_PAIRS):
        cs = slice(hp * LANES, (hp + 1) * LANES)
        sc = _pair_scores(q_ref[:, cs], kw_ref[:, cs], bias_ref[hp], masks)
        p, rl = _softmax_parts(sc, _row_max(sc))
        o_ref[:, cs] = _pair_output(p, rl, vw_ref[:, cs], masks).astype(o_ref.dtype)


def _band_attention_sample(qkvm, rel_bias, cache_k, cache_v, n_batch, n_new):
    n_cache = cache_k.shape[1]
    n_keys = -(-(n_cache + n_new) // LANES) * LANES
    q_pos = PAST_LEN + np.arange(n_new)[:, None]
    j = np.arange(n_keys)[None, :]
    k_pos = PAST_LEN - n_cache + j
    ok = ((j < n_cache + n_new) & (k_pos >= 0) & (k_pos // CHUNK <= q_pos // CHUNK)
          & (k_pos // CHUNK >= q_pos // CHUNK - BAND_PAST_CHUNKS))
    bias = _rel_bias_tile(rel_bias, n_new, n_keys, n_cache, ok)
    new = lambda col: pl.BlockSpec((n_new, ATT_WIDTH), lambda b: (b, col))
    cache = pl.BlockSpec((1, n_cache, ATT_WIDTH), lambda b: (b, 0, 0))
    return pl.pallas_call(
        functools.partial(_band_attn_sample_kernel, n_cache=n_cache, n_new=n_new),
        grid=(n_batch,),
        in_specs=[new(0), new(1), new(2), cache, cache,
                  pl.BlockSpec((N_PAIRS, 2 * n_new, n_keys), lambda b: (0, 0, 0))],
        out_specs=pl.BlockSpec((n_new, ATT_WIDTH), lambda b: (b, 0)),
        out_shape=jax.ShapeDtypeStruct((n_batch * n_new, ATT_WIDTH), BF16),
        scratch_shapes=[pltpu.VMEM((n_keys, ATT_WIDTH), BF16),
                        pltpu.VMEM((n_keys, ATT_WIDTH), BF16)],
        compiler_params=_params(("parallel",), VMEM_MID_MIB),
        name="band_attn_sample",
    )(qkvm, qkvm, qkvm, cache_k, cache_v, bias)


def _mem_attn_kernel(q_ref, k_ref, v_ref, o_ref):
    k = k_ref[0].astype(BF16)
    v = v_ref[0].astype(BF16)
    for h in range(MEM_HEADS):
        cs = slice(h * MEM_HEAD_DIM, (h + 1) * MEM_HEAD_DIM)
        sc = _dot_nt(q_ref[:, cs], k[:, cs]) * (MEM_HEAD_DIM ** -0.5 * LOG2E)
        mx = jnp.max(sc, axis=1, keepdims=True)
        p = jnp.exp2(sc - mx)
        l = jnp.sum(p, axis=1, keepdims=True)
        o = _dot(p.astype(BF16), v[:, cs]) * (1.0 / l)
        o_ref[:, cs] = o.astype(o_ref.dtype)


def _memory_attention(qkvm, mem_k, mem_v, n_batch, seq, tq):
    nt = seq // tq
    mem = pl.BlockSpec((1, N_MEM, MEM_WIDTH), lambda b, t: (b, 0, 0))
    return pl.pallas_call(
        _mem_attn_kernel,
        grid=(n_batch, nt),
        in_specs=[pl.BlockSpec((tq, MEM_WIDTH), lambda b, t: (b * nt + t, 3)), mem, mem],
        out_specs=pl.BlockSpec((tq, MEM_WIDTH), lambda b, t: (b * nt + t, 0)),
        out_shape=jax.ShapeDtypeStruct((n_batch * seq, MEM_WIDTH), BF16),
        compiler_params=_params(("parallel", "arbitrary"), VMEM_MID_MIB),
        name="mem_attn",
    )(qkvm, mem_k, mem_v)


def _gate_merge_kernel(u_ref, os_ref, oa_ref, om_ref, wgs_ref, wga_ref, wgm_ref,
                       wbs_ref, wba_ref, wbm_ref, out_ref):
    u = u_ref[...]

    def branch(o_ref, wg_ref, wb_ref):
        return jax.nn.sigmoid(_dot(u, wg_ref[...])) * _dot(o_ref[...], wb_ref[...])

    merged = (branch(os_ref, wgs_ref, wbs_ref) + branch(oa_ref, wga_ref, wba_ref)
              + branch(om_ref, wgm_ref, wbm_ref))
    out_ref[...] = merged.astype(out_ref.dtype)


def _gate_merge(u, o_s, o_a, o_m, w_gate, wb_s, wb_a, wb_m, tm, tn):
    m, d = u.shape
    gate = lambda b: pl.BlockSpec((d, tn), lambda i, n: (0, b * (d // tn) + n))
    wb = pl.BlockSpec((SSM_WIDTH, tn), lambda i, n: (0, n))
    ob = pl.BlockSpec((tm, SSM_WIDTH), lambda i, n: (i, 0))
    return pl.pallas_call(
        _gate_merge_kernel,
        grid=(m // tm, d // tn),
        in_specs=[pl.BlockSpec((tm, d), lambda i, n: (i, 0)), ob, ob, ob,
                  gate(0), gate(1), gate(2), wb, wb, wb],
        out_specs=pl.BlockSpec((tm, tn), lambda i, n: (i, n)),
        out_shape=jax.ShapeDtypeStruct((m, d), BF16),
        compiler_params=_params(("parallel", "arbitrary"), VMEM_BIG_MIB),
        name="gate_merge",
    )(u, o_s, o_a, o_m, w_gate, w_gate, w_gate, wb_s, wb_a, wb_m)


def _out_proj_kernel(x_ref, w_ref, h_ref, g_ref, o_ref):
    o_ref[...] = h_ref[...] + _rms(_dot(x_ref[...], w_ref[...]), g_ref[...])


def _out_proj(x, w_out, h, g_post, tm):
    m, d = h.shape
    row = lambda: pl.BlockSpec((tm, d), lambda i: (i, 0))
    return pl.pallas_call(
        _out_proj_kernel,
        grid=(m // tm,),
        in_specs=[row(), pl.BlockSpec((d, d), lambda i: (0, 0)), row(),
                  pl.BlockSpec((1, d), lambda i: (0, 0))],
        out_specs=row(),
        out_shape=jax.ShapeDtypeStruct((m, d), F32),
        compiler_params=_params(("parallel",), VMEM_BIG_MIB),
        name="out_proj",
    )(x, w_out, h, g_post)


def _layer(x, w, *, n_batch, seq, s0, mem_k, mem_v, cache_k, cache_v, tm, keep):
    n_tok = n_batch * seq
    n_chunks = seq // SSM_CHUNK
    big_tm = min(1024, n_tok)
    h1, u = _ffn(x, w["ffn1_norm_pre"], w["ffn1_norm_post"], w["ffn1_w_gate"], w["ffn1_w_up"],
                 w["ffn1_w_down"], w["mix_norm_pre"], big_tm, 512)

    qkvm, k_keep, v_keep, us_r = _in_proj(u, w["w_qkvm"], w["w_ssm"], tm, seq, keep)
    kv = (k_keep, v_keep)

    nb = 2 if (n_chunks > 1 and n_batch % 2 == 0) else (1 if n_chunks > 1 else n_batch)
    y_r, s_fin = _ssm_core(us_r, s0, w["ssm_bd"], w["ssm_cd"], w["ssm_pw"], n_batch, n_chunks, nb)
    o_s = _ssm_glu(y_r, us_r, w["ssm_d"], w["ssm_w_glu"], w["ssm_b_glu"], min(64, n_tok // SSM_CHUNK))

    if cache_k is None:
        o_a = _band_attention_prompt(qkvm, w["att_rel_bias"], n_batch, seq)
    else:
        o_a = _band_attention_sample(qkvm, w["att_rel_bias"], cache_k, cache_v, n_batch, seq)
    o_m = _memory_attention(qkvm, mem_k, mem_v, n_batch, seq, min(2048, seq))

    merged = _gate_merge(u, o_s, o_a, o_m, w["w_gate"], w["w_branch_ssm"], w["w_branch_att"],
                         w["w_branch_mem"], big_tm, 512)
    h2 = _out_proj(merged, w["w_out"], h1, w["mix_norm_post"], tm)
    y = _ffn(h2, w["ffn2_norm_pre"], w["ffn2_norm_post"], w["ffn2_w_gate"], w["ffn2_w_up"],
             w["ffn2_w_down"], None, big_tm, 512)
    return y, kv, s_fin


def kernel(x_prompt, x_sample, mem_prompt, cache_att_k, cache_att_v, cache_mem_k, cache_mem_v, state_ssm_re, state_ssm_im, ffn1_norm_pre, ffn1_norm_post, ffn1_w_gate, ffn1_w_up, ffn1_w_down, mix_norm_pre, mix_norm_post, w_in, ssm_a_re, ssm_a_im, ssm_log_dt, ssm_b_re, ssm_b_im, ssm_c_re, ssm_c_im, ssm_d, ssm_w_glu, ssm_b_glu, att_rel_bias, mem_norm, w_mem_k, w_mem_v, w_branch_ssm, w_branch_att, w_branch_mem, w_out, ffn2_norm_pre, ffn2_norm_post, ffn2_w_gate, ffn2_w_up, ffn2_w_down):
    n_bp, t_p, d = x_prompt.shape
    n_bs, t_s, _ = x_sample.shape
    depth = ffn1_norm_pre.shape[0]
    assert depth == 1 and d == D_MODEL
    keep = min(BAND_PAST, t_p)
    l = 0

    vec = lambda a: a[l].reshape(1, -1).astype(F32)
    mat = lambda a: _cast_bf16(a[l])
    bd, cd, pw = _ssm_params(ssm_a_re[l], ssm_a_im[l], ssm_log_dt[l], ssm_b_re[l],
                             ssm_b_im[l], ssm_c_re[l], ssm_c_im[l])
    w = {
        "ffn1_norm_pre": vec(ffn1_norm_pre), "ffn1_norm_post": vec(ffn1_norm_post),
        "ffn1_w_gate": mat(ffn1_w_gate), "ffn1_w_up": mat(ffn1_w_up), "ffn1_w_down": mat(ffn1_w_down),
        "mix_norm_pre": vec(mix_norm_pre), "mix_norm_post": vec(mix_norm_post),
        "w_ssm": _cast_bf16(w_in[l], 0, COL_Q),
        "w_qkvm": _cast_bf16(w_in[l], COL_Q, COL_GATE - COL_Q),
        "w_gate": _cast_bf16(w_in[l], COL_GATE),
        "ssm_bd": bd, "ssm_cd": cd, "ssm_pw": pw,
        "ssm_d": vec(ssm_d), "ssm_w_glu": mat(ssm_w_glu), "ssm_b_glu": vec(ssm_b_glu),
        "att_rel_bias": att_rel_bias[l].astype(F32),
        "w_branch_ssm": mat(w_branch_ssm), "w_branch_att": mat(w_branch_att),
        "w_branch_mem": mat(w_branch_mem), "w_out": mat(w_out),
        "ffn2_norm_pre": vec(ffn2_norm_pre), "ffn2_norm_post": vec(ffn2_norm_post),
        "ffn2_w_gate": mat(ffn2_w_gate), "ffn2_w_up": mat(ffn2_w_up), "ffn2_w_down": mat(ffn2_w_down),
    }

    w_mem = jnp.concatenate([w_mem_k[l], w_mem_v[l]], axis=1).astype(BF16)
    mk_p, mv_p = _memory_kv(mem_prompt.reshape(n_bp * N_MEM, d), vec(mem_norm), w_mem, 512)
    mk_p = mk_p.reshape(n_bp, N_MEM, MEM_WIDTH)
    mv_p = mv_p.reshape(n_bp, N_MEM, MEM_WIDTH)
    zero_state = jnp.zeros((OCTETS, n_bp, OCT_STATE), F32)
    y_p, kv_p, sfin_p = _layer(x_prompt.reshape(n_bp * t_p, d), w, n_batch=n_bp, seq=t_p, s0=zero_state,
                               mem_k=mk_p, mem_v=mv_p, cache_k=None, cache_v=None, tm=512, keep=keep)
    sre_p, sim_p = _state_from_octets(sfin_p)

    n_cache = cache_att_k.shape[2]
    s0_s = _state_to_octets(state_ssm_re[l].astype(F32), state_ssm_im[l].astype(F32))
    y_s, kv_s, sfin_s = _layer(x_sample.reshape(n_bs * t_s, d), w, n_batch=n_bs, seq=t_s, s0=s0_s,
                               mem_k=cache_mem_k[l].reshape(n_bs, N_MEM, MEM_WIDTH),
                               mem_v=cache_mem_v[l].reshape(n_bs, N_MEM, MEM_WIDTH),
                               cache_k=cache_att_k[l].reshape(n_bs, n_cache, ATT_WIDTH),
                               cache_v=cache_att_v[l].reshape(n_bs, n_cache, ATT_WIDTH),
                               tm=n_bs * t_s, keep=t_s)
    sre_s, sim_s = _state_from_octets(sfin_s)

    heads = lambda a, nb, t: a.reshape(1, nb, t, ATT_HEADS, ATT_HEAD_DIM)
    memh = lambda a: a.reshape(1, n_bp, N_MEM, MEM_HEADS, MEM_HEAD_DIM)
    return (y_p.reshape(n_bp, t_p, d), y_s.reshape(n_bs, t_s, d),
            heads(kv_p[0], n_bp, keep), heads(kv_p[1], n_bp, keep),
            memh(mk_p), memh(mv_p), sre_p[None], sim_p[None],
            heads(kv_s[0], n_bs, t_s), heads(kv_s[1], n_bs, t_s),
            sre_s[None], sim_s[None])
```

```python
import functools
import math

import numpy as np
import jax
import jax.numpy as jnp
from jax import lax
from jax.experimental import pallas as pl
from jax.experimental.pallas import tpu as pltpu

F32 = jnp.float32
BF16 = jnp.bfloat16

D_MODEL = 2048
CHUNK = 64
BAND_PAST_CHUNKS = 8
BAND_PAST = BAND_PAST_CHUNKS * CHUNK
ATT_HEADS = 16
ATT_HEAD_DIM = 64
ATT_WIDTH = ATT_HEADS * ATT_HEAD_DIM
REL_CLIP = 128
SSM_GROUP = 16
SSM_WIDTH = 1024
SSM_GROUPS = SSM_WIDTH // SSM_GROUP
SSM_STATE = 64
N_MEM = 256
MEM_HEADS = 4
MEM_HEAD_DIM = 256
MEM_WIDTH = MEM_HEADS * MEM_HEAD_DIM
EPS = 1e-6
NEG_INF = -1e30
PAST_LEN = 4096
LOG2E = math.log2(math.e)

COL_Q = SSM_WIDTH
COL_GATE = SSM_WIDTH + 3 * ATT_WIDTH + MEM_WIDTH

SSM_CHUNK = 16
OCTETS = 8
OCT_GROUPS = SSM_GROUPS // OCTETS
LANES = 128
MXU_DIM = 256
OCT_COLS = SSM_CHUNK * LANES
OCT_HALF = OCT_GROUPS * SSM_STATE
OCT_STATE = 2 * OCT_HALF

Q_TILE = 512
Q_SUB = 128
K_WIN = Q_SUB + BAND_PAST

MIB = 1024 * 1024
VMEM_BIG_MIB = 56
VMEM_MID_MIB = 48
VMEM_SMALL_MIB = 32


def _params(sem, vmem_mib):
    return pltpu.CompilerParams(dimension_semantics=sem, vmem_limit_bytes=vmem_mib * MIB)


def _dot(a, b):
    return jnp.dot(a, b, preferred_element_type=F32)


def _dot_nt(a, b):
    return lax.dot_general(a, b, (((1,), (1,)), ((), ())), preferred_element_type=F32)


def _rms(xf, g):
    y = xf * lax.rsqrt(jnp.mean(xf * xf, axis=-1, keepdims=True) + EPS)
    return y * g


CAST_BLOCK_BYTES = 8 * MIB


def _cast_kernel(x_ref, o_ref):
    o_ref[...] = x_ref[...].astype(o_ref.dtype)


def _cast_bf16(w, col0=0, n_cols=None):
    rows, cols = w.shape
    n_cols = cols - col0 if n_cols is None else n_cols
    cw = math.gcd(col0, n_cols)
    assert cw % LANES == 0 and col0 + n_cols <= cols
    tr = rows
    while tr * cw * 4 > CAST_BLOCK_BYTES and tr % 32 == 0:
        tr //= 2
    cb0 = col0 // cw
    return pl.pallas_call(
        _cast_kernel,
        grid=(rows // tr, n_cols // cw),
        in_specs=[pl.BlockSpec((tr, cw), lambda i, j: (i, cb0 + j))],
        out_specs=pl.BlockSpec((tr, cw), lambda i, j: (i, j)),
        out_shape=jax.ShapeDtypeStruct((rows, n_cols), BF16),
        compiler_params=_params(("parallel", "parallel"), VMEM_SMALL_MIB),
        name="cast_bf16",
    )(w)


FFN_SLICES = 8


def _ffn_kernel(hp_ref, hn_ref, gpre_ref, gpost_ref, wg_ref, wu_ref, wd_ref, *rest,
                n_tiles, emit_next):
    if emit_next:
        gnext_ref, out_ref, nxt_ref, *scratch = rest
    else:
        out_ref, *scratch = rest
    xn_refs, acc_refs = scratch[:2], scratch[2:]
    r = pl.program_id(0)
    f = pl.program_id(1)
    rs = hp_ref.shape[0]
    rows = pl.ds(pl.multiple_of(jnp.minimum(f, FFN_SLICES - 1) * rs, rs), rs)

    def pre_norm_slice(slot):
        xn_refs[slot][rows, :] = _rms(hn_ref[...], gpre_ref[...]).astype(BF16)

    def finish_slice(slot):
        hn = hp_ref[...] + 0.5 * _rms(acc_refs[slot][rows, :], gpost_ref[...])
        out_ref[...] = hn
        if emit_next:
            nxt_ref[...] = _rms(hn, gnext_ref[...]).astype(BF16)

    def matmul_chunk(slot):
        xn = xn_refs[slot][...]
        g = _dot(xn, wg_ref[...])
        u = _dot(xn, wu_ref[...])
        a = (g * jax.nn.sigmoid(g)) * u
        d = _dot(a.astype(BF16), wd_ref[...])
        acc_refs[slot][...] = jnp.where(f == 0, d, acc_refs[slot][...] + d)

    @pl.when((r == 0) & (f == 0))
    def _():
        acc_refs[1][...] = jnp.zeros_like(acc_refs[1])

    @pl.when(r == 0)
    def _():
        pre_norm_slice(0)

    for parity in range(2):
        @pl.when((r >= 1) & (r <= n_tiles) & (lax.rem(r, 2) == parity))
        def _():
            finish_slice(parity)
            pre_norm_slice(parity)
            matmul_chunk(1 - parity)

    @pl.when(r == n_tiles + 1)
    def _():
        finish_slice((n_tiles + 1) % 2)


def _ffn(h, g_pre, g_post, wg, wu, wd, g_next, tm, tf):
    emit_next = g_next is not None
    m, d = h.shape
    f_dim = wg.shape[1]
    nf = f_dim // tf
    n = m // tm
    rs = tm // FFN_SLICES
    assert m % tm == 0 and f_dim % tf == 0 and tm % FFN_SLICES == 0 and nf >= FFN_SLICES
    sl = lambda f: jnp.minimum(f, FFN_SLICES - 1)
    done = lambda r, f: (jnp.maximum(r - 2, 0) * FFN_SLICES + jnp.where(r >= 2, sl(f), 0), 0)
    ahead = lambda r, f: (jnp.minimum(r, n - 1) * FFN_SLICES + sl(f), 0)
    chunk = lambda r, f: jnp.where((r >= 1) & (r <= n), f, 0)
    vec = pl.BlockSpec((1, d), lambda r, f: (0, 0))
    in_specs = [pl.BlockSpec((rs, d), done), pl.BlockSpec((rs, d), ahead), vec, vec,
                pl.BlockSpec((d, tf), lambda r, f: (0, chunk(r, f))),
                pl.BlockSpec((d, tf), lambda r, f: (0, chunk(r, f))),
                pl.BlockSpec((tf, d), lambda r, f: (chunk(r, f), 0))]
    args = [h, h, g_pre, g_post, wg, wu, wd]
    out_shape = [jax.ShapeDtypeStruct((m, d), F32)]
    out_specs = [pl.BlockSpec((rs, d), done)]
    if emit_next:
        in_specs.append(vec)
        args.append(g_next)
        out_shape.append(jax.ShapeDtypeStruct((m, d), BF16))
        out_specs.append(pl.BlockSpec((rs, d), done))
    res = pl.pallas_call(
        functools.partial(_ffn_kernel, n_tiles=n, emit_next=emit_next),
        grid=(n + 2, nf),
        in_specs=in_specs,
        out_specs=out_specs,
        out_shape=out_shape,
        scratch_shapes=[pltpu.VMEM((tm, d), BF16), pltpu.VMEM((tm, d), BF16),
                        pltpu.VMEM((tm, d), F32), pltpu.VMEM((tm, d), F32)],
        compiler_params=_params(("arbitrary", "arbitrary"), VMEM_BIG_MIB),
        name="ffn",
    )(*args)
    return res if emit_next else res[0]


def _memory_kv_kernel(x_ref, g_ref, w_ref, k_ref, v_ref):
    x = _rms(x_ref[...], g_ref[...]).astype(BF16)
    k_ref[...] = _dot(x, w_ref[:, :MEM_WIDTH])
    v_ref[...] = _dot(x, w_ref[:, MEM_WIDTH:])


def _memory_kv(x, gain, w, tm):
    m, k = x.shape
    tm = min(tm, m)
    assert m % tm == 0 and w.shape == (k, 2 * MEM_WIDTH)
    out = pl.BlockSpec((tm, MEM_WIDTH), lambda i: (i, 0))
    return pl.pallas_call(
        _memory_kv_kernel,
        grid=(m // tm,),
        in_specs=[pl.BlockSpec((tm, k), lambda i: (i, 0)), pl.BlockSpec((1, k), lambda i: (0, 0)),
                  pl.BlockSpec((k, 2 * MEM_WIDTH), lambda i: (0, 0))],
        out_specs=[out, out],
        out_shape=[jax.ShapeDtypeStruct((m, MEM_WIDTH), F32)] * 2,
        compiler_params=_params(("parallel",), VMEM_MID_MIB),
        name="memory_kv",
    )(x, gain, w)


def _in_proj_kernel(x_ref, wq_ref, ws_ref, o_ref, k_ref, v_ref, us_ref, nat_ref):
    x = x_ref[...]
    for j, keep_ref in enumerate((None, k_ref, v_ref, None)):
        cs = slice(j * ATT_WIDTH, (j + 1) * ATT_WIDTH)
        r = _dot(x, wq_ref[:, cs])
        o_ref[:, cs] = r.astype(o_ref.dtype)
        if keep_ref is not None:
            keep_ref[...] = r
    r = _dot(x, ws_ref[...])
    tc = nat_ref.shape[1] // SSM_CHUNK
    for m in range(OCTETS):
        nat_ref[m] = r[:, m * LANES:(m + 1) * LANES]
    for j in range(SSM_CHUNK):
        for m in range(OCTETS):
            us_ref[m, :, j * LANES:(j + 1) * LANES] = nat_ref[m, pl.ds(j, tc, stride=SSM_CHUNK), :]


def _in_proj(u, w_qkvm, w_ssm, tm, seq, keep):
    n_tok = u.shape[0]
    assert n_tok % tm == 0 and tm % SSM_CHUNK == 0 and MEM_WIDTH == ATT_WIDTH
    tc = tm // SSM_CHUNK
    if keep == seq:
        kept_rows, kept = n_tok, pl.BlockSpec((tm, ATT_WIDTH), lambda i: (i, 0))
    else:
        assert keep == tm and seq % tm == 0
        kept_rows, kept = (n_tok // seq) * keep, pl.BlockSpec((tm, ATT_WIDTH), lambda i: (i // (seq // tm), 0))
    return pl.pallas_call(
        _in_proj_kernel,
        grid=(n_tok // tm,),
        in_specs=[pl.BlockSpec((tm, D_MODEL), lambda i: (i, 0)),
                  pl.BlockSpec((D_MODEL, 4 * ATT_WIDTH), lambda i: (0, 0)),
                  pl.BlockSpec((D_MODEL, SSM_WIDTH), lambda i: (0, 0))],
        out_specs=[pl.BlockSpec((tm, 4 * ATT_WIDTH), lambda i: (i, 0)), kept, kept,
                   pl.BlockSpec((OCTETS, tc, OCT_COLS), lambda i: (0, i, 0))],
        out_shape=[jax.ShapeDtypeStruct((n_tok, 4 * ATT_WIDTH), BF16),
                   jax.ShapeDtypeStruct((kept_rows, ATT_WIDTH), F32),
                   jax.ShapeDtypeStruct((kept_rows, ATT_WIDTH), F32),
                   jax.ShapeDtypeStruct((OCTETS, n_tok // SSM_CHUNK, OCT_COLS), F32)],
        scratch_shapes=[pltpu.VMEM((OCTETS, tm, LANES), F32)],
        compiler_params=_params(("arbitrary",), VMEM_BIG_MIB),
        name="in_proj",
    )(u, w_qkvm, w_ssm)


def _ssm_params(a_re, a_im, log_dt, b_re, b_im, c_re, c_im):
    dt = jnp.exp(log_dt)[:, None]
    mag = jnp.exp(a_re * dt)
    ab_re = mag * jnp.cos(a_im * dt)
    ab_im = mag * jnp.sin(a_im * dt)
    den = a_re * a_re + a_im * a_im
    n_re = ab_re - 1.0
    n_im = ab_im
    k_re = (n_re * a_re + n_im * a_im) / den
    k_im = (n_im * a_re - n_re * a_im) / den
    bb_re = k_re[..., None] * b_re - k_im[..., None] * b_im
    bb_im = k_re[..., None] * b_im + k_im[..., None] * b_re
    pr = [jnp.ones_like(ab_re)]
    pi = [jnp.zeros_like(ab_re)]
    for _ in range(SSM_CHUNK):
        pr.append(pr[-1] * ab_re - pi[-1] * ab_im)
        pi.append(pr[-2] * ab_im + pi[-1] * ab_re)
    n_pw = SSM_CHUNK + 1
    pw = jnp.concatenate([jnp.stack(pr).reshape(n_pw, OCTETS, OCT_HALF),
                          jnp.stack(pi).reshape(n_pw, OCTETS, OCT_HALF)], axis=2)
    pw = jnp.transpose(pw, (1, 0, 2))
    eye = jnp.eye(OCT_GROUPS, dtype=F32)

    def expand(x):
        x4 = x.reshape(OCTETS, OCT_GROUPS, x.shape[1], SSM_STATE)
        out = eye[None, :, None, :, None] * x4[:, :, :, None, :]
        return out.reshape(OCTETS, OCT_GROUPS * x.shape[1], OCT_HALF)

    bd = jnp.concatenate([expand(jnp.transpose(bb_re, (0, 2, 1))),
                          expand(jnp.transpose(bb_im, (0, 2, 1)))], axis=2)
    cd = jnp.concatenate([expand(c_re), expand(c_im)], axis=2)
    return bd, cd, pw


def _state_to_octets(s_re, s_im):
    b = s_re.shape[0]
    s = jnp.stack([s_re, s_im], 0).reshape(2, b, OCTETS, OCT_GROUPS, SSM_STATE)
    return jnp.transpose(s, (2, 1, 0, 3, 4)).reshape(OCTETS, b, OCT_STATE)


def _state_from_octets(s):
    b = s.shape[1]
    s = s.reshape(OCTETS, b, 2, OCT_GROUPS, SSM_STATE)
    s = jnp.transpose(s, (2, 1, 0, 3, 4)).reshape(2, b, SSM_GROUPS, SSM_STATE)
    return s[0], s[1]


def _split_bf16(x):
    hi = x.astype(BF16)
    return hi, (x - hi.astype(F32)).astype(BF16)


def _ssm_build_operators(bd_ref, cd_ref, pw_ref, t8_ref, wus_ref, wso_ref):
    bd_re, bd_im = bd_ref[0, :, :OCT_HALF], bd_ref[0, :, OCT_HALF:]
    cd_re, cd_im = cd_ref[0, :, :OCT_HALF], cd_ref[0, :, OCT_HALF:]
    c_hi, c_lo = _split_bf16(jnp.concatenate([cd_re, -cd_im], axis=1))
    blk = lambda i: slice(i * LANES, (i + 1) * LANES)
    for a in range(SSM_CHUNK // 2):
        t8_ref[blk(2 * a + 1), blk(2 * a)] = jnp.zeros((LANES, LANES), BF16)
    for k in range(SSM_CHUNK + 1):
        p_re = pw_ref[0, k:k + 1, :OCT_HALF]
        p_im = pw_ref[0, k:k + 1, OCT_HALF:]
        if k < SSM_CHUNK:
            e = jnp.concatenate([bd_re * p_re - bd_im * p_im, bd_re * p_im + bd_im * p_re], axis=1)
            wus_ref[blk(SSM_CHUNK - 1 - k), :] = e.astype(BF16)
            e_hi, e_lo = _split_bf16(e)
            lag = (_dot_nt(e_hi, c_hi) + _dot_nt(e_hi, c_lo) + _dot_nt(e_lo, c_hi)).astype(BF16)
            for j in range(SSM_CHUNK - k):
                t8_ref[blk(j), blk(j + k)] = lag
        if k >= 1:
            g = jnp.concatenate([cd_re * p_re - cd_im * p_im, -(cd_re * p_im + cd_im * p_re)], axis=1)
            wso_ref[blk(k - 1), :] = g.astype(BF16)


def _ssm_core_kernel(us_ref, s0_ref, bd_ref, cd_ref, pw_ref, d_ref, y_ref, sfin_ref,
                     t8_ref, wus_ref, wso_ref, ds_ref, sp_ref, *, nb, n_chunks):
    @pl.when(pl.program_id(1) == 0)
    def _():
        _ssm_build_operators(bd_ref, cd_ref, pw_ref, t8_ref, wus_ref, wso_ref)

    ub = us_ref[0].astype(BF16)
    ds = _dot(ub, wus_ref[...])
    a16 = pw_ref[0, SSM_CHUNK:SSM_CHUNK + 1, :]
    n_blk = OCT_HALF // LANES
    blk = lambda k: slice(k * LANES, (k + 1) * LANES)

    def advance(s_re, s_im, d_re, d_im, a_re, a_im):
        return a_re * s_re - a_im * s_im + d_re, a_re * s_im + a_im * s_re + d_im

    if n_chunks == 1:
        s0 = s0_ref[0, 0]
        n_re, n_im = advance(s0[:, :OCT_HALF], s0[:, OCT_HALF:], ds[:, :OCT_HALF], ds[:, OCT_HALF:],
                             a16[:, :OCT_HALF], a16[:, OCT_HALF:])
        sfin_ref[0, 0] = jnp.concatenate([n_re, n_im], axis=1)
        spb = s0.astype(BF16)
    else:
        for k in range(2 * n_blk):
            ds_ref[k] = ds[:, blk(k)]

        def body(c, carry):
            at = pl.ds(c, nb, stride=n_chunks)
            new = list(carry)
            for k in range(n_blk):
                sp_ref[k, at, :] = carry[k]
                sp_ref[n_blk + k, at, :] = carry[n_blk + k]
                new[k], new[n_blk + k] = advance(carry[k], carry[n_blk + k], ds_ref[k, at, :],
                                                 ds_ref[n_blk + k, at, :], a16[:, blk(k)],
                                                 a16[:, blk(n_blk + k)])
            return tuple(new)

        init = tuple(s0_ref[0, 0, :, blk(k)] for k in range(2 * n_blk))
        fin = lax.fori_loop(0, n_chunks, body, init, unroll=2)
        for k in range(2 * n_blk):
            sfin_ref[0, 0, :, blk(k)] = fin[k]
        spb = jnp.concatenate([sp_ref[k] for k in range(2 * n_blk)], axis=1).astype(BF16)
    rows = ub.shape[0]
    d_skip = d_ref[0]
    for nt in range(OCT_COLS // MXU_DIM):
        k_hi = (nt + 1) * MXU_DIM
        cs = slice(nt * MXU_DIM, k_hi)
        y = _dot(ub[:, :k_hi], t8_ref[:k_hi, cs]) + _dot_nt(spb, wso_ref[cs, :])
        for j in range(MXU_DIM // LANES):
            i = nt * (MXU_DIM // LANES) + j
            y_ref[pl.ds(i, rows, stride=SSM_CHUNK), :] = (y[:, blk(j)] + d_skip * us_ref[0, :, blk(i)])


def _ssm_core(us_r, s0, bd, cd, pw, d_oct, n_batch, n_chunks, nb):
    nc = us_r.shape[1]
    rows = nb * n_chunks
    assert n_batch % nb == 0 and nc == n_batch * n_chunks
    nr = n_batch // nb
    s0 = s0.reshape(OCTETS, nr, nb, OCT_STATE)
    par = lambda rows_: pl.BlockSpec((1, rows_, OCT_STATE), lambda m, r: (m, 0, 0))
    y, sfin = pl.pallas_call(
        functools.partial(_ssm_core_kernel, nb=nb, n_chunks=n_chunks),
        grid=(OCTETS, nr),
        in_specs=[pl.BlockSpec((1, rows, OCT_COLS), lambda m, r: (m, r, 0)),
                  pl.BlockSpec((1, 1, nb, OCT_STATE), lambda m, r: (m, r, 0, 0)),
                  par(LANES), par(LANES), par(SSM_CHUNK + 1),
                  pl.BlockSpec((1, 1, LANES), lambda m, r: (m, 0, 0))],
        out_specs=[pl.BlockSpec((rows * SSM_CHUNK, LANES), lambda m, r: (r, m)),
                   pl.BlockSpec((1, 1, nb, OCT_STATE), lambda m, r: (m, r, 0, 0))],
        out_shape=[jax.ShapeDtypeStruct((nc * SSM_CHUNK, SSM_WIDTH), F32),
                   jax.ShapeDtypeStruct((OCTETS, nr, nb, OCT_STATE), F32)],
        scratch_shapes=[pltpu.VMEM((OCT_COLS, OCT_COLS), BF16),
                        pltpu.VMEM((OCT_COLS, OCT_STATE), BF16),
                        pltpu.VMEM((OCT_COLS, OCT_STATE), BF16),
                        pltpu.VMEM((OCT_STATE // LANES, rows, LANES), F32),
                        pltpu.VMEM((OCT_STATE // LANES, rows, LANES), F32)],
        compiler_params=_params(("arbitrary", "arbitrary"), VMEM_BIG_MIB),
        name="ssm_core",
    )(us_r, s0, bd, cd, pw, d_oct)
    return y, sfin.reshape(OCTETS, n_batch, OCT_STATE)


def _gelu_tanh(x):
    c = math.sqrt(2.0 / math.pi)
    return x * (0.5 * (1.0 + jnp.tanh(c * (x + 0.044715 * (x * x * x)))))


def _ssm_glu_kernel(y_ref, w_ref, b_ref, o_ref):
    yg = _gelu_tanh(y_ref[...])
    z = _dot(yg.astype(BF16), w_ref[...]) + b_ref[...]
    o_ref[...] = (yg * jax.nn.sigmoid(z)).astype(o_ref.dtype)


def _ssm_glu(y, w_glu, b_glu, tm):
    n_tok = y.shape[0]
    assert n_tok % tm == 0
    row = pl.BlockSpec((tm, SSM_WIDTH), lambda i: (i, 0))
    return pl.pallas_call(
        _ssm_glu_kernel,
        grid=(n_tok // tm,),
        in_specs=[row, pl.BlockSpec((SSM_WIDTH, SSM_WIDTH), lambda i: (0, 0)),
                  pl.BlockSpec((1, SSM_WIDTH), lambda i: (0, 0))],
        out_specs=row,
        out_shape=jax.ShapeDtypeStruct((n_tok, SSM_WIDTH), BF16),
        compiler_params=_params(("parallel",), VMEM_MID_MIB),
        name="ssm_glu",
    )(y, w_glu, b_glu)


N_PAIRS = ATT_HEADS // 2


def _head_masks():
    lane = lax.broadcasted_iota(jnp.int32, (1, LANES), 1)
    return (lane < ATT_HEAD_DIM, lane >= ATT_HEAD_DIM)


def _pair_scores(q2, kw, bias2, masks):
    qq = jnp.concatenate([jnp.where(m, q2, jnp.zeros_like(q2)) for m in masks], axis=0)
    return _dot_nt(qq, kw) * (ATT_HEAD_DIM ** -0.5 * LOG2E) + bias2


def _row_max(sc):
    return jnp.broadcast_to(jnp.max(sc, axis=1, keepdims=True), (sc.shape[0], LANES))


def _softmax_parts(sc, mx):
    p = jnp.exp2(sc - jnp.concatenate([mx] * (sc.shape[1] // LANES), axis=1))
    l = jnp.sum(p, axis=1, keepdims=True)
    return p.astype(BF16), jnp.broadcast_to(1.0 / l, (sc.shape[0], LANES))


def _pair_output(p, rl, vw, masks):
    o2 = _dot(p, vw) * rl
    n_q = o2.shape[0] // 2
    return jnp.where(masks[0], o2[:n_q], o2[n_q:])


def _band_attn_kernel(q_ref, kp_ref, kc_ref, vp_ref, vc_ref, bias_ref, o_ref,
                      kw_ref, vw_ref, sc_ref, mx_ref, p_ref, rl_ref):
    kw_ref[0:BAND_PAST] = kp_ref[...]
    kw_ref[BAND_PAST:BAND_PAST + Q_TILE] = kc_ref[...]
    vw_ref[0:BAND_PAST] = vp_ref[...]
    vw_ref[BAND_PAST:BAND_PAST + Q_TILE] = vc_ref[...]
    masks = _head_masks()
    cs = lambda hp: slice(hp * LANES, (hp + 1) * LANES)

    def sub(s, carry):
        r0 = pl.multiple_of(s * Q_SUB, Q_SUB)
        rows = pl.ds(r0, Q_SUB)
        win = pl.ds(r0, K_WIN)
        for hp in range(N_PAIRS):
            sc = _pair_scores(q_ref[rows, cs(hp)], kw_ref[win, cs(hp)], bias_ref[hp], masks)
            sc_ref[hp] = sc
            mx_ref[hp] = _row_max(sc)

        @pl.when(pl.program_id(1) == 0)
        def _():
            col = lax.broadcasted_iota(jnp.int32, (1, K_WIN), 1)
            extra = jnp.where(col + r0 < BAND_PAST, NEG_INF, 0.0)
            for hp in range(N_PAIRS):
                sc = sc_ref[hp] + extra
                sc_ref[hp] = sc
                mx_ref[hp] = _row_max(sc)

        for hp in range(N_PAIRS):
            p_ref[hp], rl_ref[hp] = _softmax_parts(sc_ref[hp], mx_ref[hp])
        for hp in range(N_PAIRS):
            o = _pair_output(p_ref[hp], rl_ref[hp], vw_ref[win, cs(hp)], masks)
            o_ref[rows, cs(hp)] = o.astype(o_ref.dtype)
        return carry

    lax.fori_loop(0, Q_TILE // Q_SUB, sub, 0)


def _rel_bias_tile(rel_bias, n_q, n_k, offset, ok):
    r = np.arange(n_q - 1 + n_k)
    idx = np.clip(offset + n_q - 1 - r, -REL_CLIP, REL_CLIP) + REL_CLIP
    v = jnp.pad(rel_bias[:, idx] * LOG2E, ((0, 0), (0, 1)))
    w = n_q + n_k
    flat = jnp.tile(v, (1, n_q))[:, :n_q * (w - 1)]
    toep = flat.reshape(rel_bias.shape[0], n_q, w - 1)[:, :, n_q - 1:]
    return jnp.where(ok[None], toep, NEG_INF).reshape(N_PAIRS, 2 * n_q, n_k)


def _band_attention_prompt(qkvm, rel_bias, n_batch, seq):
    assert seq % Q_TILE == 0 and Q_TILE % BAND_PAST == 0
    nt = seq // Q_TILE
    qc = np.arange(Q_SUB)[:, None] // CHUNK
    kc = np.arange(K_WIN)[None, :] // CHUNK
    bias = _rel_bias_tile(rel_bias, Q_SUB, K_WIN, BAND_PAST, (kc >= qc) & (kc <= qc + BAND_PAST_CHUNKS))
    blk = (Q_TILE, ATT_WIDTH)
    ratio = Q_TILE // BAND_PAST
    cur = lambda col: pl.BlockSpec(blk, lambda b, t: (b * nt + t, col))
    prev = lambda col: pl.BlockSpec(
        (BAND_PAST, ATT_WIDTH), lambda b, t: (jnp.maximum((b * nt + t) * ratio - 1, b * nt * ratio), col))
    return pl.pallas_call(
        _band_attn_kernel,
        grid=(n_batch, nt),
        in_specs=[cur(0), prev(1), cur(1), prev(2), cur(2),
                  pl.BlockSpec((N_PAIRS, 2 * Q_SUB, K_WIN), lambda b, t: (0, 0, 0))],
        out_specs=pl.BlockSpec(blk, lambda b, t: (b * nt + t, 0)),
        out_shape=jax.ShapeDtypeStruct((n_batch * seq, ATT_WIDTH), BF16),
        scratch_shapes=[pltpu.VMEM((BAND_PAST + Q_TILE, ATT_WIDTH), BF16),
                        pltpu.VMEM((BAND_PAST + Q_TILE, ATT_WIDTH), BF16),
                        pltpu.VMEM((N_PAIRS, 2 * Q_SUB, K_WIN), F32),
                        pltpu.VMEM((N_PAIRS, 2 * Q_SUB, LANES), F32),
                        pltpu.VMEM((N_PAIRS, 2 * Q_SUB, K_WIN), BF16),
                        pltpu.VMEM((N_PAIRS, 2 * Q_SUB, LANES), F32)],
        compiler_params=_params(("parallel", "arbitrary"), VMEM_MID_MIB),
        name="band_attn",
    )(qkvm, qkvm, qkvm, qkvm, qkvm, bias)


def _band_attn_sample_kernel(q_ref, kn_ref, vn_ref, ck_ref, cv_ref, bias_ref, o_ref, kw_ref, vw_ref,
                             *, n_cache, n_new):
    kw_ref[...] = jnp.zeros_like(kw_ref)
    vw_ref[...] = jnp.zeros_like(vw_ref)
    kw_ref[0:n_cache] = ck_ref[0].astype(BF16)
    vw_ref[0:n_cache] = cv_ref[0].astype(BF16)
    kw_ref[n_cache:n_cache + n_new] = kn_ref[...]
    vw_ref[n_cache:n_cache + n_new] = vn_ref[...]
    masks = _head_masks()
    for hp in range(N_PAIRS):
        cs = slice(hp * LANES, (hp + 1) * LANES)
        sc = _pair_scores(q_ref[:, cs], kw_ref[:, cs], bias_ref[hp], masks)
        p, rl = _softmax_parts(sc, _row_max(sc))
        o_ref[:, cs] = _pair_output(p, rl, vw_ref[:, cs], masks).astype(o_ref.dtype)


def _band_attention_sample(qkvm, rel_bias, cache_k, cache_v, n_batch, n_new):
    n_cache = cache_k.shape[1]
    n_keys = -(-(n_cache + n_new) // LANES) * LANES
    q_pos = PAST_LEN + np.arange(n_new)[:, None]
    j = np.arange(n_keys)[None, :]
    k_pos = PAST_LEN - n_cache + j
    ok = ((j < n_cache + n_new) & (k_pos >= 0) & (k_pos // CHUNK <= q_pos // CHUNK)
          & (k_pos // CHUNK >= q_pos // CHUNK - BAND_PAST_CHUNKS))
    bias = _rel_bias_tile(rel_bias, n_new, n_keys, n_cache, ok)
    new = lambda col: pl.BlockSpec((n_new, ATT_WIDTH), lambda b: (b, col))
    cache = pl.BlockSpec((1, n_cache, ATT_WIDTH), lambda b: (b, 0, 0))
    return pl.pallas_call(
        functools.partial(_band_attn_sample_kernel, n_cache=n_cache, n_new=n_new),
        grid=(n_batch,),
        in_specs=[new(0), new(1), new(2), cache, cache,
                  pl.BlockSpec((N_PAIRS, 2 * n_new, n_keys), lambda b: (0, 0, 0))],
        out_specs=pl.BlockSpec((n_new, ATT_WIDTH), lambda b: (b, 0)),
        out_shape=jax.ShapeDtypeStruct((n_batch * n_new, ATT_WIDTH), BF16),
        scratch_shapes=[pltpu.VMEM((n_keys, ATT_WIDTH), BF16),
                        pltpu.VMEM((n_keys, ATT_WIDTH), BF16)],
        compiler_params=_params(("parallel",), VMEM_MID_MIB),
        name="band_attn_sample",
    )(qkvm, qkvm, qkvm, cache_k, cache_v, bias)


def _mem_attn_kernel(q_ref, k_ref, v_ref, o_ref):
    k = k_ref[0].astype(BF16)
    v = v_ref[0].astype(BF16)
    for h in range(MEM_HEADS):
        cs = slice(h * MEM_HEAD_DIM, (h + 1) * MEM_HEAD_DIM)
        sc = _dot_nt(q_ref[:, cs], k[:, cs]) * (MEM_HEAD_DIM ** -0.5 * LOG2E)
        mx = jnp.max(sc, axis=1, keepdims=True)
        p = jnp.exp2(sc - mx)
        l = jnp.sum(p, axis=1, keepdims=True)
        o = _dot(p.astype(BF16), v[:, cs]) * (1.0 / l)
        o_ref[:, cs] = o.astype(o_ref.dtype)


def _memory_attention(qkvm, mem_k, mem_v, n_batch, seq, tq):
    nt = seq // tq
    mem = pl.BlockSpec((1, N_MEM, MEM_WIDTH), lambda b, t: (b, 0, 0))
    return pl.pallas_call(
        _mem_attn_kernel,
        grid=(n_batch, nt),
        in_specs=[pl.BlockSpec((tq, MEM_WIDTH), lambda b, t: (b * nt + t, 3)), mem, mem],
        out_specs=pl.BlockSpec((tq, MEM_WIDTH), lambda b, t: (b * nt + t, 0)),
        out_shape=jax.ShapeDtypeStruct((n_batch * seq, MEM_WIDTH), BF16),
        compiler_params=_params(("parallel", "arbitrary"), VMEM_MID_MIB),
        name="mem_attn",
    )(qkvm, mem_k, mem_v)


def _gate_merge_kernel(u_ref, os_ref, oa_ref, om_ref, wgs_ref, wga_ref, wgm_ref,
                       wbs_ref, wba_ref, wbm_ref, out_ref):
    u = u_ref[...]

    def branch(o_ref, wg_ref, wb_ref):
        return jax.nn.sigmoid(_dot(u, wg_ref[...])) * _dot(o_ref[...], wb_ref[...])

    merged = (branch(os_ref, wgs_ref, wbs_ref) + branch(oa_ref, wga_ref, wba_ref)
              + branch(om_ref, wgm_ref, wbm_ref))
    out_ref[...] = merged.astype(out_ref.dtype)


def _gate_merge(u, o_s, o_a, o_m, w_gate, wb_s, wb_a, wb_m, tm, tn):
    m, d = u.shape
    gate = lambda b: pl.BlockSpec((d, tn), lambda i, n: (0, b * (d // tn) + n))
    wb = pl.BlockSpec((SSM_WIDTH, tn), lambda i, n: (0, n))
    ob = pl.BlockSpec((tm, SSM_WIDTH), lambda i, n: (i, 0))
    return pl.pallas_call(
        _gate_merge_kernel,
        grid=(m // tm, d // tn),
        in_specs=[pl.BlockSpec((tm, d), lambda i, n: (i, 0)), ob, ob, ob,
                  gate(0), gate(1), gate(2), wb, wb, wb],
        out_specs=pl.BlockSpec((tm, tn), lambda i, n: (i, n)),
        out_shape=jax.ShapeDtypeStruct((m, d), BF16),
        compiler_params=_params(("parallel", "arbitrary"), VMEM_BIG_MIB),
        name="gate_merge",
    )(u, o_s, o_a, o_m, w_gate, w_gate, w_gate, wb_s, wb_a, wb_m)


def _out_proj_kernel(x_ref, w_ref, h_ref, g_ref, o_ref):
    o_ref[...] = h_ref[...] + _rms(_dot(x_ref[...], w_ref[...]), g_ref[...])


def _out_proj(x, w_out, h, g_post, tm):
    m, d = h.shape
    row = lambda: pl.BlockSpec((tm, d), lambda i: (i, 0))
    return pl.pallas_call(
        _out_proj_kernel,
        grid=(m // tm,),
        in_specs=[row(), pl.BlockSpec((d, d), lambda i: (0, 0)), row(),
                  pl.BlockSpec((1, d), lambda i: (0, 0))],
        out_specs=row(),
        out_shape=jax.ShapeDtypeStruct((m, d), F32),
        compiler_params=_params(("parallel",), VMEM_BIG_MIB),
        name="out_proj",
    )(x, w_out, h, g_post)


def _layer(x, w, *, n_batch, seq, s0, mem_k, mem_v, cache_k, cache_v, tm, keep):
    n_tok = n_batch * seq
    n_chunks = seq // SSM_CHUNK
    big_tm = min(1024, n_tok)
    h1, u = _ffn(x, w["ffn1_norm_pre"], w["ffn1_norm_post"], w["ffn1_w_gate"], w["ffn1_w_up"],
                 w["ffn1_w_down"], w["mix_norm_pre"], big_tm, 512)

    qkvm, k_keep, v_keep, us_r = _in_proj(u, w["w_qkvm"], w["w_ssm"], tm, seq, keep)
    kv = (k_keep, v_keep)

    nb = 2 if (n_chunks > 1 and n_batch % 2 == 0) else (1 if n_chunks > 1 else n_batch)
    y_s, s_fin = _ssm_core(us_r, s0, w["ssm_bd"], w["ssm_cd"], w["ssm_pw"], w["ssm_d"], n_batch,
                           n_chunks, nb)
    o_s = _ssm_glu(y_s, w["ssm_w_glu"], w["ssm_b_glu"], big_tm)

    if cache_k is None:
        o_a = _band_attention_prompt(qkvm, w["att_rel_bias"], n_batch, seq)
    else:
        o_a = _band_attention_sample(qkvm, w["att_rel_bias"], cache_k, cache_v, n_batch, seq)
    o_m = _memory_attention(qkvm, mem_k, mem_v, n_batch, seq, min(2048, seq))

    merged = _gate_merge(u, o_s, o_a, o_m, w["w_gate"], w["w_branch_ssm"], w["w_branch_att"],
                         w["w_branch_mem"], big_tm, 512)
    h2 = _out_proj(merged, w["w_out"], h1, w["mix_norm_post"], tm)
    y = _ffn(h2, w["ffn2_norm_pre"], w["ffn2_norm_post"], w["ffn2_w_gate"], w["ffn2_w_up"],
             w["ffn2_w_down"], None, big_tm, 512)
    return y, kv, s_fin


def kernel(x_prompt, x_sample, mem_prompt, cache_att_k, cache_att_v, cache_mem_k, cache_mem_v, state_ssm_re, state_ssm_im, ffn1_norm_pre, ffn1_norm_post, ffn1_w_gate, ffn1_w_up, ffn1_w_down, mix_norm_pre, mix_norm_post, w_in, ssm_a_re, ssm_a_im, ssm_log_dt, ssm_b_re, ssm_b_im, ssm_c_re, ssm_c_im, ssm_d, ssm_w_glu, ssm_b_glu, att_rel_bias, mem_norm, w_mem_k, w_mem_v, w_branch_ssm, w_branch_att, w_branch_mem, w_out, ffn2_norm_pre, ffn2_norm_post, ffn2_w_gate, ffn2_w_up, ffn2_w_down):
    n_bp, t_p, d = x_prompt.shape
    n_bs, t_s, _ = x_sample.shape
    depth = ffn1_norm_pre.shape[0]
    assert depth == 1 and d == D_MODEL
    keep = min(BAND_PAST, t_p)
    l = 0

    vec = lambda a: a[l].reshape(1, -1).astype(F32)
    mat = lambda a: _cast_bf16(a[l])
    bd, cd, pw = _ssm_params(ssm_a_re[l], ssm_a_im[l], ssm_log_dt[l], ssm_b_re[l],
                             ssm_b_im[l], ssm_c_re[l], ssm_c_im[l])
    w = {
        "ffn1_norm_pre": vec(ffn1_norm_pre), "ffn1_norm_post": vec(ffn1_norm_post),
        "ffn1_w_gate": mat(ffn1_w_gate), "ffn1_w_up": mat(ffn1_w_up), "ffn1_w_down": mat(ffn1_w_down),
        "mix_norm_pre": vec(mix_norm_pre), "mix_norm_post": vec(mix_norm_post),
        "w_ssm": _cast_bf16(w_in[l], 0, COL_Q),
        "w_qkvm": _cast_bf16(w_in[l], COL_Q, COL_GATE - COL_Q),
        "w_gate": _cast_bf16(w_in[l], COL_GATE),
        "ssm_bd": bd, "ssm_cd": cd, "ssm_pw": pw,
        "ssm_d": ssm_d[l].reshape(OCTETS, 1, LANES).astype(F32), "ssm_w_glu": mat(ssm_w_glu), "ssm_b_glu": vec(ssm_b_glu),
        "att_rel_bias": att_rel_bias[l].astype(F32),
        "w_branch_ssm": mat(w_branch_ssm), "w_branch_att": mat(w_branch_att),
        "w_branch_mem": mat(w_branch_mem), "w_out": mat(w_out),
        "ffn2_norm_pre": vec(ffn2_norm_pre), "ffn2_norm_post": vec(ffn2_norm_post),
        "ffn2_w_gate": mat(ffn2_w_gate), "ffn2_w_up": mat(ffn2_w_up), "ffn2_w_down": mat(ffn2_w_down),
    }

    w_mem = jnp.concatenate([w_mem_k[l], w_mem_v[l]], axis=1).astype(BF16)
    mk_p, mv_p = _memory_kv(mem_prompt.reshape(n_bp * N_MEM, d), vec(mem_norm), w_mem, 512)
    mk_p = mk_p.reshape(n_bp, N_MEM, MEM_WIDTH)
    mv_p = mv_p.reshape(n_bp, N_MEM, MEM_WIDTH)
    zero_state = jnp.zeros((OCTETS, n_bp, OCT_STATE), F32)
    y_p, kv_p, sfin_p = _layer(x_prompt.reshape(n_bp * t_p, d), w, n_batch=n_bp, seq=t_p, s0=zero_state,
                               mem_k=mk_p, mem_v=mv_p, cache_k=None, cache_v=None, tm=512, keep=keep)
    sre_p, sim_p = _state_from_octets(sfin_p)

    n_cache = cache_att_k.shape[2]
    s0_s = _state_to_octets(state_ssm_re[l].astype(F32), state_ssm_im[l].astype(F32))
    y_s, kv_s, sfin_s = _layer(x_sample.reshape(n_bs * t_s, d), w, n_batch=n_bs, seq=t_s, s0=s0_s,
                               mem_k=cache_mem_k[l].reshape(n_bs, N_MEM, MEM_WIDTH),
                               mem_v=cache_mem_v[l].reshape(n_bs, N_MEM, MEM_WIDTH),
                               cache_k=cache_att_k[l].reshape(n_bs, n_cache, ATT_WIDTH),
                               cache_v=cache_att_v[l].reshape(n_bs, n_cache, ATT_WIDTH),
                               tm=n_bs * t_s, keep=t_s)
    sre_s, sim_s = _state_from_octets(sfin_s)

    heads = lambda a, nb, t: a.reshape(1, nb, t, ATT_HEADS, ATT_HEAD_DIM)
    memh = lambda a: a.reshape(1, n_bp, N_MEM, MEM_HEADS, MEM_HEAD_DIM)
    return (y_p.reshape(n_bp, t_p, d), y_s.reshape(n_bs, t_s, d),
            heads(kv_p[0], n_bp, keep), heads(kv_p[1], n_bp, keep),
            memh(mk_p), memh(mv_p), sre_p[None], sim_p[None],
            heads(kv_s[0], n_bs, t_s), heads(kv_s[1], n_bs, t_s),
            sre_s[None], sim_s[None])
```

```python
import functools
import math

import numpy as np
import jax
import jax.numpy as jnp
from jax import lax
from jax.experimental import pallas as pl
from jax.experimental.pallas import tpu as pltpu

F32 = jnp.float32
BF16 = jnp.bfloat16

D_MODEL = 2048
CHUNK = 64
BAND_PAST_CHUNKS = 8
BAND_PAST = BAND_PAST_CHUNKS * CHUNK
ATT_HEADS = 16
ATT_HEAD_DIM = 64
ATT_WIDTH = ATT_HEADS * ATT_HEAD_DIM
REL_CLIP = 128
SSM_GROUP = 16
SSM_WIDTH = 1024
SSM_GROUPS = SSM_WIDTH // SSM_GROUP
SSM_STATE = 64
N_MEM = 256
MEM_HEADS = 4
MEM_HEAD_DIM = 256
MEM_WIDTH = MEM_HEADS * MEM_HEAD_DIM
EPS = 1e-6
NEG_INF = -1e30
PAST_LEN = 4096
LOG2E = math.log2(math.e)

COL_Q = SSM_WIDTH
COL_GATE = SSM_WIDTH + 3 * ATT_WIDTH + MEM_WIDTH

SSM_CHUNK = 16
OCTETS = 8
OCT_GROUPS = SSM_GROUPS // OCTETS
LANES = 128
MXU_DIM = 256
OCT_COLS = SSM_CHUNK * LANES
OCT_HALF = OCT_GROUPS * SSM_STATE
OCT_STATE = 2 * OCT_HALF

Q_TILE = 512
Q_SUB = 128
K_WIN = Q_SUB + BAND_PAST

MIB = 1024 * 1024
VMEM_BIG_MIB = 56
VMEM_MID_MIB = 48
VMEM_SMALL_MIB = 32


def _params(sem, vmem_mib):
    return pltpu.CompilerParams(dimension_semantics=sem, vmem_limit_bytes=vmem_mib * MIB)


def _dot(a, b):
    return jnp.dot(a, b, preferred_element_type=F32)


def _dot_nt(a, b):
    return lax.dot_general(a, b, (((1,), (1,)), ((), ())), preferred_element_type=F32)


def _rms(xf, g):
    y = xf * lax.rsqrt(jnp.mean(xf * xf, axis=-1, keepdims=True) + EPS)
    return y * g


CAST_BLOCK_BYTES = 8 * MIB


def _cast_kernel(x_ref, o_ref):
    o_ref[...] = x_ref[...].astype(o_ref.dtype)


def _cast_bf16(w, col0=0, n_cols=None):
    rows, cols = w.shape
    n_cols = cols - col0 if n_cols is None else n_cols
    cw = math.gcd(col0, n_cols)
    assert cw % LANES == 0 and col0 + n_cols <= cols
    tr = rows
    while tr * cw * 4 > CAST_BLOCK_BYTES and tr % 32 == 0:
        tr //= 2
    cb0 = col0 // cw
    return pl.pallas_call(
        _cast_kernel,
        grid=(rows // tr, n_cols // cw),
        in_specs=[pl.BlockSpec((tr, cw), lambda i, j: (i, cb0 + j))],
        out_specs=pl.BlockSpec((tr, cw), lambda i, j: (i, j)),
        out_shape=jax.ShapeDtypeStruct((rows, n_cols), BF16),
        compiler_params=_params(("parallel", "parallel"), VMEM_SMALL_MIB),
        name="cast_bf16",
    )(w)


FFN_SLICES = 8


def _ffn_kernel(hp_ref, hn_ref, gpre_ref, gpost_ref, wg_ref, wu_ref, wd_ref, *rest,
                n_tiles, emit_next):
    if emit_next:
        gnext_ref, out_ref, nxt_ref, *scratch = rest
    else:
        out_ref, *scratch = rest
    xn_refs, acc_refs = scratch[:2], scratch[2:]
    r = pl.program_id(0)
    f = pl.program_id(1)
    rs = hp_ref.shape[0]
    rows = pl.ds(pl.multiple_of(jnp.minimum(f, FFN_SLICES - 1) * rs, rs), rs)

    def pre_norm_slice(slot):
        xn_refs[slot][rows, :] = _rms(hn_ref[...], gpre_ref[...]).astype(BF16)

    def finish_slice(slot):
        hn = hp_ref[...] + 0.5 * _rms(acc_refs[slot][rows, :], gpost_ref[...])
        out_ref[...] = hn
        if emit_next:
            nxt_ref[...] = _rms(hn, gnext_ref[...]).astype(BF16)

    def matmul_chunk(slot):
        xn = xn_refs[slot][...]
        g = _dot(xn, wg_ref[...])
        u = _dot(xn, wu_ref[...])
        a = (g * jax.nn.sigmoid(g)) * u
        d = _dot(a.astype(BF16), wd_ref[...])
        acc_refs[slot][...] = jnp.where(f == 0, d, acc_refs[slot][...] + d)

    @pl.when((r == 0) & (f == 0))
    def _():
        acc_refs[1][...] = jnp.zeros_like(acc_refs[1])

    @pl.when(r == 0)
    def _():
        pre_norm_slice(0)

    for parity in range(2):
        @pl.when((r >= 1) & (r <= n_tiles) & (lax.rem(r, 2) == parity))
        def _():
            finish_slice(parity)
            pre_norm_slice(parity)
            matmul_chunk(1 - parity)

    @pl.when(r == n_tiles + 1)
    def _():
        finish_slice((n_tiles + 1) % 2)


def _ffn(h, g_pre, g_post, wg, wu, wd, g_next, tm, tf):
    emit_next = g_next is not None
    m, d = h.shape
    f_dim = wg.shape[1]
    nf = f_dim // tf
    n = m // tm
    rs = tm // FFN_SLICES
    assert m % tm == 0 and f_dim % tf == 0 and tm % FFN_SLICES == 0 and nf >= FFN_SLICES
    sl = lambda f: jnp.minimum(f, FFN_SLICES - 1)
    done = lambda r, f: (jnp.maximum(r - 2, 0) * FFN_SLICES + jnp.where(r >= 2, sl(f), 0), 0)
    ahead = lambda r, f: (jnp.minimum(r, n - 1) * FFN_SLICES + sl(f), 0)
    chunk = lambda r, f: jnp.where((r >= 1) & (r <= n), f, 0)
    vec = pl.BlockSpec((1, d), lambda r, f: (0, 0))
    in_specs = [pl.BlockSpec((rs, d), done), pl.BlockSpec((rs, d), ahead), vec, vec,
                pl.BlockSpec((d, tf), lambda r, f: (0, chunk(r, f))),
                pl.BlockSpec((d, tf), lambda r, f: (0, chunk(r, f))),
                pl.BlockSpec((tf, d), lambda r, f: (chunk(r, f), 0))]
    args = [h, h, g_pre, g_post, wg, wu, wd]
    out_shape = [jax.ShapeDtypeStruct((m, d), F32)]
    out_specs = [pl.BlockSpec((rs, d), done)]
    if emit_next:
        in_specs.append(vec)
        args.append(g_next)
        out_shape.append(jax.ShapeDtypeStruct((m, d), BF16))
        out_specs.append(pl.BlockSpec((rs, d), done))
    res = pl.pallas_call(
        functools.partial(_ffn_kernel, n_tiles=n, emit_next=emit_next),
        grid=(n + 2, nf),
        in_specs=in_specs,
        out_specs=out_specs,
        out_shape=out_shape,
        scratch_shapes=[pltpu.VMEM((tm, d), BF16), pltpu.VMEM((tm, d), BF16),
                        pltpu.VMEM((tm, d), F32), pltpu.VMEM((tm, d), F32)],
        compiler_params=_params(("arbitrary", "arbitrary"), VMEM_BIG_MIB),
        name="ffn",
    )(*args)
    return res if emit_next else res[0]


def _memory_kv_kernel(x_ref, g_ref, w_ref, k_ref, v_ref):
    x = _rms(x_ref[...], g_ref[...]).astype(BF16)
    k_ref[...] = _dot(x, w_ref[:, :MEM_WIDTH])
    v_ref[...] = _dot(x, w_ref[:, MEM_WIDTH:])


def _memory_kv(x, gain, w, tm):
    m, k = x.shape
    tm = min(tm, m)
    assert m % tm == 0 and w.shape == (k, 2 * MEM_WIDTH)
    out = pl.BlockSpec((tm, MEM_WIDTH), lambda i: (i, 0))
    return pl.pallas_call(
        _memory_kv_kernel,
        grid=(m // tm,),
        in_specs=[pl.BlockSpec((tm, k), lambda i: (i, 0)), pl.BlockSpec((1, k), lambda i: (0, 0)),
                  pl.BlockSpec((k, 2 * MEM_WIDTH), lambda i: (0, 0))],
        out_specs=[out, out],
        out_shape=[jax.ShapeDtypeStruct((m, MEM_WIDTH), F32)] * 2,
        compiler_params=_params(("parallel",), VMEM_MID_MIB),
        name="memory_kv",
    )(x, gain, w)


def _in_proj_kernel(x_ref, wq_ref, ws_ref, o_ref, k_ref, v_ref, us_ref, nat_ref):
    x = x_ref[...]
    for j, keep_ref in enumerate((None, k_ref, v_ref, None)):
        cs = slice(j * ATT_WIDTH, (j + 1) * ATT_WIDTH)
        r = _dot(x, wq_ref[:, cs])
        o_ref[:, cs] = r.astype(o_ref.dtype)
        if keep_ref is not None:
            keep_ref[...] = r
    r = _dot(x, ws_ref[...])
    tc = nat_ref.shape[1] // SSM_CHUNK
    for m in range(OCTETS):
        nat_ref[m] = r[:, m * LANES:(m + 1) * LANES]
    for j in range(SSM_CHUNK):
        for m in range(OCTETS):
            us_ref[m, :, j * LANES:(j + 1) * LANES] = nat_ref[m, pl.ds(j, tc, stride=SSM_CHUNK), :]


def _in_proj(u, w_qkvm, w_ssm, tm, seq, keep):
    n_tok = u.shape[0]
    assert n_tok % tm == 0 and tm % SSM_CHUNK == 0 and MEM_WIDTH == ATT_WIDTH
    tc = tm // SSM_CHUNK
    if keep == seq:
        kept_rows, kept = n_tok, pl.BlockSpec((tm, ATT_WIDTH), lambda i: (i, 0))
    else:
        assert keep == tm and seq % tm == 0
        kept_rows, kept = (n_tok // seq) * keep, pl.BlockSpec((tm, ATT_WIDTH), lambda i: (i // (seq // tm), 0))
    return pl.pallas_call(
        _in_proj_kernel,
        grid=(n_tok // tm,),
        in_specs=[pl.BlockSpec((tm, D_MODEL), lambda i: (i, 0)),
                  pl.BlockSpec((D_MODEL, 4 * ATT_WIDTH), lambda i: (0, 0)),
                  pl.BlockSpec((D_MODEL, SSM_WIDTH), lambda i: (0, 0))],
        out_specs=[pl.BlockSpec((tm, 4 * ATT_WIDTH), lambda i: (i, 0)), kept, kept,
                   pl.BlockSpec((OCTETS, tc, OCT_COLS), lambda i: (0, i, 0))],
        out_shape=[jax.ShapeDtypeStruct((n_tok, 4 * ATT_WIDTH), BF16),
                   jax.ShapeDtypeStruct((kept_rows, ATT_WIDTH), F32),
                   jax.ShapeDtypeStruct((kept_rows, ATT_WIDTH), F32),
                   jax.ShapeDtypeStruct((OCTETS, n_tok // SSM_CHUNK, OCT_COLS), F32)],
        scratch_shapes=[pltpu.VMEM((OCTETS, tm, LANES), F32)],
        compiler_params=_params(("arbitrary",), VMEM_BIG_MIB),
        name="in_proj",
    )(u, w_qkvm, w_ssm)


def _ssm_params(a_re, a_im, log_dt, b_re, b_im, c_re, c_im):
    dt = jnp.exp(log_dt)[:, None]
    mag = jnp.exp(a_re * dt)
    ab_re = mag * jnp.cos(a_im * dt)
    ab_im = mag * jnp.sin(a_im * dt)
    den = a_re * a_re + a_im * a_im
    n_re = ab_re - 1.0
    n_im = ab_im
    k_re = (n_re * a_re + n_im * a_im) / den
    k_im = (n_im * a_re - n_re * a_im) / den
    bb_re = k_re[..., None] * b_re - k_im[..., None] * b_im
    bb_im = k_re[..., None] * b_im + k_im[..., None] * b_re
    pr = [jnp.ones_like(ab_re)]
    pi = [jnp.zeros_like(ab_re)]
    for _ in range(SSM_CHUNK):
        pr.append(pr[-1] * ab_re - pi[-1] * ab_im)
        pi.append(pr[-2] * ab_im + pi[-1] * ab_re)
    n_pw = SSM_CHUNK + 1
    pw = jnp.concatenate([jnp.stack(pr).reshape(n_pw, OCTETS, OCT_HALF),
                          jnp.stack(pi).reshape(n_pw, OCTETS, OCT_HALF)], axis=2)
    pw = jnp.transpose(pw, (1, 0, 2))
    eye = jnp.eye(OCT_GROUPS, dtype=F32)

    def expand(x):
        x4 = x.reshape(OCTETS, OCT_GROUPS, x.shape[1], SSM_STATE)
        out = eye[None, :, None, :, None] * x4[:, :, :, None, :]
        return out.reshape(OCTETS, OCT_GROUPS * x.shape[1], OCT_HALF)

    bd = jnp.concatenate([expand(jnp.transpose(bb_re, (0, 2, 1))),
                          expand(jnp.transpose(bb_im, (0, 2, 1)))], axis=2)
    cd = jnp.concatenate([expand(c_re), expand(c_im)], axis=2)
    return bd, cd, pw


def _state_to_octets(s_re, s_im):
    b = s_re.shape[0]
    s = jnp.stack([s_re, s_im], 0).reshape(2, b, OCTETS, OCT_GROUPS, SSM_STATE)
    return jnp.transpose(s, (2, 1, 0, 3, 4)).reshape(OCTETS, b, OCT_STATE)


def _state_from_octets(s):
    b = s.shape[1]
    s = s.reshape(OCTETS, b, 2, OCT_GROUPS, SSM_STATE)
    s = jnp.transpose(s, (2, 1, 0, 3, 4)).reshape(2, b, SSM_GROUPS, SSM_STATE)
    return s[0], s[1]


def _split_bf16(x):
    hi = x.astype(BF16)
    return hi, (x - hi.astype(F32)).astype(BF16)


def _ssm_build_operators(bd_ref, cd_ref, pw_ref, t8_ref, wus_ref, wso_ref):
    bd_re, bd_im = bd_ref[0, :, :OCT_HALF], bd_ref[0, :, OCT_HALF:]
    cd_re, cd_im = cd_ref[0, :, :OCT_HALF], cd_ref[0, :, OCT_HALF:]
    c_hi, c_lo = _split_bf16(jnp.concatenate([cd_re, -cd_im], axis=1))
    blk = lambda i: slice(i * LANES, (i + 1) * LANES)
    for a in range(SSM_CHUNK // 2):
        t8_ref[blk(2 * a + 1), blk(2 * a)] = jnp.zeros((LANES, LANES), BF16)
    for k in range(SSM_CHUNK + 1):
        p_re = pw_ref[0, k:k + 1, :OCT_HALF]
        p_im = pw_ref[0, k:k + 1, OCT_HALF:]
        if k < SSM_CHUNK:
            e = jnp.concatenate([bd_re * p_re - bd_im * p_im, bd_re * p_im + bd_im * p_re], axis=1)
            wus_ref[blk(SSM_CHUNK - 1 - k), :] = e.astype(BF16)
            e_hi, e_lo = _split_bf16(e)
            lag = (_dot_nt(e_hi, c_hi) + _dot_nt(e_hi, c_lo) + _dot_nt(e_lo, c_hi)).astype(BF16)
            for j in range(SSM_CHUNK - k):
                t8_ref[blk(j), blk(j + k)] = lag
        if k >= 1:
            g = jnp.concatenate([cd_re * p_re - cd_im * p_im, -(cd_re * p_im + cd_im * p_re)], axis=1)
            wso_ref[blk(k - 1), :] = g.astype(BF16)


def _ssm_core_kernel(us_ref, s0_ref, bd_ref, cd_ref, pw_ref, d_ref, y_ref, sfin_ref,
                     t8_ref, wus_ref, wso_ref, ds_ref, sp_ref, *, nb, n_chunks):
    @pl.when(pl.program_id(1) == 0)
    def _():
        _ssm_build_operators(bd_ref, cd_ref, pw_ref, t8_ref, wus_ref, wso_ref)

    ub = us_ref[0].astype(BF16)
    ds = _dot(ub, wus_ref[...])
    a16 = pw_ref[0, SSM_CHUNK:SSM_CHUNK + 1, :]
    n_blk = OCT_HALF // LANES
    blk = lambda k: slice(k * LANES, (k + 1) * LANES)

    def advance(s_re, s_im, d_re, d_im, a_re, a_im):
        return a_re * s_re - a_im * s_im + d_re, a_re * s_im + a_im * s_re + d_im

    if n_chunks == 1:
        s0 = s0_ref[0, 0]
        n_re, n_im = advance(s0[:, :OCT_HALF], s0[:, OCT_HALF:], ds[:, :OCT_HALF], ds[:, OCT_HALF:],
                             a16[:, :OCT_HALF], a16[:, OCT_HALF:])
        sfin_ref[0, 0] = jnp.concatenate([n_re, n_im], axis=1)
        spb = s0.astype(BF16)
    else:
        for k in range(2 * n_blk):
            ds_ref[k] = ds[:, blk(k)]

        def body(c, carry):
            at = pl.ds(c, nb, stride=n_chunks)
            new = list(carry)
            for k in range(n_blk):
                sp_ref[k, at, :] = carry[k]
                sp_ref[n_blk + k, at, :] = carry[n_blk + k]
                new[k], new[n_blk + k] = advance(carry[k], carry[n_blk + k], ds_ref[k, at, :],
                                                 ds_ref[n_blk + k, at, :], a16[:, blk(k)],
                                                 a16[:, blk(n_blk + k)])
            return tuple(new)

        init = tuple(s0_ref[0, 0, :, blk(k)] for k in range(2 * n_blk))
        fin = lax.fori_loop(0, n_chunks, body, init, unroll=2)
        for k in range(2 * n_blk):
            sfin_ref[0, 0, :, blk(k)] = fin[k]
        spb = jnp.concatenate([sp_ref[k] for k in range(2 * n_blk)], axis=1).astype(BF16)
    rows = ub.shape[0]
    d_skip = d_ref[0]
    for nt in range(OCT_COLS // MXU_DIM):
        k_hi = (nt + 1) * MXU_DIM
        cs = slice(nt * MXU_DIM, k_hi)
        y = _dot(ub[:, :k_hi], t8_ref[:k_hi, cs]) + _dot_nt(spb, wso_ref[cs, :])
        for j in range(MXU_DIM // LANES):
            i = nt * (MXU_DIM // LANES) + j
            y_ref[pl.ds(i, rows, stride=SSM_CHUNK), :] = _gelu_tanh(
                y[:, blk(j)] + d_skip * us_ref[0, :, blk(i)])


def _ssm_core(us_r, s0, bd, cd, pw, d_oct, n_batch, n_chunks, nb):
    nc = us_r.shape[1]
    rows = nb * n_chunks
    assert n_batch % nb == 0 and nc == n_batch * n_chunks
    nr = n_batch // nb
    s0 = s0.reshape(OCTETS, nr, nb, OCT_STATE)
    par = lambda rows_: pl.BlockSpec((1, rows_, OCT_STATE), lambda m, r: (m, 0, 0))
    y, sfin = pl.pallas_call(
        functools.partial(_ssm_core_kernel, nb=nb, n_chunks=n_chunks),
        grid=(OCTETS, nr),
        in_specs=[pl.BlockSpec((1, rows, OCT_COLS), lambda m, r: (m, r, 0)),
                  pl.BlockSpec((1, 1, nb, OCT_STATE), lambda m, r: (m, r, 0, 0)),
                  par(LANES), par(LANES), par(SSM_CHUNK + 1),
                  pl.BlockSpec((1, 1, LANES), lambda m, r: (m, 0, 0))],
        out_specs=[pl.BlockSpec((rows * SSM_CHUNK, LANES), lambda m, r: (r, m)),
                   pl.BlockSpec((1, 1, nb, OCT_STATE), lambda m, r: (m, r, 0, 0))],
        out_shape=[jax.ShapeDtypeStruct((nc * SSM_CHUNK, SSM_WIDTH), F32),
                   jax.ShapeDtypeStruct((OCTETS, nr, nb, OCT_STATE), F32)],
        scratch_shapes=[pltpu.VMEM((OCT_COLS, OCT_COLS), BF16),
                        pltpu.VMEM((OCT_COLS, OCT_STATE), BF16),
                        pltpu.VMEM((OCT_COLS, OCT_STATE), BF16),
                        pltpu.VMEM((OCT_STATE // LANES, rows, LANES), F32),
                        pltpu.VMEM((OCT_STATE // LANES, rows, LANES), F32)],
        compiler_params=_params(("arbitrary", "arbitrary"), VMEM_BIG_MIB),
        name="ssm_core",
    )(us_r, s0, bd, cd, pw, d_oct)
    return y, sfin.reshape(OCTETS, n_batch, OCT_STATE)


def _gelu_tanh(x):
    c = math.sqrt(2.0 / math.pi)
    return x * (0.5 * (1.0 + jnp.tanh(c * (x + 0.044715 * (x * x * x)))))


def _ssm_glu_kernel(y_ref, w_ref, b_ref, o_ref):
    yg = y_ref[...]
    z = _dot(yg.astype(BF16), w_ref[...]) + b_ref[...]
    o_ref[...] = (yg * jax.nn.sigmoid(z)).astype(o_ref.dtype)


def _ssm_glu(y, w_glu, b_glu, tm):
    n_tok = y.shape[0]
    assert n_tok % tm == 0
    row = pl.BlockSpec((tm, SSM_WIDTH), lambda i: (i, 0))
    return pl.pallas_call(
        _ssm_glu_kernel,
        grid=(n_tok // tm,),
        in_specs=[row, pl.BlockSpec((SSM_WIDTH, SSM_WIDTH), lambda i: (0, 0)),
                  pl.BlockSpec((1, SSM_WIDTH), lambda i: (0, 0))],
        out_specs=row,
        out_shape=jax.ShapeDtypeStruct((n_tok, SSM_WIDTH), BF16),
        compiler_params=_params(("parallel",), VMEM_MID_MIB),
        name="ssm_glu",
    )(y, w_glu, b_glu)


N_PAIRS = ATT_HEADS // 2


def _head_masks():
    lane = lax.broadcasted_iota(jnp.int32, (1, LANES), 1)
    return (lane < ATT_HEAD_DIM, lane >= ATT_HEAD_DIM)


def _pair_scores(q2, kw, bias2, masks):
    qq = jnp.concatenate([jnp.where(m, q2, jnp.zeros_like(q2)) for m in masks], axis=0)
    return _dot_nt(qq, kw) * (ATT_HEAD_DIM ** -0.5 * LOG2E) + bias2


def _row_max(sc):
    return jnp.broadcast_to(jnp.max(sc, axis=1, keepdims=True), (sc.shape[0], LANES))


def _softmax_parts(sc, mx):
    p = jnp.exp2(sc - jnp.concatenate([mx] * (sc.shape[1] // LANES), axis=1))
    l = jnp.sum(p, axis=1, keepdims=True)
    return p.astype(BF16), jnp.broadcast_to(1.0 / l, (sc.shape[0], LANES))


def _pair_output(p, rl, vw, masks):
    o2 = _dot(p, vw) * rl
    n_q = o2.shape[0] // 2
    return jnp.where(masks[0], o2[:n_q], o2[n_q:])


def _band_attn_kernel(q_ref, kp_ref, kc_ref, vp_ref, vc_ref, bias_ref, o_ref,
                      kw_ref, vw_ref, sc_ref, mx_ref, p_ref, rl_ref):
    kw_ref[0:BAND_PAST] = kp_ref[...]
    kw_ref[BAND_PAST:BAND_PAST + Q_TILE] = kc_ref[...]
    vw_ref[0:BAND_PAST] = vp_ref[...]
    vw_ref[BAND_PAST:BAND_PAST + Q_TILE] = vc_ref[...]
    masks = _head_masks()
    cs = lambda hp: slice(hp * LANES, (hp + 1) * LANES)

    def sub(s, carry):
        r0 = pl.multiple_of(s * Q_SUB, Q_SUB)
        rows = pl.ds(r0, Q_SUB)
        win = pl.ds(r0, K_WIN)
        for hp in range(N_PAIRS):
            sc = _pair_scores(q_ref[rows, cs(hp)], kw_ref[win, cs(hp)], bias_ref[hp], masks)
            sc_ref[hp] = sc
            mx_ref[hp] = _row_max(sc)

        @pl.when(pl.program_id(1) == 0)
        def _():
            col = lax.broadcasted_iota(jnp.int32, (1, K_WIN), 1)
            extra = jnp.where(col + r0 < BAND_PAST, NEG_INF, 0.0)
            for hp in range(N_PAIRS):
                sc = sc_ref[hp] + extra
                sc_ref[hp] = sc
                mx_ref[hp] = _row_max(sc)

        for hp in range(N_PAIRS):
            p_ref[hp], rl_ref[hp] = _softmax_parts(sc_ref[hp], mx_ref[hp])
        for hp in range(N_PAIRS):
            o = _pair_output(p_ref[hp], rl_ref[hp], vw_ref[win, cs(hp)], masks)
            o_ref[rows, cs(hp)] = o.astype(o_ref.dtype)
        return carry

    lax.fori_loop(0, Q_TILE // Q_SUB, sub, 0)


def _rel_bias_tile(rel_bias, n_q, n_k, offset, ok):
    r = np.arange(n_q - 1 + n_k)
    idx = np.clip(offset + n_q - 1 - r, -REL_CLIP, REL_CLIP) + REL_CLIP
    v = jnp.pad(rel_bias[:, idx] * LOG2E, ((0, 0), (0, 1)))
    w = n_q + n_k
    flat = jnp.tile(v, (1, n_q))[:, :n_q * (w - 1)]
    toep = flat.reshape(rel_bias.shape[0], n_q, w - 1)[:, :, n_q - 1:]
    return jnp.where(ok[None], toep, NEG_INF).reshape(N_PAIRS, 2 * n_q, n_k)


def _band_attention_prompt(qkvm, rel_bias, n_batch, seq):
    assert seq % Q_TILE == 0 and Q_TILE % BAND_PAST == 0
    nt = seq // Q_TILE
    qc = np.arange(Q_SUB)[:, None] // CHUNK
    kc = np.arange(K_WIN)[None, :] // CHUNK
    bias = _rel_bias_tile(rel_bias, Q_SUB, K_WIN, BAND_PAST, (kc >= qc) & (kc <= qc + BAND_PAST_CHUNKS))
    blk = (Q_TILE, ATT_WIDTH)
    ratio = Q_TILE // BAND_PAST
    cur = lambda col: pl.BlockSpec(blk, lambda b, t: (b * nt + t, col))
    prev = lambda col: pl.BlockSpec(
        (BAND_PAST, ATT_WIDTH), lambda b, t: (jnp.maximum((b * nt + t) * ratio - 1, b * nt * ratio), col))
    return pl.pallas_call(
        _band_attn_kernel,
        grid=(n_batch, nt),
        in_specs=[cur(0), prev(1), cur(1), prev(2), cur(2),
                  pl.BlockSpec((N_PAIRS, 2 * Q_SUB, K_WIN), lambda b, t: (0, 0, 0))],
        out_specs=pl.BlockSpec(blk, lambda b, t: (b * nt + t, 0)),
        out_shape=jax.ShapeDtypeStruct((n_batch * seq, ATT_WIDTH), BF16),
        scratch_shapes=[pltpu.VMEM((BAND_PAST + Q_TILE, ATT_WIDTH), BF16),
                        pltpu.VMEM((BAND_PAST + Q_TILE, ATT_WIDTH), BF16),
                        pltpu.VMEM((N_PAIRS, 2 * Q_SUB, K_WIN), F32),
                        pltpu.VMEM((N_PAIRS, 2 * Q_SUB, LANES), F32),
                        pltpu.VMEM((N_PAIRS, 2 * Q_SUB, K_WIN), BF16),
                        pltpu.VMEM((N_PAIRS, 2 * Q_SUB, LANES), F32)],
        compiler_params=_params(("parallel", "arbitrary"), VMEM_MID_MIB),
        name="band_attn",
    )(qkvm, qkvm, qkvm, qkvm, qkvm, bias)


def _band_attn_sample_kernel(q_ref, kn_ref, vn_ref, ck_ref, cv_ref, bias_ref, o_ref, kw_ref, vw_ref,
                             *, n_cache, n_new):
    kw_ref[...] = jnp.zeros_like(kw_ref)
    vw_ref[...] = jnp.zeros_like(vw_ref)
    kw_ref[0:n_cache] = ck_ref[0].astype(BF16)
    vw_ref[0:n_cache] = cv_ref[0].astype(BF16)
    kw_ref[n_cache:n_cache + n_new] = kn_ref[...]
    vw_ref[n_cache:n_cache + n_new] = vn_ref[...]
    masks = _head_masks()
    for hp in range(N_PAIRS):
        cs = slice(hp * LANES, (hp + 1) * LANES)
        sc = _pair_scores(q_ref[:, cs], kw_ref[:, cs], bias_ref[hp], masks)
        p, rl = _softmax_parts(sc, _row_max(sc))
        o_ref[:, cs] = _pair_output(p, rl, vw_ref[:, cs], masks).astype(o_ref.dtype)


def _band_attention_sample(qkvm, rel_bias, cache_k, cache_v, n_batch, n_new):
    n_cache = cache_k.shape[1]
    n_keys = -(-(n_cache + n_new) // LANES) * LANES
    q_pos = PAST_LEN + np.arange(n_new)[:, None]
    j = np.arange(n_keys)[None, :]
    k_pos = PAST_LEN - n_cache + j
    ok = ((j < n_cache + n_new) & (k_pos >= 0) & (k_pos // CHUNK <= q_pos // CHUNK)
          & (k_pos // CHUNK >= q_pos // CHUNK - BAND_PAST_CHUNKS))
    bias = _rel_bias_tile(rel_bias, n_new, n_keys, n_cache, ok)
    new = lambda col: pl.BlockSpec((n_new, ATT_WIDTH), lambda b: (b, col))
    cache = pl.BlockSpec((1, n_cache, ATT_WIDTH), lambda b: (b, 0, 0))
    return pl.pallas_call(
        functools.partial(_band_attn_sample_kernel, n_cache=n_cache, n_new=n_new),
        grid=(n_batch,),
        in_specs=[new(0), new(1), new(2), cache, cache,
                  pl.BlockSpec((N_PAIRS, 2 * n_new, n_keys), lambda b: (0, 0, 0))],
        out_specs=pl.BlockSpec((n_new, ATT_WIDTH), lambda b: (b, 0)),
        out_shape=jax.ShapeDtypeStruct((n_batch * n_new, ATT_WIDTH), BF16),
        scratch_shapes=[pltpu.VMEM((n_keys, ATT_WIDTH), BF16),
                        pltpu.VMEM((n_keys, ATT_WIDTH), BF16)],
        compiler_params=_params(("parallel",), VMEM_MID_MIB),
        name="band_attn_sample",
    )(qkvm, qkvm, qkvm, cache_k, cache_v, bias)


def _mem_attn_kernel(q_ref, k_ref, v_ref, o_ref):
    k = k_ref[0].astype(BF16)
    v = v_ref[0].astype(BF16)
    for h in range(MEM_HEADS):
        cs = slice(h * MEM_HEAD_DIM, (h + 1) * MEM_HEAD_DIM)
        sc = _dot_nt(q_ref[:, cs], k[:, cs]) * (MEM_HEAD_DIM ** -0.5 * LOG2E)
        mx = jnp.max(sc, axis=1, keepdims=True)
        p = jnp.exp2(sc - mx)
        l = jnp.sum(p, axis=1, keepdims=True)
        o = _dot(p.astype(BF16), v[:, cs]) * (1.0 / l)
        o_ref[:, cs] = o.astype(o_ref.dtype)


def _memory_attention(qkvm, mem_k, mem_v, n_batch, seq, tq):
    nt = seq // tq
    mem = pl.BlockSpec((1, N_MEM, MEM_WIDTH), lambda b, t: (b, 0, 0))
    return pl.pallas_call(
        _mem_attn_kernel,
        grid=(n_batch, nt),
        in_specs=[pl.BlockSpec((tq, MEM_WIDTH), lambda b, t: (b * nt + t, 3)), mem, mem],
        out_specs=pl.BlockSpec((tq, MEM_WIDTH), lambda b, t: (b * nt + t, 0)),
        out_shape=jax.ShapeDtypeStruct((n_batch * seq, MEM_WIDTH), BF16),
        compiler_params=_params(("parallel", "arbitrary"), VMEM_MID_MIB),
        name="mem_attn",
    )(qkvm, mem_k, mem_v)


def _gate_merge_kernel(u_ref, os_ref, oa_ref, om_ref, wgs_ref, wga_ref, wgm_ref,
                       wbs_ref, wba_ref, wbm_ref, out_ref):
    u = u_ref[...]

    def branch(o_ref, wg_ref, wb_ref):
        return jax.nn.sigmoid(_dot(u, wg_ref[...])) * _dot(o_ref[...], wb_ref[...])

    merged = (branch(os_ref, wgs_ref, wbs_ref) + branch(oa_ref, wga_ref, wba_ref)
              + branch(om_ref, wgm_ref, wbm_ref))
    out_ref[...] = merged.astype(out_ref.dtype)


def _gate_merge(u, o_s, o_a, o_m, w_gate, wb_s, wb_a, wb_m, tm, tn):
    m, d = u.shape
    gate = lambda b: pl.BlockSpec((d, tn), lambda i, n: (0, b * (d // tn) + n))
    wb = pl.BlockSpec((SSM_WIDTH, tn), lambda i, n: (0, n))
    ob = pl.BlockSpec((tm, SSM_WIDTH), lambda i, n: (i, 0))
    return pl.pallas_call(
        _gate_merge_kernel,
        grid=(m // tm, d // tn),
        in_specs=[pl.BlockSpec((tm, d), lambda i, n: (i, 0)), ob, ob, ob,
                  gate(0), gate(1), gate(2), wb, wb, wb],
        out_specs=pl.BlockSpec((tm, tn), lambda i, n: (i, n)),
        out_shape=jax.ShapeDtypeStruct((m, d), BF16),
        compiler_params=_params(("parallel", "arbitrary"), VMEM_BIG_MIB),
        name="gate_merge",
    )(u, o_s, o_a, o_m, w_gate, w_gate, w_gate, wb_s, wb_a, wb_m)


def _out_proj_kernel(x_ref, w_ref, h_ref, g_ref, o_ref):
    o_ref[...] = h_ref[...] + _rms(_dot(x_ref[...], w_ref[...]), g_ref[...])


def _out_proj(x, w_out, h, g_post, tm):
    m, d = h.shape
    row = lambda: pl.BlockSpec((tm, d), lambda i: (i, 0))
    return pl.pallas_call(
        _out_proj_kernel,
        grid=(m // tm,),
        in_specs=[row(), pl.BlockSpec((d, d), lambda i: (0, 0)), row(),
                  pl.BlockSpec((1, d), lambda i: (0, 0))],
        out_specs=row(),
        out_shape=jax.ShapeDtypeStruct((m, d), F32),
        compiler_params=_params(("parallel",), VMEM_BIG_MIB),
        name="out_proj",
    )(x, w_out, h, g_post)


def _layer(x, w, *, n_batch, seq, s0, mem_k, mem_v, cache_k, cache_v, tm, keep):
    n_tok = n_batch * seq
    n_chunks = seq // SSM_CHUNK
    big_tm = min(1024, n_tok)
    h1, u = _ffn(x, w["ffn1_norm_pre"], w["ffn1_norm_post"], w["ffn1_w_gate"], w["ffn1_w_up"],
                 w["ffn1_w_down"], w["mix_norm_pre"], big_tm, 512)

    qkvm, k_keep, v_keep, us_r = _in_proj(u, w["w_qkvm"], w["w_ssm"], tm, seq, keep)
    kv = (k_keep, v_keep)

    nb = 2 if (n_chunks > 1 and n_batch % 2 == 0) else (1 if n_chunks > 1 else n_batch)
    y_s, s_fin = _ssm_core(us_r, s0, w["ssm_bd"], w["ssm_cd"], w["ssm_pw"], w["ssm_d"], n_batch,
                           n_chunks, nb)
    o_s = _ssm_glu(y_s, w["ssm_w_glu"], w["ssm_b_glu"], big_tm)

    if cache_k is None:
        o_a = _band_attention_prompt(qkvm, w["att_rel_bias"], n_batch, seq)
    else:
        o_a = _band_attention_sample(qkvm, w["att_rel_bias"], cache_k, cache_v, n_batch, seq)
    o_m = _memory_attention(qkvm, mem_k, mem_v, n_batch, seq, min(4096, seq))

    merged = _gate_merge(u, o_s, o_a, o_m, w["w_gate"], w["w_branch_ssm"], w["w_branch_att"],
                         w["w_branch_mem"], big_tm, 512)
    h2 = _out_proj(merged, w["w_out"], h1, w["mix_norm_post"], tm)
    y = _ffn(h2, w["ffn2_norm_pre"], w["ffn2_norm_post"], w["ffn2_w_gate"], w["ffn2_w_up"],
             w["ffn2_w_down"], None, big_tm, 512)
    return y, kv, s_fin


def kernel(x_prompt, x_sample, mem_prompt, cache_att_k, cache_att_v, cache_mem_k, cache_mem_v, state_ssm_re, state_ssm_im, ffn1_norm_pre, ffn1_norm_post, ffn1_w_gate, ffn1_w_up, ffn1_w_down, mix_norm_pre, mix_norm_post, w_in, ssm_a_re, ssm_a_im, ssm_log_dt, ssm_b_re, ssm_b_im, ssm_c_re, ssm_c_im, ssm_d, ssm_w_glu, ssm_b_glu, att_rel_bias, mem_norm, w_mem_k, w_mem_v, w_branch_ssm, w_branch_att, w_branch_mem, w_out, ffn2_norm_pre, ffn2_norm_post, ffn2_w_gate, ffn2_w_up, ffn2_w_down):
    n_bp, t_p, d = x_prompt.shape
    n_bs, t_s, _ = x_sample.shape
    depth = ffn1_norm_pre.shape[0]
    assert depth == 1 and d == D_MODEL
    keep = min(BAND_PAST, t_p)
    l = 0

    vec = lambda a: a[l].reshape(1, -1).astype(F32)
    mat = lambda a: _cast_bf16(a[l])
    bd, cd, pw = _ssm_params(ssm_a_re[l], ssm_a_im[l], ssm_log_dt[l], ssm_b_re[l],
                             ssm_b_im[l], ssm_c_re[l], ssm_c_im[l])
    w = {
        "ffn1_norm_pre": vec(ffn1_norm_pre), "ffn1_norm_post": vec(ffn1_norm_post),
        "ffn1_w_gate": mat(ffn1_w_gate), "ffn1_w_up": mat(ffn1_w_up), "ffn1_w_down": mat(ffn1_w_down),
        "mix_norm_pre": vec(mix_norm_pre), "mix_norm_post": vec(mix_norm_post),
        "w_ssm": _cast_bf16(w_in[l], 0, COL_Q),
        "w_qkvm": _cast_bf16(w_in[l], COL_Q, COL_GATE - COL_Q),
        "w_gate": _cast_bf16(w_in[l], COL_GATE),
        "ssm_bd": bd, "ssm_cd": cd, "ssm_pw": pw,
        "ssm_d": ssm_d[l].reshape(OCTETS, 1, LANES).astype(F32), "ssm_w_glu": mat(ssm_w_glu), "ssm_b_glu": vec(ssm_b_glu),
        "att_rel_bias": att_rel_bias[l].astype(F32),
        "w_branch_ssm": mat(w_branch_ssm), "w_branch_att": mat(w_branch_att),
        "w_branch_mem": mat(w_branch_mem), "w_out": mat(w_out),
        "ffn2_norm_pre": vec(ffn2_norm_pre), "ffn2_norm_post": vec(ffn2_norm_post),
        "ffn2_w_gate": mat(ffn2_w_gate), "ffn2_w_up": mat(ffn2_w_up), "ffn2_w_down": mat(ffn2_w_down),
    }

    w_mem = jnp.concatenate([w_mem_k[l], w_mem_v[l]], axis=1).astype(BF16)
    mk_p, mv_p = _memory_kv(mem_prompt.reshape(n_bp * N_MEM, d), vec(mem_norm), w_mem, 512)
    mk_p = mk_p.reshape(n_bp, N_MEM, MEM_WIDTH)
    mv_p = mv_p.reshape(n_bp, N_MEM, MEM_WIDTH)
    zero_state = jnp.zeros((OCTETS, n_bp, OCT_STATE), F32)
    y_p, kv_p, sfin_p = _layer(x_prompt.reshape(n_bp * t_p, d), w, n_batch=n_bp, seq=t_p, s0=zero_state,
                               mem_k=mk_p, mem_v=mv_p, cache_k=None, cache_v=None, tm=512, keep=keep)
    sre_p, sim_p = _state_from_octets(sfin_p)

    n_cache = cache_att_k.shape[2]
    s0_s = _state_to_octets(state_ssm_re[l].astype(F32), state_ssm_im[l].astype(F32))
    y_s, kv_s, sfin_s = _layer(x_sample.reshape(n_bs * t_s, d), w, n_batch=n_bs, seq=t_s, s0=s0_s,
                               mem_k=cache_mem_k[l].reshape(n_bs, N_MEM, MEM_WIDTH),
                               mem_v=cache_mem_v[l].reshape(n_bs, N_MEM, MEM_WIDTH),
                               cache_k=cache_att_k[l].reshape(n_bs, n_cache, ATT_WIDTH),
                               cache_v=cache_att_v[l].reshape(n_bs, n_cache, ATT_WIDTH),
                               tm=n_bs * t_s, keep=t_s)
    sre_s, sim_s = _state_from_octets(sfin_s)

    heads = lambda a, nb, t: a.reshape(1, nb, t, ATT_HEADS, ATT_HEAD_DIM)
    memh = lambda a: a.reshape(1, n_bp, N_MEM, MEM_HEADS, MEM_HEAD_DIM)
    return (y_p.reshape(n_bp, t_p, d), y_s.reshape(n_bs, t_s, d),
            heads(kv_p[0], n_bp, keep), heads(kv_p[1], n_bp, keep),
            memh(mk_p), memh(mv_p), sre_p[None], sim_p[None],
            heads(kv_s[0], n_bs, t_s), heads(kv_s[1], n_bs, t_s),
            sre_s[None], sim_s[None])
```

```python
import functools
import math

import numpy as np
import jax
import jax.numpy as jnp
from jax import lax
from jax.experimental import pallas as pl
from jax.experimental.pallas import tpu as pltpu

F32 = jnp.float32
BF16 = jnp.bfloat16

D_MODEL = 2048
CHUNK = 64
BAND_PAST_CHUNKS = 8
BAND_PAST = BAND_PAST_CHUNKS * CHUNK
ATT_HEADS = 16
ATT_HEAD_DIM = 64
ATT_WIDTH = ATT_HEADS * ATT_HEAD_DIM
REL_CLIP = 128
SSM_GROUP = 16
SSM_WIDTH = 1024
SSM_GROUPS = SSM_WIDTH // SSM_GROUP
SSM_STATE = 64
N_MEM = 256
MEM_HEADS = 4
MEM_HEAD_DIM = 256
MEM_WIDTH = MEM_HEADS * MEM_HEAD_DIM
EPS = 1e-6
NEG_INF = -1e30
PAST_LEN = 4096
LOG2E = math.log2(math.e)

COL_Q = SSM_WIDTH
COL_GATE = SSM_WIDTH + 3 * ATT_WIDTH + MEM_WIDTH

SSM_CHUNK = 16
OCTETS = 8
OCT_GROUPS = SSM_GROUPS // OCTETS
LANES = 128
MXU_DIM = 256
OCT_COLS = SSM_CHUNK * LANES
OCT_HALF = OCT_GROUPS * SSM_STATE
OCT_STATE = 2 * OCT_HALF

Q_TILE = 512
Q_SUB = 128
K_WIN = Q_SUB + BAND_PAST

MIB = 1024 * 1024
VMEM_BIG_MIB = 56
VMEM_MID_MIB = 48
VMEM_SMALL_MIB = 32


def _params(sem, vmem_mib):
    return pltpu.CompilerParams(dimension_semantics=sem, vmem_limit_bytes=vmem_mib * MIB)


def _dot(a, b):
    return jnp.dot(a, b, preferred_element_type=F32)


def _dot_nt(a, b):
    return lax.dot_general(a, b, (((1,), (1,)), ((), ())), preferred_element_type=F32)


def _rms(xf, g):
    y = xf * lax.rsqrt(jnp.mean(xf * xf, axis=-1, keepdims=True) + EPS)
    return y * g


CAST_BLOCK_BYTES = 8 * MIB


def _cast_kernel(x_ref, o_ref):
    o_ref[...] = x_ref[...].astype(o_ref.dtype)


def _cast_bf16(w, col0=0, n_cols=None):
    rows, cols = w.shape
    n_cols = cols - col0 if n_cols is None else n_cols
    cw = math.gcd(col0, n_cols)
    assert cw % LANES == 0 and col0 + n_cols <= cols
    tr = rows
    while tr * cw * 4 > CAST_BLOCK_BYTES and tr % 32 == 0:
        tr //= 2
    cb0 = col0 // cw
    return pl.pallas_call(
        _cast_kernel,
        grid=(rows // tr, n_cols // cw),
        in_specs=[pl.BlockSpec((tr, cw), lambda i, j: (i, cb0 + j))],
        out_specs=pl.BlockSpec((tr, cw), lambda i, j: (i, j)),
        out_shape=jax.ShapeDtypeStruct((rows, n_cols), BF16),
        compiler_params=_params(("parallel", "parallel"), VMEM_SMALL_MIB),
        name="cast_bf16",
    )(w)


FFN_SLICES = 8


def _ffn_kernel(hp_ref, hn_ref, gpre_ref, gpost_ref, wg_ref, wu_ref, wd_ref, *rest,
                n_tiles, emit_next):
    if emit_next:
        gnext_ref, out_ref, nxt_ref, *scratch = rest
    else:
        out_ref, *scratch = rest
    xn_refs, acc_refs = scratch[:2], scratch[2:]
    r = pl.program_id(0)
    f = pl.program_id(1)
    rs = hp_ref.shape[0]
    rows = pl.ds(pl.multiple_of(jnp.minimum(f, FFN_SLICES - 1) * rs, rs), rs)

    def pre_norm_slice(slot):
        xn_refs[slot][rows, :] = _rms(hn_ref[...], gpre_ref[...]).astype(BF16)

    def finish_slice(slot):
        hn = hp_ref[...] + 0.5 * _rms(acc_refs[slot][rows, :], gpost_ref[...])
        out_ref[...] = hn
        if emit_next:
            nxt_ref[...] = _rms(hn, gnext_ref[...]).astype(BF16)

    def matmul_chunk(slot):
        xn = xn_refs[slot][...]
        g = _dot(xn, wg_ref[...])
        u = _dot(xn, wu_ref[...])
        a = (g * jax.nn.sigmoid(g)) * u
        d = _dot(a.astype(BF16), wd_ref[...])
        acc_refs[slot][...] = jnp.where(f == 0, d, acc_refs[slot][...] + d)

    @pl.when((r == 0) & (f == 0))
    def _():
        acc_refs[1][...] = jnp.zeros_like(acc_refs[1])

    @pl.when(r == 0)
    def _():
        pre_norm_slice(0)

    for parity in range(2):
        @pl.when((r >= 1) & (r <= n_tiles) & (lax.rem(r, 2) == parity))
        def _():
            finish_slice(parity)
            pre_norm_slice(parity)
            matmul_chunk(1 - parity)

    @pl.when(r == n_tiles + 1)
    def _():
        finish_slice((n_tiles + 1) % 2)


def _ffn(h, g_pre, g_post, wg, wu, wd, g_next, tm, tf):
    emit_next = g_next is not None
    m, d = h.shape
    f_dim = wg.shape[1]
    nf = f_dim // tf
    n = m // tm
    rs = tm // FFN_SLICES
    assert m % tm == 0 and f_dim % tf == 0 and tm % FFN_SLICES == 0 and nf >= FFN_SLICES
    sl = lambda f: jnp.minimum(f, FFN_SLICES - 1)
    done = lambda r, f: (jnp.maximum(r - 2, 0) * FFN_SLICES + jnp.where(r >= 2, sl(f), 0), 0)
    ahead = lambda r, f: (jnp.minimum(r, n - 1) * FFN_SLICES + sl(f), 0)
    chunk = lambda r, f: jnp.where((r >= 1) & (r <= n), f, 0)
    vec = pl.BlockSpec((1, d), lambda r, f: (0, 0))
    in_specs = [pl.BlockSpec((rs, d), done), pl.BlockSpec((rs, d), ahead), vec, vec,
                pl.BlockSpec((d, tf), lambda r, f: (0, chunk(r, f))),
                pl.BlockSpec((d, tf), lambda r, f: (0, chunk(r, f))),
                pl.BlockSpec((tf, d), lambda r, f: (chunk(r, f), 0))]
    args = [h, h, g_pre, g_post, wg, wu, wd]
    out_shape = [jax.ShapeDtypeStruct((m, d), F32)]
    out_specs = [pl.BlockSpec((rs, d), done)]
    if emit_next:
        in_specs.append(vec)
        args.append(g_next)
        out_shape.append(jax.ShapeDtypeStruct((m, d), BF16))
        out_specs.append(pl.BlockSpec((rs, d), done))
    res = pl.pallas_call(
        functools.partial(_ffn_kernel, n_tiles=n, emit_next=emit_next),
        grid=(n + 2, nf),
        in_specs=in_specs,
        out_specs=out_specs,
        out_shape=out_shape,
        scratch_shapes=[pltpu.VMEM((tm, d), BF16), pltpu.VMEM((tm, d), BF16),
                        pltpu.VMEM((tm, d), F32), pltpu.VMEM((tm, d), F32)],
        compiler_params=_params(("arbitrary", "arbitrary"), VMEM_BIG_MIB),
        name="ffn",
    )(*args)
    return res if emit_next else res[0]


def _memory_kv_kernel(x_ref, g_ref, w_ref, k_ref, v_ref):
    x = _rms(x_ref[...], g_ref[...]).astype(BF16)
    k_ref[...] = _dot(x, w_ref[:, :MEM_WIDTH])
    v_ref[...] = _dot(x, w_ref[:, MEM_WIDTH:])


def _memory_kv(x, gain, w, tm):
    m, k = x.shape
    tm = min(tm, m)
    assert m % tm == 0 and w.shape == (k, 2 * MEM_WIDTH)
    out = pl.BlockSpec((tm, MEM_WIDTH), lambda i: (i, 0))
    return pl.pallas_call(
        _memory_kv_kernel,
        grid=(m // tm,),
        in_specs=[pl.BlockSpec((tm, k), lambda i: (i, 0)), pl.BlockSpec((1, k), lambda i: (0, 0)),
                  pl.BlockSpec((k, 2 * MEM_WIDTH), lambda i: (0, 0))],
        out_specs=[out, out],
        out_shape=[jax.ShapeDtypeStruct((m, MEM_WIDTH), F32)] * 2,
        compiler_params=_params(("parallel",), VMEM_MID_MIB),
        name="memory_kv",
    )(x, gain, w)


def _in_proj_kernel(x_ref, wq_ref, ws_ref, o_ref, k_ref, v_ref, us_ref, nat_ref):
    x = x_ref[...]
    for j, keep_ref in enumerate((None, k_ref, v_ref, None)):
        cs = slice(j * ATT_WIDTH, (j + 1) * ATT_WIDTH)
        r = _dot(x, wq_ref[:, cs])
        o_ref[:, cs] = r.astype(o_ref.dtype)
        if keep_ref is not None:
            keep_ref[...] = r
    r = _dot(x, ws_ref[...])
    tc = nat_ref.shape[1] // SSM_CHUNK
    for m in range(OCTETS):
        nat_ref[m] = r[:, m * LANES:(m + 1) * LANES]
    for j in range(SSM_CHUNK):
        for m in range(OCTETS):
            us_ref[m, :, j * LANES:(j + 1) * LANES] = nat_ref[m, pl.ds(j, tc, stride=SSM_CHUNK), :]


def _in_proj(u, w_qkvm, w_ssm, tm, seq, keep):
    n_tok = u.shape[0]
    assert n_tok % tm == 0 and tm % SSM_CHUNK == 0 and MEM_WIDTH == ATT_WIDTH
    tc = tm // SSM_CHUNK
    if keep == seq:
        kept_rows, kept = n_tok, pl.BlockSpec((tm, ATT_WIDTH), lambda i: (i, 0))
    else:
        assert keep == tm and seq % tm == 0
        kept_rows, kept = (n_tok // seq) * keep, pl.BlockSpec((tm, ATT_WIDTH), lambda i: (i // (seq // tm), 0))
    return pl.pallas_call(
        _in_proj_kernel,
        grid=(n_tok // tm,),
        in_specs=[pl.BlockSpec((tm, D_MODEL), lambda i: (i, 0)),
                  pl.BlockSpec((D_MODEL, 4 * ATT_WIDTH), lambda i: (0, 0)),
                  pl.BlockSpec((D_MODEL, SSM_WIDTH), lambda i: (0, 0))],
        out_specs=[pl.BlockSpec((tm, 4 * ATT_WIDTH), lambda i: (i, 0)), kept, kept,
                   pl.BlockSpec((OCTETS, tc, OCT_COLS), lambda i: (0, i, 0))],
        out_shape=[jax.ShapeDtypeStruct((n_tok, 4 * ATT_WIDTH), BF16),
                   jax.ShapeDtypeStruct((kept_rows, ATT_WIDTH), F32),
                   jax.ShapeDtypeStruct((kept_rows, ATT_WIDTH), F32),
                   jax.ShapeDtypeStruct((OCTETS, n_tok // SSM_CHUNK, OCT_COLS), F32)],
        scratch_shapes=[pltpu.VMEM((OCTETS, tm, LANES), F32)],
        compiler_params=_params(("arbitrary",), VMEM_BIG_MIB),
        name="in_proj",
    )(u, w_qkvm, w_ssm)


def _ssm_params(a_re, a_im, log_dt, b_re, b_im, c_re, c_im):
    dt = jnp.exp(log_dt)[:, None]
    mag = jnp.exp(a_re * dt)
    ab_re = mag * jnp.cos(a_im * dt)
    ab_im = mag * jnp.sin(a_im * dt)
    den = a_re * a_re + a_im * a_im
    n_re = ab_re - 1.0
    n_im = ab_im
    k_re = (n_re * a_re + n_im * a_im) / den
    k_im = (n_im * a_re - n_re * a_im) / den
    bb_re = k_re[..., None] * b_re - k_im[..., None] * b_im
    bb_im = k_re[..., None] * b_im + k_im[..., None] * b_re
    pr = [jnp.ones_like(ab_re)]
    pi = [jnp.zeros_like(ab_re)]
    for _ in range(SSM_CHUNK):
        pr.append(pr[-1] * ab_re - pi[-1] * ab_im)
        pi.append(pr[-2] * ab_im + pi[-1] * ab_re)
    n_pw = SSM_CHUNK + 1
    pw = jnp.concatenate([jnp.stack(pr).reshape(n_pw, OCTETS, OCT_HALF),
                          jnp.stack(pi).reshape(n_pw, OCTETS, OCT_HALF)], axis=2)
    pw = jnp.transpose(pw, (1, 0, 2))
    eye = jnp.eye(OCT_GROUPS, dtype=F32)

    def expand(x):
        x4 = x.reshape(OCTETS, OCT_GROUPS, x.shape[1], SSM_STATE)
        out = eye[None, :, None, :, None] * x4[:, :, :, None, :]
        return out.reshape(OCTETS, OCT_GROUPS * x.shape[1], OCT_HALF)

    bd = jnp.concatenate([expand(jnp.transpose(bb_re, (0, 2, 1))),
                          expand(jnp.transpose(bb_im, (0, 2, 1)))], axis=2)
    cd = jnp.concatenate([expand(c_re), expand(c_im)], axis=2)
    return bd, cd, pw


def _state_to_octets(s_re, s_im):
    b = s_re.shape[0]
    s = jnp.stack([s_re, s_im], 0).reshape(2, b, OCTETS, OCT_GROUPS, SSM_STATE)
    return jnp.transpose(s, (2, 1, 0, 3, 4)).reshape(OCTETS, b, OCT_STATE)


def _state_from_octets(s):
    b = s.shape[1]
    s = s.reshape(OCTETS, b, 2, OCT_GROUPS, SSM_STATE)
    s = jnp.transpose(s, (2, 1, 0, 3, 4)).reshape(2, b, SSM_GROUPS, SSM_STATE)
    return s[0], s[1]


def _split_bf16(x):
    hi = x.astype(BF16)
    return hi, (x - hi.astype(F32)).astype(BF16)


def _ssm_build_operators(bd_ref, cd_ref, pw_ref, t8_ref, wus_ref, wso_ref):
    bd_re, bd_im = bd_ref[0, :, :OCT_HALF], bd_ref[0, :, OCT_HALF:]
    cd_re, cd_im = cd_ref[0, :, :OCT_HALF], cd_ref[0, :, OCT_HALF:]
    c_hi, c_lo = _split_bf16(jnp.concatenate([cd_re, -cd_im], axis=1))
    blk = lambda i: slice(i * LANES, (i + 1) * LANES)
    for a in range(SSM_CHUNK // 2):
        t8_ref[blk(2 * a + 1), blk(2 * a)] = jnp.zeros((LANES, LANES), BF16)
    for k in range(SSM_CHUNK + 1):
        p_re = pw_ref[0, k:k + 1, :OCT_HALF]
        p_im = pw_ref[0, k:k + 1, OCT_HALF:]
        if k < SSM_CHUNK:
            e = jnp.concatenate([bd_re * p_re - bd_im * p_im, bd_re * p_im + bd_im * p_re], axis=1)
            wus_ref[blk(SSM_CHUNK - 1 - k), :] = e.astype(BF16)
            e_hi, e_lo = _split_bf16(e)
            lag = (_dot_nt(e_hi, c_hi) + _dot_nt(e_hi, c_lo) + _dot_nt(e_lo, c_hi)).astype(BF16)
            for j in range(SSM_CHUNK - k):
                t8_ref[blk(j), blk(j + k)] = lag
        if k >= 1:
            g = jnp.concatenate([cd_re * p_re - cd_im * p_im, -(cd_re * p_im + cd_im * p_re)], axis=1)
            wso_ref[blk(k - 1), :] = g.astype(BF16)


def _ssm_core_kernel(us_ref, s0_ref, bd_ref, cd_ref, pw_ref, d_ref, y_ref, sfin_ref,
                     t8_ref, wus_ref, wso_ref, ds_ref, sp_ref, *, nb, n_chunks):
    @pl.when(pl.program_id(1) == 0)
    def _():
        _ssm_build_operators(bd_ref, cd_ref, pw_ref, t8_ref, wus_ref, wso_ref)

    ub = us_ref[0].astype(BF16)
    ds = _dot(ub, wus_ref[...])
    a16 = pw_ref[0, SSM_CHUNK:SSM_CHUNK + 1, :]
    n_blk = OCT_HALF // LANES
    blk = lambda k: slice(k * LANES, (k + 1) * LANES)

    def advance(s_re, s_im, d_re, d_im, a_re, a_im):
        return a_re * s_re - a_im * s_im + d_re, a_re * s_im + a_im * s_re + d_im

    if n_chunks == 1:
        s0 = s0_ref[0, 0]
        n_re, n_im = advance(s0[:, :OCT_HALF], s0[:, OCT_HALF:], ds[:, :OCT_HALF], ds[:, OCT_HALF:],
                             a16[:, :OCT_HALF], a16[:, OCT_HALF:])
        sfin_ref[0, 0] = jnp.concatenate([n_re, n_im], axis=1)
        spb = s0.astype(BF16)
    else:
        for k in range(2 * n_blk):
            ds_ref[k] = ds[:, blk(k)]

        def body(c, carry):
            at = pl.ds(c, nb, stride=n_chunks)
            new = list(carry)
            for k in range(n_blk):
                sp_ref[k, at, :] = carry[k]
                sp_ref[n_blk + k, at, :] = carry[n_blk + k]
                new[k], new[n_blk + k] = advance(carry[k], carry[n_blk + k], ds_ref[k, at, :],
                                                 ds_ref[n_blk + k, at, :], a16[:, blk(k)],
                                                 a16[:, blk(n_blk + k)])
            return tuple(new)

        init = tuple(s0_ref[0, 0, :, blk(k)] for k in range(2 * n_blk))
        fin = lax.fori_loop(0, n_chunks, body, init, unroll=2)
        for k in range(2 * n_blk):
            sfin_ref[0, 0, :, blk(k)] = fin[k]
        spb = jnp.concatenate([sp_ref[k] for k in range(2 * n_blk)], axis=1).astype(BF16)
    rows = ub.shape[0]
    d_skip = d_ref[0]
    for nt in range(OCT_COLS // MXU_DIM):
        k_hi = (nt + 1) * MXU_DIM
        cs = slice(nt * MXU_DIM, k_hi)
        y = _dot(ub[:, :k_hi], t8_ref[:k_hi, cs]) + _dot_nt(spb, wso_ref[cs, :])
        for j in range(MXU_DIM // LANES):
            i = nt * (MXU_DIM // LANES) + j
            y_ref[pl.ds(i, rows, stride=SSM_CHUNK), :] = (y[:, blk(j)] + d_skip * us_ref[0, :, blk(i)])


def _ssm_core(us_r, s0, bd, cd, pw, d_oct, n_batch, n_chunks, nb):
    nc = us_r.shape[1]
    rows = nb * n_chunks
    assert n_batch % nb == 0 and nc == n_batch * n_chunks
    nr = n_batch // nb
    s0 = s0.reshape(OCTETS, nr, nb, OCT_STATE)
    par = lambda rows_: pl.BlockSpec((1, rows_, OCT_STATE), lambda m, r: (m, 0, 0))
    y, sfin = pl.pallas_call(
        functools.partial(_ssm_core_kernel, nb=nb, n_chunks=n_chunks),
        grid=(OCTETS, nr),
        in_specs=[pl.BlockSpec((1, rows, OCT_COLS), lambda m, r: (m, r, 0)),
                  pl.BlockSpec((1, 1, nb, OCT_STATE), lambda m, r: (m, r, 0, 0)),
                  par(LANES), par(LANES), par(SSM_CHUNK + 1),
                  pl.BlockSpec((1, 1, LANES), lambda m, r: (m, 0, 0))],
        out_specs=[pl.BlockSpec((rows * SSM_CHUNK, LANES), lambda m, r: (r, m)),
                   pl.BlockSpec((1, 1, nb, OCT_STATE), lambda m, r: (m, r, 0, 0))],
        out_shape=[jax.ShapeDtypeStruct((nc * SSM_CHUNK, SSM_WIDTH), F32),
                   jax.ShapeDtypeStruct((OCTETS, nr, nb, OCT_STATE), F32)],
        scratch_shapes=[pltpu.VMEM((OCT_COLS, OCT_COLS), BF16),
                        pltpu.VMEM((OCT_COLS, OCT_STATE), BF16),
                        pltpu.VMEM((OCT_COLS, OCT_STATE), BF16),
                        pltpu.VMEM((OCT_STATE // LANES, rows, LANES), F32),
                        pltpu.VMEM((OCT_STATE // LANES, rows, LANES), F32)],
        compiler_params=_params(("arbitrary", "arbitrary"), VMEM_BIG_MIB),
        name="ssm_core",
    )(us_r, s0, bd, cd, pw, d_oct)
    return y, sfin.reshape(OCTETS, n_batch, OCT_STATE)


def _gelu_tanh(x):
    c = math.sqrt(2.0 / math.pi)
    return x * (0.5 * (1.0 + jnp.tanh(c * (x + 0.044715 * (x * x * x)))))


def _ssm_glu_kernel(y_ref, w_ref, b_ref, o_ref):
    yg = _gelu_tanh(y_ref[...])
    z = _dot(yg.astype(BF16), w_ref[...]) + b_ref[...]
    o_ref[...] = (yg * jax.nn.sigmoid(z)).astype(o_ref.dtype)


def _ssm_glu(y, w_glu, b_glu, tm):
    n_tok = y.shape[0]
    assert n_tok % tm == 0
    row = pl.BlockSpec((tm, SSM_WIDTH), lambda i: (i, 0))
    return pl.pallas_call(
        _ssm_glu_kernel,
        grid=(n_tok // tm,),
        in_specs=[row, pl.BlockSpec((SSM_WIDTH, SSM_WIDTH), lambda i: (0, 0)),
                  pl.BlockSpec((1, SSM_WIDTH), lambda i: (0, 0))],
        out_specs=row,
        out_shape=jax.ShapeDtypeStruct((n_tok, SSM_WIDTH), BF16),
        compiler_params=_params(("parallel",), VMEM_MID_MIB),
        name="ssm_glu",
    )(y, w_glu, b_glu)


N_PAIRS = ATT_HEADS // 2


def _head_masks():
    lane = lax.broadcasted_iota(jnp.int32, (1, LANES), 1)
    return (lane < ATT_HEAD_DIM, lane >= ATT_HEAD_DIM)


def _pair_scores(q2, kw, bias2, masks):
    qq = jnp.concatenate([jnp.where(m, q2, jnp.zeros_like(q2)) for m in masks], axis=0)
    return _dot_nt(qq, kw) * (ATT_HEAD_DIM ** -0.5 * LOG2E) + bias2


def _row_max(sc):
    return jnp.broadcast_to(jnp.max(sc, axis=1, keepdims=True), (sc.shape[0], LANES))


def _probabilities(sc, mx):
    return jnp.exp2(sc - jnp.concatenate([mx] * (sc.shape[1] // LANES), axis=1)).astype(BF16)


def _values_with_ones(v, first):
    lane = lax.broadcasted_iota(jnp.int32, (1, v.shape[1]), 1)
    in_first = lane % LANES < ATT_HEAD_DIM
    return jnp.where(in_first if first else jnp.logical_not(in_first), v, jnp.ones_like(v))


def _pair_output(p, va, vb, masks):
    n_q = p.shape[0] // 2
    oa = _dot(p[:n_q], va)
    ob = _dot(p[n_q:], vb)
    num = jnp.where(masks[0], oa, ob)
    den = jnp.where(masks[0], pltpu.roll(oa, ATT_HEAD_DIM, 1), pltpu.roll(ob, ATT_HEAD_DIM, 1))
    return num * (1.0 / den)


def _band_attn_kernel(q_ref, kp_ref, kc_ref, vp_ref, vc_ref, bias_ref, o_ref,
                      kw_ref, va_ref, vb_ref, sc_ref, mx_ref, p_ref):
    kw_ref[0:BAND_PAST] = kp_ref[...]
    kw_ref[BAND_PAST:BAND_PAST + Q_TILE] = kc_ref[...]
    for v_ref, first in ((va_ref, True), (vb_ref, False)):
        v_ref[0:BAND_PAST] = _values_with_ones(vp_ref[...], first)
        v_ref[BAND_PAST:BAND_PAST + Q_TILE] = _values_with_ones(vc_ref[...], first)
    masks = _head_masks()
    cs = lambda hp: slice(hp * LANES, (hp + 1) * LANES)

    def sub(s, carry):
        r0 = pl.multiple_of(s * Q_SUB, Q_SUB)
        rows = pl.ds(r0, Q_SUB)
        win = pl.ds(r0, K_WIN)
        for hp in range(N_PAIRS):
            sc = _pair_scores(q_ref[rows, cs(hp)], kw_ref[win, cs(hp)], bias_ref[hp], masks)
            sc_ref[hp] = sc
            mx_ref[hp] = _row_max(sc)

        @pl.when(pl.program_id(1) == 0)
        def _():
            col = lax.broadcasted_iota(jnp.int32, (1, K_WIN), 1)
            extra = jnp.where(col + r0 < BAND_PAST, NEG_INF, 0.0)
            for hp in range(N_PAIRS):
                sc = sc_ref[hp] + extra
                sc_ref[hp] = sc
                mx_ref[hp] = _row_max(sc)

        for hp in range(N_PAIRS):
            p_ref[hp] = _probabilities(sc_ref[hp], mx_ref[hp])
        for hp in range(N_PAIRS):
            o = _pair_output(p_ref[hp], va_ref[win, cs(hp)], vb_ref[win, cs(hp)], masks)
            o_ref[rows, cs(hp)] = o.astype(o_ref.dtype)
        return carry

    lax.fori_loop(0, Q_TILE // Q_SUB, sub, 0)


def _rel_bias_tile(rel_bias, n_q, n_k, offset, ok):
    r = np.arange(n_q - 1 + n_k)
    idx = np.clip(offset + n_q - 1 - r, -REL_CLIP, REL_CLIP) + REL_CLIP
    v = jnp.pad(rel_bias[:, idx] * LOG2E, ((0, 0), (0, 1)))
    w = n_q + n_k
    flat = jnp.tile(v, (1, n_q))[:, :n_q * (w - 1)]
    toep = flat.reshape(rel_bias.shape[0], n_q, w - 1)[:, :, n_q - 1:]
    return jnp.where(ok[None], toep, NEG_INF).reshape(N_PAIRS, 2 * n_q, n_k)


def _band_attention_prompt(qkvm, rel_bias, n_batch, seq):
    assert seq % Q_TILE == 0 and Q_TILE % BAND_PAST == 0
    nt = seq // Q_TILE
    qc = np.arange(Q_SUB)[:, None] // CHUNK
    kc = np.arange(K_WIN)[None, :] // CHUNK
    bias = _rel_bias_tile(rel_bias, Q_SUB, K_WIN, BAND_PAST, (kc >= qc) & (kc <= qc + BAND_PAST_CHUNKS))
    blk = (Q_TILE, ATT_WIDTH)
    ratio = Q_TILE // BAND_PAST
    cur = lambda col: pl.BlockSpec(blk, lambda b, t: (b * nt + t, col))
    prev = lambda col: pl.BlockSpec(
        (BAND_PAST, ATT_WIDTH), lambda b, t: (jnp.maximum((b * nt + t) * ratio - 1, b * nt * ratio), col))
    return pl.pallas_call(
        _band_attn_kernel,
        grid=(n_batch, nt),
        in_specs=[cur(0), prev(1), cur(1), prev(2), cur(2),
                  pl.BlockSpec((N_PAIRS, 2 * Q_SUB, K_WIN), lambda b, t: (0, 0, 0))],
        out_specs=pl.BlockSpec(blk, lambda b, t: (b * nt + t, 0)),
        out_shape=jax.ShapeDtypeStruct((n_batch * seq, ATT_WIDTH), BF16),
        scratch_shapes=[pltpu.VMEM((BAND_PAST + Q_TILE, ATT_WIDTH), BF16),
                        pltpu.VMEM((BAND_PAST + Q_TILE, ATT_WIDTH), BF16),
                        pltpu.VMEM((BAND_PAST + Q_TILE, ATT_WIDTH), BF16),
                        pltpu.VMEM((N_PAIRS, 2 * Q_SUB, K_WIN), F32),
                        pltpu.VMEM((N_PAIRS, 2 * Q_SUB, LANES), F32),
                        pltpu.VMEM((N_PAIRS, 2 * Q_SUB, K_WIN), BF16)],
        compiler_params=_params(("parallel", "arbitrary"), VMEM_MID_MIB),
        name="band_attn",
    )(qkvm, qkvm, qkvm, qkvm, qkvm, bias)


def _band_attn_sample_kernel(q_ref, kn_ref, vn_ref, ck_ref, cv_ref, bias_ref, o_ref, kw_ref, vw_ref,
                             *, n_cache, n_new):
    kw_ref[...] = jnp.zeros_like(kw_ref)
    vw_ref[...] = jnp.zeros_like(vw_ref)
    kw_ref[0:n_cache] = ck_ref[0].astype(BF16)
    vw_ref[0:n_cache] = cv_ref[0].astype(BF16)
    kw_ref[n_cache:n_cache + n_new] = kn_ref[...]
    vw_ref[n_cache:n_cache + n_new] = vn_ref[...]
    masks = _head_masks()
    for hp in range(N_PAIRS):
        cs = slice(hp * LANES, (hp + 1) * LANES)
        sc = _pair_scores(q_ref[:, cs], kw_ref[:, cs], bias_ref[hp], masks)
        v = vw_ref[:, cs]
        o = _pair_output(_probabilities(sc, _row_max(sc)), _values_with_ones(v, True),
                         _values_with_ones(v, False), masks)
        o_ref[:, cs] = o.astype(o_ref.dtype)


def _band_attention_sample(qkvm, rel_bias, cache_k, cache_v, n_batch, n_new):
    n_cache = cache_k.shape[1]
    n_keys = -(-(n_cache + n_new) // LANES) * LANES
    q_pos = PAST_LEN + np.arange(n_new)[:, None]
    j = np.arange(n_keys)[None, :]
    k_pos = PAST_LEN - n_cache + j
    ok = ((j < n_cache + n_new) & (k_pos >= 0) & (k_pos // CHUNK <= q_pos // CHUNK)
          & (k_pos // CHUNK >= q_pos // CHUNK - BAND_PAST_CHUNKS))
    bias = _rel_bias_tile(rel_bias, n_new, n_keys, n_cache, ok)
    new = lambda col: pl.BlockSpec((n_new, ATT_WIDTH), lambda b: (b, col))
    cache = pl.BlockSpec((1, n_cache, ATT_WIDTH), lambda b: (b, 0, 0))
    return pl.pallas_call(
        functools.partial(_band_attn_sample_kernel, n_cache=n_cache, n_new=n_new),
        grid=(n_batch,),
        in_specs=[new(0), new(1), new(2), cache, cache,
                  pl.BlockSpec((N_PAIRS, 2 * n_new, n_keys), lambda b: (0, 0, 0))],
        out_specs=pl.BlockSpec((n_new, ATT_WIDTH), lambda b: (b, 0)),
        out_shape=jax.ShapeDtypeStruct((n_batch * n_new, ATT_WIDTH), BF16),
        scratch_shapes=[pltpu.VMEM((n_keys, ATT_WIDTH), BF16),
                        pltpu.VMEM((n_keys, ATT_WIDTH), BF16)],
        compiler_params=_params(("parallel",), VMEM_MID_MIB),
        name="band_attn_sample",
    )(qkvm, qkvm, qkvm, cache_k, cache_v, bias)


def _mem_attn_kernel(q_ref, k_ref, v_ref, o_ref):
    k = k_ref[0].astype(BF16)
    v = v_ref[0].astype(BF16)
    for h in range(MEM_HEADS):
        cs = slice(h * MEM_HEAD_DIM, (h + 1) * MEM_HEAD_DIM)
        sc = _dot_nt(q_ref[:, cs], k[:, cs]) * (MEM_HEAD_DIM ** -0.5 * LOG2E)
        mx = jnp.max(sc, axis=1, keepdims=True)
        p = jnp.exp2(sc - mx)
        l = jnp.sum(p, axis=1, keepdims=True)
        o = _dot(p.astype(BF16), v[:, cs]) * (1.0 / l)
        o_ref[:, cs] = o.astype(o_ref.dtype)


def _memory_attention(qkvm, mem_k, mem_v, n_batch, seq, tq):
    nt = seq // tq
    mem = pl.BlockSpec((1, N_MEM, MEM_WIDTH), lambda b, t: (b, 0, 0))
    return pl.pallas_call(
        _mem_attn_kernel,
        grid=(n_batch, nt),
        in_specs=[pl.BlockSpec((tq, MEM_WIDTH), lambda b, t: (b * nt + t, 3)), mem, mem],
        out_specs=pl.BlockSpec((tq, MEM_WIDTH), lambda b, t: (b * nt + t, 0)),
        out_shape=jax.ShapeDtypeStruct((n_batch * seq, MEM_WIDTH), BF16),
        compiler_params=_params(("parallel", "arbitrary"), VMEM_MID_MIB),
        name="mem_attn",
    )(qkvm, mem_k, mem_v)


def _gate_merge_kernel(u_ref, os_ref, oa_ref, om_ref, wgs_ref, wga_ref, wgm_ref,
                       wbs_ref, wba_ref, wbm_ref, out_ref):
    u = u_ref[...]

    def branch(o_ref, wg_ref, wb_ref):
        return jax.nn.sigmoid(_dot(u, wg_ref[...])) * _dot(o_ref[...], wb_ref[...])

    merged = (branch(os_ref, wgs_ref, wbs_ref) + branch(oa_ref, wga_ref, wba_ref)
              + branch(om_ref, wgm_ref, wbm_ref))
    out_ref[...] = merged.astype(out_ref.dtype)


def _gate_merge(u, o_s, o_a, o_m, w_gate, wb_s, wb_a, wb_m, tm, tn):
    m, d = u.shape
    gate = lambda b: pl.BlockSpec((d, tn), lambda i, n: (0, b * (d // tn) + n))
    wb = pl.BlockSpec((SSM_WIDTH, tn), lambda i, n: (0, n))
    ob = pl.BlockSpec((tm, SSM_WIDTH), lambda i, n: (i, 0))
    return pl.pallas_call(
        _gate_merge_kernel,
        grid=(m // tm, d // tn),
        in_specs=[pl.BlockSpec((tm, d), lambda i, n: (i, 0)), ob, ob, ob,
                  gate(0), gate(1), gate(2), wb, wb, wb],
        out_specs=pl.BlockSpec((tm, tn), lambda i, n: (i, n)),
        out_shape=jax.ShapeDtypeStruct((m, d), BF16),
        compiler_params=_params(("parallel", "arbitrary"), VMEM_BIG_MIB),
        name="gate_merge",
    )(u, o_s, o_a, o_m, w_gate, w_gate, w_gate, wb_s, wb_a, wb_m)


def _out_proj_kernel(x_ref, w_ref, h_ref, g_ref, o_ref):
    o_ref[...] = h_ref[...] + _rms(_dot(x_ref[...], w_ref[...]), g_ref[...])


def _out_proj(x, w_out, h, g_post, tm):
    m, d = h.shape
    row = lambda: pl.BlockSpec((tm, d), lambda i: (i, 0))
    return pl.pallas_call(
        _out_proj_kernel,
        grid=(m // tm,),
        in_specs=[row(), pl.BlockSpec((d, d), lambda i: (0, 0)), row(),
                  pl.BlockSpec((1, d), lambda i: (0, 0))],
        out_specs=row(),
        out_shape=jax.ShapeDtypeStruct((m, d), F32),
        compiler_params=_params(("parallel",), VMEM_BIG_MIB),
        name="out_proj",
    )(x, w_out, h, g_post)


def _layer(x, w, *, n_batch, seq, s0, mem_k, mem_v, cache_k, cache_v, tm, keep):
    n_tok = n_batch * seq
    n_chunks = seq // SSM_CHUNK
    big_tm = min(1024, n_tok)
    h1, u = _ffn(x, w["ffn1_norm_pre"], w["ffn1_norm_post"], w["ffn1_w_gate"], w["ffn1_w_up"],
                 w["ffn1_w_down"], w["mix_norm_pre"], big_tm, 512)

    qkvm, k_keep, v_keep, us_r = _in_proj(u, w["w_qkvm"], w["w_ssm"], tm, seq, keep)
    kv = (k_keep, v_keep)

    nb = 2 if (n_chunks > 1 and n_batch % 2 == 0) else (1 if n_chunks > 1 else n_batch)
    y_s, s_fin = _ssm_core(us_r, s0, w["ssm_bd"], w["ssm_cd"], w["ssm_pw"], w["ssm_d"], n_batch,
                           n_chunks, nb)
    o_s = _ssm_glu(y_s, w["ssm_w_glu"], w["ssm_b_glu"], big_tm)

    if cache_k is None:
        o_a = _band_attention_prompt(qkvm, w["att_rel_bias"], n_batch, seq)
    else:
        o_a = _band_attention_sample(qkvm, w["att_rel_bias"], cache_k, cache_v, n_batch, seq)
    o_m = _memory_attention(qkvm, mem_k, mem_v, n_batch, seq, min(4096, seq))

    merged = _gate_merge(u, o_s, o_a, o_m, w["w_gate"], w["w_branch_ssm"], w["w_branch_att"],
                         w["w_branch_mem"], big_tm, 512)
    h2 = _out_proj(merged, w["w_out"], h1, w["mix_norm_post"], tm)
    y = _ffn(h2, w["ffn2_norm_pre"], w["ffn2_norm_post"], w["ffn2_w_gate"], w["ffn2_w_up"],
             w["ffn2_w_down"], None, big_tm, 512)
    return y, kv, s_fin


def kernel(x_prompt, x_sample, mem_prompt, cache_att_k, cache_att_v, cache_mem_k, cache_mem_v, state_ssm_re, state_ssm_im, ffn1_norm_pre, ffn1_norm_post, ffn1_w_gate, ffn1_w_up, ffn1_w_down, mix_norm_pre, mix_norm_post, w_in, ssm_a_re, ssm_a_im, ssm_log_dt, ssm_b_re, ssm_b_im, ssm_c_re, ssm_c_im, ssm_d, ssm_w_glu, ssm_b_glu, att_rel_bias, mem_norm, w_mem_k, w_mem_v, w_branch_ssm, w_branch_att, w_branch_mem, w_out, ffn2_norm_pre, ffn2_norm_post, ffn2_w_gate, ffn2_w_up, ffn2_w_down):
    n_bp, t_p, d = x_prompt.shape
    n_bs, t_s, _ = x_sample.shape
    depth = ffn1_norm_pre.shape[0]
    assert depth == 1 and d == D_MODEL
    keep = min(BAND_PAST, t_p)
    l = 0

    vec = lambda a: a[l].reshape(1, -1).astype(F32)
    mat = lambda a: _cast_bf16(a[l])
    bd, cd, pw = _ssm_params(ssm_a_re[l], ssm_a_im[l], ssm_log_dt[l], ssm_b_re[l],
                             ssm_b_im[l], ssm_c_re[l], ssm_c_im[l])
    w = {
        "ffn1_norm_pre": vec(ffn1_norm_pre), "ffn1_norm_post": vec(ffn1_norm_post),
        "ffn1_w_gate": mat(ffn1_w_gate), "ffn1_w_up": mat(ffn1_w_up), "ffn1_w_down": mat(ffn1_w_down),
        "mix_norm_pre": vec(mix_norm_pre), "mix_norm_post": vec(mix_norm_post),
        "w_ssm": _cast_bf16(w_in[l], 0, COL_Q),
        "w_qkvm": _cast_bf16(w_in[l], COL_Q, COL_GATE - COL_Q),
        "w_gate": _cast_bf16(w_in[l], COL_GATE),
        "ssm_bd": bd, "ssm_cd": cd, "ssm_pw": pw,
        "ssm_d": ssm_d[l].reshape(OCTETS, 1, LANES).astype(F32), "ssm_w_glu": mat(ssm_w_glu), "ssm_b_glu": vec(ssm_b_glu),
        "att_rel_bias": att_rel_bias[l].astype(F32),
        "w_branch_ssm": mat(w_branch_ssm), "w_branch_att": mat(w_branch_att),
        "w_branch_mem": mat(w_branch_mem), "w_out": mat(w_out),
        "ffn2_norm_pre": vec(ffn2_norm_pre), "ffn2_norm_post": vec(ffn2_norm_post),
        "ffn2_w_gate": mat(ffn2_w_gate), "ffn2_w_up": mat(ffn2_w_up), "ffn2_w_down": mat(ffn2_w_down),
    }

    w_mem = jnp.concatenate([w_mem_k[l], w_mem_v[l]], axis=1).astype(BF16)
    mk_p, mv_p = _memory_kv(mem_prompt.reshape(n_bp * N_MEM, d), vec(mem_norm), w_mem, 512)
    mk_p = mk_p.reshape(n_bp, N_MEM, MEM_WIDTH)
    mv_p = mv_p.reshape(n_bp, N_MEM, MEM_WIDTH)
    zero_state = jnp.zeros((OCTETS, n_bp, OCT_STATE), F32)
    y_p, kv_p, sfin_p = _layer(x_prompt.reshape(n_bp * t_p, d), w, n_batch=n_bp, seq=t_p, s0=zero_state,
                               mem_k=mk_p, mem_v=mv_p, cache_k=None, cache_v=None, tm=512, keep=keep)
    sre_p, sim_p = _state_from_octets(sfin_p)

    n_cache = cache_att_k.shape[2]
    s0_s = _state_to_octets(state_ssm_re[l].astype(F32), state_ssm_im[l].astype(F32))
    y_s, kv_s, sfin_s = _layer(x_sample.reshape(n_bs * t_s, d), w, n_batch=n_bs, seq=t_s, s0=s0_s,
                               mem_k=cache_mem_k[l].reshape(n_bs, N_MEM, MEM_WIDTH),
                               mem_v=cache_mem_v[l].reshape(n_bs, N_MEM, MEM_WIDTH),
                               cache_k=cache_att_k[l].reshape(n_bs, n_cache, ATT_WIDTH),
                               cache_v=cache_att_v[l].reshape(n_bs, n_cache, ATT_WIDTH),
                               tm=n_bs * t_s, keep=t_s)
    sre_s, sim_s = _state_from_octets(sfin_s)

    heads = lambda a, nb, t: a.reshape(1, nb, t, ATT_HEADS, ATT_HEAD_DIM)
    memh = lambda a: a.reshape(1, n_bp, N_MEM, MEM_HEADS, MEM_HEAD_DIM)
    return (y_p.reshape(n_bp, t_p, d), y_s.reshape(n_bs, t_s, d),
            heads(kv_p[0], n_bp, keep), heads(kv_p[1], n_bp, keep),
            memh(mk_p), memh(mv_p), sre_p[None], sim_p[None],
            heads(kv_s[0], n_bs, t_s), heads(kv_s[1], n_bs, t_s),
            sre_s[None], sim_s[None])
```

```python
import functools
import math

import numpy as np
import jax
import jax.numpy as jnp
from jax import lax
from jax.experimental import pallas as pl
from jax.experimental.pallas import tpu as pltpu

F32 = jnp.float32
BF16 = jnp.bfloat16

D_MODEL = 2048
CHUNK = 64
BAND_PAST_CHUNKS = 8
BAND_PAST = BAND_PAST_CHUNKS * CHUNK
ATT_HEADS = 16
ATT_HEAD_DIM = 64
ATT_WIDTH = ATT_HEADS * ATT_HEAD_DIM
REL_CLIP = 128
SSM_GROUP = 16
SSM_WIDTH = 1024
SSM_GROUPS = SSM_WIDTH // SSM_GROUP
SSM_STATE = 64
N_MEM = 256
MEM_HEADS = 4
MEM_HEAD_DIM = 256
MEM_WIDTH = MEM_HEADS * MEM_HEAD_DIM
EPS = 1e-6
NEG_INF = -1e30
PAST_LEN = 4096
LOG2E = math.log2(math.e)

COL_Q = SSM_WIDTH
COL_GATE = SSM_WIDTH + 3 * ATT_WIDTH + MEM_WIDTH

SSM_CHUNK = 16
OCTETS = 8
OCT_GROUPS = SSM_GROUPS // OCTETS
LANES = 128
MXU_DIM = 256
OCT_COLS = SSM_CHUNK * LANES
OCT_HALF = OCT_GROUPS * SSM_STATE
OCT_STATE = 2 * OCT_HALF

Q_TILE = 512
Q_SUB = 128
K_WIN = Q_SUB + BAND_PAST

MIB = 1024 * 1024
VMEM_BIG_MIB = 56
VMEM_MID_MIB = 48
VMEM_SMALL_MIB = 32


def _params(sem, vmem_mib):
    return pltpu.CompilerParams(dimension_semantics=sem, vmem_limit_bytes=vmem_mib * MIB)


def _dot(a, b):
    return jnp.dot(a, b, preferred_element_type=F32)


def _dot_nt(a, b):
    return lax.dot_general(a, b, (((1,), (1,)), ((), ())), preferred_element_type=F32)


def _rms(xf, g):
    y = xf * lax.rsqrt(jnp.mean(xf * xf, axis=-1, keepdims=True) + EPS)
    return y * g


CAST_BLOCK_BYTES = 8 * MIB


def _cast_kernel(x_ref, o_ref):
    o_ref[...] = x_ref[...].astype(o_ref.dtype)


def _cast_bf16(w, col0=0, n_cols=None):
    rows, cols = w.shape
    n_cols = cols - col0 if n_cols is None else n_cols
    cw = math.gcd(col0, n_cols)
    assert cw % LANES == 0 and col0 + n_cols <= cols
    tr = rows
    while tr * cw * 4 > CAST_BLOCK_BYTES and tr % 32 == 0:
        tr //= 2
    cb0 = col0 // cw
    return pl.pallas_call(
        _cast_kernel,
        grid=(rows // tr, n_cols // cw),
        in_specs=[pl.BlockSpec((tr, cw), lambda i, j: (i, cb0 + j))],
        out_specs=pl.BlockSpec((tr, cw), lambda i, j: (i, j)),
        out_shape=jax.ShapeDtypeStruct((rows, n_cols), BF16),
        compiler_params=_params(("parallel", "parallel"), VMEM_SMALL_MIB),
        name="cast_bf16",
    )(w)


FFN_SLICES = 8


def _ffn_kernel(hp_ref, hn_ref, gpre_ref, gpost_ref, wg_ref, wu_ref, wd_ref, *rest,
                n_tiles, emit_next):
    if emit_next:
        gnext_ref, out_ref, nxt_ref, *scratch = rest
    else:
        out_ref, *scratch = rest
    xn_refs, acc_refs = scratch[:2], scratch[2:]
    r = pl.program_id(0)
    f = pl.program_id(1)
    rs = hp_ref.shape[0]
    rows = pl.ds(pl.multiple_of(jnp.minimum(f, FFN_SLICES - 1) * rs, rs), rs)

    def pre_norm_slice(slot):
        xn_refs[slot][rows, :] = _rms(hn_ref[...], gpre_ref[...]).astype(BF16)

    def finish_slice(slot):
        hn = hp_ref[...] + 0.5 * _rms(acc_refs[slot][rows, :], gpost_ref[...])
        out_ref[...] = hn
        if emit_next:
            nxt_ref[...] = _rms(hn, gnext_ref[...]).astype(BF16)

    def matmul_chunk(slot):
        xn = xn_refs[slot][...]
        g = _dot(xn, wg_ref[...])
        u = _dot(xn, wu_ref[...])
        a = (g * jax.nn.sigmoid(g)) * u
        d = _dot(a.astype(BF16), wd_ref[...])
        acc_refs[slot][...] = jnp.where(f == 0, d, acc_refs[slot][...] + d)

    @pl.when((r == 0) & (f == 0))
    def _():
        acc_refs[1][...] = jnp.zeros_like(acc_refs[1])

    @pl.when(r == 0)
    def _():
        pre_norm_slice(0)

    for parity in range(2):
        @pl.when((r >= 1) & (r <= n_tiles) & (lax.rem(r, 2) == parity))
        def _():
            finish_slice(parity)
            pre_norm_slice(parity)
            matmul_chunk(1 - parity)

    @pl.when(r == n_tiles + 1)
    def _():
        finish_slice((n_tiles + 1) % 2)


def _ffn(h, g_pre, g_post, wg, wu, wd, g_next, tm, tf):
    emit_next = g_next is not None
    m, d = h.shape
    f_dim = wg.shape[1]
    nf = f_dim // tf
    n = m // tm
    rs = tm // FFN_SLICES
    assert m % tm == 0 and f_dim % tf == 0 and tm % FFN_SLICES == 0 and nf >= FFN_SLICES
    sl = lambda f: jnp.minimum(f, FFN_SLICES - 1)
    done = lambda r, f: (jnp.maximum(r - 2, 0) * FFN_SLICES + jnp.where(r >= 2, sl(f), 0), 0)
    ahead = lambda r, f: (jnp.minimum(r, n - 1) * FFN_SLICES + sl(f), 0)
    chunk = lambda r, f: jnp.where((r >= 1) & (r <= n), f, 0)
    vec = pl.BlockSpec((1, d), lambda r, f: (0, 0))
    in_specs = [pl.BlockSpec((rs, d), done), pl.BlockSpec((rs, d), ahead), vec, vec,
                pl.BlockSpec((d, tf), lambda r, f: (0, chunk(r, f))),
                pl.BlockSpec((d, tf), lambda r, f: (0, chunk(r, f))),
                pl.BlockSpec((tf, d), lambda r, f: (chunk(r, f), 0))]
    args = [h, h, g_pre, g_post, wg, wu, wd]
    out_shape = [jax.ShapeDtypeStruct((m, d), F32)]
    out_specs = [pl.BlockSpec((rs, d), done)]
    if emit_next:
        in_specs.append(vec)
        args.append(g_next)
        out_shape.append(jax.ShapeDtypeStruct((m, d), BF16))
        out_specs.append(pl.BlockSpec((rs, d), done))
    res = pl.pallas_call(
        functools.partial(_ffn_kernel, n_tiles=n, emit_next=emit_next),
        grid=(n + 2, nf),
        in_specs=in_specs,
        out_specs=out_specs,
        out_shape=out_shape,
        scratch_shapes=[pltpu.VMEM((tm, d), BF16), pltpu.VMEM((tm, d), BF16),
                        pltpu.VMEM((tm, d), F32), pltpu.VMEM((tm, d), F32)],
        compiler_params=_params(("arbitrary", "arbitrary"), VMEM_BIG_MIB),
        name="ffn",
    )(*args)
    return res if emit_next else res[0]


def _memory_kv_kernel(x_ref, g_ref, w_ref, k_ref, v_ref):
    x = _rms(x_ref[...], g_ref[...]).astype(BF16)
    k_ref[...] = _dot(x, w_ref[:, :MEM_WIDTH])
    v_ref[...] = _dot(x, w_ref[:, MEM_WIDTH:])


def _memory_kv(x, gain, w, tm):
    m, k = x.shape
    tm = min(tm, m)
    assert m % tm == 0 and w.shape == (k, 2 * MEM_WIDTH)
    out = pl.BlockSpec((tm, MEM_WIDTH), lambda i: (i, 0))
    return pl.pallas_call(
        _memory_kv_kernel,
        grid=(m // tm,),
        in_specs=[pl.BlockSpec((tm, k), lambda i: (i, 0)), pl.BlockSpec((1, k), lambda i: (0, 0)),
                  pl.BlockSpec((k, 2 * MEM_WIDTH), lambda i: (0, 0))],
        out_specs=[out, out],
        out_shape=[jax.ShapeDtypeStruct((m, MEM_WIDTH), F32)] * 2,
        compiler_params=_params(("parallel",), VMEM_MID_MIB),
        name="memory_kv",
    )(x, gain, w)


def _in_proj_kernel(x_ref, wq_ref, ws_ref, o_ref, k_ref, v_ref, us_ref, nat_ref):
    x = x_ref[...]
    for j, keep_ref in enumerate((None, k_ref, v_ref, None)):
        cs = slice(j * ATT_WIDTH, (j + 1) * ATT_WIDTH)
        r = _dot(x, wq_ref[:, cs])
        o_ref[:, cs] = r.astype(o_ref.dtype)
        if keep_ref is not None:
            keep_ref[...] = r
    r = _dot(x, ws_ref[...])
    tc = nat_ref.shape[1] // SSM_CHUNK
    for m in range(OCTETS):
        nat_ref[m] = r[:, m * LANES:(m + 1) * LANES]
    for j in range(SSM_CHUNK):
        for m in range(OCTETS):
            us_ref[m, :, j * LANES:(j + 1) * LANES] = nat_ref[m, pl.ds(j, tc, stride=SSM_CHUNK), :]


def _in_proj(u, w_qkvm, w_ssm, tm, seq, keep):
    n_tok = u.shape[0]
    assert n_tok % tm == 0 and tm % SSM_CHUNK == 0 and MEM_WIDTH == ATT_WIDTH
    tc = tm // SSM_CHUNK
    if keep == seq:
        kept_rows, kept = n_tok, pl.BlockSpec((tm, ATT_WIDTH), lambda i: (i, 0))
    else:
        assert keep == tm and seq % tm == 0
        kept_rows, kept = (n_tok // seq) * keep, pl.BlockSpec((tm, ATT_WIDTH), lambda i: (i // (seq // tm), 0))
    return pl.pallas_call(
        _in_proj_kernel,
        grid=(n_tok // tm,),
        in_specs=[pl.BlockSpec((tm, D_MODEL), lambda i: (i, 0)),
                  pl.BlockSpec((D_MODEL, 4 * ATT_WIDTH), lambda i: (0, 0)),
                  pl.BlockSpec((D_MODEL, SSM_WIDTH), lambda i: (0, 0))],
        out_specs=[pl.BlockSpec((tm, 4 * ATT_WIDTH), lambda i: (i, 0)), kept, kept,
                   pl.BlockSpec((OCTETS, tc, OCT_COLS), lambda i: (0, i, 0))],
        out_shape=[jax.ShapeDtypeStruct((n_tok, 4 * ATT_WIDTH), BF16),
                   jax.ShapeDtypeStruct((kept_rows, ATT_WIDTH), F32),
                   jax.ShapeDtypeStruct((kept_rows, ATT_WIDTH), F32),
                   jax.ShapeDtypeStruct((OCTETS, n_tok // SSM_CHUNK, OCT_COLS), F32)],
        scratch_shapes=[pltpu.VMEM((OCTETS, tm, LANES), F32)],
        compiler_params=_params(("arbitrary",), VMEM_BIG_MIB),
        name="in_proj",
    )(u, w_qkvm, w_ssm)


def _ssm_params(a_re, a_im, log_dt, b_re, b_im, c_re, c_im):
    dt = jnp.exp(log_dt)[:, None]
    mag = jnp.exp(a_re * dt)
    ab_re = mag * jnp.cos(a_im * dt)
    ab_im = mag * jnp.sin(a_im * dt)
    den = a_re * a_re + a_im * a_im
    n_re = ab_re - 1.0
    n_im = ab_im
    k_re = (n_re * a_re + n_im * a_im) / den
    k_im = (n_im * a_re - n_re * a_im) / den
    bb_re = k_re[..., None] * b_re - k_im[..., None] * b_im
    bb_im = k_re[..., None] * b_im + k_im[..., None] * b_re
    pr = [jnp.ones_like(ab_re)]
    pi = [jnp.zeros_like(ab_re)]
    for _ in range(SSM_CHUNK):
        pr.append(pr[-1] * ab_re - pi[-1] * ab_im)
        pi.append(pr[-2] * ab_im + pi[-1] * ab_re)
    n_pw = SSM_CHUNK + 1
    pw = jnp.concatenate([jnp.stack(pr).reshape(n_pw, OCTETS, OCT_HALF),
                          jnp.stack(pi).reshape(n_pw, OCTETS, OCT_HALF)], axis=2)
    pw = jnp.transpose(pw, (1, 0, 2))
    eye = jnp.eye(OCT_GROUPS, dtype=F32)

    def expand(x):
        x4 = x.reshape(OCTETS, OCT_GROUPS, x.shape[1], SSM_STATE)
        out = eye[None, :, None, :, None] * x4[:, :, :, None, :]
        return out.reshape(OCTETS, OCT_GROUPS * x.shape[1], OCT_HALF)

    bd = jnp.concatenate([expand(jnp.transpose(bb_re, (0, 2, 1))),
                          expand(jnp.transpose(bb_im, (0, 2, 1)))], axis=2)
    cd = jnp.concatenate([expand(c_re), expand(c_im)], axis=2)
    return bd, cd, pw


def _state_to_octets(s_re, s_im):
    b = s_re.shape[0]
    s = jnp.stack([s_re, s_im], 0).reshape(2, b, OCTETS, OCT_GROUPS, SSM_STATE)
    return jnp.transpose(s, (2, 1, 0, 3, 4)).reshape(OCTETS, b, OCT_STATE)


def _state_from_octets(s):
    b = s.shape[1]
    s = s.reshape(OCTETS, b, 2, OCT_GROUPS, SSM_STATE)
    s = jnp.transpose(s, (2, 1, 0, 3, 4)).reshape(2, b, SSM_GROUPS, SSM_STATE)
    return s[0], s[1]


def _split_bf16(x):
    hi = x.astype(BF16)
    return hi, (x - hi.astype(F32)).astype(BF16)


def _ssm_build_operators(bd_ref, cd_ref, pw_ref, t8_ref, wus_ref, wso_ref):
    bd_re, bd_im = bd_ref[0, :, :OCT_HALF], bd_ref[0, :, OCT_HALF:]
    cd_re, cd_im = cd_ref[0, :, :OCT_HALF], cd_ref[0, :, OCT_HALF:]
    c_hi, c_lo = _split_bf16(jnp.concatenate([cd_re, -cd_im], axis=1))
    blk = lambda i: slice(i * LANES, (i + 1) * LANES)
    for a in range(SSM_CHUNK // 2):
        t8_ref[blk(2 * a + 1), blk(2 * a)] = jnp.zeros((LANES, LANES), BF16)
    for k in range(SSM_CHUNK + 1):
        p_re = pw_ref[0, k:k + 1, :OCT_HALF]
        p_im = pw_ref[0, k:k + 1, OCT_HALF:]
        if k < SSM_CHUNK:
            e = jnp.concatenate([bd_re * p_re - bd_im * p_im, bd_re * p_im + bd_im * p_re], axis=1)
            wus_ref[blk(SSM_CHUNK - 1 - k), :] = e.astype(BF16)
            e_hi, e_lo = _split_bf16(e)
            lag = (_dot_nt(e_hi, c_hi) + _dot_nt(e_hi, c_lo) + _dot_nt(e_lo, c_hi)).astype(BF16)
            for j in range(SSM_CHUNK - k):
                t8_ref[blk(j), blk(j + k)] = lag
        if k >= 1:
            g = jnp.concatenate([cd_re * p_re - cd_im * p_im, -(cd_re * p_im + cd_im * p_re)], axis=1)
            wso_ref[blk(k - 1), :] = g.astype(BF16)


def _ssm_core_kernel(us_ref, s0_ref, bd_ref, cd_ref, pw_ref, d_ref, y_ref, sfin_ref,
                     t8_ref, wus_ref, wso_ref, ds_ref, sp_ref, *, nb, n_chunks):
    @pl.when(pl.program_id(1) == 0)
    def _():
        _ssm_build_operators(bd_ref, cd_ref, pw_ref, t8_ref, wus_ref, wso_ref)

    ub = us_ref[0].astype(BF16)
    ds = _dot(ub, wus_ref[...])
    a16 = pw_ref[0, SSM_CHUNK:SSM_CHUNK + 1, :]
    n_blk = OCT_HALF // LANES
    blk = lambda k: slice(k * LANES, (k + 1) * LANES)

    def advance(s_re, s_im, d_re, d_im, a_re, a_im):
        return a_re * s_re - a_im * s_im + d_re, a_re * s_im + a_im * s_re + d_im

    if n_chunks == 1:
        s0 = s0_ref[0, 0]
        n_re, n_im = advance(s0[:, :OCT_HALF], s0[:, OCT_HALF:], ds[:, :OCT_HALF], ds[:, OCT_HALF:],
                             a16[:, :OCT_HALF], a16[:, OCT_HALF:])
        sfin_ref[0, 0] = jnp.concatenate([n_re, n_im], axis=1)
        spb = s0.astype(BF16)
    else:
        for k in range(2 * n_blk):
            ds_ref[k] = ds[:, blk(k)]

        def body(c, carry):
            at = pl.ds(c, nb, stride=n_chunks)
            new = list(carry)
            for k in range(n_blk):
                sp_ref[k, at, :] = carry[k]
                sp_ref[n_blk + k, at, :] = carry[n_blk + k]
                new[k], new[n_blk + k] = advance(carry[k], carry[n_blk + k], ds_ref[k, at, :],
                                                 ds_ref[n_blk + k, at, :], a16[:, blk(k)],
                                                 a16[:, blk(n_blk + k)])
            return tuple(new)

        init = tuple(s0_ref[0, 0, :, blk(k)] for k in range(2 * n_blk))
        fin = lax.fori_loop(0, n_chunks, body, init, unroll=2)
        for k in range(2 * n_blk):
            sfin_ref[0, 0, :, blk(k)] = fin[k]
        spb = jnp.concatenate([sp_ref[k] for k in range(2 * n_blk)], axis=1).astype(BF16)
    rows = ub.shape[0]
    d_skip = d_ref[0]
    for nt in range(OCT_COLS // MXU_DIM):
        k_hi = (nt + 1) * MXU_DIM
        cs = slice(nt * MXU_DIM, k_hi)
        y = _dot(ub[:, :k_hi], t8_ref[:k_hi, cs]) + _dot_nt(spb, wso_ref[cs, :])
        for j in range(MXU_DIM // LANES):
            i = nt * (MXU_DIM // LANES) + j
            y_ref[pl.ds(i, rows, stride=SSM_CHUNK), :] = (y[:, blk(j)] + d_skip * us_ref[0, :, blk(i)])


def _ssm_core(us_r, s0, bd, cd, pw, d_oct, n_batch, n_chunks, nb):
    nc = us_r.shape[1]
    rows = nb * n_chunks
    assert n_batch % nb == 0 and nc == n_batch * n_chunks
    nr = n_batch // nb
    s0 = s0.reshape(OCTETS, nr, nb, OCT_STATE)
    par = lambda rows_: pl.BlockSpec((1, rows_, OCT_STATE), lambda m, r: (m, 0, 0))
    y, sfin = pl.pallas_call(
        functools.partial(_ssm_core_kernel, nb=nb, n_chunks=n_chunks),
        grid=(OCTETS, nr),
        in_specs=[pl.BlockSpec((1, rows, OCT_COLS), lambda m, r: (m, r, 0)),
                  pl.BlockSpec((1, 1, nb, OCT_STATE), lambda m, r: (m, r, 0, 0)),
                  par(LANES), par(LANES), par(SSM_CHUNK + 1),
                  pl.BlockSpec((1, 1, LANES), lambda m, r: (m, 0, 0))],
        out_specs=[pl.BlockSpec((rows * SSM_CHUNK, LANES), lambda m, r: (r, m)),
                   pl.BlockSpec((1, 1, nb, OCT_STATE), lambda m, r: (m, r, 0, 0))],
        out_shape=[jax.ShapeDtypeStruct((nc * SSM_CHUNK, SSM_WIDTH), F32),
                   jax.ShapeDtypeStruct((OCTETS, nr, nb, OCT_STATE), F32)],
        scratch_shapes=[pltpu.VMEM((OCT_COLS, OCT_COLS), BF16),
                        pltpu.VMEM((OCT_COLS, OCT_STATE), BF16),
                        pltpu.VMEM((OCT_COLS, OCT_STATE), BF16),
                        pltpu.VMEM((OCT_STATE // LANES, rows, LANES), F32),
                        pltpu.VMEM((OCT_STATE // LANES, rows, LANES), F32)],
        compiler_params=_params(("arbitrary", "arbitrary"), VMEM_BIG_MIB),
        name="ssm_core",
    )(us_r, s0, bd, cd, pw, d_oct)
    return y, sfin.reshape(OCTETS, n_batch, OCT_STATE)


def _gelu_tanh(x):
    c = math.sqrt(2.0 / math.pi)
    return x * (0.5 * (1.0 + jnp.tanh(c * (x + 0.044715 * (x * x * x)))))


def _ssm_glu_kernel(y_ref, w_ref, b_ref, o_ref):
    yg = _gelu_tanh(y_ref[...])
    z = _dot(yg.astype(BF16), w_ref[...]) + b_ref[...]
    o_ref[...] = (yg * jax.nn.sigmoid(z)).astype(o_ref.dtype)


def _ssm_glu(y, w_glu, b_glu, tm):
    n_tok = y.shape[0]
    assert n_tok % tm == 0
    row = pl.BlockSpec((tm, SSM_WIDTH), lambda i: (i, 0))
    return pl.pallas_call(
        _ssm_glu_kernel,
        grid=(n_tok // tm,),
        in_specs=[row, pl.BlockSpec((SSM_WIDTH, SSM_WIDTH), lambda i: (0, 0)),
                  pl.BlockSpec((1, SSM_WIDTH), lambda i: (0, 0))],
        out_specs=row,
        out_shape=jax.ShapeDtypeStruct((n_tok, SSM_WIDTH), BF16),
        compiler_params=_params(("parallel",), VMEM_MID_MIB),
        name="ssm_glu",
    )(y, w_glu, b_glu)


N_PAIRS = ATT_HEADS // 2


def _head_masks():
    lane = lax.broadcasted_iota(jnp.int32, (1, LANES), 1)
    return (lane < ATT_HEAD_DIM, lane >= ATT_HEAD_DIM)


def _pair_scores(q2, kw, bias2, masks):
    qq = jnp.concatenate([jnp.where(m, q2, jnp.zeros_like(q2)) for m in masks], axis=0)
    return _dot_nt(qq, kw) * (ATT_HEAD_DIM ** -0.5 * LOG2E) + bias2


def _row_max(sc):
    return jnp.broadcast_to(jnp.max(sc, axis=1, keepdims=True), (sc.shape[0], LANES))


def _softmax_parts(sc, mx):
    p = jnp.exp2(sc - jnp.concatenate([mx] * (sc.shape[1] // LANES), axis=1))
    l = jnp.sum(p, axis=1, keepdims=True)
    return p.astype(BF16), jnp.broadcast_to(1.0 / l, (sc.shape[0], LANES))


def _pair_output(p, rl, vw, masks):
    o2 = _dot(p, vw) * rl
    n_q = o2.shape[0] // 2
    return jnp.where(masks[0], o2[:n_q], o2[n_q:])


def _band_attn_kernel(q_ref, kp_ref, kc_ref, vp_ref, vc_ref, bias_ref, o_ref,
                      kw_ref, vw_ref, sc_ref, mx_ref, p_ref, rl_ref):
    kw_ref[0:BAND_PAST] = kp_ref[...]
    kw_ref[BAND_PAST:BAND_PAST + Q_TILE] = kc_ref[...]
    vw_ref[0:BAND_PAST] = vp_ref[...]
    vw_ref[BAND_PAST:BAND_PAST + Q_TILE] = vc_ref[...]
    masks = _head_masks()
    cs = lambda hp: slice(hp * LANES, (hp + 1) * LANES)

    def sub(s, carry):
        r0 = pl.multiple_of(s * Q_SUB, Q_SUB)
        rows = pl.ds(r0, Q_SUB)
        win = pl.ds(r0, K_WIN)
        for hp in range(N_PAIRS):
            sc = _pair_scores(q_ref[rows, cs(hp)], kw_ref[win, cs(hp)], bias_ref[hp], masks)
            sc_ref[hp] = sc
            mx_ref[hp] = _row_max(sc)

        @pl.when(pl.program_id(1) == 0)
        def _():
            col = lax.broadcasted_iota(jnp.int32, (1, K_WIN), 1)
            extra = jnp.where(col + r0 < BAND_PAST, NEG_INF, 0.0)
            for hp in range(N_PAIRS):
                sc = sc_ref[hp] + extra
                sc_ref[hp] = sc
                mx_ref[hp] = _row_max(sc)

        for hp in range(N_PAIRS):
            p_ref[hp], rl_ref[hp] = _softmax_parts(sc_ref[hp], mx_ref[hp])
        for hp in range(N_PAIRS):
            o = _pair_output(p_ref[hp], rl_ref[hp], vw_ref[win, cs(hp)], masks)
            o_ref[rows, cs(hp)] = o.astype(o_ref.dtype)
        return carry

    lax.fori_loop(0, Q_TILE // Q_SUB, sub, 0)


def _rel_bias_tile(rel_bias, n_q, n_k, offset, ok):
    r = np.arange(n_q - 1 + n_k)
    idx = np.clip(offset + n_q - 1 - r, -REL_CLIP, REL_CLIP) + REL_CLIP
    v = jnp.pad(rel_bias[:, idx] * LOG2E, ((0, 0), (0, 1)))
    w = n_q + n_k
    flat = jnp.tile(v, (1, n_q))[:, :n_q * (w - 1)]
    toep = flat.reshape(rel_bias.shape[0], n_q, w - 1)[:, :, n_q - 1:]
    return jnp.where(ok[None], toep, NEG_INF).reshape(N_PAIRS, 2 * n_q, n_k)


def _band_attention_prompt(qkvm, rel_bias, n_batch, seq):
    assert seq % Q_TILE == 0 and Q_TILE % BAND_PAST == 0
    nt = seq // Q_TILE
    qc = np.arange(Q_SUB)[:, None] // CHUNK
    kc = np.arange(K_WIN)[None, :] // CHUNK
    bias = _rel_bias_tile(rel_bias, Q_SUB, K_WIN, BAND_PAST, (kc >= qc) & (kc <= qc + BAND_PAST_CHUNKS))
    blk = (Q_TILE, ATT_WIDTH)
    ratio = Q_TILE // BAND_PAST
    cur = lambda col: pl.BlockSpec(blk, lambda b, t: (b * nt + t, col))
    prev = lambda col: pl.BlockSpec(
        (BAND_PAST, ATT_WIDTH), lambda b, t: (jnp.maximum((b * nt + t) * ratio - 1, b * nt * ratio), col))
    return pl.pallas_call(
        _band_attn_kernel,
        grid=(n_batch, nt),
        in_specs=[cur(0), prev(1), cur(1), prev(2), cur(2),
                  pl.BlockSpec((N_PAIRS, 2 * Q_SUB, K_WIN), lambda b, t: (0, 0, 0))],
        out_specs=pl.BlockSpec(blk, lambda b, t: (b * nt + t, 0)),
        out_shape=jax.ShapeDtypeStruct((n_batch * seq, ATT_WIDTH), BF16),
        scratch_shapes=[pltpu.VMEM((BAND_PAST + Q_TILE, ATT_WIDTH), BF16),
                        pltpu.VMEM((BAND_PAST + Q_TILE, ATT_WIDTH), BF16),
                        pltpu.VMEM((N_PAIRS, 2 * Q_SUB, K_WIN), F32),
                        pltpu.VMEM((N_PAIRS, 2 * Q_SUB, LANES), F32),
                        pltpu.VMEM((N_PAIRS, 2 * Q_SUB, K_WIN), BF16),
                        pltpu.VMEM((N_PAIRS, 2 * Q_SUB, LANES), F32)],
        compiler_params=_params(("parallel", "arbitrary"), VMEM_MID_MIB),
        name="band_attn",
    )(qkvm, qkvm, qkvm, qkvm, qkvm, bias)


def _band_attn_sample_kernel(q_ref, kn_ref, vn_ref, ck_ref, cv_ref, bias_ref, o_ref, kw_ref, vw_ref,
                             *, n_cache, n_new):
    kw_ref[...] = jnp.zeros_like(kw_ref)
    vw_ref[...] = jnp.zeros_like(vw_ref)
    kw_ref[0:n_cache] = ck_ref[0].astype(BF16)
    vw_ref[0:n_cache] = cv_ref[0].astype(BF16)
    kw_ref[n_cache:n_cache + n_new] = kn_ref[...]
    vw_ref[n_cache:n_cache + n_new] = vn_ref[...]
    masks = _head_masks()
    for hp in range(N_PAIRS):
        cs = slice(hp * LANES, (hp + 1) * LANES)
        sc = _pair_scores(q_ref[:, cs], kw_ref[:, cs], bias_ref[hp], masks)
        p, rl = _softmax_parts(sc, _row_max(sc))
        o_ref[:, cs] = _pair_output(p, rl, vw_ref[:, cs], masks).astype(o_ref.dtype)


def _band_attention_sample(qkvm, rel_bias, cache_k, cache_v, n_batch, n_new):
    n_cache = cache_k.shape[1]
    n_keys = -(-(n_cache + n_new) // LANES) * LANES
    q_pos = PAST_LEN + np.arange(n_new)[:, None]
    j = np.arange(n_keys)[None, :]
    k_pos = PAST_LEN - n_cache + j
    ok = ((j < n_cache + n_new) & (k_pos >= 0) & (k_pos // CHUNK <= q_pos // CHUNK)
          & (k_pos // CHUNK >= q_pos // CHUNK - BAND_PAST_CHUNKS))
    bias = _rel_bias_tile(rel_bias, n_new, n_keys, n_cache, ok)
    new = lambda col: pl.BlockSpec((n_new, ATT_WIDTH), lambda b: (b, col))
    cache = pl.BlockSpec((1, n_cache, ATT_WIDTH), lambda b: (b, 0, 0))
    return pl.pallas_call(
        functools.partial(_band_attn_sample_kernel, n_cache=n_cache, n_new=n_new),
        grid=(n_batch,),
        in_specs=[new(0), new(1), new(2), cache, cache,
                  pl.BlockSpec((N_PAIRS, 2 * n_new, n_keys), lambda b: (0, 0, 0))],
        out_specs=pl.BlockSpec((n_new, ATT_WIDTH), lambda b: (b, 0)),
        out_shape=jax.ShapeDtypeStruct((n_batch * n_new, ATT_WIDTH), BF16),
        scratch_shapes=[pltpu.VMEM((n_keys, ATT_WIDTH), BF16),
                        pltpu.VMEM((n_keys, ATT_WIDTH), BF16)],
        compiler_params=_params(("parallel",), VMEM_MID_MIB),
        name="band_attn_sample",
    )(qkvm, qkvm, qkvm, cache_k, cache_v, bias)


def _mem_attn_kernel(q_ref, k_ref, v_ref, o_ref):
    k = k_ref[0].astype(BF16)
    v = v_ref[0].astype(BF16)
    for h in range(MEM_HEADS):
        cs = slice(h * MEM_HEAD_DIM, (h + 1) * MEM_HEAD_DIM)
        sc = _dot_nt(q_ref[:, cs], k[:, cs]) * (MEM_HEAD_DIM ** -0.5 * LOG2E)
        mx = jnp.max(sc, axis=1, keepdims=True)
        p = jnp.exp2(sc - mx)
        l = jnp.sum(p, axis=1, keepdims=True)
        o = _dot(p.astype(BF16), v[:, cs]) * (1.0 / l)
        o_ref[:, cs] = o.astype(o_ref.dtype)


def _memory_attention(qkvm, mem_k, mem_v, n_batch, seq, tq):
    nt = seq // tq
    mem = pl.BlockSpec((1, N_MEM, MEM_WIDTH), lambda b, t: (b, 0, 0))
    return pl.pallas_call(
        _mem_attn_kernel,
        grid=(n_batch, nt),
        in_specs=[pl.BlockSpec((tq, MEM_WIDTH), lambda b, t: (b * nt + t, 3)), mem, mem],
        out_specs=pl.BlockSpec((tq, MEM_WIDTH), lambda b, t: (b * nt + t, 0)),
        out_shape=jax.ShapeDtypeStruct((n_batch * seq, MEM_WIDTH), BF16),
        compiler_params=_params(("parallel", "arbitrary"), VMEM_MID_MIB),
        name="mem_attn",
    )(qkvm, mem_k, mem_v)


def _gate_merge_kernel(u_ref, os_ref, oa_ref, om_ref, wgs_ref, wga_ref, wgm_ref,
                       wbs_ref, wba_ref, wbm_ref, out_ref):
    u = u_ref[...]

    def branch(o_ref, wg_ref, wb_ref):
        return jax.nn.sigmoid(_dot(u, wg_ref[...])) * _dot(o_ref[...], wb_ref[...])

    merged = (branch(os_ref, wgs_ref, wbs_ref) + branch(oa_ref, wga_ref, wba_ref)
              + branch(om_ref, wgm_ref, wbm_ref))
    out_ref[...] = merged.astype(out_ref.dtype)


def _gate_merge(u, o_s, o_a, o_m, w_gate, wb_s, wb_a, wb_m, tm, tn):
    m, d = u.shape
    gate = lambda b: pl.BlockSpec((d, tn), lambda i, n: (0, b * (d // tn) + n))
    wb = pl.BlockSpec((SSM_WIDTH, tn), lambda i, n: (0, n))
    ob = pl.BlockSpec((tm, SSM_WIDTH), lambda i, n: (i, 0))
    return pl.pallas_call(
        _gate_merge_kernel,
        grid=(m // tm, d // tn),
        in_specs=[pl.BlockSpec((tm, d), lambda i, n: (i, 0)), ob, ob, ob,
                  gate(0), gate(1), gate(2), wb, wb, wb],
        out_specs=pl.BlockSpec((tm, tn), lambda i, n: (i, n)),
        out_shape=jax.ShapeDtypeStruct((m, d), BF16),
        compiler_params=_params(("parallel", "arbitrary"), VMEM_BIG_MIB),
        name="gate_merge",
    )(u, o_s, o_a, o_m, w_gate, w_gate, w_gate, wb_s, wb_a, wb_m)


def _out_proj_kernel(x_ref, w_ref, h_ref, g_ref, o_ref):
    o_ref[...] = h_ref[...] + _rms(_dot(x_ref[...], w_ref[...]), g_ref[...])


def _out_proj(x, w_out, h, g_post, tm):
    m, d = h.shape
    row = lambda: pl.BlockSpec((tm, d), lambda i: (i, 0))
    return pl.pallas_call(
        _out_proj_kernel,
        grid=(m // tm,),
        in_specs=[row(), pl.BlockSpec((d, d), lambda i: (0, 0)), row(),
                  pl.BlockSpec((1, d), lambda i: (0, 0))],
        out_specs=row(),
        out_shape=jax.ShapeDtypeStruct((m, d), F32),
        compiler_params=_params(("parallel",), VMEM_BIG_MIB),
        name="out_proj",
    )(x, w_out, h, g_post)


def _layer(x, w, *, n_batch, seq, s0, mem_k, mem_v, cache_k, cache_v, tm, keep):
    n_tok = n_batch * seq
    n_chunks = seq // SSM_CHUNK
    big_tm = min(1024, n_tok)
    h1, u = _ffn(x, w["ffn1_norm_pre"], w["ffn1_norm_post"], w["ffn1_w_gate"], w["ffn1_w_up"],
                 w["ffn1_w_down"], w["mix_norm_pre"], big_tm, 512)

    qkvm, k_keep, v_keep, us_r = _in_proj(u, w["w_qkvm"], w["w_ssm"], tm, seq, keep)
    kv = (k_keep, v_keep)

    nb = 2 if (n_chunks > 1 and n_batch % 2 == 0) else (1 if n_chunks > 1 else n_batch)
    y_s, s_fin = _ssm_core(us_r, s0, w["ssm_bd"], w["ssm_cd"], w["ssm_pw"], w["ssm_d"], n_batch,
                           n_chunks, nb)
    o_s = _ssm_glu(y_s, w["ssm_w_glu"], w["ssm_b_glu"], big_tm)

    if cache_k is None:
        o_a = _band_attention_prompt(qkvm, w["att_rel_bias"], n_batch, seq)
    else:
        o_a = _band_attention_sample(qkvm, w["att_rel_bias"], cache_k, cache_v, n_batch, seq)
    o_m = _memory_attention(qkvm, mem_k, mem_v, n_batch, seq, min(4096, seq))

    merged = _gate_merge(u, o_s, o_a, o_m, w["w_gate"], w["w_branch_ssm"], w["w_branch_att"],
                         w["w_branch_mem"], big_tm, 512)
    h2 = _out_proj(merged, w["w_out"], h1, w["mix_norm_post"], tm)
    y = _ffn(h2, w["ffn2_norm_pre"], w["ffn2_norm_post"], w["ffn2_w_gate"], w["ffn2_w_up"],
             w["ffn2_w_down"], None, big_tm, 512)
    return y, kv, s_fin


def kernel(x_prompt, x_sample, mem_prompt, cache_att_k, cache_att_v, cache_mem_k, cache_mem_v, state_ssm_re, state_ssm_im, ffn1_norm_pre, ffn1_norm_post, ffn1_w_gate, ffn1_w_up, ffn1_w_down, mix_norm_pre, mix_norm_post, w_in, ssm_a_re, ssm_a_im, ssm_log_dt, ssm_b_re, ssm_b_im, ssm_c_re, ssm_c_im, ssm_d, ssm_w_glu, ssm_b_glu, att_rel_bias, mem_norm, w_mem_k, w_mem_v, w_branch_ssm, w_branch_att, w_branch_mem, w_out, ffn2_norm_pre, ffn2_norm_post, ffn2_w_gate, ffn2_w_up, ffn2_w_down):
    n_bp, t_p, d = x_prompt.shape
    n_bs, t_s, _ = x_sample.shape
    depth = ffn1_norm_pre.shape[0]
    assert depth == 1 and d == D_MODEL
    keep = min(BAND_PAST, t_p)
    l = 0

    vec = lambda a: a[l].reshape(1, -1).astype(F32)
    mat = lambda a: _cast_bf16(a[l])
    bd, cd, pw = _ssm_params(ssm_a_re[l], ssm_a_im[l], ssm_log_dt[l], ssm_b_re[l],
                             ssm_b_im[l], ssm_c_re[l], ssm_c_im[l])
    w = {
        "ffn1_norm_pre": vec(ffn1_norm_pre), "ffn1_norm_post": vec(ffn1_norm_post),
        "ffn1_w_gate": mat(ffn1_w_gate), "ffn1_w_up": mat(ffn1_w_up), "ffn1_w_down": mat(ffn1_w_down),
        "mix_norm_pre": vec(mix_norm_pre), "mix_norm_post": vec(mix_norm_post),
        "w_ssm": _cast_bf16(w_in[l], 0, COL_Q),
        "w_qkvm": _cast_bf16(w_in[l], COL_Q, COL_GATE - COL_Q),
        "w_gate": _cast_bf16(w_in[l], COL_GATE),
        "ssm_bd": bd, "ssm_cd": cd, "ssm_pw": pw,
        "ssm_d": ssm_d[l].reshape(OCTETS, 1, LANES).astype(F32), "ssm_w_glu": mat(ssm_w_glu), "ssm_b_glu": vec(ssm_b_glu),
        "att_rel_bias": att_rel_bias[l].astype(F32),
        "w_branch_ssm": mat(w_branch_ssm), "w_branch_att": mat(w_branch_att),
        "w_branch_mem": mat(w_branch_mem), "w_out": mat(w_out),
        "ffn2_norm_pre": vec(ffn2_norm_pre), "ffn2_norm_post": vec(ffn2_norm_post),
        "ffn2_w_gate": mat(ffn2_w_gate), "ffn2_w_up": mat(ffn2_w_up), "ffn2_w_down": mat(ffn2_w_down),
    }

    w_mem = jnp.concatenate([w_mem_k[l], w_mem_v[l]], axis=1).astype(BF16)
    mk_p, mv_p = _memory_kv(mem_prompt.reshape(n_bp * N_MEM, d), vec(mem_norm), w_mem, 512)
    mk_p = mk_p.reshape(n_bp, N_MEM, MEM_WIDTH)
    mv_p = mv_p.reshape(n_bp, N_MEM, MEM_WIDTH)
    zero_state = jnp.zeros((OCTETS, n_bp, OCT_STATE), F32)
    y_p, kv_p, sfin_p = _layer(x_prompt.reshape(n_bp * t_p, d), w, n_batch=n_bp, seq=t_p, s0=zero_state,
                               mem_k=mk_p, mem_v=mv_p, cache_k=None, cache_v=None, tm=512, keep=keep)
    sre_p, sim_p = _state_from_octets(sfin_p)

    n_cache = cache_att_k.shape[2]
    s0_s = _state_to_octets(state_ssm_re[l].astype(F32), state_ssm_im[l].astype(F32))
    y_s, kv_s, sfin_s = _layer(x_sample.reshape(n_bs * t_s, d), w, n_batch=n_bs, seq=t_s, s0=s0_s,
                               mem_k=cache_mem_k[l].reshape(n_bs, N_MEM, MEM_WIDTH),
                               mem_v=cache_mem_v[l].reshape(n_bs, N_MEM, MEM_WIDTH),
                               cache_k=cache_att_k[l].reshape(n_bs, n_cache, ATT_WIDTH),
                               cache_v=cache_att_v[l].reshape(n_bs, n_cache, ATT_WIDTH),
                               tm=n_bs * t_s, keep=t_s)
    sre_s, sim_s = _state_from_octets(sfin_s)

    heads = lambda a, nb, t: a.reshape(1, nb, t, ATT_HEADS, ATT_HEAD_DIM)
    memh = lambda a: a.reshape(1, n_bp, N_MEM, MEM_HEADS, MEM_HEAD_DIM)
    return (y_p.reshape(n_bp, t_p, d), y_s.reshape(n_bs, t_s, d),
            heads(kv_p[0], n_bp, keep), heads(kv_p[1], n_bp, keep),
            memh(mk_p), memh(mv_p), sre_p[None], sim_p[None],
            heads(kv_s[0], n_bs, t_s), heads(kv_s[1], n_bs, t_s),
            sre_s[None], sim_s[None])
```

```python
import functools
import math

import numpy as np
import jax
import jax.numpy as jnp
from jax import lax
from jax.experimental import pallas as pl
from jax.experimental.pallas import tpu as pltpu

F32 = jnp.float32
BF16 = jnp.bfloat16

D_MODEL = 2048
CHUNK = 64
BAND_PAST_CHUNKS = 8
BAND_PAST = BAND_PAST_CHUNKS * CHUNK
ATT_HEADS = 16
ATT_HEAD_DIM = 64
ATT_WIDTH = ATT_HEADS * ATT_HEAD_DIM
REL_CLIP = 128
SSM_GROUP = 16
SSM_WIDTH = 1024
SSM_GROUPS = SSM_WIDTH // SSM_GROUP
SSM_STATE = 64
N_MEM = 256
MEM_HEADS = 4
MEM_HEAD_DIM = 256
MEM_WIDTH = MEM_HEADS * MEM_HEAD_DIM
EPS = 1e-6
NEG_INF = -1e30
PAST_LEN = 4096
LOG2E = math.log2(math.e)

COL_Q = SSM_WIDTH
COL_GATE = SSM_WIDTH + 3 * ATT_WIDTH + MEM_WIDTH

SSM_CHUNK = 16
OCTETS = 8
OCT_GROUPS = SSM_GROUPS // OCTETS
LANES = 128
MXU_DIM = 256
OCT_COLS = SSM_CHUNK * LANES
OCT_HALF = OCT_GROUPS * SSM_STATE
OCT_STATE = 2 * OCT_HALF

Q_TILE = 512
Q_SUB = 128
K_WIN = Q_SUB + BAND_PAST

MIB = 1024 * 1024
VMEM_BIG_MIB = 56
VMEM_MID_MIB = 48
VMEM_SMALL_MIB = 32


def _params(sem, vmem_mib):
    return pltpu.CompilerParams(dimension_semantics=sem, vmem_limit_bytes=vmem_mib * MIB)


def _dot(a, b):
    return jnp.dot(a, b, preferred_element_type=F32)


def _dot_nt(a, b):
    return lax.dot_general(a, b, (((1,), (1,)), ((), ())), preferred_element_type=F32)


def _rms(xf, g):
    y = xf * lax.rsqrt(jnp.mean(xf * xf, axis=-1, keepdims=True) + EPS)
    return y * g


CAST_BLOCK_BYTES = 8 * MIB


def _cast_kernel(x_ref, o_ref):
    o_ref[...] = x_ref[...].astype(o_ref.dtype)


def _cast_bf16(w, col0=0, n_cols=None):
    rows, cols = w.shape
    n_cols = cols - col0 if n_cols is None else n_cols
    cw = math.gcd(col0, n_cols)
    assert cw % LANES == 0 and col0 + n_cols <= cols
    tr = rows
    while tr * cw * 4 > CAST_BLOCK_BYTES and tr % 32 == 0:
        tr //= 2
    cb0 = col0 // cw
    return pl.pallas_call(
        _cast_kernel,
        grid=(rows // tr, n_cols // cw),
        in_specs=[pl.BlockSpec((tr, cw), lambda i, j: (i, cb0 + j))],
        out_specs=pl.BlockSpec((tr, cw), lambda i, j: (i, j)),
        out_shape=jax.ShapeDtypeStruct((rows, n_cols), BF16),
        compiler_params=_params(("parallel", "parallel"), VMEM_SMALL_MIB),
        name="cast_bf16",
    )(w)


FFN_SLICES = 8


def _ffn_kernel(hp_ref, hn_ref, gpre_ref, gpost_ref, wg_ref, wu_ref, wd_ref, *rest,
                n_tiles, emit_next):
    if emit_next:
        gnext_ref, out_ref, nxt_ref, *scratch = rest
    else:
        out_ref, *scratch = rest
    xn_refs, acc_refs = scratch[:2], scratch[2:]
    r = pl.program_id(0)
    f = pl.program_id(1)
    rs = hp_ref.shape[0]
    rows = pl.ds(pl.multiple_of(jnp.minimum(f, FFN_SLICES - 1) * rs, rs), rs)

    def pre_norm_slice(slot):
        xn_refs[slot][rows, :] = _rms(hn_ref[...], gpre_ref[...]).astype(BF16)

    def finish_slice(slot):
        hn = hp_ref[...] + 0.5 * _rms(acc_refs[slot][rows, :], gpost_ref[...])
        out_ref[...] = hn
        if emit_next:
            nxt_ref[...] = _rms(hn, gnext_ref[...]).astype(BF16)

    def matmul_chunk(slot):
        xn = xn_refs[slot][...]
        g = _dot(xn, wg_ref[...])
        u = _dot(xn, wu_ref[...])
        a = (g * jax.nn.sigmoid(g)) * u
        d = _dot(a.astype(BF16), wd_ref[...])
        acc_refs[slot][...] = jnp.where(f == 0, d, acc_refs[slot][...] + d)

    @pl.when((r == 0) & (f == 0))
    def _():
        acc_refs[1][...] = jnp.zeros_like(acc_refs[1])

    @pl.when(r == 0)
    def _():
        pre_norm_slice(0)

    for parity in range(2):
        @pl.when((r >= 1) & (r <= n_tiles) & (lax.rem(r, 2) == parity))
        def _():
            finish_slice(parity)
            pre_norm_slice(parity)
            matmul_chunk(1 - parity)

    @pl.when(r == n_tiles + 1)
    def _():
        finish_slice((n_tiles + 1) % 2)


def _ffn(h, g_pre, g_post, wg, wu, wd, g_next, tm, tf):
    emit_next = g_next is not None
    m, d = h.shape
    f_dim = wg.shape[1]
    nf = f_dim // tf
    n = m // tm
    rs = tm // FFN_SLICES
    assert m % tm == 0 and f_dim % tf == 0 and tm % FFN_SLICES == 0 and nf >= FFN_SLICES
    sl = lambda f: jnp.minimum(f, FFN_SLICES - 1)
    done = lambda r, f: (jnp.maximum(r - 2, 0) * FFN_SLICES + jnp.where(r >= 2, sl(f), 0), 0)
    ahead = lambda r, f: (jnp.minimum(r, n - 1) * FFN_SLICES + sl(f), 0)
    chunk = lambda r, f: jnp.where((r >= 1) & (r <= n), f, 0)
    vec = pl.BlockSpec((1, d), lambda r, f: (0, 0))
    in_specs = [pl.BlockSpec((rs, d), done), pl.BlockSpec((rs, d), ahead), vec, vec,
                pl.BlockSpec((d, tf), lambda r, f: (0, chunk(r, f))),
                pl.BlockSpec((d, tf), lambda r, f: (0, chunk(r, f))),
                pl.BlockSpec((tf, d), lambda r, f: (chunk(r, f), 0))]
    args = [h, h, g_pre, g_post, wg, wu, wd]
    out_shape = [jax.ShapeDtypeStruct((m, d), F32)]
    out_specs = [pl.BlockSpec((rs, d), done)]
    if emit_next:
        in_specs.append(vec)
        args.append(g_next)
        out_shape.append(jax.ShapeDtypeStruct((m, d), BF16))
        out_specs.append(pl.BlockSpec((rs, d), done))
    res = pl.pallas_call(
        functools.partial(_ffn_kernel, n_tiles=n, emit_next=emit_next),
        grid=(n + 2, nf),
        in_specs=in_specs,
        out_specs=out_specs,
        out_shape=out_shape,
        scratch_shapes=[pltpu.VMEM((tm, d), BF16), pltpu.VMEM((tm, d), BF16),
                        pltpu.VMEM((tm, d), F32), pltpu.VMEM((tm, d), F32)],
        compiler_params=_params(("arbitrary", "arbitrary"), VMEM_BIG_MIB),
        name="ffn",
    )(*args)
    return res if emit_next else res[0]


def _memory_kv_kernel(x_ref, g_ref, w_ref, k_ref, v_ref):
    x = _rms(x_ref[...], g_ref[...]).astype(BF16)
    k_ref[...] = _dot(x, w_ref[:, :MEM_WIDTH])
    v_ref[...] = _dot(x, w_ref[:, MEM_WIDTH:])


def _memory_kv(x, gain, w, tm):
    m, k = x.shape
    tm = min(tm, m)
    assert m % tm == 0 and w.shape == (k, 2 * MEM_WIDTH)
    out = pl.BlockSpec((tm, MEM_WIDTH), lambda i: (i, 0))
    return pl.pallas_call(
        _memory_kv_kernel,
        grid=(m // tm,),
        in_specs=[pl.BlockSpec((tm, k), lambda i: (i, 0)), pl.BlockSpec((1, k), lambda i: (0, 0)),
                  pl.BlockSpec((k, 2 * MEM_WIDTH), lambda i: (0, 0))],
        out_specs=[out, out],
        out_shape=[jax.ShapeDtypeStruct((m, MEM_WIDTH), F32)] * 2,
        compiler_params=_params(("parallel",), VMEM_MID_MIB),
        name="memory_kv",
    )(x, gain, w)


def _in_proj_kernel(x_ref, wq_ref, ws_ref, o_ref, k_ref, v_ref, us_ref, nat_ref, *, tiles_per_seq):
    x = x_ref[...]
    last_of_seq = lax.rem(pl.program_id(0), tiles_per_seq) == tiles_per_seq - 1
    for j, keep_ref in enumerate((None, k_ref, v_ref, None)):
        cs = slice(j * ATT_WIDTH, (j + 1) * ATT_WIDTH)
        r = _dot(x, wq_ref[:, cs])
        o_ref[:, cs] = r.astype(o_ref.dtype)
        if keep_ref is None:
            continue
        if tiles_per_seq == 1:
            keep_ref[...] = r
        else:
            @pl.when(last_of_seq)
            def _():
                keep_ref[...] = r.T
    r = _dot(x, ws_ref[...])
    tc = nat_ref.shape[1] // SSM_CHUNK
    for m in range(OCTETS):
        nat_ref[m] = r[:, m * LANES:(m + 1) * LANES]
    for j in range(SSM_CHUNK):
        for m in range(OCTETS):
            us_ref[m, :, j * LANES:(j + 1) * LANES] = nat_ref[m, pl.ds(j, tc, stride=SSM_CHUNK), :]


def _in_proj(u, w_qkvm, w_ssm, tm, seq, keep):
    n_tok = u.shape[0]
    assert n_tok % tm == 0 and tm % SSM_CHUNK == 0 and MEM_WIDTH == ATT_WIDTH
    tc = tm // SSM_CHUNK
    if keep == seq:
        tiles_per_seq = 1
        kept = pl.BlockSpec((tm, ATT_WIDTH), lambda i: (i, 0))
        kept_shape = jax.ShapeDtypeStruct((n_tok, ATT_WIDTH), F32)
    else:
        assert keep == tm and seq % tm == 0
        tiles_per_seq = seq // tm
        kept = pl.BlockSpec((None, ATT_WIDTH, keep), lambda i: (i // tiles_per_seq, 0, 0))
        kept_shape = jax.ShapeDtypeStruct((n_tok // seq, ATT_WIDTH, keep), F32)
    return pl.pallas_call(
        functools.partial(_in_proj_kernel, tiles_per_seq=tiles_per_seq),
        grid=(n_tok // tm,),
        in_specs=[pl.BlockSpec((tm, D_MODEL), lambda i: (i, 0)),
                  pl.BlockSpec((D_MODEL, 4 * ATT_WIDTH), lambda i: (0, 0)),
                  pl.BlockSpec((D_MODEL, SSM_WIDTH), lambda i: (0, 0))],
        out_specs=[pl.BlockSpec((tm, 4 * ATT_WIDTH), lambda i: (i, 0)), kept, kept,
                   pl.BlockSpec((OCTETS, tc, OCT_COLS), lambda i: (0, i, 0))],
        out_shape=[jax.ShapeDtypeStruct((n_tok, 4 * ATT_WIDTH), BF16), kept_shape, kept_shape,
                   jax.ShapeDtypeStruct((OCTETS, n_tok // SSM_CHUNK, OCT_COLS), F32)],
        scratch_shapes=[pltpu.VMEM((OCTETS, tm, LANES), F32)],
        compiler_params=_params(("arbitrary",), VMEM_BIG_MIB),
        name="in_proj",
    )(u, w_qkvm, w_ssm)


def _ssm_params(a_re, a_im, log_dt, b_re, b_im, c_re, c_im):
    dt = jnp.exp(log_dt)[:, None]
    mag = jnp.exp(a_re * dt)
    ab_re = mag * jnp.cos(a_im * dt)
    ab_im = mag * jnp.sin(a_im * dt)
    den = a_re * a_re + a_im * a_im
    n_re = ab_re - 1.0
    n_im = ab_im
    k_re = (n_re * a_re + n_im * a_im) / den
    k_im = (n_im * a_re - n_re * a_im) / den
    bb_re = k_re[..., None] * b_re - k_im[..., None] * b_im
    bb_im = k_re[..., None] * b_im + k_im[..., None] * b_re
    pr = [jnp.ones_like(ab_re)]
    pi = [jnp.zeros_like(ab_re)]
    for _ in range(SSM_CHUNK):
        pr.append(pr[-1] * ab_re - pi[-1] * ab_im)
        pi.append(pr[-2] * ab_im + pi[-1] * ab_re)
    n_pw = SSM_CHUNK + 1
    pw = jnp.concatenate([jnp.stack(pr).reshape(n_pw, OCTETS, OCT_HALF),
                          jnp.stack(pi).reshape(n_pw, OCTETS, OCT_HALF)], axis=2)
    pw = jnp.transpose(pw, (1, 0, 2))
    eye = jnp.eye(OCT_GROUPS, dtype=F32)

    def expand(x):
        x4 = x.reshape(OCTETS, OCT_GROUPS, x.shape[1], SSM_STATE)
        out = eye[None, :, None, :, None] * x4[:, :, :, None, :]
        return out.reshape(OCTETS, OCT_GROUPS * x.shape[1], OCT_HALF)

    bd = jnp.concatenate([expand(jnp.transpose(bb_re, (0, 2, 1))),
                          expand(jnp.transpose(bb_im, (0, 2, 1)))], axis=2)
    cd = jnp.concatenate([expand(c_re), expand(c_im)], axis=2)
    return bd, cd, pw


def _state_to_octets(s_re, s_im):
    b = s_re.shape[0]
    s = jnp.stack([s_re, s_im], 0).reshape(2, b, OCTETS, OCT_GROUPS, SSM_STATE)
    return jnp.transpose(s, (2, 1, 0, 3, 4)).reshape(OCTETS, b, OCT_STATE)


def _state_from_octets(s):
    b = s.shape[1]
    s = s.reshape(OCTETS, b, 2, OCT_GROUPS, SSM_STATE)
    s = jnp.transpose(s, (2, 1, 0, 3, 4)).reshape(2, b, SSM_GROUPS, SSM_STATE)
    return s[0], s[1]


def _split_bf16(x):
    hi = x.astype(BF16)
    return hi, (x - hi.astype(F32)).astype(BF16)


def _ssm_build_operators(bd_ref, cd_ref, pw_ref, t8_ref, wus_ref, wso_ref):
    bd_re, bd_im = bd_ref[0, :, :OCT_HALF], bd_ref[0, :, OCT_HALF:]
    cd_re, cd_im = cd_ref[0, :, :OCT_HALF], cd_ref[0, :, OCT_HALF:]
    c_hi, c_lo = _split_bf16(jnp.concatenate([cd_re, -cd_im], axis=1))
    blk = lambda i: slice(i * LANES, (i + 1) * LANES)
    for a in range(SSM_CHUNK // 2):
        t8_ref[blk(2 * a + 1), blk(2 * a)] = jnp.zeros((LANES, LANES), BF16)
    for k in range(SSM_CHUNK + 1):
        p_re = pw_ref[0, k:k + 1, :OCT_HALF]
        p_im = pw_ref[0, k:k + 1, OCT_HALF:]
        if k < SSM_CHUNK:
            e = jnp.concatenate([bd_re * p_re - bd_im * p_im, bd_re * p_im + bd_im * p_re], axis=1)
            wus_ref[blk(SSM_CHUNK - 1 - k), :] = e.astype(BF16)
            e_hi, e_lo = _split_bf16(e)
            lag = (_dot_nt(e_hi, c_hi) + _dot_nt(e_hi, c_lo) + _dot_nt(e_lo, c_hi)).astype(BF16)
            for j in range(SSM_CHUNK - k):
                t8_ref[blk(j), blk(j + k)] = lag
        if k >= 1:
            g = jnp.concatenate([cd_re * p_re - cd_im * p_im, -(cd_re * p_im + cd_im * p_re)], axis=1)
            wso_ref[blk(k - 1), :] = g.astype(BF16)


def _ssm_core_kernel(us_ref, s0_ref, bd_ref, cd_ref, pw_ref, d_ref, y_ref, sfin_ref,
                     t8_ref, wus_ref, wso_ref, ds_ref, sp_ref, *, nb, n_chunks):
    @pl.when(pl.program_id(1) == 0)
    def _():
        _ssm_build_operators(bd_ref, cd_ref, pw_ref, t8_ref, wus_ref, wso_ref)

    ub = us_ref[0].astype(BF16)
    ds = _dot(ub, wus_ref[...])
    a16 = pw_ref[0, SSM_CHUNK:SSM_CHUNK + 1, :]
    n_blk = OCT_HALF // LANES
    blk = lambda k: slice(k * LANES, (k + 1) * LANES)

    def advance(s_re, s_im, d_re, d_im, a_re, a_im):
        return a_re * s_re - a_im * s_im + d_re, a_re * s_im + a_im * s_re + d_im

    if n_chunks == 1:
        s0 = s0_ref[0, 0]
        n_re, n_im = advance(s0[:, :OCT_HALF], s0[:, OCT_HALF:], ds[:, :OCT_HALF], ds[:, OCT_HALF:],
                             a16[:, :OCT_HALF], a16[:, OCT_HALF:])
        sfin_ref[0, 0] = jnp.concatenate([n_re, n_im], axis=1)
        spb = s0.astype(BF16)
    else:
        for k in range(2 * n_blk):
            ds_ref[k] = ds[:, blk(k)]

        def body(c, carry):
            at = pl.ds(c, nb, stride=n_chunks)
            new = list(carry)
            for k in range(n_blk):
                sp_ref[k, at, :] = carry[k]
                sp_ref[n_blk + k, at, :] = carry[n_blk + k]
                new[k], new[n_blk + k] = advance(carry[k], carry[n_blk + k], ds_ref[k, at, :],
                                                 ds_ref[n_blk + k, at, :], a16[:, blk(k)],
                                                 a16[:, blk(n_blk + k)])
            return tuple(new)

        init = tuple(s0_ref[0, 0, :, blk(k)] for k in range(2 * n_blk))
        fin = lax.fori_loop(0, n_chunks, body, init, unroll=2)
        for k in range(2 * n_blk):
            sfin_ref[0, 0, :, blk(k)] = fin[k]
        spb = jnp.concatenate([sp_ref[k] for k in range(2 * n_blk)], axis=1).astype(BF16)
    rows = ub.shape[0]
    d_skip = d_ref[0]
    for nt in range(OCT_COLS // MXU_DIM):
        k_hi = (nt + 1) * MXU_DIM
        cs = slice(nt * MXU_DIM, k_hi)
        y = _dot(ub[:, :k_hi], t8_ref[:k_hi, cs]) + _dot_nt(spb, wso_ref[cs, :])
        for j in range(MXU_DIM // LANES):
            i = nt * (MXU_DIM // LANES) + j
            y_ref[pl.ds(i, rows, stride=SSM_CHUNK), :] = (y[:, blk(j)] + d_skip * us_ref[0, :, blk(i)])


def _ssm_core(us_r, s0, bd, cd, pw, d_oct, n_batch, n_chunks, nb):
    nc = us_r.shape[1]
    rows = nb * n_chunks
    assert n_batch % nb == 0 and nc == n_batch * n_chunks
    nr = n_batch // nb
    s0 = s0.reshape(OCTETS, nr, nb, OCT_STATE)
    par = lambda rows_: pl.BlockSpec((1, rows_, OCT_STATE), lambda m, r: (m, 0, 0))
    y, sfin = pl.pallas_call(
        functools.partial(_ssm_core_kernel, nb=nb, n_chunks=n_chunks),
        grid=(OCTETS, nr),
        in_specs=[pl.BlockSpec((1, rows, OCT_COLS), lambda m, r: (m, r, 0)),
                  pl.BlockSpec((1, 1, nb, OCT_STATE), lambda m, r: (m, r, 0, 0)),
                  par(LANES), par(LANES), par(SSM_CHUNK + 1),
                  pl.BlockSpec((1, 1, LANES), lambda m, r: (m, 0, 0))],
        out_specs=[pl.BlockSpec((rows * SSM_CHUNK, LANES), lambda m, r: (r, m)),
                   pl.BlockSpec((1, 1, nb, OCT_STATE), lambda m, r: (m, r, 0, 0))],
        out_shape=[jax.ShapeDtypeStruct((nc * SSM_CHUNK, SSM_WIDTH), F32),
                   jax.ShapeDtypeStruct((OCTETS, nr, nb, OCT_STATE), F32)],
        scratch_shapes=[pltpu.VMEM((OCT_COLS, OCT_COLS), BF16),
                        pltpu.VMEM((OCT_COLS, OCT_STATE), BF16),
                        pltpu.VMEM((OCT_COLS, OCT_STATE), BF16),
                        pltpu.VMEM((OCT_STATE // LANES, rows, LANES), F32),
                        pltpu.VMEM((OCT_STATE // LANES, rows, LANES), F32)],
        compiler_params=_params(("arbitrary", "arbitrary"), VMEM_BIG_MIB),
        name="ssm_core",
    )(us_r, s0, bd, cd, pw, d_oct)
    return y, sfin.reshape(OCTETS, n_batch, OCT_STATE)


def _gelu_tanh(x):
    c = math.sqrt(2.0 / math.pi)
    return x * (0.5 * (1.0 + jnp.tanh(c * (x + 0.044715 * (x * x * x)))))


def _ssm_glu_kernel(y_ref, w_ref, b_ref, o_ref):
    yg = _gelu_tanh(y_ref[...])
    z = _dot(yg.astype(BF16), w_ref[...]) + b_ref[...]
    o_ref[...] = (yg * jax.nn.sigmoid(z)).astype(o_ref.dtype)


def _ssm_glu(y, w_glu, b_glu, tm):
    n_tok = y.shape[0]
    assert n_tok % tm == 0
    row = pl.BlockSpec((tm, SSM_WIDTH), lambda i: (i, 0))
    return pl.pallas_call(
        _ssm_glu_kernel,
        grid=(n_tok // tm,),
        in_specs=[row, pl.BlockSpec((SSM_WIDTH, SSM_WIDTH), lambda i: (0, 0)),
                  pl.BlockSpec((1, SSM_WIDTH), lambda i: (0, 0))],
        out_specs=row,
        out_shape=jax.ShapeDtypeStruct((n_tok, SSM_WIDTH), BF16),
        compiler_params=_params(("parallel",), VMEM_MID_MIB),
        name="ssm_glu",
    )(y, w_glu, b_glu)


N_PAIRS = ATT_HEADS // 2


def _head_masks():
    lane = lax.broadcasted_iota(jnp.int32, (1, LANES), 1)
    return (lane < ATT_HEAD_DIM, lane >= ATT_HEAD_DIM)


def _pair_scores(q2, kw, bias2, masks):
    qq = jnp.concatenate([jnp.where(m, q2, jnp.zeros_like(q2)) for m in masks], axis=0)
    return _dot_nt(qq, kw) * (ATT_HEAD_DIM ** -0.5 * LOG2E) + bias2


def _row_max(sc):
    return jnp.broadcast_to(jnp.max(sc, axis=1, keepdims=True), (sc.shape[0], LANES))


def _softmax_parts(sc, mx):
    p = jnp.exp2(sc - jnp.concatenate([mx] * (sc.shape[1] // LANES), axis=1))
    l = jnp.sum(p, axis=1, keepdims=True)
    return p.astype(BF16), jnp.broadcast_to(1.0 / l, (sc.shape[0], LANES))


def _pair_output(p, rl, vw, masks):
    o2 = _dot(p, vw) * rl
    n_q = o2.shape[0] // 2
    return jnp.where(masks[0], o2[:n_q], o2[n_q:])


def _band_attn_kernel(q_ref, kp_ref, kc_ref, vp_ref, vc_ref, bias_ref, o_ref,
                      kw_ref, vw_ref, sc_ref, mx_ref, p_ref, rl_ref):
    kw_ref[0:BAND_PAST] = kp_ref[...]
    kw_ref[BAND_PAST:BAND_PAST + Q_TILE] = kc_ref[...]
    vw_ref[0:BAND_PAST] = vp_ref[...]
    vw_ref[BAND_PAST:BAND_PAST + Q_TILE] = vc_ref[...]
    masks = _head_masks()
    cs = lambda hp: slice(hp * LANES, (hp + 1) * LANES)

    def sub(s, carry):
        r0 = pl.multiple_of(s * Q_SUB, Q_SUB)
        rows = pl.ds(r0, Q_SUB)
        win = pl.ds(r0, K_WIN)
        for hp in range(N_PAIRS):
            sc = _pair_scores(q_ref[rows, cs(hp)], kw_ref[win, cs(hp)], bias_ref[hp], masks)
            sc_ref[hp] = sc
            mx_ref[hp] = _row_max(sc)

        @pl.when(pl.program_id(1) == 0)
        def _():
            col = lax.broadcasted_iota(jnp.int32, (1, K_WIN), 1)
            extra = jnp.where(col + r0 < BAND_PAST, NEG_INF, 0.0)
            for hp in range(N_PAIRS):
                sc = sc_ref[hp] + extra
                sc_ref[hp] = sc
                mx_ref[hp] = _row_max(sc)

        for hp in range(N_PAIRS):
            p_ref[hp], rl_ref[hp] = _softmax_parts(sc_ref[hp], mx_ref[hp])
        for hp in range(N_PAIRS):
            o = _pair_output(p_ref[hp], rl_ref[hp], vw_ref[win, cs(hp)], masks)
            o_ref[rows, cs(hp)] = o.astype(o_ref.dtype)
        return carry

    lax.fori_loop(0, Q_TILE // Q_SUB, sub, 0)


def _rel_bias_tile(rel_bias, n_q, n_k, offset, ok):
    r = np.arange(n_q - 1 + n_k)
    idx = np.clip(offset + n_q - 1 - r, -REL_CLIP, REL_CLIP) + REL_CLIP
    v = jnp.pad(rel_bias[:, idx] * LOG2E, ((0, 0), (0, 1)))
    w = n_q + n_k
    flat = jnp.tile(v, (1, n_q))[:, :n_q * (w - 1)]
    toep = flat.reshape(rel_bias.shape[0], n_q, w - 1)[:, :, n_q - 1:]
    return jnp.where(ok[None], toep, NEG_INF).reshape(N_PAIRS, 2 * n_q, n_k)


def _band_attention_prompt(qkvm, rel_bias, n_batch, seq):
    assert seq % Q_TILE == 0 and Q_TILE % BAND_PAST == 0
    nt = seq // Q_TILE
    qc = np.arange(Q_SUB)[:, None] // CHUNK
    kc = np.arange(K_WIN)[None, :] // CHUNK
    bias = _rel_bias_tile(rel_bias, Q_SUB, K_WIN, BAND_PAST, (kc >= qc) & (kc <= qc + BAND_PAST_CHUNKS))
    blk = (Q_TILE, ATT_WIDTH)
    ratio = Q_TILE // BAND_PAST
    cur = lambda col: pl.BlockSpec(blk, lambda b, t: (b * nt + t, col))
    prev = lambda col: pl.BlockSpec(
        (BAND_PAST, ATT_WIDTH), lambda b, t: (jnp.maximum((b * nt + t) * ratio - 1, b * nt * ratio), col))
    return pl.pallas_call(
        _band_attn_kernel,
        grid=(n_batch, nt),
        in_specs=[cur(0), prev(1), cur(1), prev(2), cur(2),
                  pl.BlockSpec((N_PAIRS, 2 * Q_SUB, K_WIN), lambda b, t: (0, 0, 0))],
        out_specs=pl.BlockSpec(blk, lambda b, t: (b * nt + t, 0)),
        out_shape=jax.ShapeDtypeStruct((n_batch * seq, ATT_WIDTH), BF16),
        scratch_shapes=[pltpu.VMEM((BAND_PAST + Q_TILE, ATT_WIDTH), BF16),
                        pltpu.VMEM((BAND_PAST + Q_TILE, ATT_WIDTH), BF16),
                        pltpu.VMEM((N_PAIRS, 2 * Q_SUB, K_WIN), F32),
                        pltpu.VMEM((N_PAIRS, 2 * Q_SUB, LANES), F32),
                        pltpu.VMEM((N_PAIRS, 2 * Q_SUB, K_WIN), BF16),
                        pltpu.VMEM((N_PAIRS, 2 * Q_SUB, LANES), F32)],
        compiler_params=_params(("parallel", "arbitrary"), VMEM_MID_MIB),
        name="band_attn",
    )(qkvm, qkvm, qkvm, qkvm, qkvm, bias)


def _band_attn_sample_kernel(q_ref, kn_ref, vn_ref, ck_ref, cv_ref, bias_ref, o_ref, kw_ref, vw_ref,
                             *, n_cache, n_new):
    kw_ref[...] = jnp.zeros_like(kw_ref)
    vw_ref[...] = jnp.zeros_like(vw_ref)
    kw_ref[0:n_cache] = ck_ref[0].astype(BF16)
    vw_ref[0:n_cache] = cv_ref[0].astype(BF16)
    kw_ref[n_cache:n_cache + n_new] = kn_ref[...]
    vw_ref[n_cache:n_cache + n_new] = vn_ref[...]
    masks = _head_masks()
    for hp in range(N_PAIRS):
        cs = slice(hp * LANES, (hp + 1) * LANES)
        sc = _pair_scores(q_ref[:, cs], kw_ref[:, cs], bias_ref[hp], masks)
        p, rl = _softmax_parts(sc, _row_max(sc))
        o_ref[:, cs] = _pair_output(p, rl, vw_ref[:, cs], masks).astype(o_ref.dtype)


def _band_attention_sample(qkvm, rel_bias, cache_k, cache_v, n_batch, n_new):
    n_cache = cache_k.shape[1]
    n_keys = -(-(n_cache + n_new) // LANES) * LANES
    q_pos = PAST_LEN + np.arange(n_new)[:, None]
    j = np.arange(n_keys)[None, :]
    k_pos = PAST_LEN - n_cache + j
    ok = ((j < n_cache + n_new) & (k_pos >= 0) & (k_pos // CHUNK <= q_pos // CHUNK)
          & (k_pos // CHUNK >= q_pos // CHUNK - BAND_PAST_CHUNKS))
    bias = _rel_bias_tile(rel_bias, n_new, n_keys, n_cache, ok)
    new = lambda col: pl.BlockSpec((n_new, ATT_WIDTH), lambda b: (b, col))
    cache = pl.BlockSpec((1, n_cache, ATT_WIDTH), lambda b: (b, 0, 0))
    return pl.pallas_call(
        functools.partial(_band_attn_sample_kernel, n_cache=n_cache, n_new=n_new),
        grid=(n_batch,),
        in_specs=[new(0), new(1), new(2), cache, cache,
                  pl.BlockSpec((N_PAIRS, 2 * n_new, n_keys), lambda b: (0, 0, 0))],
        out_specs=pl.BlockSpec((n_new, ATT_WIDTH), lambda b: (b, 0)),
        out_shape=jax.ShapeDtypeStruct((n_batch * n_new, ATT_WIDTH), BF16),
        scratch_shapes=[pltpu.VMEM((n_keys, ATT_WIDTH), BF16),
                        pltpu.VMEM((n_keys, ATT_WIDTH), BF16)],
        compiler_params=_params(("parallel",), VMEM_MID_MIB),
        name="band_attn_sample",
    )(qkvm, qkvm, qkvm, cache_k, cache_v, bias)


def _mem_attn_kernel(q_ref, k_ref, v_ref, o_ref):
    k = k_ref[0].astype(BF16)
    v = v_ref[0].astype(BF16)
    for h in range(MEM_HEADS):
        cs = slice(h * MEM_HEAD_DIM, (h + 1) * MEM_HEAD_DIM)
        sc = _dot_nt(q_ref[:, cs], k[:, cs]) * (MEM_HEAD_DIM ** -0.5 * LOG2E)
        mx = jnp.max(sc, axis=1, keepdims=True)
        p = jnp.exp2(sc - mx)
        l = jnp.sum(p, axis=1, keepdims=True)
        o = _dot(p.astype(BF16), v[:, cs]) * (1.0 / l)
        o_ref[:, cs] = o.astype(o_ref.dtype)


def _memory_attention(qkvm, mem_k, mem_v, n_batch, seq, tq):
    nt = seq // tq
    mem = pl.BlockSpec((1, N_MEM, MEM_WIDTH), lambda b, t: (b, 0, 0))
    return pl.pallas_call(
        _mem_attn_kernel,
        grid=(n_batch, nt),
        in_specs=[pl.BlockSpec((tq, MEM_WIDTH), lambda b, t: (b * nt + t, 3)), mem, mem],
        out_specs=pl.BlockSpec((tq, MEM_WIDTH), lambda b, t: (b * nt + t, 0)),
        out_shape=jax.ShapeDtypeStruct((n_batch * seq, MEM_WIDTH), BF16),
        compiler_params=_params(("parallel", "arbitrary"), VMEM_MID_MIB),
        name="mem_attn",
    )(qkvm, mem_k, mem_v)


def _gate_merge_kernel(u_ref, os_ref, oa_ref, om_ref, wgs_ref, wga_ref, wgm_ref,
                       wbs_ref, wba_ref, wbm_ref, out_ref):
    u = u_ref[...]

    def branch(o_ref, wg_ref, wb_ref):
        return jax.nn.sigmoid(_dot(u, wg_ref[...])) * _dot(o_ref[...], wb_ref[...])

    merged = (branch(os_ref, wgs_ref, wbs_ref) + branch(oa_ref, wga_ref, wba_ref)
              + branch(om_ref, wgm_ref, wbm_ref))
    out_ref[...] = merged.astype(out_ref.dtype)


def _gate_merge(u, o_s, o_a, o_m, w_gate, wb_s, wb_a, wb_m, tm, tn):
    m, d = u.shape
    gate = lambda b: pl.BlockSpec((d, tn), lambda i, n: (0, b * (d // tn) + n))
    wb = pl.BlockSpec((SSM_WIDTH, tn), lambda i, n: (0, n))
    ob = pl.BlockSpec((tm, SSM_WIDTH), lambda i, n: (i, 0))
    return pl.pallas_call(
        _gate_merge_kernel,
        grid=(m // tm, d // tn),
        in_specs=[pl.BlockSpec((tm, d), lambda i, n: (i, 0)), ob, ob, ob,
                  gate(0), gate(1), gate(2), wb, wb, wb],
        out_specs=pl.BlockSpec((tm, tn), lambda i, n: (i, n)),
        out_shape=jax.ShapeDtypeStruct((m, d), BF16),
        compiler_params=_params(("parallel", "arbitrary"), VMEM_BIG_MIB),
        name="gate_merge",
    )(u, o_s, o_a, o_m, w_gate, w_gate, w_gate, wb_s, wb_a, wb_m)


def _out_proj_kernel(x_ref, w_ref, h_ref, g_ref, o_ref):
    o_ref[...] = h_ref[...] + _rms(_dot(x_ref[...], w_ref[...]), g_ref[...])


def _out_proj(x, w_out, h, g_post, tm):
    m, d = h.shape
    row = lambda: pl.BlockSpec((tm, d), lambda i: (i, 0))
    return pl.pallas_call(
        _out_proj_kernel,
        grid=(m // tm,),
        in_specs=[row(), pl.BlockSpec((d, d), lambda i: (0, 0)), row(),
                  pl.BlockSpec((1, d), lambda i: (0, 0))],
        out_specs=row(),
        out_shape=jax.ShapeDtypeStruct((m, d), F32),
        compiler_params=_params(("parallel",), VMEM_BIG_MIB),
        name="out_proj",
    )(x, w_out, h, g_post)


def _layer(x, w, *, n_batch, seq, s0, mem_k, mem_v, cache_k, cache_v, tm, keep):
    n_tok = n_batch * seq
    n_chunks = seq // SSM_CHUNK
    big_tm = min(1024, n_tok)
    h1, u = _ffn(x, w["ffn1_norm_pre"], w["ffn1_norm_post"], w["ffn1_w_gate"], w["ffn1_w_up"],
                 w["ffn1_w_down"], w["mix_norm_pre"], big_tm, 512)

    qkvm, k_keep, v_keep, us_r = _in_proj(u, w["w_qkvm"], w["w_ssm"], tm, seq, keep)
    kv = (k_keep, v_keep)

    nb = 2 if (n_chunks > 1 and n_batch % 2 == 0) else (1 if n_chunks > 1 else n_batch)
    y_s, s_fin = _ssm_core(us_r, s0, w["ssm_bd"], w["ssm_cd"], w["ssm_pw"], w["ssm_d"], n_batch,
                           n_chunks, nb)
    o_s = _ssm_glu(y_s, w["ssm_w_glu"], w["ssm_b_glu"], big_tm)

    if cache_k is None:
        o_a = _band_attention_prompt(qkvm, w["att_rel_bias"], n_batch, seq)
    else:
        o_a = _band_attention_sample(qkvm, w["att_rel_bias"], cache_k, cache_v, n_batch, seq)
    o_m = _memory_attention(qkvm, mem_k, mem_v, n_batch, seq, min(4096, seq))

    merged = _gate_merge(u, o_s, o_a, o_m, w["w_gate"], w["w_branch_ssm"], w["w_branch_att"],
                         w["w_branch_mem"], big_tm, 512)
    h2 = _out_proj(merged, w["w_out"], h1, w["mix_norm_post"], tm)
    y = _ffn(h2, w["ffn2_norm_pre"], w["ffn2_norm_post"], w["ffn2_w_gate"], w["ffn2_w_up"],
             w["ffn2_w_down"], None, big_tm, 512)
    return y, kv, s_fin


def kernel(x_prompt, x_sample, mem_prompt, cache_att_k, cache_att_v, cache_mem_k, cache_mem_v, state_ssm_re, state_ssm_im, ffn1_norm_pre, ffn1_norm_post, ffn1_w_gate, ffn1_w_up, ffn1_w_down, mix_norm_pre, mix_norm_post, w_in, ssm_a_re, ssm_a_im, ssm_log_dt, ssm_b_re, ssm_b_im, ssm_c_re, ssm_c_im, ssm_d, ssm_w_glu, ssm_b_glu, att_rel_bias, mem_norm, w_mem_k, w_mem_v, w_branch_ssm, w_branch_att, w_branch_mem, w_out, ffn2_norm_pre, ffn2_norm_post, ffn2_w_gate, ffn2_w_up, ffn2_w_down):
    n_bp, t_p, d = x_prompt.shape
    n_bs, t_s, _ = x_sample.shape
    depth = ffn1_norm_pre.shape[0]
    assert depth == 1 and d == D_MODEL
    keep = min(BAND_PAST, t_p)
    l = 0

    vec = lambda a: a[l].reshape(1, -1).astype(F32)
    mat = lambda a: _cast_bf16(a[l])
    bd, cd, pw = _ssm_params(ssm_a_re[l], ssm_a_im[l], ssm_log_dt[l], ssm_b_re[l],
                             ssm_b_im[l], ssm_c_re[l], ssm_c_im[l])
    w = {
        "ffn1_norm_pre": vec(ffn1_norm_pre), "ffn1_norm_post": vec(ffn1_norm_post),
        "ffn1_w_gate": mat(ffn1_w_gate), "ffn1_w_up": mat(ffn1_w_up), "ffn1_w_down": mat(ffn1_w_down),
        "mix_norm_pre": vec(mix_norm_pre), "mix_norm_post": vec(mix_norm_post),
        "w_ssm": _cast_bf16(w_in[l], 0, COL_Q),
        "w_qkvm": _cast_bf16(w_in[l], COL_Q, COL_GATE - COL_Q),
        "w_gate": _cast_bf16(w_in[l], COL_GATE),
        "ssm_bd": bd, "ssm_cd": cd, "ssm_pw": pw,
        "ssm_d": ssm_d[l].reshape(OCTETS, 1, LANES).astype(F32), "ssm_w_glu": mat(ssm_w_glu), "ssm_b_glu": vec(ssm_b_glu),
        "att_rel_bias": att_rel_bias[l].astype(F32),
        "w_branch_ssm": mat(w_branch_ssm), "w_branch_att": mat(w_branch_att),
        "w_branch_mem": mat(w_branch_mem), "w_out": mat(w_out),
        "ffn2_norm_pre": vec(ffn2_norm_pre), "ffn2_norm_post": vec(ffn2_norm_post),
        "ffn2_w_gate": mat(ffn2_w_gate), "ffn2_w_up": mat(ffn2_w_up), "ffn2_w_down": mat(ffn2_w_down),
    }

    w_mem = jnp.concatenate([w_mem_k[l], w_mem_v[l]], axis=1).astype(BF16)
    mk_p, mv_p = _memory_kv(mem_prompt.reshape(n_bp * N_MEM, d), vec(mem_norm), w_mem, 512)
    mk_p = mk_p.reshape(n_bp, N_MEM, MEM_WIDTH)
    mv_p = mv_p.reshape(n_bp, N_MEM, MEM_WIDTH)
    zero_state = jnp.zeros((OCTETS, n_bp, OCT_STATE), F32)
    y_p, kv_p, sfin_p = _layer(x_prompt.reshape(n_bp * t_p, d), w, n_batch=n_bp, seq=t_p, s0=zero_state,
                               mem_k=mk_p, mem_v=mv_p, cache_k=None, cache_v=None, tm=512, keep=keep)
    sre_p, sim_p = _state_from_octets(sfin_p)

    n_cache = cache_att_k.shape[2]
    s0_s = _state_to_octets(state_ssm_re[l].astype(F32), state_ssm_im[l].astype(F32))
    y_s, kv_s, sfin_s = _layer(x_sample.reshape(n_bs * t_s, d), w, n_batch=n_bs, seq=t_s, s0=s0_s,
                               mem_k=cache_mem_k[l].reshape(n_bs, N_MEM, MEM_WIDTH),
                               mem_v=cache_mem_v[l].reshape(n_bs, N_MEM, MEM_WIDTH),
                               cache_k=cache_att_k[l].reshape(n_bs, n_cache, ATT_WIDTH),
                               cache_v=cache_att_v[l].reshape(n_bs, n_cache, ATT_WIDTH),
                               tm=n_bs * t_s, keep=t_s)
    sre_s, sim_s = _state_from_octets(sfin_s)

    def heads(a, nb, t):
        if a.ndim == 3:
            return jnp.transpose(a.reshape(1, nb, ATT_HEADS, ATT_HEAD_DIM, t), (0, 1, 4, 2, 3))
        return a.reshape(1, nb, t, ATT_HEADS, ATT_HEAD_DIM)

    memh = lambda a: a.reshape(1, n_bp, N_MEM, MEM_HEADS, MEM_HEAD_DIM)
    return (y_p.reshape(n_bp, t_p, d), y_s.reshape(n_bs, t_s, d),
            heads(kv_p[0], n_bp, keep), heads(kv_p[1], n_bp, keep),
            memh(mk_p), memh(mv_p), sre_p[None], sim_p[None],
            heads(kv_s[0], n_bs, t_s), heads(kv_s[1], n_bs, t_s),
            sre_s[None], sim_s[None])
```

```python
import functools
import math

import numpy as np
import jax
import jax.numpy as jnp
from jax import lax
from jax.experimental import pallas as pl
from jax.experimental.pallas import tpu as pltpu

F32 = jnp.float32
BF16 = jnp.bfloat16

D_MODEL = 2048
CHUNK = 64
BAND_PAST_CHUNKS = 8
BAND_PAST = BAND_PAST_CHUNKS * CHUNK
ATT_HEADS = 16
ATT_HEAD_DIM = 64
ATT_WIDTH = ATT_HEADS * ATT_HEAD_DIM
REL_CLIP = 128
SSM_GROUP = 16
SSM_WIDTH = 1024
SSM_GROUPS = SSM_WIDTH // SSM_GROUP
SSM_STATE = 64
N_MEM = 256
MEM_HEADS = 4
MEM_HEAD_DIM = 256
MEM_WIDTH = MEM_HEADS * MEM_HEAD_DIM
EPS = 1e-6
NEG_INF = -1e30
PAST_LEN = 4096
LOG2E = math.log2(math.e)

COL_Q = SSM_WIDTH
COL_GATE = SSM_WIDTH + 3 * ATT_WIDTH + MEM_WIDTH

SSM_CHUNK = 16
OCTETS = 8
OCT_GROUPS = SSM_GROUPS // OCTETS
LANES = 128
MXU_DIM = 256
OCT_COLS = SSM_CHUNK * LANES
OCT_HALF = OCT_GROUPS * SSM_STATE
OCT_STATE = 2 * OCT_HALF

Q_TILE = 512
Q_SUB = 128
K_WIN = Q_SUB + BAND_PAST

MIB = 1024 * 1024
VMEM_BIG_MIB = 56
VMEM_MID_MIB = 48
VMEM_SMALL_MIB = 32


def _params(sem, vmem_mib):
    return pltpu.CompilerParams(dimension_semantics=sem, vmem_limit_bytes=vmem_mib * MIB)


def _dot(a, b):
    return jnp.dot(a, b, preferred_element_type=F32)


def _dot_nt(a, b):
    return lax.dot_general(a, b, (((1,), (1,)), ((), ())), preferred_element_type=F32)


def _rms(xf, g):
    y = xf * lax.rsqrt(jnp.mean(xf * xf, axis=-1, keepdims=True) + EPS)
    return y * g


CAST_BLOCK_BYTES = 8 * MIB


def _cast_kernel(x_ref, o_ref):
    o_ref[...] = x_ref[...].astype(o_ref.dtype)


def _cast_bf16(w, col0=0, n_cols=None):
    rows, cols = w.shape
    n_cols = cols - col0 if n_cols is None else n_cols
    cw = math.gcd(col0, n_cols)
    assert cw % LANES == 0 and col0 + n_cols <= cols
    tr = rows
    while tr * cw * 4 > CAST_BLOCK_BYTES and tr % 32 == 0:
        tr //= 2
    cb0 = col0 // cw
    return pl.pallas_call(
        _cast_kernel,
        grid=(rows // tr, n_cols // cw),
        in_specs=[pl.BlockSpec((tr, cw), lambda i, j: (i, cb0 + j))],
        out_specs=pl.BlockSpec((tr, cw), lambda i, j: (i, j)),
        out_shape=jax.ShapeDtypeStruct((rows, n_cols), BF16),
        compiler_params=_params(("parallel", "parallel"), VMEM_SMALL_MIB),
        name="cast_bf16",
    )(w)


FFN_SLICES = 8


def _ffn_kernel(hp_ref, hn_ref, gpre_ref, gpost_ref, wg_ref, wu_ref, wd_ref, *rest,
                n_tiles, emit_next):
    if emit_next:
        gnext_ref, out_ref, nxt_ref, *scratch = rest
    else:
        out_ref, *scratch = rest
    xn_refs, acc_refs = scratch[:2], scratch[2:]
    r = pl.program_id(0)
    f = pl.program_id(1)
    rs = hp_ref.shape[0]
    rows = pl.ds(pl.multiple_of(jnp.minimum(f, FFN_SLICES - 1) * rs, rs), rs)

    def pre_norm_slice(slot):
        xn_refs[slot][rows, :] = _rms(hn_ref[...], gpre_ref[...]).astype(BF16)

    def finish_slice(slot):
        hn = hp_ref[...] + 0.5 * _rms(acc_refs[slot][rows, :], gpost_ref[...])
        out_ref[...] = hn
        if emit_next:
            nxt_ref[...] = _rms(hn, gnext_ref[...]).astype(BF16)

    def matmul_chunk(slot):
        xn = xn_refs[slot][...]
        g = _dot(xn, wg_ref[...])
        u = _dot(xn, wu_ref[...])
        a = (g * jax.nn.sigmoid(g)) * u
        d = _dot(a.astype(BF16), wd_ref[...])
        acc_refs[slot][...] = jnp.where(f == 0, d, acc_refs[slot][...] + d)

    @pl.when((r == 0) & (f == 0))
    def _():
        acc_refs[1][...] = jnp.zeros_like(acc_refs[1])

    @pl.when(r == 0)
    def _():
        pre_norm_slice(0)

    for parity in range(2):
        @pl.when((r >= 1) & (r <= n_tiles) & (lax.rem(r, 2) == parity))
        def _():
            finish_slice(parity)
            pre_norm_slice(parity)
            matmul_chunk(1 - parity)

    @pl.when(r == n_tiles + 1)
    def _():
        finish_slice((n_tiles + 1) % 2)


def _ffn(h, g_pre, g_post, wg, wu, wd, g_next, tm, tf):
    emit_next = g_next is not None
    m, d = h.shape
    f_dim = wg.shape[1]
    nf = f_dim // tf
    n = m // tm
    rs = tm // FFN_SLICES
    assert m % tm == 0 and f_dim % tf == 0 and tm % FFN_SLICES == 0 and nf >= FFN_SLICES
    sl = lambda f: jnp.minimum(f, FFN_SLICES - 1)
    done = lambda r, f: (jnp.maximum(r - 2, 0) * FFN_SLICES + jnp.where(r >= 2, sl(f), 0), 0)
    ahead = lambda r, f: (jnp.minimum(r, n - 1) * FFN_SLICES + sl(f), 0)
    chunk = lambda r, f: jnp.where((r >= 1) & (r <= n), f, 0)
    vec = pl.BlockSpec((1, d), lambda r, f: (0, 0))
    in_specs = [pl.BlockSpec((rs, d), done), pl.BlockSpec((rs, d), ahead), vec, vec,
                pl.BlockSpec((d, tf), lambda r, f: (0, chunk(r, f))),
                pl.BlockSpec((d, tf), lambda r, f: (0, chunk(r, f))),
                pl.BlockSpec((tf, d), lambda r, f: (chunk(r, f), 0))]
    args = [h, h, g_pre, g_post, wg, wu, wd]
    out_shape = [jax.ShapeDtypeStruct((m, d), F32)]
    out_specs = [pl.BlockSpec((rs, d), done)]
    if emit_next:
        in_specs.append(vec)
        args.append(g_next)
        out_shape.append(jax.ShapeDtypeStruct((m, d), BF16))
        out_specs.append(pl.BlockSpec((rs, d), done))
    res = pl.pallas_call(
        functools.partial(_ffn_kernel, n_tiles=n, emit_next=emit_next),
        grid=(n + 2, nf),
        in_specs=in_specs,
        out_specs=out_specs,
        out_shape=out_shape,
        scratch_shapes=[pltpu.VMEM((tm, d), BF16), pltpu.VMEM((tm, d), BF16),
                        pltpu.VMEM((tm, d), F32), pltpu.VMEM((tm, d), F32)],
        compiler_params=_params(("arbitrary", "arbitrary"), VMEM_BIG_MIB),
        name="ffn",
    )(*args)
    return res if emit_next else res[0]


def _memory_kv_kernel(x_ref, g_ref, w_ref, k_ref, v_ref):
    x = _rms(x_ref[...], g_ref[...]).astype(BF16)
    k_ref[...] = _dot(x, w_ref[:, :MEM_WIDTH])
    v_ref[...] = _dot(x, w_ref[:, MEM_WIDTH:])


def _memory_kv(x, gain, w, tm):
    m, k = x.shape
    tm = min(tm, m)
    assert m % tm == 0 and w.shape == (k, 2 * MEM_WIDTH)
    out = pl.BlockSpec((tm, MEM_WIDTH), lambda i: (i, 0))
    return pl.pallas_call(
        _memory_kv_kernel,
        grid=(m // tm,),
        in_specs=[pl.BlockSpec((tm, k), lambda i: (i, 0)), pl.BlockSpec((1, k), lambda i: (0, 0)),
                  pl.BlockSpec((k, 2 * MEM_WIDTH), lambda i: (0, 0))],
        out_specs=[out, out],
        out_shape=[jax.ShapeDtypeStruct((m, MEM_WIDTH), F32)] * 2,
        compiler_params=_params(("parallel",), VMEM_MID_MIB),
        name="memory_kv",
    )(x, gain, w)


def _in_proj_kernel(x_ref, wq_ref, ws_ref, o_ref, k_ref, v_ref, us_ref, nat_ref):
    x = x_ref[...]
    for j, keep_ref in enumerate((None, k_ref, v_ref, None)):
        cs = slice(j * ATT_WIDTH, (j + 1) * ATT_WIDTH)
        r = _dot(x, wq_ref[:, cs])
        o_ref[:, cs] = r.astype(o_ref.dtype)
        if keep_ref is not None:
            keep_ref[...] = r
    r = _dot(x, ws_ref[...])
    tc = nat_ref.shape[1] // SSM_CHUNK
    for m in range(OCTETS):
        nat_ref[m] = r[:, m * LANES:(m + 1) * LANES]
    for j in range(SSM_CHUNK):
        for m in range(OCTETS):
            us_ref[m, :, j * LANES:(j + 1) * LANES] = nat_ref[m, pl.ds(j, tc, stride=SSM_CHUNK), :]


def _in_proj(u, w_qkvm, w_ssm, tm, seq, keep):
    n_tok = u.shape[0]
    assert n_tok % tm == 0 and tm % SSM_CHUNK == 0 and MEM_WIDTH == ATT_WIDTH
    tc = tm // SSM_CHUNK
    if keep == seq:
        kept_rows, kept = n_tok, pl.BlockSpec((tm, ATT_WIDTH), lambda i: (i, 0))
    else:
        assert keep == tm and seq % tm == 0
        kept_rows, kept = (n_tok // seq) * keep, pl.BlockSpec((tm, ATT_WIDTH), lambda i: (i // (seq // tm), 0))
    return pl.pallas_call(
        _in_proj_kernel,
        grid=(n_tok // tm,),
        in_specs=[pl.BlockSpec((tm, D_MODEL), lambda i: (i, 0)),
                  pl.BlockSpec((D_MODEL, 4 * ATT_WIDTH), lambda i: (0, 0)),
                  pl.BlockSpec((D_MODEL, SSM_WIDTH), lambda i: (0, 0))],
        out_specs=[pl.BlockSpec((tm, 4 * ATT_WIDTH), lambda i: (i, 0)), kept, kept,
                   pl.BlockSpec((OCTETS, tc, OCT_COLS), lambda i: (0, i, 0))],
        out_shape=[jax.ShapeDtypeStruct((n_tok, 4 * ATT_WIDTH), BF16),
                   jax.ShapeDtypeStruct((kept_rows, ATT_WIDTH), F32),
                   jax.ShapeDtypeStruct((kept_rows, ATT_WIDTH), F32),
                   jax.ShapeDtypeStruct((OCTETS, n_tok // SSM_CHUNK, OCT_COLS), F32)],
        scratch_shapes=[pltpu.VMEM((OCTETS, tm, LANES), F32)],
        compiler_params=_params(("arbitrary",), VMEM_BIG_MIB),
        name="in_proj",
    )(u, w_qkvm, w_ssm)


def _ssm_params(a_re, a_im, log_dt, b_re, b_im, c_re, c_im):
    dt = jnp.exp(log_dt)[:, None]
    mag = jnp.exp(a_re * dt)
    ab_re = mag * jnp.cos(a_im * dt)
    ab_im = mag * jnp.sin(a_im * dt)
    den = a_re * a_re + a_im * a_im
    n_re = ab_re - 1.0
    n_im = ab_im
    k_re = (n_re * a_re + n_im * a_im) / den
    k_im = (n_im * a_re - n_re * a_im) / den
    bb_re = k_re[..., None] * b_re - k_im[..., None] * b_im
    bb_im = k_re[..., None] * b_im + k_im[..., None] * b_re
    pr = [jnp.ones_like(ab_re)]
    pi = [jnp.zeros_like(ab_re)]
    for _ in range(SSM_CHUNK):
        pr.append(pr[-1] * ab_re - pi[-1] * ab_im)
        pi.append(pr[-2] * ab_im + pi[-1] * ab_re)
    n_pw = SSM_CHUNK + 1
    pw = jnp.concatenate([jnp.stack(pr).reshape(n_pw, OCTETS, OCT_HALF),
                          jnp.stack(pi).reshape(n_pw, OCTETS, OCT_HALF)], axis=2)
    pw = jnp.transpose(pw, (1, 0, 2))
    eye = jnp.eye(OCT_GROUPS, dtype=F32)

    def expand(x):
        x4 = x.reshape(OCTETS, OCT_GROUPS, x.shape[1], SSM_STATE)
        out = eye[None, :, None, :, None] * x4[:, :, :, None, :]
        return out.reshape(OCTETS, OCT_GROUPS * x.shape[1], OCT_HALF)

    bd = jnp.concatenate([expand(jnp.transpose(bb_re, (0, 2, 1))),
                          expand(jnp.transpose(bb_im, (0, 2, 1)))], axis=2)
    cd = jnp.concatenate([expand(c_re), expand(c_im)], axis=2)
    return bd, cd, pw


def _state_to_octets(s_re, s_im):
    b = s_re.shape[0]
    s = jnp.stack([s_re, s_im], 0).reshape(2, b, OCTETS, OCT_GROUPS, SSM_STATE)
    return jnp.transpose(s, (2, 1, 0, 3, 4)).reshape(OCTETS, b, OCT_STATE)


def _state_from_octets(s):
    b = s.shape[1]
    s = s.reshape(OCTETS, b, 2, OCT_GROUPS, SSM_STATE)
    s = jnp.transpose(s, (2, 1, 0, 3, 4)).reshape(2, b, SSM_GROUPS, SSM_STATE)
    return s[0], s[1]


def _split_bf16(x):
    hi = x.astype(BF16)
    return hi, (x - hi.astype(F32)).astype(BF16)


def _ssm_build_operators(bd_ref, cd_ref, pw_ref, t8_ref, wus_ref, wso_ref):
    bd_re, bd_im = bd_ref[0, :, :OCT_HALF], bd_ref[0, :, OCT_HALF:]
    cd_re, cd_im = cd_ref[0, :, :OCT_HALF], cd_ref[0, :, OCT_HALF:]
    c_hi, c_lo = _split_bf16(jnp.concatenate([cd_re, -cd_im], axis=1))
    blk = lambda i: slice(i * LANES, (i + 1) * LANES)
    for a in range(SSM_CHUNK // 2):
        t8_ref[blk(2 * a + 1), blk(2 * a)] = jnp.zeros((LANES, LANES), BF16)
    for k in range(SSM_CHUNK + 1):
        p_re = pw_ref[0, k:k + 1, :OCT_HALF]
        p_im = pw_ref[0, k:k + 1, OCT_HALF:]
        if k < SSM_CHUNK:
            e = jnp.concatenate([bd_re * p_re - bd_im * p_im, bd_re * p_im + bd_im * p_re], axis=1)
            wus_ref[blk(SSM_CHUNK - 1 - k), :] = e.astype(BF16)
            e_hi, e_lo = _split_bf16(e)
            lag = (_dot_nt(e_hi, c_hi) + _dot_nt(e_hi, c_lo) + _dot_nt(e_lo, c_hi)).astype(BF16)
            for j in range(SSM_CHUNK - k):
                t8_ref[blk(j), blk(j + k)] = lag
        if k >= 1:
            g = jnp.concatenate([cd_re * p_re - cd_im * p_im, -(cd_re * p_im + cd_im * p_re)], axis=1)
            wso_ref[blk(k - 1), :] = g.astype(BF16)


def _ssm_core_kernel(us_ref, s0_ref, bd_ref, cd_ref, pw_ref, d_ref, y_ref, sfin_ref,
                     t8_ref, wus_ref, wso_ref, ds_ref, sp_ref, *, nb, n_chunks):
    @pl.when(pl.program_id(1) == 0)
    def _():
        _ssm_build_operators(bd_ref, cd_ref, pw_ref, t8_ref, wus_ref, wso_ref)

    ub = us_ref[0].astype(BF16)
    ds = _dot(ub, wus_ref[...])
    a16 = pw_ref[0, SSM_CHUNK:SSM_CHUNK + 1, :]
    n_blk = OCT_HALF // LANES
    blk = lambda k: slice(k * LANES, (k + 1) * LANES)

    def advance(s_re, s_im, d_re, d_im, a_re, a_im):
        return a_re * s_re - a_im * s_im + d_re, a_re * s_im + a_im * s_re + d_im

    if n_chunks == 1:
        s0 = s0_ref[0, 0]
        n_re, n_im = advance(s0[:, :OCT_HALF], s0[:, OCT_HALF:], ds[:, :OCT_HALF], ds[:, OCT_HALF:],
                             a16[:, :OCT_HALF], a16[:, OCT_HALF:])
        sfin_ref[0, 0] = jnp.concatenate([n_re, n_im], axis=1)
        spb = s0.astype(BF16)
    else:
        for k in range(2 * n_blk):
            ds_ref[k] = ds[:, blk(k)]

        def body(c, carry):
            at = pl.ds(c, nb, stride=n_chunks)
            new = list(carry)
            for k in range(n_blk):
                sp_ref[k, at, :] = carry[k]
                sp_ref[n_blk + k, at, :] = carry[n_blk + k]
                new[k], new[n_blk + k] = advance(carry[k], carry[n_blk + k], ds_ref[k, at, :],
                                                 ds_ref[n_blk + k, at, :], a16[:, blk(k)],
                                                 a16[:, blk(n_blk + k)])
            return tuple(new)

        init = tuple(s0_ref[0, 0, :, blk(k)] for k in range(2 * n_blk))
        fin = lax.fori_loop(0, n_chunks, body, init, unroll=2)
        for k in range(2 * n_blk):
            sfin_ref[0, 0, :, blk(k)] = fin[k]
        spb = jnp.concatenate([sp_ref[k] for k in range(2 * n_blk)], axis=1).astype(BF16)
    rows = ub.shape[0]
    d_skip = d_ref[0]
    for nt in range(OCT_COLS // MXU_DIM):
        k_hi = (nt + 1) * MXU_DIM
        cs = slice(nt * MXU_DIM, k_hi)
        y = _dot(ub[:, :k_hi], t8_ref[:k_hi, cs]) + _dot_nt(spb, wso_ref[cs, :])
        for j in range(MXU_DIM // LANES):
            i = nt * (MXU_DIM // LANES) + j
            y_ref[pl.ds(i, rows, stride=SSM_CHUNK), :] = (y[:, blk(j)] + d_skip * us_ref[0, :, blk(i)])


def _ssm_core(us_r, s0, bd, cd, pw, d_oct, n_batch, n_chunks, nb):
    nc = us_r.shape[1]
    rows = nb * n_chunks
    assert n_batch % nb == 0 and nc == n_batch * n_chunks
    nr = n_batch // nb
    s0 = s0.reshape(OCTETS, nr, nb, OCT_STATE)
    par = lambda rows_: pl.BlockSpec((1, rows_, OCT_STATE), lambda m, r: (m, 0, 0))
    y, sfin = pl.pallas_call(
        functools.partial(_ssm_core_kernel, nb=nb, n_chunks=n_chunks),
        grid=(OCTETS, nr),
        in_specs=[pl.BlockSpec((1, rows, OCT_COLS), lambda m, r: (m, r, 0)),
                  pl.BlockSpec((1, 1, nb, OCT_STATE), lambda m, r: (m, r, 0, 0)),
                  par(LANES), par(LANES), par(SSM_CHUNK + 1),
                  pl.BlockSpec((1, 1, LANES), lambda m, r: (m, 0, 0))],
        out_specs=[pl.BlockSpec((rows * SSM_CHUNK, LANES), lambda m, r: (r, m)),
                   pl.BlockSpec((1, 1, nb, OCT_STATE), lambda m, r: (m, r, 0, 0))],
        out_shape=[jax.ShapeDtypeStruct((nc * SSM_CHUNK, SSM_WIDTH), F32),
                   jax.ShapeDtypeStruct((OCTETS, nr, nb, OCT_STATE), F32)],
        scratch_shapes=[pltpu.VMEM((OCT_COLS, OCT_COLS), BF16),
                        pltpu.VMEM((OCT_COLS, OCT_STATE), BF16),
                        pltpu.VMEM((OCT_COLS, OCT_STATE), BF16),
                        pltpu.VMEM((OCT_STATE // LANES, rows, LANES), F32),
                        pltpu.VMEM((OCT_STATE // LANES, rows, LANES), F32)],
        compiler_params=_params(("arbitrary", "arbitrary"), VMEM_BIG_MIB),
        name="ssm_core",
    )(us_r, s0, bd, cd, pw, d_oct)
    return y, sfin.reshape(OCTETS, n_batch, OCT_STATE)


def _gelu_tanh(x):
    c = math.sqrt(2.0 / math.pi)
    return x * (0.5 * (1.0 + jnp.tanh(c * (x + 0.044715 * (x * x * x)))))


def _ssm_glu_kernel(y_ref, w_ref, b_ref, o_ref):
    yg = _gelu_tanh(y_ref[...])
    z = _dot(yg.astype(BF16), w_ref[...]) + b_ref[...]
    o_ref[...] = (yg * jax.nn.sigmoid(z)).astype(o_ref.dtype)


def _ssm_glu(y, w_glu, b_glu, tm):
    n_tok = y.shape[0]
    assert n_tok % tm == 0
    row = pl.BlockSpec((tm, SSM_WIDTH), lambda i: (i, 0))
    return pl.pallas_call(
        _ssm_glu_kernel,
        grid=(n_tok // tm,),
        in_specs=[row, pl.BlockSpec((SSM_WIDTH, SSM_WIDTH), lambda i: (0, 0)),
                  pl.BlockSpec((1, SSM_WIDTH), lambda i: (0, 0))],
        out_specs=row,
        out_shape=jax.ShapeDtypeStruct((n_tok, SSM_WIDTH), BF16),
        compiler_params=_params(("parallel",), VMEM_MID_MIB),
        name="ssm_glu",
    )(y, w_glu, b_glu)


N_PAIRS = ATT_HEADS // 2


def _head_masks():
    lane = lax.broadcasted_iota(jnp.int32, (1, LANES), 1)
    return (lane < ATT_HEAD_DIM, lane >= ATT_HEAD_DIM)


def _pair_scores(q2, kw, bias2, masks):
    qq = jnp.concatenate([jnp.where(m, q2, jnp.zeros_like(q2)) for m in masks], axis=0)
    return _dot_nt(qq, kw) * (ATT_HEAD_DIM ** -0.5 * LOG2E) + bias2


def _row_max(sc):
    return jnp.broadcast_to(jnp.max(sc, axis=1, keepdims=True), (sc.shape[0], LANES))


def _softmax_parts(sc, mx):
    p = jnp.exp2(sc - jnp.concatenate([mx] * (sc.shape[1] // LANES), axis=1))
    l = jnp.sum(p, axis=1, keepdims=True)
    return p.astype(BF16), jnp.broadcast_to(1.0 / l, (sc.shape[0], LANES))


def _pair_output(p, rl, vw, masks):
    o2 = _dot(p, vw) * rl
    n_q = o2.shape[0] // 2
    return jnp.where(masks[0], o2[:n_q], o2[n_q:])


def _band_attn_kernel(q_ref, kp_ref, kc_ref, vp_ref, vc_ref, bias_ref, o_ref,
                      kw_ref, vw_ref, sc_ref, mx_ref, p_ref, rl_ref):
    kw_ref[0:BAND_PAST] = kp_ref[...]
    kw_ref[BAND_PAST:BAND_PAST + Q_TILE] = kc_ref[...]
    vw_ref[0:BAND_PAST] = vp_ref[...]
    vw_ref[BAND_PAST:BAND_PAST + Q_TILE] = vc_ref[...]
    masks = _head_masks()
    cs = lambda hp: slice(hp * LANES, (hp + 1) * LANES)

    def sub(s, carry):
        r0 = pl.multiple_of(s * Q_SUB, Q_SUB)
        rows = pl.ds(r0, Q_SUB)
        win = pl.ds(r0, K_WIN)
        for hp in range(N_PAIRS):
            sc = _pair_scores(q_ref[rows, cs(hp)], kw_ref[win, cs(hp)], bias_ref[hp], masks)
            sc_ref[hp] = sc
            mx_ref[hp] = _row_max(sc)

        @pl.when(pl.program_id(1) == 0)
        def _():
            col = lax.broadcasted_iota(jnp.int32, (1, K_WIN), 1)
            extra = jnp.where(col + r0 < BAND_PAST, NEG_INF, 0.0)
            for hp in range(N_PAIRS):
                sc = sc_ref[hp] + extra
                sc_ref[hp] = sc
                mx_ref[hp] = _row_max(sc)

        for hp in range(N_PAIRS):
            p_ref[hp], rl_ref[hp] = _softmax_parts(sc_ref[hp], mx_ref[hp])
        for hp in range(N_PAIRS):
            o = _pair_output(p_ref[hp], rl_ref[hp], vw_ref[win, cs(hp)], masks)
            o_ref[rows, cs(hp)] = o.astype(o_ref.dtype)
        return carry

    lax.fori_loop(0, Q_TILE // Q_SUB, sub, 0)


def _rel_bias_tile(rel_bias, n_q, n_k, offset, ok):
    r = np.arange(n_q - 1 + n_k)
    idx = np.clip(offset + n_q - 1 - r, -REL_CLIP, REL_CLIP) + REL_CLIP
    v = jnp.pad(rel_bias[:, idx] * LOG2E, ((0, 0), (0, 1)))
    w = n_q + n_k
    flat = jnp.tile(v, (1, n_q))[:, :n_q * (w - 1)]
    toep = flat.reshape(rel_bias.shape[0], n_q, w - 1)[:, :, n_q - 1:]
    return jnp.where(ok[None], toep, NEG_INF).reshape(N_PAIRS, 2 * n_q, n_k)


def _band_attention_prompt(qkvm, rel_bias, n_batch, seq):
    assert seq % Q_TILE == 0 and Q_TILE % BAND_PAST == 0
    nt = seq // Q_TILE
    qc = np.arange(Q_SUB)[:, None] // CHUNK
    kc = np.arange(K_WIN)[None, :] // CHUNK
    bias = _rel_bias_tile(rel_bias, Q_SUB, K_WIN, BAND_PAST, (kc >= qc) & (kc <= qc + BAND_PAST_CHUNKS))
    blk = (Q_TILE, ATT_WIDTH)
    ratio = Q_TILE // BAND_PAST
    cur = lambda col: pl.BlockSpec(blk, lambda b, t: (b * nt + t, col))
    prev = lambda col: pl.BlockSpec(
        (BAND_PAST, ATT_WIDTH), lambda b, t: (jnp.maximum((b * nt + t) * ratio - 1, b * nt * ratio), col))
    return pl.pallas_call(
        _band_attn_kernel,
        grid=(n_batch, nt),
        in_specs=[cur(0), prev(1), cur(1), prev(2), cur(2),
                  pl.BlockSpec((N_PAIRS, 2 * Q_SUB, K_WIN), lambda b, t: (0, 0, 0))],
        out_specs=pl.BlockSpec(blk, lambda b, t: (b * nt + t, 0)),
        out_shape=jax.ShapeDtypeStruct((n_batch * seq, ATT_WIDTH), BF16),
        scratch_shapes=[pltpu.VMEM((BAND_PAST + Q_TILE, ATT_WIDTH), BF16),
                        pltpu.VMEM((BAND_PAST + Q_TILE, ATT_WIDTH), BF16),
                        pltpu.VMEM((N_PAIRS, 2 * Q_SUB, K_WIN), F32),
                        pltpu.VMEM((N_PAIRS, 2 * Q_SUB, LANES), F32),
                        pltpu.VMEM((N_PAIRS, 2 * Q_SUB, K_WIN), BF16),
                        pltpu.VMEM((N_PAIRS, 2 * Q_SUB, LANES), F32)],
        compiler_params=_params(("parallel", "arbitrary"), VMEM_MID_MIB),
        name="band_attn",
    )(qkvm, qkvm, qkvm, qkvm, qkvm, bias)


def _band_attn_sample_kernel(q_ref, kn_ref, vn_ref, ckt_ref, cvt_ref, bias_ref, o_ref, kn_pad_ref,
                             vn_pad_ref, *, n_cache, n_new):
    kn_pad_ref[...] = jnp.zeros_like(kn_pad_ref)
    vn_pad_ref[...] = jnp.zeros_like(vn_pad_ref)
    kn_pad_ref[0:n_new] = kn_ref[...]
    vn_pad_ref[0:n_new] = vn_ref[...]
    masks = _head_masks()
    for hp in range(N_PAIRS):
        cs = slice(hp * LANES, (hp + 1) * LANES)
        q2 = q_ref[:, cs]
        qq = jnp.concatenate([jnp.where(m, q2, jnp.zeros_like(q2)) for m in masks], axis=0)
        raw = jnp.concatenate([_dot(qq, ckt_ref[0, hp].astype(BF16)), _dot_nt(qq, kn_pad_ref[:, cs])],
                              axis=1)
        sc = raw * (ATT_HEAD_DIM ** -0.5 * LOG2E) + bias_ref[hp]
        p, rl = _softmax_parts(sc, _row_max(sc))
        o2 = (_dot_nt(p[:, :n_cache], cvt_ref[0, hp].astype(BF16))
              + _dot(p[:, n_cache:], vn_pad_ref[:, cs])) * rl
        o_ref[:, cs] = jnp.where(masks[0], o2[:n_new], o2[n_new:]).astype(o_ref.dtype)


def _band_attention_sample(qkvm, rel_bias, cache_kt, cache_vt, n_batch, n_new):
    n_cache = cache_kt.shape[3]
    n_keys = n_cache + LANES
    q_pos = PAST_LEN + np.arange(n_new)[:, None]
    j = np.arange(n_keys)[None, :]
    k_pos = PAST_LEN - n_cache + j
    ok = ((j < n_cache + n_new) & (k_pos >= 0) & (k_pos // CHUNK <= q_pos // CHUNK)
          & (k_pos // CHUNK >= q_pos // CHUNK - BAND_PAST_CHUNKS))
    bias = _rel_bias_tile(rel_bias, n_new, n_keys, n_cache, ok)
    new = lambda col: pl.BlockSpec((n_new, ATT_WIDTH), lambda b: (b, col))
    cache = pl.BlockSpec((1, N_PAIRS, LANES, n_cache), lambda b: (b, 0, 0, 0))
    return pl.pallas_call(
        functools.partial(_band_attn_sample_kernel, n_cache=n_cache, n_new=n_new),
        grid=(n_batch,),
        in_specs=[new(0), new(1), new(2), cache, cache,
                  pl.BlockSpec((N_PAIRS, 2 * n_new, n_keys), lambda b: (0, 0, 0))],
        out_specs=pl.BlockSpec((n_new, ATT_WIDTH), lambda b: (b, 0)),
        out_shape=jax.ShapeDtypeStruct((n_batch * n_new, ATT_WIDTH), BF16),
        scratch_shapes=[pltpu.VMEM((LANES, ATT_WIDTH), BF16),
                        pltpu.VMEM((LANES, ATT_WIDTH), BF16)],
        compiler_params=_params(("parallel",), VMEM_MID_MIB),
        name="band_attn_sample",
    )(qkvm, qkvm, qkvm, cache_kt, cache_vt, bias)


def _mem_attn_kernel(q_ref, k_ref, v_ref, o_ref):
    k = k_ref[0].astype(BF16)
    v = v_ref[0].astype(BF16)
    for h in range(MEM_HEADS):
        cs = slice(h * MEM_HEAD_DIM, (h + 1) * MEM_HEAD_DIM)
        sc = _dot_nt(q_ref[:, cs], k[:, cs]) * (MEM_HEAD_DIM ** -0.5 * LOG2E)
        mx = jnp.max(sc, axis=1, keepdims=True)
        p = jnp.exp2(sc - mx)
        l = jnp.sum(p, axis=1, keepdims=True)
        o = _dot(p.astype(BF16), v[:, cs]) * (1.0 / l)
        o_ref[:, cs] = o.astype(o_ref.dtype)


def _memory_attention(qkvm, mem_k, mem_v, n_batch, seq, tq):
    nt = seq // tq
    mem = pl.BlockSpec((1, N_MEM, MEM_WIDTH), lambda b, t: (b, 0, 0))
    return pl.pallas_call(
        _mem_attn_kernel,
        grid=(n_batch, nt),
        in_specs=[pl.BlockSpec((tq, MEM_WIDTH), lambda b, t: (b * nt + t, 3)), mem, mem],
        out_specs=pl.BlockSpec((tq, MEM_WIDTH), lambda b, t: (b * nt + t, 0)),
        out_shape=jax.ShapeDtypeStruct((n_batch * seq, MEM_WIDTH), BF16),
        compiler_params=_params(("parallel", "arbitrary"), VMEM_MID_MIB),
        name="mem_attn",
    )(qkvm, mem_k, mem_v)


def _gate_merge_kernel(u_ref, os_ref, oa_ref, om_ref, wgs_ref, wga_ref, wgm_ref,
                       wbs_ref, wba_ref, wbm_ref, out_ref):
    u = u_ref[...]

    def branch(o_ref, wg_ref, wb_ref):
        return jax.nn.sigmoid(_dot(u, wg_ref[...])) * _dot(o_ref[...], wb_ref[...])

    merged = (branch(os_ref, wgs_ref, wbs_ref) + branch(oa_ref, wga_ref, wba_ref)
              + branch(om_ref, wgm_ref, wbm_ref))
    out_ref[...] = merged.astype(out_ref.dtype)


def _gate_merge(u, o_s, o_a, o_m, w_gate, wb_s, wb_a, wb_m, tm, tn):
    m, d = u.shape
    gate = lambda b: pl.BlockSpec((d, tn), lambda i, n: (0, b * (d // tn) + n))
    wb = pl.BlockSpec((SSM_WIDTH, tn), lambda i, n: (0, n))
    ob = pl.BlockSpec((tm, SSM_WIDTH), lambda i, n: (i, 0))
    return pl.pallas_call(
        _gate_merge_kernel,
        grid=(m // tm, d // tn),
        in_specs=[pl.BlockSpec((tm, d), lambda i, n: (i, 0)), ob, ob, ob,
                  gate(0), gate(1), gate(2), wb, wb, wb],
        out_specs=pl.BlockSpec((tm, tn), lambda i, n: (i, n)),
        out_shape=jax.ShapeDtypeStruct((m, d), BF16),
        compiler_params=_params(("parallel", "arbitrary"), VMEM_BIG_MIB),
        name="gate_merge",
    )(u, o_s, o_a, o_m, w_gate, w_gate, w_gate, wb_s, wb_a, wb_m)


def _out_proj_kernel(x_ref, w_ref, h_ref, g_ref, o_ref):
    o_ref[...] = h_ref[...] + _rms(_dot(x_ref[...], w_ref[...]), g_ref[...])


def _out_proj(x, w_out, h, g_post, tm):
    m, d = h.shape
    row = lambda: pl.BlockSpec((tm, d), lambda i: (i, 0))
    return pl.pallas_call(
        _out_proj_kernel,
        grid=(m // tm,),
        in_specs=[row(), pl.BlockSpec((d, d), lambda i: (0, 0)), row(),
                  pl.BlockSpec((1, d), lambda i: (0, 0))],
        out_specs=row(),
        out_shape=jax.ShapeDtypeStruct((m, d), F32),
        compiler_params=_params(("parallel",), VMEM_BIG_MIB),
        name="out_proj",
    )(x, w_out, h, g_post)


def _layer(x, w, *, n_batch, seq, s0, mem_k, mem_v, cache_k, cache_v, tm, keep):
    n_tok = n_batch * seq
    n_chunks = seq // SSM_CHUNK
    big_tm = min(1024, n_tok)
    h1, u = _ffn(x, w["ffn1_norm_pre"], w["ffn1_norm_post"], w["ffn1_w_gate"], w["ffn1_w_up"],
                 w["ffn1_w_down"], w["mix_norm_pre"], big_tm, 512)

    qkvm, k_keep, v_keep, us_r = _in_proj(u, w["w_qkvm"], w["w_ssm"], tm, seq, keep)
    kv = (k_keep, v_keep)

    nb = 2 if (n_chunks > 1 and n_batch % 2 == 0) else (1 if n_chunks > 1 else n_batch)
    y_s, s_fin = _ssm_core(us_r, s0, w["ssm_bd"], w["ssm_cd"], w["ssm_pw"], w["ssm_d"], n_batch,
                           n_chunks, nb)
    o_s = _ssm_glu(y_s, w["ssm_w_glu"], w["ssm_b_glu"], big_tm)

    if cache_k is None:
        o_a = _band_attention_prompt(qkvm, w["att_rel_bias"], n_batch, seq)
    else:
        o_a = _band_attention_sample(qkvm, w["att_rel_bias"], cache_k, cache_v, n_batch, seq)
    o_m = _memory_attention(qkvm, mem_k, mem_v, n_batch, seq, min(4096, seq))

    merged = _gate_merge(u, o_s, o_a, o_m, w["w_gate"], w["w_branch_ssm"], w["w_branch_att"],
                         w["w_branch_mem"], big_tm, 512)
    h2 = _out_proj(merged, w["w_out"], h1, w["mix_norm_post"], tm)
    y = _ffn(h2, w["ffn2_norm_pre"], w["ffn2_norm_post"], w["ffn2_w_gate"], w["ffn2_w_up"],
             w["ffn2_w_down"], None, big_tm, 512)
    return y, kv, s_fin


def kernel(x_prompt, x_sample, mem_prompt, cache_att_k, cache_att_v, cache_mem_k, cache_mem_v, state_ssm_re, state_ssm_im, ffn1_norm_pre, ffn1_norm_post, ffn1_w_gate, ffn1_w_up, ffn1_w_down, mix_norm_pre, mix_norm_post, w_in, ssm_a_re, ssm_a_im, ssm_log_dt, ssm_b_re, ssm_b_im, ssm_c_re, ssm_c_im, ssm_d, ssm_w_glu, ssm_b_glu, att_rel_bias, mem_norm, w_mem_k, w_mem_v, w_branch_ssm, w_branch_att, w_branch_mem, w_out, ffn2_norm_pre, ffn2_norm_post, ffn2_w_gate, ffn2_w_up, ffn2_w_down):
    n_bp, t_p, d = x_prompt.shape
    n_bs, t_s, _ = x_sample.shape
    depth = ffn1_norm_pre.shape[0]
    assert depth == 1 and d == D_MODEL
    keep = min(BAND_PAST, t_p)
    l = 0

    vec = lambda a: a[l].reshape(1, -1).astype(F32)
    mat = lambda a: _cast_bf16(a[l])
    bd, cd, pw = _ssm_params(ssm_a_re[l], ssm_a_im[l], ssm_log_dt[l], ssm_b_re[l],
                             ssm_b_im[l], ssm_c_re[l], ssm_c_im[l])
    w = {
        "ffn1_norm_pre": vec(ffn1_norm_pre), "ffn1_norm_post": vec(ffn1_norm_post),
        "ffn1_w_gate": mat(ffn1_w_gate), "ffn1_w_up": mat(ffn1_w_up), "ffn1_w_down": mat(ffn1_w_down),
        "mix_norm_pre": vec(mix_norm_pre), "mix_norm_post": vec(mix_norm_post),
        "w_ssm": _cast_bf16(w_in[l], 0, COL_Q),
        "w_qkvm": _cast_bf16(w_in[l], COL_Q, COL_GATE - COL_Q),
        "w_gate": _cast_bf16(w_in[l], COL_GATE),
        "ssm_bd": bd, "ssm_cd": cd, "ssm_pw": pw,
        "ssm_d": ssm_d[l].reshape(OCTETS, 1, LANES).astype(F32), "ssm_w_glu": mat(ssm_w_glu), "ssm_b_glu": vec(ssm_b_glu),
        "att_rel_bias": att_rel_bias[l].astype(F32),
        "w_branch_ssm": mat(w_branch_ssm), "w_branch_att": mat(w_branch_att),
        "w_branch_mem": mat(w_branch_mem), "w_out": mat(w_out),
        "ffn2_norm_pre": vec(ffn2_norm_pre), "ffn2_norm_post": vec(ffn2_norm_post),
        "ffn2_w_gate": mat(ffn2_w_gate), "ffn2_w_up": mat(ffn2_w_up), "ffn2_w_down": mat(ffn2_w_down),
    }

    w_mem = jnp.concatenate([w_mem_k[l], w_mem_v[l]], axis=1).astype(BF16)
    mk_p, mv_p = _memory_kv(mem_prompt.reshape(n_bp * N_MEM, d), vec(mem_norm), w_mem, 512)
    mk_p = mk_p.reshape(n_bp, N_MEM, MEM_WIDTH)
    mv_p = mv_p.reshape(n_bp, N_MEM, MEM_WIDTH)
    zero_state = jnp.zeros((OCTETS, n_bp, OCT_STATE), F32)
    y_p, kv_p, sfin_p = _layer(x_prompt.reshape(n_bp * t_p, d), w, n_batch=n_bp, seq=t_p, s0=zero_state,
                               mem_k=mk_p, mem_v=mv_p, cache_k=None, cache_v=None, tm=512, keep=keep)
    sre_p, sim_p = _state_from_octets(sfin_p)

    n_cache = cache_att_k.shape[2]
    pairs_major = lambda c: jnp.transpose(c, (0, 2, 3, 1)).reshape(n_bs, N_PAIRS, LANES, n_cache)
    s0_s =_state_to_octets(state_ssm_re[l].astype(F32), state_ssm_im[l].astype(F32))
    y_s, kv_s, sfin_s = _layer(x_sample.reshape(n_bs * t_s, d), w, n_batch=n_bs, seq=t_s, s0=s0_s,
                               mem_k=cache_mem_k[l].reshape(n_bs, N_MEM, MEM_WIDTH),
                               mem_v=cache_mem_v[l].reshape(n_bs, N_MEM, MEM_WIDTH),
                               cache_k=pairs_major(cache_att_k[l]), cache_v=pairs_major(cache_att_v[l]),
                               tm=n_bs * t_s, keep=t_s)
    sre_s, sim_s = _state_from_octets(sfin_s)

    heads = lambda a, nb, t: a.reshape(1, nb, t, ATT_HEADS, ATT_HEAD_DIM)
    memh = lambda a: a.reshape(1, n_bp, N_MEM, MEM_HEADS, MEM_HEAD_DIM)
    return (y_p.reshape(n_bp, t_p, d), y_s.reshape(n_bs, t_s, d),
            heads(kv_p[0], n_bp, keep), heads(kv_p[1], n_bp, keep),
            memh(mk_p), memh(mv_p), sre_p[None], sim_p[None],
            heads(kv_s[0], n_bs, t_s), heads(kv_s[1], n_bs, t_s),
            sre_s[None], sim_s[None])
```

```python
import functools
import math

import numpy as np
import jax
import jax.numpy as jnp
from jax import lax
from jax.experimental import pallas as pl
from jax.experimental.pallas import tpu as pltpu

F32 = jnp.float32
BF16 = jnp.bfloat16

D_MODEL = 2048
CHUNK = 64
BAND_PAST_CHUNKS = 8
BAND_PAST = BAND_PAST_CHUNKS * CHUNK
ATT_HEADS = 16
ATT_HEAD_DIM = 64
ATT_WIDTH = ATT_HEADS * ATT_HEAD_DIM
REL_CLIP = 128
SSM_GROUP = 16
SSM_WIDTH = 1024
SSM_GROUPS = SSM_WIDTH // SSM_GROUP
SSM_STATE = 64
N_MEM = 256
MEM_HEADS = 4
MEM_HEAD_DIM = 256
MEM_WIDTH = MEM_HEADS * MEM_HEAD_DIM
EPS = 1e-6
NEG_INF = -1e30
PAST_LEN = 4096
LOG2E = math.log2(math.e)

COL_Q = SSM_WIDTH
COL_GATE = SSM_WIDTH + 3 * ATT_WIDTH + MEM_WIDTH

SSM_CHUNK = 16
OCTETS = 8
OCT_GROUPS = SSM_GROUPS // OCTETS
LANES = 128
MXU_DIM = 256
OCT_COLS = SSM_CHUNK * LANES
OCT_HALF = OCT_GROUPS * SSM_STATE
OCT_STATE = 2 * OCT_HALF

Q_TILE = 512
Q_SUB = 128
K_WIN = Q_SUB + BAND_PAST

MIB = 1024 * 1024
VMEM_BIG_MIB = 56
VMEM_MID_MIB = 48
VMEM_SMALL_MIB = 32


def _params(sem, vmem_mib):
    return pltpu.CompilerParams(dimension_semantics=sem, vmem_limit_bytes=vmem_mib * MIB)


def _dot(a, b):
    return jnp.dot(a, b, preferred_element_type=F32)


def _dot_nt(a, b):
    return lax.dot_general(a, b, (((1,), (1,)), ((), ())), preferred_element_type=F32)


def _rms(xf, g):
    y = xf * lax.rsqrt(jnp.mean(xf * xf, axis=-1, keepdims=True) + EPS)
    return y * g


CAST_BLOCK_BYTES = 8 * MIB


def _cast_kernel(x_ref, o_ref):
    o_ref[...] = x_ref[...].astype(o_ref.dtype)


def _cast_bf16(w, col0=0, n_cols=None):
    rows, cols = w.shape
    n_cols = cols - col0 if n_cols is None else n_cols
    cw = math.gcd(col0, n_cols)
    assert cw % LANES == 0 and col0 + n_cols <= cols
    tr = rows
    while tr * cw * 4 > CAST_BLOCK_BYTES and tr % 32 == 0:
        tr //= 2
    cb0 = col0 // cw
    return pl.pallas_call(
        _cast_kernel,
        grid=(rows // tr, n_cols // cw),
        in_specs=[pl.BlockSpec((tr, cw), lambda i, j: (i, cb0 + j))],
        out_specs=pl.BlockSpec((tr, cw), lambda i, j: (i, j)),
        out_shape=jax.ShapeDtypeStruct((rows, n_cols), BF16),
        compiler_params=_params(("parallel", "parallel"), VMEM_SMALL_MIB),
        name="cast_bf16",
    )(w)


FFN_SLICES = 8


def _ffn_kernel(hp_ref, hn_ref, gpre_ref, gpost_ref, wg_ref, wu_ref, wd_ref, *rest,
                n_tiles, emit_next):
    if emit_next:
        gnext_ref, out_ref, nxt_ref, *scratch = rest
    else:
        out_ref, *scratch = rest
    xn_refs, acc_refs = scratch[:2], scratch[2:]
    r = pl.program_id(0)
    f = pl.program_id(1)
    rs = hp_ref.shape[0]
    rows = pl.ds(pl.multiple_of(jnp.minimum(f, FFN_SLICES - 1) * rs, rs), rs)

    def pre_norm_slice(slot):
        xn_refs[slot][rows, :] = _rms(hn_ref[...], gpre_ref[...]).astype(BF16)

    def finish_slice(slot):
        hn = hp_ref[...] + 0.5 * _rms(acc_refs[slot][rows, :], gpost_ref[...])
        out_ref[...] = hn
        if emit_next:
            nxt_ref[...] = _rms(hn, gnext_ref[...]).astype(BF16)

    def matmul_chunk(slot):
        xn = xn_refs[slot][...]
        g = _dot(xn, wg_ref[...])
        u = _dot(xn, wu_ref[...])
        a = (g * jax.nn.sigmoid(g)) * u
        d = _dot(a.astype(BF16), wd_ref[...])
        acc_refs[slot][...] = jnp.where(f == 0, d, acc_refs[slot][...] + d)

    @pl.when((r == 0) & (f == 0))
    def _():
        acc_refs[1][...] = jnp.zeros_like(acc_refs[1])

    @pl.when(r == 0)
    def _():
        pre_norm_slice(0)

    for parity in range(2):
        @pl.when((r >= 1) & (r <= n_tiles) & (lax.rem(r, 2) == parity))
        def _():
            finish_slice(parity)
            pre_norm_slice(parity)
            matmul_chunk(1 - parity)

    @pl.when(r == n_tiles + 1)
    def _():
        finish_slice((n_tiles + 1) % 2)


def _ffn(h, g_pre, g_post, wg, wu, wd, g_next, tm, tf):
    emit_next = g_next is not None
    m, d = h.shape
    f_dim = wg.shape[1]
    nf = f_dim // tf
    n = m // tm
    rs = tm // FFN_SLICES
    assert m % tm == 0 and f_dim % tf == 0 and tm % FFN_SLICES == 0 and nf >= FFN_SLICES
    sl = lambda f: jnp.minimum(f, FFN_SLICES - 1)
    done = lambda r, f: (jnp.maximum(r - 2, 0) * FFN_SLICES + jnp.where(r >= 2, sl(f), 0), 0)
    ahead = lambda r, f: (jnp.minimum(r, n - 1) * FFN_SLICES + sl(f), 0)
    chunk = lambda r, f: jnp.where((r >= 1) & (r <= n), f, 0)
    vec = pl.BlockSpec((1, d), lambda r, f: (0, 0))
    in_specs = [pl.BlockSpec((rs, d), done), pl.BlockSpec((rs, d), ahead), vec, vec,
                pl.BlockSpec((d, tf), lambda r, f: (0, chunk(r, f))),
                pl.BlockSpec((d, tf), lambda r, f: (0, chunk(r, f))),
                pl.BlockSpec((tf, d), lambda r, f: (chunk(r, f), 0))]
    args = [h, h, g_pre, g_post, wg, wu, wd]
    out_shape = [jax.ShapeDtypeStruct((m, d), F32)]
    out_specs = [pl.BlockSpec((rs, d), done)]
    if emit_next:
        in_specs.append(vec)
        args.append(g_next)
        out_shape.append(jax.ShapeDtypeStruct((m, d), BF16))
        out_specs.append(pl.BlockSpec((rs, d), done))
    res = pl.pallas_call(
        functools.partial(_ffn_kernel, n_tiles=n, emit_next=emit_next),
        grid=(n + 2, nf),
        in_specs=in_specs,
        out_specs=out_specs,
        out_shape=out_shape,
        scratch_shapes=[pltpu.VMEM((tm, d), BF16), pltpu.VMEM((tm, d), BF16),
                        pltpu.VMEM((tm, d), F32), pltpu.VMEM((tm, d), F32)],
        compiler_params=_params(("arbitrary", "arbitrary"), VMEM_BIG_MIB),
        name="ffn",
    )(*args)
    return res if emit_next else res[0]


def _memory_kv_kernel(x_ref, g_ref, w_ref, k_ref, v_ref):
    x = _rms(x_ref[...], g_ref[...]).astype(BF16)
    k_ref[...] = _dot(x, w_ref[:, :MEM_WIDTH])
    v_ref[...] = _dot(x, w_ref[:, MEM_WIDTH:])


def _memory_kv(x, gain, w, tm):
    m, k = x.shape
    tm = min(tm, m)
    assert m % tm == 0 and w.shape == (k, 2 * MEM_WIDTH)
    out = pl.BlockSpec((tm, MEM_WIDTH), lambda i: (i, 0))
    return pl.pallas_call(
        _memory_kv_kernel,
        grid=(m // tm,),
        in_specs=[pl.BlockSpec((tm, k), lambda i: (i, 0)), pl.BlockSpec((1, k), lambda i: (0, 0)),
                  pl.BlockSpec((k, 2 * MEM_WIDTH), lambda i: (0, 0))],
        out_specs=[out, out],
        out_shape=[jax.ShapeDtypeStruct((m, MEM_WIDTH), F32)] * 2,
        compiler_params=_params(("parallel",), VMEM_MID_MIB),
        name="memory_kv",
    )(x, gain, w)


def _in_proj_kernel(x_ref, wq_ref, ws_ref, o_ref, k_ref, v_ref, us_ref, nat_ref):
    x = x_ref[...]
    for j, keep_ref in enumerate((None, k_ref, v_ref, None)):
        cs = slice(j * ATT_WIDTH, (j + 1) * ATT_WIDTH)
        r = _dot(x, wq_ref[:, cs])
        o_ref[:, cs] = r.astype(o_ref.dtype)
        if keep_ref is not None:
            keep_ref[...] = r
    r = _dot(x, ws_ref[...])
    tc = nat_ref.shape[1] // SSM_CHUNK
    for m in range(OCTETS):
        nat_ref[m] = r[:, m * LANES:(m + 1) * LANES]
    for j in range(SSM_CHUNK):
        for m in range(OCTETS):
            us_ref[m, :, j * LANES:(j + 1) * LANES] = nat_ref[m, pl.ds(j, tc, stride=SSM_CHUNK), :]


def _in_proj(u, w_qkvm, w_ssm, tm, seq, keep):
    n_tok = u.shape[0]
    assert n_tok % tm == 0 and tm % SSM_CHUNK == 0 and MEM_WIDTH == ATT_WIDTH
    tc = tm // SSM_CHUNK
    if keep == seq:
        kept_rows, kept = n_tok, pl.BlockSpec((tm, ATT_WIDTH), lambda i: (i, 0))
    else:
        assert keep == tm and seq % tm == 0
        kept_rows, kept = (n_tok // seq) * keep, pl.BlockSpec((tm, ATT_WIDTH), lambda i: (i // (seq // tm), 0))
    return pl.pallas_call(
        _in_proj_kernel,
        grid=(n_tok // tm,),
        in_specs=[pl.BlockSpec((tm, D_MODEL), lambda i: (i, 0)),
                  pl.BlockSpec((D_MODEL, 4 * ATT_WIDTH), lambda i: (0, 0)),
                  pl.BlockSpec((D_MODEL, SSM_WIDTH), lambda i: (0, 0))],
        out_specs=[pl.BlockSpec((tm, 4 * ATT_WIDTH), lambda i: (i, 0)), kept, kept,
                   pl.BlockSpec((OCTETS, tc, OCT_COLS), lambda i: (0, i, 0))],
        out_shape=[jax.ShapeDtypeStruct((n_tok, 4 * ATT_WIDTH), BF16),
                   jax.ShapeDtypeStruct((kept_rows, ATT_WIDTH), F32),
                   jax.ShapeDtypeStruct((kept_rows, ATT_WIDTH), F32),
                   jax.ShapeDtypeStruct((OCTETS, n_tok // SSM_CHUNK, OCT_COLS), F32)],
        scratch_shapes=[pltpu.VMEM((OCTETS, tm, LANES), F32)],
        compiler_params=_params(("arbitrary",), VMEM_BIG_MIB),
        name="in_proj",
    )(u, w_qkvm, w_ssm)


def _ssm_params(a_re, a_im, log_dt, b_re, b_im, c_re, c_im):
    dt = jnp.exp(log_dt)[:, None]
    mag = jnp.exp(a_re * dt)
    ab_re = mag * jnp.cos(a_im * dt)
    ab_im = mag * jnp.sin(a_im * dt)
    den = a_re * a_re + a_im * a_im
    n_re = ab_re - 1.0
    n_im = ab_im
    k_re = (n_re * a_re + n_im * a_im) / den
    k_im = (n_im * a_re - n_re * a_im) / den
    bb_re = k_re[..., None] * b_re - k_im[..., None] * b_im
    bb_im = k_re[..., None] * b_im + k_im[..., None] * b_re
    pr = [jnp.ones_like(ab_re)]
    pi = [jnp.zeros_like(ab_re)]
    for _ in range(SSM_CHUNK):
        pr.append(pr[-1] * ab_re - pi[-1] * ab_im)
        pi.append(pr[-2] * ab_im + pi[-1] * ab_re)
    n_pw = SSM_CHUNK + 1
    pw = jnp.concatenate([jnp.stack(pr).reshape(n_pw, OCTETS, OCT_HALF),
                          jnp.stack(pi).reshape(n_pw, OCTETS, OCT_HALF)], axis=2)
    pw = jnp.transpose(pw, (1, 0, 2))
    eye = jnp.eye(OCT_GROUPS, dtype=F32)

    def expand(x):
        x4 = x.reshape(OCTETS, OCT_GROUPS, x.shape[1], SSM_STATE)
        out = eye[None, :, None, :, None] * x4[:, :, :, None, :]
        return out.reshape(OCTETS, OCT_GROUPS * x.shape[1], OCT_HALF)

    bd = jnp.concatenate([expand(jnp.transpose(bb_re, (0, 2, 1))),
                          expand(jnp.transpose(bb_im, (0, 2, 1)))], axis=2)
    cd = jnp.concatenate([expand(c_re), expand(c_im)], axis=2)
    return bd, cd, pw


def _state_to_octets(s_re, s_im):
    b = s_re.shape[0]
    s = jnp.stack([s_re, s_im], 0).reshape(2, b, OCTETS, OCT_GROUPS, SSM_STATE)
    return jnp.transpose(s, (2, 1, 0, 3, 4)).reshape(OCTETS, b, OCT_STATE)


def _state_from_octets(s):
    b = s.shape[1]
    s = s.reshape(OCTETS, b, 2, OCT_GROUPS, SSM_STATE)
    s = jnp.transpose(s, (2, 1, 0, 3, 4)).reshape(2, b, SSM_GROUPS, SSM_STATE)
    return s[0], s[1]


def _split_bf16(x):
    hi = x.astype(BF16)
    return hi, (x - hi.astype(F32)).astype(BF16)


def _ssm_build_operators(bd_ref, cd_ref, pw_ref, t8_ref, wus_ref, wso_ref):
    bd_re, bd_im = bd_ref[0, :, :OCT_HALF], bd_ref[0, :, OCT_HALF:]
    cd_re, cd_im = cd_ref[0, :, :OCT_HALF], cd_ref[0, :, OCT_HALF:]
    c_hi, c_lo = _split_bf16(jnp.concatenate([cd_re, -cd_im], axis=1))
    blk = lambda i: slice(i * LANES, (i + 1) * LANES)
    for a in range(SSM_CHUNK // 2):
        t8_ref[blk(2 * a + 1), blk(2 * a)] = jnp.zeros((LANES, LANES), BF16)
    for k in range(SSM_CHUNK + 1):
        p_re = pw_ref[0, k:k + 1, :OCT_HALF]
        p_im = pw_ref[0, k:k + 1, OCT_HALF:]
        if k < SSM_CHUNK:
            e = jnp.concatenate([bd_re * p_re - bd_im * p_im, bd_re * p_im + bd_im * p_re], axis=1)
            wus_ref[blk(SSM_CHUNK - 1 - k), :] = e.astype(BF16)
            e_hi, e_lo = _split_bf16(e)
            lag = (_dot_nt(e_hi, c_hi) + _dot_nt(e_hi, c_lo) + _dot_nt(e_lo, c_hi)).astype(BF16)
            for j in range(SSM_CHUNK - k):
                t8_ref[blk(j), blk(j + k)] = lag
        if k >= 1:
            g = jnp.concatenate([cd_re * p_re - cd_im * p_im, -(cd_re * p_im + cd_im * p_re)], axis=1)
            wso_ref[blk(k - 1), :] = g.astype(BF16)


def _ssm_core_kernel(us_ref, s0_ref, bd_ref, cd_ref, pw_ref, d_ref, y_ref, sfin_ref,
                     t8_ref, wus_ref, wso_ref, ds_ref, sp_ref, *, nb, n_chunks):
    @pl.when(pl.program_id(1) == 0)
    def _():
        _ssm_build_operators(bd_ref, cd_ref, pw_ref, t8_ref, wus_ref, wso_ref)

    ub = us_ref[0].astype(BF16)
    ds = _dot(ub, wus_ref[...])
    a16 = pw_ref[0, SSM_CHUNK:SSM_CHUNK + 1, :]
    n_blk = OCT_HALF // LANES
    blk = lambda k: slice(k * LANES, (k + 1) * LANES)

    def advance(s_re, s_im, d_re, d_im, a_re, a_im):
        return a_re * s_re - a_im * s_im + d_re, a_re * s_im + a_im * s_re + d_im

    if n_chunks == 1:
        s0 = s0_ref[0, 0]
        n_re, n_im = advance(s0[:, :OCT_HALF], s0[:, OCT_HALF:], ds[:, :OCT_HALF], ds[:, OCT_HALF:],
                             a16[:, :OCT_HALF], a16[:, OCT_HALF:])
        sfin_ref[0, 0] = jnp.concatenate([n_re, n_im], axis=1)
        spb = s0.astype(BF16)
    else:
        for k in range(2 * n_blk):
            ds_ref[k] = ds[:, blk(k)]

        def body(c, carry):
            at = pl.ds(c, nb, stride=n_chunks)
            new = list(carry)
            for k in range(n_blk):
                sp_ref[k, at, :] = carry[k]
                sp_ref[n_blk + k, at, :] = carry[n_blk + k]
                new[k], new[n_blk + k] = advance(carry[k], carry[n_blk + k], ds_ref[k, at, :],
                                                 ds_ref[n_blk + k, at, :], a16[:, blk(k)],
                                                 a16[:, blk(n_blk + k)])
            return tuple(new)

        init = tuple(s0_ref[0, 0, :, blk(k)] for k in range(2 * n_blk))
        fin = lax.fori_loop(0, n_chunks, body, init, unroll=2)
        for k in range(2 * n_blk):
            sfin_ref[0, 0, :, blk(k)] = fin[k]
        spb = jnp.concatenate([sp_ref[k] for k in range(2 * n_blk)], axis=1).astype(BF16)
    rows = ub.shape[0]
    d_skip = d_ref[0]
    for nt in range(OCT_COLS // MXU_DIM):
        k_hi = (nt + 1) * MXU_DIM
        cs = slice(nt * MXU_DIM, k_hi)
        y = _dot(ub[:, :k_hi], t8_ref[:k_hi, cs]) + _dot_nt(spb, wso_ref[cs, :])
        for j in range(MXU_DIM // LANES):
            i = nt * (MXU_DIM // LANES) + j
            y_ref[pl.ds(i, rows, stride=SSM_CHUNK), :] = (y[:, blk(j)] + d_skip * us_ref[0, :, blk(i)])


def _ssm_core(us_r, s0, bd, cd, pw, d_oct, n_batch, n_chunks, nb):
    nc = us_r.shape[1]
    rows = nb * n_chunks
    assert n_batch % nb == 0 and nc == n_batch * n_chunks
    nr = n_batch // nb
    s0 = s0.reshape(OCTETS, nr, nb, OCT_STATE)
    par = lambda rows_: pl.BlockSpec((1, rows_, OCT_STATE), lambda m, r: (m, 0, 0))
    y, sfin = pl.pallas_call(
        functools.partial(_ssm_core_kernel, nb=nb, n_chunks=n_chunks),
        grid=(OCTETS, nr),
        in_specs=[pl.BlockSpec((1, rows, OCT_COLS), lambda m, r: (m, r, 0)),
                  pl.BlockSpec((1, 1, nb, OCT_STATE), lambda m, r: (m, r, 0, 0)),
                  par(LANES), par(LANES), par(SSM_CHUNK + 1),
                  pl.BlockSpec((1, 1, LANES), lambda m, r: (m, 0, 0))],
        out_specs=[pl.BlockSpec((rows * SSM_CHUNK, LANES), lambda m, r: (r, m)),
                   pl.BlockSpec((1, 1, nb, OCT_STATE), lambda m, r: (m, r, 0, 0))],
        out_shape=[jax.ShapeDtypeStruct((nc * SSM_CHUNK, SSM_WIDTH), F32),
                   jax.ShapeDtypeStruct((OCTETS, nr, nb, OCT_STATE), F32)],
        scratch_shapes=[pltpu.VMEM((OCT_COLS, OCT_COLS), BF16),
                        pltpu.VMEM((OCT_COLS, OCT_STATE), BF16),
                        pltpu.VMEM((OCT_COLS, OCT_STATE), BF16),
                        pltpu.VMEM((OCT_STATE // LANES, rows, LANES), F32),
                        pltpu.VMEM((OCT_STATE // LANES, rows, LANES), F32)],
        compiler_params=_params(("arbitrary", "arbitrary"), VMEM_BIG_MIB),
        name="ssm_core",
    )(us_r, s0, bd, cd, pw, d_oct)
    return y, sfin.reshape(OCTETS, n_batch, OCT_STATE)


def _gelu_tanh(x):
    c = math.sqrt(2.0 / math.pi)
    return x * (0.5 * (1.0 + jnp.tanh(c * (x + 0.044715 * (x * x * x)))))


def _ssm_glu_kernel(y_ref, w_ref, b_ref, o_ref):
    yg = _gelu_tanh(y_ref[...])
    z = _dot(yg.astype(BF16), w_ref[...]) + b_ref[...]
    o_ref[...] = (yg * jax.nn.sigmoid(z)).astype(o_ref.dtype)


def _ssm_glu(y, w_glu, b_glu, tm):
    n_tok = y.shape[0]
    assert n_tok % tm == 0
    row = pl.BlockSpec((tm, SSM_WIDTH), lambda i: (i, 0))
    return pl.pallas_call(
        _ssm_glu_kernel,
        grid=(n_tok // tm,),
        in_specs=[row, pl.BlockSpec((SSM_WIDTH, SSM_WIDTH), lambda i: (0, 0)),
                  pl.BlockSpec((1, SSM_WIDTH), lambda i: (0, 0))],
        out_specs=row,
        out_shape=jax.ShapeDtypeStruct((n_tok, SSM_WIDTH), BF16),
        compiler_params=_params(("parallel",), VMEM_MID_MIB),
        name="ssm_glu",
    )(y, w_glu, b_glu)


N_PAIRS = ATT_HEADS // 2


def _head_masks():
    lane = lax.broadcasted_iota(jnp.int32, (1, LANES), 1)
    return (lane < ATT_HEAD_DIM, lane >= ATT_HEAD_DIM)


def _pair_scores(q2, kw, bias2, masks):
    qq = jnp.concatenate([jnp.where(m, q2, jnp.zeros_like(q2)) for m in masks], axis=0)
    return _dot_nt(qq, kw) * (ATT_HEAD_DIM ** -0.5 * LOG2E) + bias2


def _row_max(sc):
    return jnp.broadcast_to(jnp.max(sc, axis=1, keepdims=True), (sc.shape[0], LANES))


def _softmax_parts(sc, mx):
    p = jnp.exp2(sc - jnp.concatenate([mx] * (sc.shape[1] // LANES), axis=1))
    l = jnp.sum(p, axis=1, keepdims=True)
    return p.astype(BF16), jnp.broadcast_to(1.0 / l, (sc.shape[0], LANES))


def _pair_output(p, rl, vw, masks):
    o2 = _dot(p, vw) * rl
    n_q = o2.shape[0] // 2
    return jnp.where(masks[0], o2[:n_q], o2[n_q:])


def _band_attn_kernel(q_ref, kp_ref, kc_ref, vp_ref, vc_ref, bias_ref, o_ref,
                      sc_ref, mx_ref, p_ref, rl_ref):
    masks = _head_masks()
    cs = lambda hp: slice(hp * LANES, (hp + 1) * LANES)

    def window(prev_ref, cur_ref, r0, cols):
        return jnp.concatenate([prev_ref[r0:BAND_PAST, cols], cur_ref[0:r0 + Q_SUB, cols]], axis=0)

    for r0 in range(0, Q_TILE, Q_SUB):
        rows = slice(r0, r0 + Q_SUB)
        for hp in range(N_PAIRS):
            sc = _pair_scores(q_ref[rows, cs(hp)], window(kp_ref, kc_ref, r0, cs(hp)), bias_ref[hp], masks)
            sc_ref[hp] = sc
            mx_ref[hp] = _row_max(sc)

        @pl.when(pl.program_id(1) == 0)
        def _():
            col = lax.broadcasted_iota(jnp.int32, (1, K_WIN), 1)
            extra = jnp.where(col + r0 < BAND_PAST, NEG_INF, 0.0)
            for hp in range(N_PAIRS):
                sc = sc_ref[hp] + extra
                sc_ref[hp] = sc
                mx_ref[hp] = _row_max(sc)

        for hp in range(N_PAIRS):
            p_ref[hp], rl_ref[hp] = _softmax_parts(sc_ref[hp], mx_ref[hp])
        for hp in range(N_PAIRS):
            o = _pair_output(p_ref[hp], rl_ref[hp], window(vp_ref, vc_ref, r0, cs(hp)), masks)
            o_ref[rows, cs(hp)] = o.astype(o_ref.dtype)


def _rel_bias_tile(rel_bias, n_q, n_k, offset, ok):
    r = np.arange(n_q - 1 + n_k)
    idx = np.clip(offset + n_q - 1 - r, -REL_CLIP, REL_CLIP) + REL_CLIP
    v = jnp.pad(rel_bias[:, idx] * LOG2E, ((0, 0), (0, 1)))
    w = n_q + n_k
    flat = jnp.tile(v, (1, n_q))[:, :n_q * (w - 1)]
    toep = flat.reshape(rel_bias.shape[0], n_q, w - 1)[:, :, n_q - 1:]
    return jnp.where(ok[None], toep, NEG_INF).reshape(N_PAIRS, 2 * n_q, n_k)


def _band_attention_prompt(qkvm, rel_bias, n_batch, seq):
    assert seq % Q_TILE == 0 and Q_TILE % BAND_PAST == 0
    nt = seq // Q_TILE
    qc = np.arange(Q_SUB)[:, None] // CHUNK
    kc = np.arange(K_WIN)[None, :] // CHUNK
    bias = _rel_bias_tile(rel_bias, Q_SUB, K_WIN, BAND_PAST, (kc >= qc) & (kc <= qc + BAND_PAST_CHUNKS))
    blk = (Q_TILE, ATT_WIDTH)
    ratio = Q_TILE // BAND_PAST
    cur = lambda col: pl.BlockSpec(blk, lambda b, t: (b * nt + t, col))
    prev = lambda col: pl.BlockSpec(
        (BAND_PAST, ATT_WIDTH), lambda b, t: (jnp.maximum((b * nt + t) * ratio - 1, b * nt * ratio), col))
    return pl.pallas_call(
        _band_attn_kernel,
        grid=(n_batch, nt),
        in_specs=[cur(0), prev(1), cur(1), prev(2), cur(2),
                  pl.BlockSpec((N_PAIRS, 2 * Q_SUB, K_WIN), lambda b, t: (0, 0, 0))],
        out_specs=pl.BlockSpec(blk, lambda b, t: (b * nt + t, 0)),
        out_shape=jax.ShapeDtypeStruct((n_batch * seq, ATT_WIDTH), BF16),
        scratch_shapes=[pltpu.VMEM((N_PAIRS, 2 * Q_SUB, K_WIN), F32),
                        pltpu.VMEM((N_PAIRS, 2 * Q_SUB, LANES), F32),
                        pltpu.VMEM((N_PAIRS, 2 * Q_SUB, K_WIN), BF16),
                        pltpu.VMEM((N_PAIRS, 2 * Q_SUB, LANES), F32)],
        compiler_params=_params(("parallel", "arbitrary"), VMEM_MID_MIB),
        name="band_attn",
    )(qkvm, qkvm, qkvm, qkvm, qkvm, bias)


def _band_attn_sample_kernel(q_ref, kn_ref, vn_ref, ckt_ref, cvt_ref, bias_ref, o_ref, kn_pad_ref,
                             vn_pad_ref, *, n_cache, n_new):
    kn_pad_ref[...] = jnp.zeros_like(kn_pad_ref)
    vn_pad_ref[...] = jnp.zeros_like(vn_pad_ref)
    kn_pad_ref[0:n_new] = kn_ref[...]
    vn_pad_ref[0:n_new] = vn_ref[...]
    masks = _head_masks()
    for hp in range(N_PAIRS):
        cs = slice(hp * LANES, (hp + 1) * LANES)
        q2 = q_ref[:, cs]
        qq = jnp.concatenate([jnp.where(m, q2, jnp.zeros_like(q2)) for m in masks], axis=0)
        raw = jnp.concatenate([_dot(qq, ckt_ref[0, hp].astype(BF16)), _dot_nt(qq, kn_pad_ref[:, cs])],
                              axis=1)
        sc = raw * (ATT_HEAD_DIM ** -0.5 * LOG2E) + bias_ref[hp]
        p, rl = _softmax_parts(sc, _row_max(sc))
        o2 = (_dot_nt(p[:, :n_cache], cvt_ref[0, hp].astype(BF16))
              + _dot(p[:, n_cache:], vn_pad_ref[:, cs])) * rl
        o_ref[:, cs] = jnp.where(masks[0], o2[:n_new], o2[n_new:]).astype(o_ref.dtype)


def _band_attention_sample(qkvm, rel_bias, cache_kt, cache_vt, n_batch, n_new):
    n_cache = cache_kt.shape[3]
    n_keys = n_cache + LANES
    q_pos = PAST_LEN + np.arange(n_new)[:, None]
    j = np.arange(n_keys)[None, :]
    k_pos = PAST_LEN - n_cache + j
    ok = ((j < n_cache + n_new) & (k_pos >= 0) & (k_pos // CHUNK <= q_pos // CHUNK)
          & (k_pos // CHUNK >= q_pos // CHUNK - BAND_PAST_CHUNKS))
    bias = _rel_bias_tile(rel_bias, n_new, n_keys, n_cache, ok)
    new = lambda col: pl.BlockSpec((n_new, ATT_WIDTH), lambda b: (b, col))
    cache = pl.BlockSpec((1, N_PAIRS, LANES, n_cache), lambda b: (b, 0, 0, 0))
    return pl.pallas_call(
        functools.partial(_band_attn_sample_kernel, n_cache=n_cache, n_new=n_new),
        grid=(n_batch,),
        in_specs=[new(0), new(1), new(2), cache, cache,
                  pl.BlockSpec((N_PAIRS, 2 * n_new, n_keys), lambda b: (0, 0, 0))],
        out_specs=pl.BlockSpec((n_new, ATT_WIDTH), lambda b: (b, 0)),
        out_shape=jax.ShapeDtypeStruct((n_batch * n_new, ATT_WIDTH), BF16),
        scratch_shapes=[pltpu.VMEM((LANES, ATT_WIDTH), BF16),
                        pltpu.VMEM((LANES, ATT_WIDTH), BF16)],
        compiler_params=_params(("parallel",), VMEM_MID_MIB),
        name="band_attn_sample",
    )(qkvm, qkvm, qkvm, cache_kt, cache_vt, bias)


def _mem_attn_kernel(q_ref, k_ref, v_ref, o_ref):
    k = k_ref[0].astype(BF16)
    v = v_ref[0].astype(BF16)
    for h in range(MEM_HEADS):
        cs = slice(h * MEM_HEAD_DIM, (h + 1) * MEM_HEAD_DIM)
        sc = _dot_nt(q_ref[:, cs], k[:, cs]) * (MEM_HEAD_DIM ** -0.5 * LOG2E)
        mx = jnp.max(sc, axis=1, keepdims=True)
        p = jnp.exp2(sc - mx)
        l = jnp.sum(p, axis=1, keepdims=True)
        o = _dot(p.astype(BF16), v[:, cs]) * (1.0 / l)
        o_ref[:, cs] = o.astype(o_ref.dtype)


def _memory_attention(qkvm, mem_k, mem_v, n_batch, seq, tq):
    nt = seq // tq
    mem = pl.BlockSpec((1, N_MEM, MEM_WIDTH), lambda b, t: (b, 0, 0))
    return pl.pallas_call(
        _mem_attn_kernel,
        grid=(n_batch, nt),
        in_specs=[pl.BlockSpec((tq, MEM_WIDTH), lambda b, t: (b * nt + t, 3)), mem, mem],
        out_specs=pl.BlockSpec((tq, MEM_WIDTH), lambda b, t: (b * nt + t, 0)),
        out_shape=jax.ShapeDtypeStruct((n_batch * seq, MEM_WIDTH), BF16),
        compiler_params=_params(("parallel", "arbitrary"), VMEM_MID_MIB),
        name="mem_attn",
    )(qkvm, mem_k, mem_v)


def _gate_merge_kernel(u_ref, os_ref, oa_ref, om_ref, wgs_ref, wga_ref, wgm_ref,
                       wbs_ref, wba_ref, wbm_ref, out_ref):
    u = u_ref[...]

    def branch(o_ref, wg_ref, wb_ref):
        return jax.nn.sigmoid(_dot(u, wg_ref[...])) * _dot(o_ref[...], wb_ref[...])

    merged = (branch(os_ref, wgs_ref, wbs_ref) + branch(oa_ref, wga_ref, wba_ref)
              + branch(om_ref, wgm_ref, wbm_ref))
    out_ref[...] = merged.astype(out_ref.dtype)


def _gate_merge(u, o_s, o_a, o_m, w_gate, wb_s, wb_a, wb_m, tm, tn):
    m, d = u.shape
    gate = lambda b: pl.BlockSpec((d, tn), lambda i, n: (0, b * (d // tn) + n))
    wb = pl.BlockSpec((SSM_WIDTH, tn), lambda i, n: (0, n))
    ob = pl.BlockSpec((tm, SSM_WIDTH), lambda i, n: (i, 0))
    return pl.pallas_call(
        _gate_merge_kernel,
        grid=(m // tm, d // tn),
        in_specs=[pl.BlockSpec((tm, d), lambda i, n: (i, 0)), ob, ob, ob,
                  gate(0), gate(1), gate(2), wb, wb, wb],
        out_specs=pl.BlockSpec((tm, tn), lambda i, n: (i, n)),
        out_shape=jax.ShapeDtypeStruct((m, d), BF16),
        compiler_params=_params(("parallel", "arbitrary"), VMEM_BIG_MIB),
        name="gate_merge",
    )(u, o_s, o_a, o_m, w_gate, w_gate, w_gate, wb_s, wb_a, wb_m)


def _out_proj_kernel(x_ref, w_ref, h_ref, g_ref, o_ref):
    o_ref[...] = h_ref[...] + _rms(_dot(x_ref[...], w_ref[...]), g_ref[...])


def _out_proj(x, w_out, h, g_post, tm):
    m, d = h.shape
    row = lambda: pl.BlockSpec((tm, d), lambda i: (i, 0))
    return pl.pallas_call(
        _out_proj_kernel,
        grid=(m // tm,),
        in_specs=[row(), pl.BlockSpec((d, d), lambda i: (0, 0)), row(),
                  pl.BlockSpec((1, d), lambda i: (0, 0))],
        out_specs=row(),
        out_shape=jax.ShapeDtypeStruct((m, d), F32),
        compiler_params=_params(("parallel",), VMEM_BIG_MIB),
        name="out_proj",
    )(x, w_out, h, g_post)


def _layer(x, w, *, n_batch, seq, s0, mem_k, mem_v, cache_k, cache_v, tm, keep):
    n_tok = n_batch * seq
    n_chunks = seq // SSM_CHUNK
    big_tm = min(1024, n_tok)
    h1, u = _ffn(x, w["ffn1_norm_pre"], w["ffn1_norm_post"], w["ffn1_w_gate"], w["ffn1_w_up"],
                 w["ffn1_w_down"], w["mix_norm_pre"], big_tm, 512)

    qkvm, k_keep, v_keep, us_r = _in_proj(u, w["w_qkvm"], w["w_ssm"], tm, seq, keep)
    kv = (k_keep, v_keep)

    nb = 2 if (n_chunks > 1 and n_batch % 2 == 0) else (1 if n_chunks > 1 else n_batch)
    y_s, s_fin = _ssm_core(us_r, s0, w["ssm_bd"], w["ssm_cd"], w["ssm_pw"], w["ssm_d"], n_batch,
                           n_chunks, nb)
    o_s = _ssm_glu(y_s, w["ssm_w_glu"], w["ssm_b_glu"], big_tm)

    if cache_k is None:
        o_a = _band_attention_prompt(qkvm, w["att_rel_bias"], n_batch, seq)
    else:
        o_a = _band_attention_sample(qkvm, w["att_rel_bias"], cache_k, cache_v, n_batch, seq)
    o_m = _memory_attention(qkvm, mem_k, mem_v, n_batch, seq, min(4096, seq))

    merged = _gate_merge(u, o_s, o_a, o_m, w["w_gate"], w["w_branch_ssm"], w["w_branch_att"],
                         w["w_branch_mem"], big_tm, 512)
    h2 = _out_proj(merged, w["w_out"], h1, w["mix_norm_post"], tm)
    y = _ffn(h2, w["ffn2_norm_pre"], w["ffn2_norm_post"], w["ffn2_w_gate"], w["ffn2_w_up"],
             w["ffn2_w_down"], None, big_tm, 512)
    return y, kv, s_fin


def kernel(x_prompt, x_sample, mem_prompt, cache_att_k, cache_att_v, cache_mem_k, cache_mem_v, state_ssm_re, state_ssm_im, ffn1_norm_pre, ffn1_norm_post, ffn1_w_gate, ffn1_w_up, ffn1_w_down, mix_norm_pre, mix_norm_post, w_in, ssm_a_re, ssm_a_im, ssm_log_dt, ssm_b_re, ssm_b_im, ssm_c_re, ssm_c_im, ssm_d, ssm_w_glu, ssm_b_glu, att_rel_bias, mem_norm, w_mem_k, w_mem_v, w_branch_ssm, w_branch_att, w_branch_mem, w_out, ffn2_norm_pre, ffn2_norm_post, ffn2_w_gate, ffn2_w_up, ffn2_w_down):
    n_bp, t_p, d = x_prompt.shape
    n_bs, t_s, _ = x_sample.shape
    depth = ffn1_norm_pre.shape[0]
    assert depth == 1 and d == D_MODEL
    keep = min(BAND_PAST, t_p)
    l = 0

    vec = lambda a: a[l].reshape(1, -1).astype(F32)
    mat = lambda a: _cast_bf16(a[l])
    bd, cd, pw = _ssm_params(ssm_a_re[l], ssm_a_im[l], ssm_log_dt[l], ssm_b_re[l],
                             ssm_b_im[l], ssm_c_re[l], ssm_c_im[l])
    w = {
        "ffn1_norm_pre": vec(ffn1_norm_pre), "ffn1_norm_post": vec(ffn1_norm_post),
        "ffn1_w_gate": mat(ffn1_w_gate), "ffn1_w_up": mat(ffn1_w_up), "ffn1_w_down": mat(ffn1_w_down),
        "mix_norm_pre": vec(mix_norm_pre), "mix_norm_post": vec(mix_norm_post),
        "w_ssm": _cast_bf16(w_in[l], 0, COL_Q),
        "w_qkvm": _cast_bf16(w_in[l], COL_Q, COL_GATE - COL_Q),
        "w_gate": _cast_bf16(w_in[l], COL_GATE),
        "ssm_bd": bd, "ssm_cd": cd, "ssm_pw": pw,
        "ssm_d": ssm_d[l].reshape(OCTETS, 1, LANES).astype(F32), "ssm_w_glu": mat(ssm_w_glu), "ssm_b_glu": vec(ssm_b_glu),
        "att_rel_bias": att_rel_bias[l].astype(F32),
        "w_branch_ssm": mat(w_branch_ssm), "w_branch_att": mat(w_branch_att),
        "w_branch_mem": mat(w_branch_mem), "w_out": mat(w_out),
        "ffn2_norm_pre": vec(ffn2_norm_pre), "ffn2_norm_post": vec(ffn2_norm_post),
        "ffn2_w_gate": mat(ffn2_w_gate), "ffn2_w_up": mat(ffn2_w_up), "ffn2_w_down": mat(ffn2_w_down),
    }

    w_mem = jnp.concatenate([w_mem_k[l], w_mem_v[l]], axis=1).astype(BF16)
    mk_p, mv_p = _memory_kv(mem_prompt.reshape(n_bp * N_MEM, d), vec(mem_norm), w_mem, 512)
    mk_p = mk_p.reshape(n_bp, N_MEM, MEM_WIDTH)
    mv_p = mv_p.reshape(n_bp, N_MEM, MEM_WIDTH)
    zero_state = jnp.zeros((OCTETS, n_bp, OCT_STATE), F32)
    y_p, kv_p, sfin_p = _layer(x_prompt.reshape(n_bp * t_p, d), w, n_batch=n_bp, seq=t_p, s0=zero_state,
                               mem_k=mk_p, mem_v=mv_p, cache_k=None, cache_v=None, tm=512, keep=keep)
    sre_p, sim_p = _state_from_octets(sfin_p)

    n_cache = cache_att_k.shape[2]
    pairs_major = lambda c: jnp.transpose(c, (0, 2, 3, 1)).reshape(n_bs, N_PAIRS, LANES, n_cache)
    s0_s =_state_to_octets(state_ssm_re[l].astype(F32), state_ssm_im[l].astype(F32))
    y_s, kv_s, sfin_s = _layer(x_sample.reshape(n_bs * t_s, d), w, n_batch=n_bs, seq=t_s, s0=s0_s,
                               mem_k=cache_mem_k[l].reshape(n_bs, N_MEM, MEM_WIDTH),
                               mem_v=cache_mem_v[l].reshape(n_bs, N_MEM, MEM_WIDTH),
                               cache_k=pairs_major(cache_att_k[l]), cache_v=pairs_major(cache_att_v[l]),
                               tm=n_bs * t_s, keep=t_s)
    sre_s, sim_s = _state_from_octets(sfin_s)

    heads = lambda a, nb, t: a.reshape(1, nb, t, ATT_HEADS, ATT_HEAD_DIM)
    memh = lambda a: a.reshape(1, n_bp, N_MEM, MEM_HEADS, MEM_HEAD_DIM)
    return (y_p.reshape(n_bp, t_p, d), y_s.reshape(n_bs, t_s, d),
            heads(kv_p[0], n_bp, keep), heads(kv_p[1], n_bp, keep),
            memh(mk_p), memh(mv_p), sre_p[None], sim_p[None],
            heads(kv_s[0], n_bs, t_s), heads(kv_s[1], n_bs, t_s),
            sre_s[None], sim_s[None])
```
